```python
import math
import jax
import jax.numpy as jnp
from jax import lax
import numpy as np

D_MODEL = 1024
BATCH = 16
SEQ = 256
DEPTH = 2
DEC_BATCH = 4
DEC_SEQ = 4096
PAST_LEN = 256

GRID_W = 64
N_BRANCH = 4
MIX_W = D_MODEL // N_BRANCH
HEADS = 4
HEAD_DIM = MIX_W // HEADS
NA_WIN_H = 8
NA_WIN_W = 16
GLA_GATE_RANK = 16
GLA_GATE_NORM = 16.0
GDN_CONV = 3
DIFF_D = HEAD_DIM // 2
CHUNK = 64
Q_BLOCK = 128
N_EXPERTS = 16
N_GROUPS = 4
EXPERTS_PER_GROUP = N_EXPERTS // N_GROUPS
TOP_K = 2
D_FF_EXPERT = D_MODEL // 4
ROPE_BASE = 10000.0
EPS = 1e-6
NEG = -1e30

IN_SPLITS = (
    ('na_q', MIX_W), ('na_k', MIX_W), ('na_v', MIX_W),
    ('gla_q', MIX_W), ('gla_k', MIX_W), ('gla_v', MIX_W),
    ('gla_gf', GLA_GATE_RANK), ('gla_gb', GLA_GATE_RANK), ('gla_og', MIX_W),
    ('gdn_qkv', 3 * MIX_W), ('gdn_af', HEADS), ('gdn_ab', HEADS),
    ('gdn_bf', HEADS), ('gdn_bb', HEADS), ('gdn_og', MIX_W),
    ('diff_q', MIX_W), ('diff_k', MIX_W), ('diff_v', MIX_W),
    ('branch_gate', N_BRANCH * D_MODEL),
)
IN_TOTAL = sum(size for _, size in IN_SPLITS)

kernel_name = 'hybrid_prefix_flow_step'


def _rmsnorm(x, w):
    xf = x.astype(jnp.float32)
    y = xf * lax.rsqrt(jnp.mean(xf * xf, axis=-1, keepdims=True) + EPS)
    return (y * w.astype(jnp.float32)).astype(x.dtype)


def _l2norm(x):
    return x * lax.rsqrt(jnp.sum(x * x, axis=-1, keepdims=True) + EPS)


def _heads(x):
    return x.reshape(*x.shape[:-1], HEADS, x.shape[-1] // HEADS)


def _project(h, w):
    z = h @ w
    out, off = {}, 0
    for name, size in IN_SPLITS:
        out[name] = z[..., off:off + size]
        off += size
    return out


def _rope_axis(x, pos):
    half = x.shape[-1] // 2
    inv = ROPE_BASE ** (-jnp.arange(half, dtype=jnp.float32) / half)
    ang = pos.astype(jnp.float32)[:, None] * inv
    cos, sin = jnp.cos(ang)[None, :, None, :], jnp.sin(ang)[None, :, None, :]
    x1, x2 = x[..., :half], x[..., half:]
    return jnp.concatenate([x1 * cos - x2 * sin, x1 * sin + x2 * cos], axis=-1)


def _rope_2d(x, rows, cols):
    a = x.shape[-1] // 2
    return jnp.concatenate([_rope_axis(x[..., :a], rows), _rope_axis(x[..., a:], cols)], axis=-1)


def _to_chunks(x):
    B, L, H, d = x.shape
    return x.reshape(B, L // CHUNK, CHUNK, H, d).transpose(1, 0, 3, 2, 4)


def _chunk_heads(x):
    B, L, H = x.shape
    return x.reshape(B, L // CHUNK, CHUNK, H).transpose(1, 0, 3, 2)


def _from_chunks(x):
    n, B, H, C, d = x.shape
    return x.transpose(1, 0, 3, 2, 4).reshape(B, n * C, H, d)


def _centred_conv(x, w):
    pad = GDN_CONV // 2
    L = x.shape[1]
    xp = jnp.pad(x, ((0, 0), (pad, pad), (0, 0)))
    return sum(xp[:, j:j + L] * w[j] for j in range(GDN_CONV))


def _softmax_attend(q, k, v):
    s = jnp.einsum('bqhd,bkhd->bhqk', q.astype(jnp.float32), k.astype(jnp.float32)) * HEAD_DIM ** -0.5
    return jnp.einsum('bhqk,bkhd->bqhd', jax.nn.softmax(s, axis=-1), v.astype(jnp.float32))


def _na_latent(q, k, v, ck, cv, rpb):
    f32 = jnp.float32
    B, N, H, d = q.shape
    rows = N // GRID_W
    kh = min(NA_WIN_H, rows)
    r = np.arange(rows)
    cidx = np.arange(GRID_W)
    row_idx = np.clip(r - kh // 2, 0, rows - kh)[:, None] + np.arange(kh)[None, :]
    col_start = np.clip(cidx - NA_WIN_W // 2, 0, GRID_W - NA_WIN_W)
    col_ok = (cidx[None, :] >= col_start[:, None]) & (cidx[None, :] < col_start[:, None] + NA_WIN_W)
    drow = row_idx - r[:, None] + NA_WIN_H - 1
    dcol = np.clip(cidx[None, :] - cidx[:, None], 1 - NA_WIN_W, NA_WIN_W - 1) + NA_WIN_W - 1
    bias = rpb.astype(f32)[:, drow][..., dcol].transpose(0, 1, 3, 2, 4)
    bias = jnp.where(col_ok[:, None, :], bias, NEG)
    scale = d ** -0.5
    qg = q.astype(f32).reshape(B, rows, GRID_W, H, d)
    kb = k.astype(f32).reshape(B, rows, GRID_W, H, d)[:, row_idx]
    vb = v.astype(f32).reshape(B, rows, GRID_W, H, d)[:, row_idx]
    s_loc = jnp.einsum('brqhd,brikhd->bhrqik', qg, kb) * scale + bias[None]
    s_ctx = jnp.einsum('brqhd,bchd->bhrqc', qg, ck.astype(f32)) * scale
    nloc = kh * GRID_W
    p = jax.nn.softmax(jnp.concatenate([s_loc.reshape(B, H, rows, GRID_W, nloc), s_ctx], axis=-1), axis=-1)
    p_loc = p[..., :nloc].reshape(B, H, rows, GRID_W, kh, GRID_W)
    o = (jnp.einsum('bhrqik,brikhd->brqhd', p_loc, vb)
         + jnp.einsum('bhrqc,bchd->brqhd', p[..., nloc:], cv.astype(f32)))
    return o.reshape(B, N, H, d)


def _diff_attend(q, k, v, lam):
    f32 = jnp.float32
    B, Q, H, _ = q.shape
    K = k.shape[1]
    qf = q.astype(f32).reshape(B, Q, H, 2, DIFF_D)
    kf = k.astype(f32).reshape(B, K, H, 2, DIFF_D)
    s = jnp.einsum('bqhmd,bkhmd->bhmqk', qf, kf) * DIFF_D ** -0.5
    p = jax.nn.softmax(s, axis=-1)
    a = p[:, :, 0] - lam * p[:, :, 1]
    return jnp.einsum('bhqk,bkhd->bqhd', a, v.astype(f32))


def _diff_latent(q, k, v, ck, cv, lam):
    f32 = jnp.float32
    B, N, H, dd = q.shape
    t = jnp.arange(N)
    rows, cols = t // GRID_W, t % GRID_W

    def rope(a):
        return _rope_2d(a.astype(f32).reshape(B, N, H * 2, DIFF_D), rows, cols).reshape(B, N, H, dd)

    kk = jnp.concatenate([rope(k), ck.astype(f32)], axis=1)
    vv = jnp.concatenate([v.astype(f32), cv.astype(f32)], axis=1)
    qb = rope(q).reshape(B, N // Q_BLOCK, Q_BLOCK, H, dd).transpose(1, 0, 2, 3, 4)
    o = lax.map(lambda qi: _diff_attend(qi, kk, vv, lam), qb)
    return o.transpose(1, 0, 2, 3, 4).reshape(B, N, H, dd)


def _gla_scan(q, k, v, g, s0):
    f32 = jnp.float32
    causal = jnp.tril(jnp.ones((CHUNK, CHUNK), dtype=bool))[:, :, None]

    def step(S, inp):
        qc, kc, vc, gc = inp
        b = jnp.cumsum(gc, axis=2)
        o = jnp.einsum('bhik,bhkv->bhiv', qc * jnp.exp(b), S)
        decay = jnp.exp(jnp.where(causal, b[:, :, :, None, :] - b[:, :, None, :, :], NEG))
        att = jnp.einsum('bhik,bhjk,bhijk->bhij', qc, kc, decay)
        o = o + jnp.einsum('bhij,bhjv->bhiv', att, vc)
        b_last = b[:, :, -1:, :]
        S = (jnp.exp(b_last)[:, :, 0, :, None] * S
             + jnp.einsum('bhjk,bhjv->bhkv', kc * jnp.exp(b_last - b), vc))
        return S, o

    xs = tuple(_to_chunks(a.astype(f32)) for a in (q, k, v, g))
    S, o = lax.scan(step, s0.astype(f32), xs)
    return _from_chunks(o), S


def _gdn_scan(q, k, v, beta, g, s0):
    f32 = jnp.float32
    qc, kc, vc = (_to_chunks(a.astype(f32)) for a in (q, k, v))
    bc, gc = (_chunk_heads(a.astype(f32)) for a in (beta, g))
    d = jnp.cumsum(gc, axis=-1)
    incl = jnp.tril(jnp.ones((CHUNK, CHUNK), dtype=bool))
    strict = jnp.tril(jnp.ones((CHUNK, CHUNK), dtype=bool), -1)
    decay = jnp.exp(jnp.where(incl, d[..., :, None] - d[..., None, :], NEG))
    m = jnp.where(strict, bc[..., :, None] * jnp.einsum('nbhik,nbhjk->nbhij', kc, kc) * decay, 0.0)
    rhs = jnp.concatenate([vc * bc[..., None], kc * (bc * jnp.exp(d))[..., None]], axis=-1)
    x = lax.linalg.triangular_solve(jnp.eye(CHUNK, dtype=f32) + m, rhs,
                                    left_side=True, lower=True, unit_diagonal=True)
    u, w = x[..., :v.shape[-1]], x[..., v.shape[-1]:]
    att = jnp.einsum('nbhik,nbhjk->nbhij', qc, kc) * decay

    def step(S, inp):
        qi, ki, ui, wi, di, ai = inp
        v_new = ui - jnp.einsum('bhik,bhkv->bhiv', wi, S)
        o = (jnp.einsum('bhik,bhkv->bhiv', qi * jnp.exp(di)[..., None], S)
             + jnp.einsum('bhij,bhjv->bhiv', ai, v_new))
        S = (jnp.exp(di[..., -1])[..., None, None] * S
             + jnp.einsum('bhjk,bhjv->bhkv', ki * jnp.exp(di[..., -1:] - di)[..., None], v_new))
        return S, o

    S, o = lax.scan(step, s0.astype(f32), (qc, kc, u, w, d, att))
    return _from_chunks(o), S


def _gla_mixer(z, gate_up, gate_b, norm_w, s0):
    f32 = jnp.float32
    q = _heads(z['gla_q'].astype(f32)) * HEAD_DIM ** -0.5
    k = _heads(z['gla_k'].astype(f32))
    v = _heads(z['gla_v'].astype(f32))
    outs, states = [], []
    for d, name in enumerate(('gla_gf', 'gla_gb')):
        g = _heads(jax.nn.log_sigmoid(z[name].astype(f32) @ gate_up[d].astype(f32)
                                      + gate_b[d].astype(f32)) / GLA_GATE_NORM)
        flip = (lambda a: a[:, ::-1]) if d else (lambda a: a)
        o, s = _gla_scan(flip(q), flip(k), flip(v), flip(g), s0[:, d])
        outs.append(flip(o))
        states.append(s)
    o = _rmsnorm(outs[0] + outs[1], norm_w) * jax.nn.silu(_heads(z['gla_og'].astype(f32)))
    B, L = o.shape[:2]
    return o.reshape(B, L, MIX_W), jnp.stack(states, axis=1)


def _gdn_mixer(z, conv_w, a_log, dt_bias, norm_w, s0):
    f32 = jnp.float32
    qkv = jax.nn.silu(_centred_conv(z['gdn_qkv'].astype(f32), conv_w.astype(f32)))
    q, k, v = (_heads(a) for a in jnp.split(qkv, 3, axis=-1))
    q = _l2norm(q) * HEAD_DIM ** -0.5
    k = _l2norm(k)
    outs, states = [], []
    for d, (an, bn) in enumerate((('gdn_af', 'gdn_bf'), ('gdn_ab', 'gdn_bb'))):
        g = -jnp.exp(a_log[d].astype(f32)) * jax.nn.softplus(z[an].astype(f32) + dt_bias[d].astype(f32))
        beta = jax.nn.sigmoid(z[bn].astype(f32))
        flip = (lambda a: a[:, ::-1]) if d else (lambda a: a)
        o, s = _gdn_scan(flip(q), flip(k), flip(v), flip(beta), flip(g), s0[:, d])
        outs.append(flip(o))
        states.append(s)
    o = _rmsnorm(outs[0] + outs[1], norm_w) * jax.nn.silu(_heads(z['gdn_og'].astype(f32)))
    B, L = o.shape[:2]
    return o.reshape(B, L, MIX_W), jnp.stack(states, axis=1)


def _moe(h, w_router, b_router, w1, w3, w2):
    f32 = jnp.float32
    B, L, D = h.shape
    t = h.reshape(B * L, D)
    s = jax.nn.sigmoid((t @ w_router).astype(f32))
    sel = s + b_router.astype(f32)
    gscore = lax.top_k(sel.reshape(-1, N_GROUPS, EXPERTS_PER_GROUP), TOP_K)[0].sum(-1)
    gmask = jnp.argmax(gscore, axis=-1)[:, None] == (jnp.arange(N_EXPERTS) // EXPERTS_PER_GROUP)[None, :]
    _, idx = lax.top_k(jnp.where(gmask, sel, NEG), TOP_K)
    wsel = jnp.take_along_axis(s, idx, axis=-1)
    wsel = wsel / jnp.sum(wsel, axis=-1, keepdims=True)
    gates = jnp.einsum('tk,tke->te', wsel, jax.nn.one_hot(idx, N_EXPERTS, dtype=f32))
    out = 0.0
    for e in range(N_EXPERTS):
        he = jax.nn.silu(t @ w1[e]) * (t @ w3[e])
        out = out + gates[:, e:e + 1] * (he @ w2[e])
    return out.reshape(B, L, D).astype(h.dtype)


def _layer(x, cond, P, l, cache):
    f32 = jnp.float32
    B, L, _ = x.shape
    mod = (jax.nn.silu(cond) @ P['w_ada'] + P['b_ada'])[:, None, :]
    sh1, sc1, g1, sh2, sc2, g2 = jnp.split(mod, 6, axis=-1)
    h = _rmsnorm(x, P['norm1']) * (1.0 + sc1) + sh1
    z = _project(h, P['w_in'])
    na_q, na_k, na_v = (_heads(z[n]) for n in ('na_q', 'na_k', 'na_v'))
    df_q, df_k, df_v = (_heads(z[n]) for n in ('diff_q', 'diff_k', 'diff_v'))
    lam_init = 0.8 - 0.6 * math.exp(-0.3 * l)
    lq1, lk1, lq2, lk2 = P['diff_lam'].astype(f32)
    lam = jnp.exp(jnp.sum(lq1 * lk1)) - jnp.exp(jnp.sum(lq2 * lk2)) + lam_init
    if cache is None:
        o_na = _softmax_attend(na_q, na_k, na_v)
        o_df = _diff_attend(df_q, df_k, df_v, lam)
        s_gla0 = jnp.zeros((B, 2, HEADS, HEAD_DIM, HEAD_DIM), f32)
        s_gdn0 = s_gla0
    else:
        c_na_k, c_na_v, c_df_k, c_df_v, s_gla0, s_gdn0 = cache
        o_na = _na_latent(na_q, na_k, na_v, c_na_k, c_na_v, P['na_rpb'])
        o_df = _diff_latent(df_q, df_k, df_v, c_df_k, c_df_v, lam)
    o_gla, s_gla = _gla_mixer(z, P['gla_up'], P['gla_ub'], P['gla_norm'], s_gla0)
    o_gdn, s_gdn = _gdn_mixer(z, P['gdn_conv'], P['gdn_A_log'], P['gdn_dt_bias'], P['gdn_norm'], s_gdn0)
    o_df = _rmsnorm(o_df, P['diff_subln']) * (1.0 - lam_init)
    branches = jnp.stack([o_na.reshape(B, L, MIX_W), o_gla, o_gdn, o_df.reshape(B, L, MIX_W)],
                         axis=2).astype(x.dtype)
    gates = jax.nn.sigmoid(z['branch_gate'].astype(f32)).reshape(B, L, N_BRANCH, D_MODEL)
    merged = jnp.sum(gates * jnp.einsum('blnc,ncd->blnd', branches, P['w_branch']), axis=2).astype(x.dtype)
    x = x + g1 * (merged @ P['w_out'])
    h2 = _rmsnorm(x, P['norm2']) * (1.0 + sc2) + sh2
    x = x + g2 * _moe(h2, P['w_router'], P['b_router'], P['w_e1'], P['w_e3'], P['w_e2'])
    if cache is None:
        return x, (na_k, na_v, df_k, df_v, s_gla, s_gdn)
    return x, None


def setup_inputs(seed: int = 0) -> dict:
    key = jax.random.key(seed)
    ks = iter(jax.random.split(key, 48))
    f32 = jnp.float32

    def nrm(shape, scale):
        return scale * jax.random.normal(next(ks), shape, f32)

    def gain(shape):
        return 1.0 + nrm(shape, 0.05)

    a_val = jax.random.uniform(next(ks), (DEPTH, 2, HEADS), f32, 1.0, 16.0)
    dt = jnp.exp(jax.random.uniform(next(ks), (DEPTH, 2, HEADS), f32, math.log(1e-3), math.log(1e-1)))
    kv_shape = (DEC_BATCH, DEPTH, PAST_LEN, HEADS, HEAD_DIM)
    st_shape = (DEC_BATCH, DEPTH, 2, HEADS, HEAD_DIM, HEAD_DIM)
    return {
        'x_prompt': nrm((BATCH, SEQ, D_MODEL), 1.0),
        'x_sample': nrm((DEC_BATCH, DEC_SEQ, D_MODEL), 1.0),
        'cache_na_k': nrm(kv_shape, 1.0),
        'cache_na_v': nrm(kv_shape, 1.0),
        'cache_diff_k': nrm(kv_shape, 1.0),
        'cache_diff_v': nrm(kv_shape, 1.0),
        'state_gla': nrm(st_shape, 2.0),
        'state_gdn': nrm(st_shape, 0.3),
        'c': nrm((DEC_BATCH, D_MODEL), 1.0),
        'c_ctx': nrm((D_MODEL,), 1.0),
        'w_ada': nrm((DEPTH, D_MODEL, 6 * D_MODEL), 0.5 * D_MODEL ** -0.5),
        'b_ada': nrm((DEPTH, 6 * D_MODEL), 0.02),
        'norm1_w': gain((DEPTH, D_MODEL)),
        'norm2_w': gain((DEPTH, D_MODEL)),
        'w_in': nrm((DEPTH, D_MODEL, IN_TOTAL), D_MODEL ** -0.5),
        'na_rpb': nrm((DEPTH, HEADS, 2 * NA_WIN_H - 1, 2 * NA_WIN_W - 1), 0.1),
        'gla_gate_up': nrm((DEPTH, 2, GLA_GATE_RANK, MIX_W), GLA_GATE_RANK ** -0.5),
        'gla_gate_bias': nrm((DEPTH, 2, MIX_W), 0.1),
        'gla_norm_w': gain((DEPTH, HEAD_DIM)),
        'gdn_conv_w': nrm((DEPTH, GDN_CONV, 3 * MIX_W), GDN_CONV ** -0.5),
        'gdn_A_log': jnp.log(a_val),
        'gdn_dt_bias': dt + jnp.log(-jnp.expm1(-dt)),
        'gdn_norm_w': gain((DEPTH, HEAD_DIM)),
        'diff_lambda': nrm((DEPTH, 4, DIFF_D), 0.1),
        'diff_subln_w': gain((DEPTH, HEAD_DIM)),
        'w_branch': nrm((DEPTH, N_BRANCH, MIX_W, D_MODEL), MIX_W ** -0.5),
        'w_out': nrm((DEPTH, D_MODEL, D_MODEL), D_MODEL ** -0.5),
        'w_router': nrm((D_MODEL, N_EXPERTS), D_MODEL ** -0.5),
        'b_router': nrm((N_EXPERTS,), 0.01),
        'w_e1': nrm((DEPTH, N_EXPERTS, D_MODEL, D_FF_EXPERT), D_MODEL ** -0.5),
        'w_e3': nrm((DEPTH, N_EXPERTS, D_MODEL, D_FF_EXPERT), D_MODEL ** -0.5),
        'w_e2': nrm((DEPTH, N_EXPERTS, D_FF_EXPERT, D_MODEL), D_FF_EXPERT ** -0.5),
        'final_norm_w': gain((D_MODEL,)),
    }


def reference(x_prompt, x_sample, cache_na_k, cache_na_v, cache_diff_k, cache_diff_v, state_gla, state_gdn,
              c, c_ctx, w_ada, b_ada, norm1_w, norm2_w, w_in, na_rpb, gla_gate_up, gla_gate_bias, gla_norm_w,
              gdn_conv_w, gdn_A_log, gdn_dt_bias, gdn_norm_w, diff_lambda, diff_subln_w, w_branch, w_out,
              w_router, b_router, w_e1, w_e3, w_e2, final_norm_w):
    xp, xs = x_prompt, x_sample
    ctx_cond = c_ctx[None, :]
    new = []
    for l in range(DEPTH):
        P = {
            'w_ada': w_ada[l], 'b_ada': b_ada[l], 'norm1': norm1_w[l], 'norm2': norm2_w[l],
            'w_in': w_in[l], 'na_rpb': na_rpb[l],
            'gla_up': gla_gate_up[l], 'gla_ub': gla_gate_bias[l], 'gla_norm': gla_norm_w[l],
            'gdn_conv': gdn_conv_w[l], 'gdn_A_log': gdn_A_log[l], 'gdn_dt_bias': gdn_dt_bias[l],
            'gdn_norm': gdn_norm_w[l], 'diff_lam': diff_lambda[l], 'diff_subln': diff_subln_w[l],
            'w_branch': w_branch[l], 'w_out': w_out[l],
            'w_router': w_router, 'b_router': b_router,
            'w_e1': w_e1[l], 'w_e3': w_e3[l], 'w_e2': w_e2[l],
        }
        xp, ctx_tensors = _layer(xp, ctx_cond, P, l, None)
        new.append(ctx_tensors)
        cache_l = (cache_na_k[:, l], cache_na_v[:, l], cache_diff_k[:, l], cache_diff_v[:, l],
                   state_gla[:, l], state_gdn[:, l])
        xs, _ = _layer(xs, c, P, l, cache_l)
    y_prompt = _rmsnorm(xp, final_norm_w)
    y_sample = _rmsnorm(xs, final_norm_w)
    new_na_k = jnp.stack([t[0] for t in new], axis=1)
    new_na_v = jnp.stack([t[1] for t in new], axis=1)
    new_diff_k = jnp.stack([t[2] for t in new], axis=1)
    new_diff_v = jnp.stack([t[3] for t in new], axis=1)
    new_state_gla = jnp.stack([t[4] for t in new], axis=1)
    new_state_gdn = jnp.stack([t[5] for t in new], axis=1)
    return (y_prompt, y_sample, new_na_k, new_na_v, new_diff_k, new_diff_v, new_state_gla, new_state_gdn)
```

```python
import functools
import math

import numpy as np
import jax
import jax.numpy as jnp
from jax import lax
from jax.experimental import pallas as pl
from jax.experimental.pallas import tpu as pltpu

F32 = jnp.float32
BF16 = jnp.bfloat16

D_MODEL = 1024
HEADS = 4
HEAD_DIM = 64
MIX_W = HEADS * HEAD_DIM
GRID_W = 64
NA_WIN_H = 8
NA_WIN_W = 16
GATE_RANK = 16
GATE_NORM = 16.0
GDN_CONV = 3
CHUNK = 64
N_EXPERTS = 16
N_GROUPS = 4
GROUP_E = N_EXPERTS // N_GROUPS
D_FF = D_MODEL // 4
ROPE_BASE = 10000.0
EPS = 1e-6
NEG = -1e30
SMALL_W = 128
V7X_VMEM_LIMIT = 56 * 1024 * 1024
LEAF = 8

_IN_SPLITS = (
    ('na_q', MIX_W), ('na_k', MIX_W), ('na_v', MIX_W),
    ('gla_q', MIX_W), ('gla_k', MIX_W), ('gla_v', MIX_W),
    ('gla_gf', GATE_RANK), ('gla_gb', GATE_RANK), ('gla_og', MIX_W),
    ('gdn_qkv', 3 * MIX_W), ('gdn_af', HEADS), ('gdn_ab', HEADS),
    ('gdn_bf', HEADS), ('gdn_bb', HEADS), ('gdn_og', MIX_W),
    ('diff_q', MIX_W), ('diff_k', MIX_W), ('diff_v', MIX_W),
    ('branch_gate', 4 * D_MODEL),
)
_OFF = {}
_o = 0
for _n, _s in _IN_SPLITS:
    _OFF[_n] = (_o, _o + _s)
    _o += _s

_N_LVL = 6
_ROW_B = 2 * _N_LVL * CHUNK
_ROW_BL = _ROW_B + CHUNK
_ROW_TOT = _ROW_BL + CHUNK
_L_ROWS = _ROW_TOT + 16


def _params(sem):
    return pltpu.CompilerParams(dimension_semantics=sem, vmem_limit_bytes=V7X_VMEM_LIMIT)


def _dot(a, b):
    return jnp.dot(a, b, preferred_element_type=F32)


def _bdot(a, b):
    return _dot(a.astype(BF16), b.astype(BF16))


def _bdot_nt(a, b):
    return lax.dot_general(a.astype(BF16), b.astype(BF16), (((1,), (1,)), ((), ())),
                           preferred_element_type=F32)


def _bdot_tn(a, b):
    return lax.dot_general(a.astype(BF16), b.astype(BF16), (((0,), (0,)), ((), ())),
                           preferred_element_type=F32)


def _split(x):
    hi = x.astype(BF16)
    lo = (x - hi.astype(F32)).astype(BF16)
    return hi, lo


def _dot_rx(x, m):
    hi, lo = _split(x)
    return _dot(hi, m) + _dot(lo, m)


def _dot_lx(m, x):
    hi, lo = _split(x)
    return _dot(m, hi) + _dot(m, lo)


def _dot3(a, b):
    ah, al = _split(a)
    bh, bl = _split(b)
    return _dot(ah, bh) + _dot(ah, bl) + _dot(al, bh)


def _silu(x):
    return x * jax.nn.sigmoid(x)


def _softplus(x):
    return jnp.maximum(x, 0.0) + jnp.log(1.0 + jnp.exp(-jnp.abs(x)))


def _rms_rows(x, w):
    return x * lax.rsqrt(jnp.mean(x * x, axis=-1, keepdims=True) + EPS) * w


def _head_sum(x, bm16):
    return _dot_rx(x, bm16)


def _bd_rows(x, bm):
    return jnp.concatenate([x] * HEADS, axis=0) * bm


def _lane_mask(lo, width, n=MIX_W):
    lane = lax.broadcasted_iota(jnp.int32, (1, n), 1)
    return (lane >= lo) & (lane < lo + width)


def _chunk_consts():
    c = CHUNK
    i = np.arange(c)[:, None]
    t = np.arange(c)[None, :]
    blocks, masks = [], []
    for lvl in range(_N_LVL):
        s = 1 << lvl
        upper = ((i // s) % 2) == 1
        lq = upper & (t > (i // s) * s) & (t <= i)
        lk = (~upper) & (t > i) & (t <= ((i // s) + 1) * s)
        blocks += [lq, lk]
        masks.append(((i // (2 * s)) == (t // (2 * s))) & upper & (((t // s) % 2) == 0))
    masks.append(i == t)
    blocks += [t <= i, t > i]
    fwd_l = np.concatenate([b.astype(np.float32) for b in blocks], 0)
    rev_l = np.concatenate([b[::-1, ::-1].astype(np.float32) for b in blocks], 0)
    tot = np.ones((16, c), np.float32)
    l_all = np.stack([np.concatenate([fwd_l, tot], 0), np.concatenate([rev_l, tot], 0)])
    m_fwd = np.stack([np.tile(m.astype(np.float32), (1, HEADS)) for m in masks])
    m_rev = np.stack([np.tile(m[::-1, ::-1].astype(np.float32), (1, HEADS)) for m in masks])
    assert l_all.shape[1] == _L_ROWS
    return jnp.asarray(l_all, BF16), jnp.asarray(np.stack([m_fwd, m_rev]), F32)


def _block_mask():
    r = np.arange(MIX_W)
    return (r[:, None] // HEAD_DIM == r[None, :] // HEAD_DIM).astype(np.float32)


def _solve_masks():
    r = np.arange(MIX_W)[:, None]
    c = np.arange(MIX_W)[None, :]
    out = [(r // LEAF == c // LEAF)]
    s = LEAF
    while s < CHUNK:
        out.append((r // (2 * s) == c // (2 * s)) & (r // s != c // s))
        s *= 2
    return jnp.asarray(np.stack(out).astype(np.float32))


def _rope_consts(n_tok):
    t = np.arange(n_tok)
    pos = np.stack([t // GRID_W, t % GRID_W], 0).astype(np.float32)
    lane = np.arange(MIX_W)
    u = lane % 32
    axis = u // 16
    w = u % 16
    first = w < 8
    inv = ROPE_BASE ** (-(w % 8).astype(np.float32) / 8.0)
    ang = pos[axis, :].T * inv[None, :]
    cos = np.cos(ang)
    sin = np.sin(ang) * np.where(first, -1.0, 1.0)[None, :]
    partner = np.where(first, lane + 8, lane - 8)
    perm = np.zeros((MIX_W, MIX_W), np.float32)
    perm[partner, lane] = 1.0
    return jnp.asarray(cos, F32), jnp.asarray(sin, F32), jnp.asarray(perm, BF16)


def _ada_kernel(c_ref, w_ref, b_ref, o_ref):
    c = c_ref[...]
    o_ref[0] = _bdot(_silu(c), w_ref[0]) + b_ref[0]


def _ada(cond, w_ada16, b_ada):
    depth, _, n = w_ada16.shape
    tn = 1536
    return pl.pallas_call(
        _ada_kernel,
        grid=(depth, n // tn),
        in_specs=[pl.BlockSpec((8, D_MODEL), lambda l, j: (0, 0)),
                  pl.BlockSpec((1, D_MODEL, tn), lambda l, j: (l, 0, j)),
                  pl.BlockSpec((1, 1, tn), lambda l, j: (l, 0, j))],
        out_specs=pl.BlockSpec((1, 8, tn), lambda l, j: (l, 0, j)),
        out_shape=jax.ShapeDtypeStruct((depth, 8, n), F32),
        compiler_params=_params(("parallel", "parallel")),
        name="ada_mod",
    )(cond, w_ada16, b_ada.reshape(depth, 1, n))


def _proj_kernel(x_ref, mod_ref, nw_ref, w_ref, o_na, o_gla, o_gdn, o_diff, o_small):
    mod = mod_ref[0]
    h = _rms_rows(x_ref[...], nw_ref[...]) * (1.0 + mod[1:2]) + mod[0:1]
    hb = h.astype(BF16)
    o_na[...] = _dot(hb, w_ref[:, 0:768])
    o_gla[...] = _dot(hb, w_ref[:, 768:1792])
    o_gdn[...] = _dot(hb, w_ref[:, 1792:2816])
    o_diff[...] = _dot(hb, w_ref[:, 2816:3584])
    o_small[...] = _dot(hb, w_ref[:, 3584:3712])


def _proj(x, mod_l, nw, wcat, tm, mod_row):
    t = x.shape[0]
    widths = (768, 1024, 1024, 768, SMALL_W)
    return pl.pallas_call(
        _proj_kernel,
        grid=(t // tm,),
        in_specs=[pl.BlockSpec((tm, D_MODEL), lambda i: (i, 0)),
                  pl.BlockSpec((1, 6, D_MODEL), lambda i: (mod_row(i), 0, 0)),
                  pl.BlockSpec((1, D_MODEL), lambda i: (0, 0)),
                  pl.BlockSpec(wcat.shape, lambda i: (0, 0))],
        out_specs=[pl.BlockSpec((tm, w), lambda i: (i, 0)) for w in widths],
        out_shape=[jax.ShapeDtypeStruct((t, w), F32) for w in widths],
        compiler_params=_params(("parallel",)),
        name="in_proj",
    )(x, mod_l, nw, wcat)


def _attn_core(q, k, v, n_sub, lam):
    kb = k.astype(BF16)
    vb = v.astype(BF16)
    sub_w = HEAD_DIM // n_sub
    scale = sub_w ** -0.5
    out = jnp.zeros(q.shape, F32)
    for h in range(HEADS):
        probs = []
        for m in range(n_sub):
            msk = jnp.where(_lane_mask(h * HEAD_DIM + m * sub_w, sub_w), scale, 0.0)
            s = _bdot_nt(q * msk, kb)
            e = jnp.exp(s - jnp.max(s, axis=-1, keepdims=True))
            probs.append(e * (1.0 / jnp.sum(e, axis=-1, keepdims=True)))
        a = probs[0] if n_sub == 1 else probs[0] - lam * probs[1]
        out = jnp.where(_lane_mask(h * HEAD_DIM, HEAD_DIM), _dot(a.astype(BF16), vb), out)
    return out


def _diff_lambda(dl, lam_init):
    a = jnp.sum(dl[0:1] * dl[1:2], axis=-1, keepdims=True)
    b = jnp.sum(dl[2:3] * dl[3:4], axis=-1, keepdims=True)
    return jnp.exp(a) - jnp.exp(b) + lam_init


def _attn_kernel(q_ref, k_ref, v_ref, dl_ref, o_ref, *, n_sub, lam_init):
    lam = _diff_lambda(dl_ref[...], lam_init) if n_sub == 2 else None
    o_ref[...] = _attn_core(q_ref[...], k_ref[...], v_ref[...], n_sub, lam)


def _attention(q_arr, q_col, k_arr, k_col, v_arr, v_col, dl, *, n_sub, lam_init, n_seq, lq, lk,
               q_base, k_base, tq, name):
    nq = lq // tq
    return pl.pallas_call(
        functools.partial(_attn_kernel, n_sub=n_sub, lam_init=lam_init),
        grid=(n_seq, nq),
        in_specs=[pl.BlockSpec((tq, MIX_W), lambda s, j: (q_base // tq + s * nq + j, q_col)),
                  pl.BlockSpec((lk, MIX_W), lambda s, j: (k_base // lk + s, k_col)),
                  pl.BlockSpec((lk, MIX_W), lambda s, j: (k_base // lk + s, v_col)),
                  pl.BlockSpec(dl.shape, lambda s, j: (0, 0))],
        out_specs=pl.BlockSpec((tq, MIX_W), lambda s, j: (s * nq + j, 0)),
        out_shape=jax.ShapeDtypeStruct((n_seq * lq, MIX_W), F32),
        compiler_params=_params(("parallel", "parallel")),
        name=name,
    )(q_arr, k_arr, v_arr, dl)


def _na_lat_kernel(q_ref, k_ref, v_ref, ck_ref, cv_ref, bias_ref, o_ref, *, rows, kh):
    r = pl.program_id(1)
    start = jnp.clip(r - kh // 2, 0, rows - kh)
    cls = r - start
    ws = pl.multiple_of(start * GRID_W, GRID_W)
    kw = k_ref[pl.ds(ws, kh * GRID_W), :].astype(BF16)
    vw = v_ref[pl.ds(ws, kh * GRID_W), :].astype(BF16)
    ck = ck_ref[...].astype(BF16)
    cv = cv_ref[...].astype(BF16)
    q = q_ref[...]
    scale = HEAD_DIM ** -0.5
    out = jnp.zeros(q.shape, F32)
    for h in range(HEADS):
        msk = jnp.where(_lane_mask(h * HEAD_DIM, HEAD_DIM), scale, 0.0)
        qm = (q * msk).astype(BF16)
        s_loc = _bdot_nt(qm, kw) + bias_ref[h, pl.ds(cls, 1)][0]
        s_ctx = _bdot_nt(qm, ck)
        mx = jnp.maximum(jnp.max(s_loc, axis=-1, keepdims=True), jnp.max(s_ctx, axis=-1, keepdims=True))
        e_loc = jnp.exp(s_loc - mx)
        e_ctx = jnp.exp(s_ctx - mx)
        inv = 1.0 / (jnp.sum(e_loc, axis=-1, keepdims=True) + jnp.sum(e_ctx, axis=-1, keepdims=True))
        oh = _dot((e_loc * inv).astype(BF16), vw) + _dot((e_ctx * inv).astype(BF16), cv)
        out = jnp.where(_lane_mask(h * HEAD_DIM, HEAD_DIM), oh, out)
    o_ref[...] = out


def _na_bias_table(rpb, kh):
    cidx = np.arange(GRID_W)
    cls = np.arange(kh)
    drow = np.arange(kh)[None, :] - cls[:, None] + NA_WIN_H - 1
    col_start = np.clip(cidx - NA_WIN_W // 2, 0, GRID_W - NA_WIN_W)
    col_ok = (cidx[None, :] >= col_start[:, None]) & (cidx[None, :] < col_start[:, None] + NA_WIN_W)
    dcol = np.clip(cidx[None, :] - cidx[:, None], 1 - NA_WIN_W, NA_WIN_W - 1) + NA_WIN_W - 1
    bias = rpb.astype(F32)[:, drow][..., dcol].transpose(0, 1, 3, 2, 4)
    bias = jnp.where(col_ok[:, None, :], bias, NEG)
    return bias.reshape(HEADS, kh, GRID_W, kh * GRID_W)


def _na_latent(z_na, ck, cv, bias, n_b, n_tok, p_len):
    rows = n_tok // GRID_W
    kh = min(NA_WIN_H, rows)
    return pl.pallas_call(
        functools.partial(_na_lat_kernel, rows=rows, kh=kh),
        grid=(n_b, rows),
        in_specs=[pl.BlockSpec((GRID_W, MIX_W), lambda b, r: (b * rows + r, 0)),
                  pl.BlockSpec((n_tok, MIX_W), lambda b, r: (b, 1)),
                  pl.BlockSpec((n_tok, MIX_W), lambda b, r: (b, 2)),
                  pl.BlockSpec((p_len, MIX_W), lambda b, r: (b, 0)),
                  pl.BlockSpec((p_len, MIX_W), lambda b, r: (b, 0)),
                  pl.BlockSpec(bias.shape, lambda b, r: (0, 0, 0, 0))],
        out_specs=pl.BlockSpec((GRID_W, MIX_W), lambda b, r: (b * rows + r, 0)),
        out_shape=jax.ShapeDtypeStruct((n_b * n_tok, MIX_W), F32),
        compiler_params=_params(("parallel", "arbitrary")),
        name="na_latent",
    )(z_na, z_na, z_na, ck, cv, bias)


def _rope_kernel(q_ref, k_ref, cos_ref, sin_ref, p_ref, qo_ref, ko_ref):
    cos = cos_ref[...]
    sin = sin_ref[...]
    p = p_ref[...]
    q = q_ref[...]
    k = k_ref[...]
    qo_ref[...] = q * cos + _dot_rx(q, p) * sin
    ko_ref[...] = k * cos + _dot_rx(k, p) * sin


def _rope(z_diff, cos, sin, perm, n_b, n_tok, tm):
    nt = n_tok // tm
    spec_t = pl.BlockSpec((tm, MIX_W), lambda i: (i % nt, 0))
    return pl.pallas_call(
        _rope_kernel,
        grid=(n_b * nt,),
        in_specs=[pl.BlockSpec((tm, MIX_W), lambda i: (i, 0)),
                  pl.BlockSpec((tm, MIX_W), lambda i: (i, 1)),
                  spec_t, spec_t,
                  pl.BlockSpec(perm.shape, lambda i: (0, 0))],
        out_specs=[pl.BlockSpec((tm, MIX_W), lambda i: (i, 0))] * 2,
        out_shape=[jax.ShapeDtypeStruct((n_b * n_tok, MIX_W), F32)] * 2,
        compiler_params=_params(("parallel",)),
        name="diff_rope",
    )(z_diff, z_diff, cos, sin, perm)


def _gdn_pre_kernel(x_ref, prev_ref, next_ref, edge_ref, w_ref, bm_ref, o_ref, *, tm):
    x = x_ref[...]
    w = w_ref[...]
    edge = edge_ref[...]
    row = lax.broadcasted_iota(jnp.int32, (tm, 1), 0)
    x_prev = jnp.where(row == 0, prev_ref[7:8, :], pltpu.roll(x, 1, 0)) * edge[:, 0:1]
    x_next = jnp.where(row == tm - 1, next_ref[0:1, :], pltpu.roll(x, tm - 1, 0)) * edge[:, 1:2]
    y = _silu(x_prev * w[0:1] + x * w[1:2] + x_next * w[2:3])
    bm16 = bm_ref[...]
    q = y[:, 0:MIX_W]
    k = y[:, MIX_W:2 * MIX_W]
    o_ref[:, 0:MIX_W] = q * lax.rsqrt(_head_sum(q * q, bm16) + EPS) * (HEAD_DIM ** -0.5)
    o_ref[:, MIX_W:2 * MIX_W] = k * lax.rsqrt(_head_sum(k * k, bm16) + EPS)
    o_ref[:, 2 * MIX_W:3 * MIX_W] = y[:, 2 * MIX_W:3 * MIX_W]


def _seq_edges(lat_tokens, n_tok, ctx_tokens, l_ctx):
    pos = np.concatenate([np.arange(lat_tokens) % n_tok, np.arange(ctx_tokens) % l_ctx])
    seq = np.concatenate([np.full(lat_tokens, n_tok), np.full(ctx_tokens, l_ctx)])
    edge = np.zeros((lat_tokens + ctx_tokens, 8), np.float32)
    edge[:, 0] = pos != 0
    edge[:, 1] = pos != seq - 1
    return jnp.asarray(edge)


def _gdn_pre(z_gdn, edge, conv_w, bm16, tm):
    t = z_gdn.shape[0]
    w3 = 3 * MIX_W
    nb8 = t // 8
    return pl.pallas_call(
        functools.partial(_gdn_pre_kernel, tm=tm),
        grid=(t // tm,),
        in_specs=[pl.BlockSpec((tm, w3), lambda i: (i, 0)),
                  pl.BlockSpec((8, w3), lambda i: (jnp.maximum(i * (tm // 8) - 1, 0), 0)),
                  pl.BlockSpec((8, w3), lambda i: (jnp.minimum((i + 1) * (tm // 8), nb8 - 1), 0)),
                  pl.BlockSpec((tm, 8), lambda i: (i, 0)),
                  pl.BlockSpec(conv_w.shape, lambda i: (0, 0)),
                  pl.BlockSpec(bm16.shape, lambda i: (0, 0))],
        out_specs=pl.BlockSpec((tm, w3), lambda i: (i, 0)),
        out_shape=jax.ShapeDtypeStruct((t, w3), F32),
        compiler_params=_params(("parallel",)),
        name="gdn_pre",
    )(z_gdn, z_gdn, z_gdn, edge, conv_w, bm16)


def _decay_factors(l_ref, g):
    return jnp.exp(_dot_lx(l_ref[0], g))


def _hier_scores(q_sides, diag, k, f, m_ref, bm):
    outs = [None] * len(q_sides)
    for lvl in range(_N_LVL):
        fq = f[2 * lvl * CHUNK:(2 * lvl + 1) * CHUNK]
        fk = f[(2 * lvl + 1) * CHUNK:(2 * lvl + 2) * CHUNK]
        kbd = _bd_rows(k * fk, bm).astype(BF16)
        m = m_ref[0, lvl]
        for n, qs in enumerate(q_sides):
            a = _bdot_nt(qs * fq, kbd) * m
            outs[n] = a if outs[n] is None else outs[n] + a
    if any(diag):
        kbd = _bd_rows(k, bm).astype(BF16)
        for n, qs in enumerate(q_sides):
            if diag[n]:
                outs[n] = outs[n] + _bdot_nt(qs, kbd) * m_ref[0, _N_LVL]
    return outs


def _gla_kernel(q_ref, k_ref, v_ref, zs_ref, gup_ref, gb_ref, l_ref, m_ref, bm_ref, s0_ref,
                o_ref, st_ref, st_scr, *, n_chunks):
    n = pl.program_id(2)

    @pl.when(n == 0)
    def _():
        st_scr[...] = s0_ref[0, 0]

    bm = bm_ref[...]
    q = q_ref[...] * (HEAD_DIM ** -0.5)
    k = k_ref[...]
    v = v_ref[...]
    gp = _bdot(zs_ref[...], gup_ref[0]) + gb_ref[0]
    g = (jnp.minimum(gp, 0.0) - jnp.log(1.0 + jnp.exp(-jnp.abs(gp)))) * (1.0 / GATE_NORM)
    f = _decay_factors(l_ref, g)
    (att,) = _hier_scores([q], [True], k, f, m_ref, bm)
    st = st_scr[...]
    fb = f[_ROW_B:_ROW_B + CHUNK]
    fbl = f[_ROW_BL:_ROW_BL + CHUNK]
    ftot = f[_ROW_TOT:_ROW_TOT + 1]
    o_ref[0] = _bdot_nt(q * fb, st) + _bdot(att, _bd_rows(v, bm))
    st_new = st * ftot + _bdot_tn(v, k * fbl) * bm
    st_scr[...] = st_new

    @pl.when(n == n_chunks - 1)
    def _():
        st_ref[0, 0] = st_new


def _tri_inverse(mbd, sm_ref):
    r = lax.broadcasted_iota(jnp.int32, (MIX_W, MIX_W), 0)
    c = lax.broadcasted_iota(jnp.int32, (MIX_W, MIX_W), 1)
    eye = jnp.where(r == c, 1.0, 0.0)
    md = mbd * sm_ref[0]
    m2 = _dot3(md, md)
    m4 = _dot3(m2, m2)
    t = eye - md
    t = t + _dot3(t, m2)
    t = t + _dot3(t, m4)
    for lvl in range(1, sm_ref.shape[0]):
        t = t - _dot3(_dot3(t, mbd * sm_ref[lvl]), t)
    return t


def _gdn_kernel(q_ref, k_ref, v_ref, zs_ref, ea_ref, eb_ref, alog_ref, dtb_ref, l_ref, m_ref, bm_ref,
                sm_ref, s0_ref, o_ref, sf_ref, s_scr, *, n_chunks):
    n = pl.program_id(2)

    @pl.when(n == 0)
    def _():
        s_scr[...] = s0_ref[0, 0]

    bm = bm_ref[...]
    q = q_ref[...]
    k = k_ref[...]
    v = v_ref[...]
    zs = zs_ref[...]
    g = -jnp.exp(alog_ref[0]) * _softplus(_dot_rx(zs, ea_ref[0]) + dtb_ref[0])
    beta = jax.nn.sigmoid(_dot_rx(zs, eb_ref[0]))
    f = _decay_factors(l_ref, g)
    kb = k * beta
    m_cat, att = _hier_scores([kb, q], [False, True], k, f, m_ref, bm)
    t_inv = _tri_inverse(_bd_rows(m_cat, bm), sm_ref)
    fb = f[_ROW_B:_ROW_B + CHUNK]
    fbl = f[_ROW_BL:_ROW_BL + CHUNK]
    ftot = f[_ROW_TOT:_ROW_TOT + 1]
    u_bd = _dot3(t_inv, _bd_rows(v * beta, bm))
    w_bd = _dot3(t_inv, _bd_rows(kb * fb, bm))
    s = s_scr[...]
    v_new = u_bd - _bdot(w_bd, s)
    o_ref[0] = _bdot(q * fb, s) + _bdot(att, v_new)
    s_new = s * ftot + _bdot_tn(_bd_rows(k * fbl, bm), v_new)
    s_scr[...] = s_new

    @pl.when(n == n_chunks - 1)
    def _():
        sf_ref[0, 0] = s_new


def _scan_specs(n_seq, l_seq, base, col0):
    nc = l_seq // CHUNK

    def tok(col):
        return pl.BlockSpec((CHUNK, MIX_W),
                            lambda b, d, n: (base // CHUNK + b * nc + n + d * (nc - 1 - 2 * n), col))

    return nc, [tok(col0), tok(col0 + 1), tok(col0 + 2)], \
        pl.BlockSpec((CHUNK, SMALL_W), lambda b, d, n: (base // CHUNK + b * nc + n + d * (nc - 1 - 2 * n), 0))


def _dir_spec(arr):
    nd = arr.ndim
    return pl.BlockSpec((1,) + arr.shape[1:], lambda b, d, n: (d,) + (0,) * (nd - 1))


def _gla(z_gla, z_small, gup, gbias, l_all, m_all, bm, s0, n_seq, l_seq, base):
    nc, qkv_specs, zs_spec = _scan_specs(n_seq, l_seq, base, 0)
    state_spec = pl.BlockSpec((1, 1, MIX_W, MIX_W), lambda b, d, n: (b, d, 0, 0))
    return pl.pallas_call(
        functools.partial(_gla_kernel, n_chunks=nc),
        grid=(n_seq, 2, nc),
        in_specs=qkv_specs + [zs_spec, _dir_spec(gup), _dir_spec(gbias), _dir_spec(l_all), _dir_spec(m_all),
                              pl.BlockSpec(bm.shape, lambda b, d, n: (0, 0)), state_spec],
        out_specs=[pl.BlockSpec((1, CHUNK, MIX_W), lambda b, d, n: (d, b * nc + n + d * (nc - 1 - 2 * n), 0)),
                   state_spec],
        out_shape=[jax.ShapeDtypeStruct((2, n_seq * l_seq, MIX_W), F32),
                   jax.ShapeDtypeStruct((n_seq, 2, MIX_W, MIX_W), F32)],
        scratch_shapes=[pltpu.VMEM((MIX_W, MIX_W), F32)],
        compiler_params=_params(("parallel", "parallel", "arbitrary")),
        name="gla_scan",
    )(z_gla, z_gla, z_gla, z_small, gup, gbias, l_all, m_all, bm, s0)


def _gdn(qkv_n, z_small, ea, eb, alog, dtb, l_all, m_all, bm, sm, s0, n_seq, l_seq, base):
    nc, qkv_specs, zs_spec = _scan_specs(n_seq, l_seq, base, 0)
    state_spec = pl.BlockSpec((1, 1, MIX_W, MIX_W), lambda b, d, n: (b, d, 0, 0))
    return pl.pallas_call(
        functools.partial(_gdn_kernel, n_chunks=nc),
        grid=(n_seq, 2, nc),
        in_specs=qkv_specs + [zs_spec, _dir_spec(ea), _dir_spec(eb), _dir_spec(alog), _dir_spec(dtb),
                              _dir_spec(l_all), _dir_spec(m_all),
                              pl.BlockSpec(bm.shape, lambda b, d, n: (0, 0)),
                              pl.BlockSpec(sm.shape, lambda b, d, n: (0, 0, 0)), state_spec],
        out_specs=[pl.BlockSpec((1, CHUNK, MIX_W), lambda b, d, n: (d, b * nc + n + d * (nc - 1 - 2 * n), 0)),
                   state_spec],
        out_shape=[jax.ShapeDtypeStruct((2, n_seq * l_seq, MIX_W), F32),
                   jax.ShapeDtypeStruct((n_seq, 2, MIX_W, MIX_W), F32)],
        scratch_shapes=[pltpu.VMEM((MIX_W, MIX_W), F32)],
        compiler_params=_params(("parallel", "parallel", "arbitrary")),
        name="gdn_scan",
    )(qkv_n, qkv_n, qkv_n, z_small, ea, eb, alog, dtb, l_all, m_all, bm, sm, s0)


def _merge_kernel(x_ref, mod_ref, n1_ref, na_ref, glaf_ref, glab_ref, glag_ref, gdnf_ref, gdnb_ref, gdng_ref,
                  df_ref, hn_ref, bm_ref, wg_ref, wb_ref, wo_ref, o_ref, *, lam_init):
    mod = mod_ref[0]
    x = x_ref[...]
    h = _rms_rows(x, n1_ref[...]) * (1.0 + mod[1:2]) + mod[0:1]
    hb = h.astype(BF16)
    bm16 = bm_ref[...]

    def head_norm(o, w):
        return o * lax.rsqrt(_head_sum(o * o, bm16) * (1.0 / HEAD_DIM) + EPS) * w

    branches = (
        na_ref[...],
        head_norm(glaf_ref[0] + glab_ref[0], hn_ref[0]) * _silu(glag_ref[...]),
        head_norm(gdnf_ref[0] + gdnb_ref[0], hn_ref[1]) * _silu(gdng_ref[...]),
        head_norm(df_ref[...], hn_ref[2]) * (1.0 - lam_init),
    )
    merged = None
    for n, br in enumerate(branches):
        gate = jax.nn.sigmoid(_dot(hb, wg_ref[:, n * D_MODEL:(n + 1) * D_MODEL]))
        term = gate * _bdot(br, wb_ref[n])
        merged = term if merged is None else merged + term
    o_ref[...] = x + mod[2:3] * _bdot(merged, wo_ref[...])


def _merge(x, mod_l, n1, o_na, o_gla, z_gla, o_gdn, z_gdn, o_df, hn, bm16, wg, wb, wo, tm, mod_row, lam_init):
    t = x.shape[0]
    tok = lambda col: pl.BlockSpec((tm, MIX_W), lambda i: (i, col))
    dirs = lambda d: pl.BlockSpec((1, tm, MIX_W), lambda i: (d, i, 0))
    full = lambda a: pl.BlockSpec(a.shape, lambda i: (0,) * a.ndim)
    return pl.pallas_call(
        functools.partial(_merge_kernel, lam_init=lam_init),
        grid=(t // tm,),
        in_specs=[pl.BlockSpec((tm, D_MODEL), lambda i: (i, 0)),
                  pl.BlockSpec((1, 6, D_MODEL), lambda i: (mod_row(i), 0, 0)),
                  full(n1), tok(0), dirs(0), dirs(1), tok(3), dirs(0), dirs(1), tok(3), tok(0),
                  full(hn), full(bm16), full(wg), full(wb), full(wo)],
        out_specs=pl.BlockSpec((tm, D_MODEL), lambda i: (i, 0)),
        out_shape=jax.ShapeDtypeStruct((t, D_MODEL), F32),
        compiler_params=_params(("parallel",)),
        name="merge_out",
    )(x, mod_l, n1, o_na, o_gla, o_gla, z_gla, o_gdn, o_gdn, z_gdn, o_df, hn, bm16, wg, wb, wo)


def _route(h2, wr_ref, br_ref):
    lane_i = lax.broadcasted_iota(jnp.int32, (1, SMALL_W), 1)
    lane = lane_i.astype(F32)
    gid = lax.shift_right_logical(lane_i, 2).astype(F32)
    s = jax.nn.sigmoid(_dot3(h2, wr_ref[...]))
    sel = s + br_ref[...]
    far = float(SMALL_W)

    def first_max(vals):
        mx = jnp.max(vals, axis=-1, keepdims=True)
        idx = jnp.min(jnp.where(vals == mx, lane, far), axis=-1, keepdims=True)
        return mx, idx

    best = None
    for g in range(N_GROUPS):
        vals = jnp.where(gid == float(g), sel, NEG)
        a, ia = first_max(vals)
        b, _ = first_max(jnp.where(lane == ia, NEG, vals))
        score = a + b
        if best is None:
            best, gi = score, jnp.zeros_like(score)
        else:
            better = score > best
            gi = jnp.where(better, float(g), gi)
            best = jnp.where(better, score, best)
    vals = jnp.where(gid == gi, sel, NEG)
    _, i1 = first_max(vals)
    _, i2 = first_max(jnp.where(lane == i1, NEG, vals))
    s1 = jnp.sum(jnp.where(lane == i1, s, 0.0), axis=-1, keepdims=True)
    s2 = jnp.sum(jnp.where(lane == i2, s, 0.0), axis=-1, keepdims=True)
    inv = 1.0 / (s1 + s2)
    return jnp.where(lane == i1, s1 * inv, 0.0) + jnp.where(lane == i2, s2 * inv, 0.0)


def _moe_kernel(x_ref, mod_ref, n2_ref, wr_ref, br_ref, ex_ref, w1_ref, w3_ref, w2_ref, o_ref,
                h_scr, g_scr, acc_scr):
    g = pl.program_id(1)
    mod = mod_ref[0]

    @pl.when(g == 0)
    def _():
        h2 = _rms_rows(x_ref[...], n2_ref[...]) * (1.0 + mod[4:5]) + mod[3:4]
        h_scr[...] = h2.astype(BF16)
        g_scr[...] = _route(h2, wr_ref, br_ref)
        acc_scr[...] = jnp.zeros_like(acc_scr)

    hb = h_scr[...]
    gexp = _dot_rx(g_scr[...], ex_ref[0])
    he = _silu(_dot(hb, w1_ref[...])) * _dot(hb, w3_ref[...]) * gexp
    acc_scr[...] += _bdot(he, w2_ref[...])

    @pl.when(g == N_GROUPS - 1)
    def _():
        o_ref[...] = x_ref[...] + mod[5:6] * acc_scr[...]


def _moe(x, mod_l, n2, wr, br, expand, w1, w3, w2, tm, mod_row):
    t = x.shape[0]
    gw = GROUP_E * D_FF
    return pl.pallas_call(
        _moe_kernel,
        grid=(t // tm, N_GROUPS),
        in_specs=[pl.BlockSpec((tm, D_MODEL), lambda i, g: (i, 0)),
                  pl.BlockSpec((1, 6, D_MODEL), lambda i, g: (mod_row(i), 0, 0)),
                  pl.BlockSpec(n2.shape, lambda i, g: (0, 0)),
                  pl.BlockSpec(wr.shape, lambda i, g: (0, 0)),
                  pl.BlockSpec(br.shape, lambda i, g: (0, 0)),
                  pl.BlockSpec((1, SMALL_W, gw), lambda i, g: (g, 0, 0)),
                  pl.BlockSpec((D_MODEL, gw), lambda i, g: (0, g)),
                  pl.BlockSpec((D_MODEL, gw), lambda i, g: (0, g)),
                  pl.BlockSpec((gw, D_MODEL), lambda i, g: (g, 0))],
        out_specs=pl.BlockSpec((tm, D_MODEL), lambda i, g: (i, 0)),
        out_shape=jax.ShapeDtypeStruct((t, D_MODEL), F32),
        scratch_shapes=[pltpu.VMEM((tm, D_MODEL), BF16), pltpu.VMEM((tm, SMALL_W), F32),
                        pltpu.VMEM((tm, D_MODEL), F32)],
        compiler_params=_params(("parallel", "arbitrary")),
        name="moe",
    )(x, mod_l, n2, wr, br, expand, w1, w3, w2)


def _final_kernel(x_ref, w_ref, o_ref):
    o_ref[...] = _rms_rows(x_ref[...], w_ref[...])


def _final_norm(x, w, tm):
    t = x.shape[0]
    return pl.pallas_call(
        _final_kernel,
        grid=(t // tm,),
        in_specs=[pl.BlockSpec((tm, D_MODEL), lambda i: (i, 0)), pl.BlockSpec(w.shape, lambda i: (0, 0))],
        out_specs=pl.BlockSpec((tm, D_MODEL), lambda i: (i, 0)),
        out_shape=jax.ShapeDtypeStruct((t, D_MODEL), F32),
        compiler_params=_params(("parallel",)),
        name="final_norm",
    )(x, w)


def _block_diag_state(s, transpose):
    if transpose:
        s = jnp.swapaxes(s, -1, -2)
    eye = jnp.eye(HEADS, dtype=s.dtype)
    out = jnp.einsum('bdhij,hg->bdhigj', s, eye)
    return out.reshape(s.shape[0], 2, MIX_W, MIX_W)


def _diag_blocks(s_bd, transpose):
    b = s_bd.shape[0]
    s = s_bd.reshape(b, 2, HEADS, HEAD_DIM, HEADS, HEAD_DIM)
    s = jnp.stack([s[:, :, h, :, h, :] for h in range(HEADS)], axis=2)
    return jnp.swapaxes(s, -1, -2) if transpose else s


def _pick_tile(*lengths):
    for tm in (512, 256, 128, 64):
        if all(n % tm == 0 for n in lengths):
            return tm
    raise ValueError("token counts must be multiples of 64")


def kernel(x_prompt, x_sample, cache_na_k, cache_na_v, cache_diff_k, cache_diff_v, state_gla, state_gdn, c, c_ctx, w_ada, b_ada, norm1_w, norm2_w, w_in, na_rpb, gla_gate_up, gla_gate_bias, gla_norm_w, gdn_conv_w, gdn_A_log, gdn_dt_bias, gdn_norm_w, diff_lambda, diff_subln_w, w_branch, w_out, w_router, b_router, w_e1, w_e3, w_e2, final_norm_w):
    n_ctx, l_ctx, _ = x_prompt.shape
    n_lat, n_tok, _ = x_sample.shape
    depth = w_ada.shape[0]
    p_len = cache_na_k.shape[2]
    lat_tokens = n_lat * n_tok
    ctx_tokens = n_ctx * l_ctx
    assert n_lat + 1 <= 8 and n_tok % l_ctx == 0 and n_tok % GRID_W == 0 and l_ctx % CHUNK == 0
    tm = _pick_tile(n_tok, ctx_tokens)
    tq_lat = _pick_tile(n_tok) // 4 if _pick_tile(n_tok) >= 512 else 64

    def mod_row(i):
        return jnp.where(i * tm < lat_tokens, (i * tm) // n_tok, n_lat)

    x = jnp.concatenate([x_sample.reshape(lat_tokens, D_MODEL), x_prompt.reshape(ctx_tokens, D_MODEL)], 0)
    cond = jnp.zeros((8, D_MODEL), F32).at[:n_lat].set(c).at[n_lat].set(c_ctx)
    mod = _ada(cond, w_ada.astype(BF16), b_ada).reshape(depth, 8, 6, D_MODEL)

    l_all, m_all = _chunk_consts()
    bm = jnp.asarray(_block_mask(), F32)
    bm16 = bm.astype(BF16)
    sm = _solve_masks()
    cos, sin, perm = _rope_consts(n_tok)
    edge = _seq_edges(lat_tokens, n_tok, ctx_tokens, l_ctx)
    lane = np.arange(MIX_W)
    small_rows = np.arange(SMALL_W)
    def expand_cols(first_col):
        return jnp.asarray((small_rows[:, None] == first_col + lane[None, :] // HEAD_DIM).astype(np.float32), BF16)
    ea = jnp.stack([expand_cols(32), expand_cols(36)])
    eb = jnp.stack([expand_cols(40), expand_cols(44)])
    ex_rows = np.arange(SMALL_W)[:, None]
    ex_cols = np.arange(GROUP_E * D_FF)[None, :]
    expand = jnp.asarray(np.stack([(ex_rows == g * GROUP_E + ex_cols // D_FF) for g in range(N_GROUPS)])
                         .astype(np.float32), BF16)
    wr = jnp.zeros((D_MODEL, SMALL_W), F32).at[:, :N_EXPERTS].set(w_router)
    br = jnp.zeros((1, SMALL_W), F32).at[0, :N_EXPERTS].set(b_router)

    def cols(w, *names):
        return [w[:, _OFF[n][0]:_OFF[n][1]] for n in names]

    zeros_state = jnp.zeros((n_ctx, 2, MIX_W, MIX_W), F32)
    new_kv, new_gla, new_gdn = [], [], []
    for l in range(depth):
        lam_init = 0.8 - 0.6 * math.exp(-0.3 * l)
        wl = w_in[l]
        wcat = jnp.concatenate(
            cols(wl, 'na_q', 'na_k', 'na_v', 'gla_q', 'gla_k', 'gla_v', 'gla_og', 'gdn_qkv', 'gdn_og',
                 'diff_q', 'diff_k', 'diff_v', 'gla_gf', 'gla_gb', 'gdn_af', 'gdn_ab', 'gdn_bf', 'gdn_bb')
            + [jnp.zeros((D_MODEL, SMALL_W - 2 * GATE_RANK - 4 * HEADS), F32)], axis=1).astype(BF16)
        w_gate = wl[:, _OFF['branch_gate'][0]:].astype(BF16)
        z_na, z_gla, z_gdn, z_diff, z_small = _proj(x, mod[l], norm1_w[l][None], wcat, tm, mod_row)

        ck = cache_na_k[:, l].reshape(n_lat * p_len, MIX_W)
        cv = cache_na_v[:, l].reshape(n_lat * p_len, MIX_W)
        dl = diff_lambda[l]
        na_lat = _na_latent(z_na, ck, cv, _na_bias_table(na_rpb[l], min(NA_WIN_H, n_tok // GRID_W)),
                            n_lat, n_tok, p_len)
        na_ctx = _attention(z_na, 0, z_na, 1, z_na, 2, dl, n_sub=1, lam_init=0.0, n_seq=n_ctx, lq=l_ctx,
                            lk=l_ctx, q_base=lat_tokens, k_base=lat_tokens, tq=l_ctx, name="na_ctx")
        o_na = jnp.concatenate([na_lat, na_ctx], 0)

        q_r, k_r = _rope(z_diff, cos, sin, perm, n_lat, n_tok, tm)
        kk = jnp.concatenate([k_r.reshape(n_lat, n_tok, MIX_W), cache_diff_k[:, l].reshape(n_lat, p_len, MIX_W)], 1)
        vv = jnp.concatenate([z_diff[:lat_tokens, 2 * MIX_W:].reshape(n_lat, n_tok, MIX_W),
                              cache_diff_v[:, l].reshape(n_lat, p_len, MIX_W)], 1)
        lkv = n_tok + p_len
        df_lat = _attention(q_r, 0, kk.reshape(n_lat * lkv, MIX_W), 0, vv.reshape(n_lat * lkv, MIX_W), 0, dl,
                            n_sub=2, lam_init=lam_init, n_seq=n_lat, lq=n_tok, lk=lkv, q_base=0, k_base=0,
                            tq=tq_lat, name="diff_latent")
        df_ctx = _attention(z_diff, 0, z_diff, 1, z_diff, 2, dl, n_sub=2, lam_init=lam_init, n_seq=n_ctx,
                            lq=l_ctx, lk=l_ctx, q_base=lat_tokens, k_base=lat_tokens, tq=l_ctx, name="diff_ctx")
        o_df = jnp.concatenate([df_lat, df_ctx], 0)

        gup = jnp.zeros((2, SMALL_W, MIX_W), F32)
        gup = gup.at[0, 0:GATE_RANK].set(gla_gate_up[l, 0]).at[1, GATE_RANK:2 * GATE_RANK].set(gla_gate_up[l, 1])
        gbias = gla_gate_bias[l][:, None, :]
        gla_lat, _ = _gla(z_gla, z_small, gup, gbias, l_all, m_all, bm,
                          _block_diag_state(state_gla[:, l], True), n_lat, n_tok, 0)
        gla_ctx, st_ctx = _gla(z_gla, z_small, gup, gbias, l_all, m_all, bm, zeros_state, n_ctx, l_ctx, lat_tokens)
        o_gla = jnp.concatenate([gla_lat, gla_ctx], 1)

        qkv_n = _gdn_pre(z_gdn, edge, gdn_conv_w[l], bm16, tm)
        alog = jnp.repeat(gdn_A_log[l], HEAD_DIM, axis=-1)[:, None, :]
        dtb = jnp.repeat(gdn_dt_bias[l], HEAD_DIM, axis=-1)[:, None, :]
        gdn_lat, _ = _gdn(qkv_n, z_small, ea, eb, alog, dtb, l_all, m_all, bm, sm,
                          _block_diag_state(state_gdn[:, l], False), n_lat, n_tok, 0)
        gdn_ctx, s_ctx = _gdn(qkv_n, z_small, ea, eb, alog, dtb, l_all, m_all, bm, sm, zeros_state,
                              n_ctx, l_ctx, lat_tokens)
        o_gdn = jnp.concatenate([gdn_lat, gdn_ctx], 1)

        hn = jnp.stack([jnp.tile(gla_norm_w[l], HEADS), jnp.tile(gdn_norm_w[l], HEADS),
                        jnp.tile(diff_subln_w[l], HEADS)])[:, None, :]
        x = _merge(x, mod[l], norm1_w[l][None], o_na, o_gla, z_gla, o_gdn, z_gdn, o_df, hn, bm16, w_gate,
                   w_branch[l].astype(BF16), w_out[l].astype(BF16), tm, mod_row, lam_init)
        w1 = w_e1[l].transpose(1, 0, 2).reshape(D_MODEL, N_EXPERTS * D_FF).astype(BF16)
        w3 = w_e3[l].transpose(1, 0, 2).reshape(D_MODEL, N_EXPERTS * D_FF).astype(BF16)
        w2 = w_e2[l].reshape(N_EXPERTS * D_FF, D_MODEL).astype(BF16)
        x = _moe(x, mod[l], norm2_w[l][None], wr, br, expand, w1, w3, w2, tm, mod_row)

        def ctx_heads(z, col):
            return z[lat_tokens:, col * MIX_W:(col + 1) * MIX_W].reshape(n_ctx, l_ctx, HEADS, HEAD_DIM)
        new_kv.append((ctx_heads(z_na, 1), ctx_heads(z_na, 2), ctx_heads(z_diff, 1), ctx_heads(z_diff, 2)))
        new_gla.append(_diag_blocks(st_ctx, True))
        new_gdn.append(_diag_blocks(s_ctx, False))

    y = _final_norm(x, final_norm_w[None], tm)
    y_sample = y[:lat_tokens].reshape(n_lat, n_tok, D_MODEL)
    y_prompt = y[lat_tokens:].reshape(n_ctx, l_ctx, D_MODEL)
    stack = lambda j: jnp.stack([t[j] for t in new_kv], axis=1)
    return (y_prompt, y_sample, stack(0), stack(1), stack(2), stack(3),
            jnp.stack(new_gla, axis=1), jnp.stack(new_gdn, axis=1))
```

```python
import functools
import math

import numpy as np
import jax
import jax.numpy as jnp
from jax import lax
from jax.experimental import pallas as pl
from jax.experimental.pallas import tpu as pltpu

F32 = jnp.float32
BF16 = jnp.bfloat16

D_MODEL = 1024
HEADS = 4
HEAD_DIM = 64
MIX_W = HEADS * HEAD_DIM
GRID_W = 64
NA_WIN_H = 8
NA_WIN_W = 16
GATE_RANK = 16
GATE_NORM = 16.0
GDN_CONV = 3
CHUNK = 64
N_EXPERTS = 16
N_GROUPS = 4
GROUP_E = N_EXPERTS // N_GROUPS
D_FF = D_MODEL // 4
ROPE_BASE = 10000.0
EPS = 1e-6
NEG = -1e30
SMALL_W = 128
V7X_VMEM_LIMIT = 56 * 1024 * 1024
LEAF = 8
ATTN_HEAD_GROUP = 2
NA_ROWS_PER_STEP = 4
GLA_SEQS_PER_STEP = 4
GDN_SEQS_PER_STEP = 2

_IN_SPLITS = (
    ('na_q', MIX_W), ('na_k', MIX_W), ('na_v', MIX_W),
    ('gla_q', MIX_W), ('gla_k', MIX_W), ('gla_v', MIX_W),
    ('gla_gf', GATE_RANK), ('gla_gb', GATE_RANK), ('gla_og', MIX_W),
    ('gdn_qkv', 3 * MIX_W), ('gdn_af', HEADS), ('gdn_ab', HEADS),
    ('gdn_bf', HEADS), ('gdn_bb', HEADS), ('gdn_og', MIX_W),
    ('diff_q', MIX_W), ('diff_k', MIX_W), ('diff_v', MIX_W),
    ('branch_gate', 4 * D_MODEL),
)
_OFF = {}
_o = 0
for _n, _s in _IN_SPLITS:
    _OFF[_n] = (_o, _o + _s)
    _o += _s

_N_LVL = 6
_ROW_B = 2 * _N_LVL * CHUNK
_ROW_BL = _ROW_B + CHUNK
_ROW_TOT = _ROW_BL + CHUNK
_L_ROWS = _ROW_TOT + 16


def _params(sem):
    return pltpu.CompilerParams(dimension_semantics=sem, vmem_limit_bytes=V7X_VMEM_LIMIT)


def _dot(a, b):
    return jnp.dot(a, b, preferred_element_type=F32)


def _bdot(a, b):
    return _dot(a.astype(BF16), b.astype(BF16))


def _bdot_nt(a, b):
    return lax.dot_general(a.astype(BF16), b.astype(BF16), (((1,), (1,)), ((), ())),
                           preferred_element_type=F32)


def _bdot_tn(a, b):
    return lax.dot_general(a.astype(BF16), b.astype(BF16), (((0,), (0,)), ((), ())),
                           preferred_element_type=F32)


def _split(x):
    hi = x.astype(BF16)
    lo = (x - hi.astype(F32)).astype(BF16)
    return hi, lo


def _dot_rx(x, m):
    hi, lo = _split(x)
    return _dot(hi, m) + _dot(lo, m)


def _dot_lx(m, x):
    hi, lo = _split(x)
    return _dot(m, hi) + _dot(m, lo)


def _dot3(a, b):
    ah, al = _split(a)
    bh, bl = _split(b)
    return _dot(ah, bh) + _dot(ah, bl) + _dot(al, bh)


def _silu(x):
    return x * jax.nn.sigmoid(x)


def _softplus(x):
    return jnp.maximum(x, 0.0) + jnp.log(1.0 + jnp.exp(-jnp.abs(x)))


def _rms_rows(x, w):
    return x * lax.rsqrt(jnp.mean(x * x, axis=-1, keepdims=True) + EPS) * w


def _head_sum(x, bm16):
    return _dot_rx(x, bm16)


def _bd_rows(x, bm):
    return jnp.concatenate([x] * HEADS, axis=0) * bm


def _lane_mask(lo, width, n=MIX_W):
    lane = lax.broadcasted_iota(jnp.int32, (1, n), 1)
    return (lane >= lo) & (lane < lo + width)


def _chunk_consts():
    c = CHUNK
    i = np.arange(c)[:, None]
    t = np.arange(c)[None, :]
    blocks, masks = [], []
    for lvl in range(_N_LVL):
        s = 1 << lvl
        upper = ((i // s) % 2) == 1
        lq = upper & (t > (i // s) * s) & (t <= i)
        lk = (~upper) & (t > i) & (t <= ((i // s) + 1) * s)
        blocks += [lq, lk]
        masks.append(((i // (2 * s)) == (t // (2 * s))) & upper & (((t // s) % 2) == 0))
    masks.append(i == t)
    blocks += [t <= i, t > i]
    fwd_l = np.concatenate([b.astype(np.float32) for b in blocks], 0)
    rev_l = np.concatenate([b[::-1, ::-1].astype(np.float32) for b in blocks], 0)
    tot = np.ones((16, c), np.float32)
    l_all = np.stack([np.concatenate([fwd_l, tot], 0), np.concatenate([rev_l, tot], 0)])
    m_fwd = np.stack([np.tile(m.astype(np.float32), (1, HEADS)) for m in masks])
    m_rev = np.stack([np.tile(m[::-1, ::-1].astype(np.float32), (1, HEADS)) for m in masks])
    assert l_all.shape[1] == _L_ROWS
    return jnp.asarray(l_all, BF16), jnp.asarray(np.stack([m_fwd, m_rev]), F32)


def _gdn_consts():
    c = CHUNK
    i = np.arange(c)[:, None]
    t = np.arange(c)[None, :]
    lb = (t <= i).astype(np.float32)
    lbl = (t > i).astype(np.float32)
    tot = np.ones((16, c), np.float32)
    strict = (t < i).astype(np.float32)
    incl = (t <= i).astype(np.float32)
    rev = lambda a: a[::-1, ::-1]
    lsm = np.stack([np.concatenate([lb, lbl, tot], 0), np.concatenate([rev(lb), rev(lbl), tot], 0)])
    lbt = np.stack([np.tile(lb.T, (1, HEADS)), np.tile(rev(lb).T, (1, HEADS))])
    pm = np.stack([np.stack([np.tile(strict, (1, HEADS)), np.tile(incl, (1, HEADS))]),
                   np.stack([np.tile(rev(strict), (1, HEADS)), np.tile(rev(incl), (1, HEADS))])])
    return jnp.asarray(lsm, BF16), jnp.asarray(lbt, F32), jnp.asarray(pm, F32)


def _block_mask():
    r = np.arange(MIX_W)
    return (r[:, None] // HEAD_DIM == r[None, :] // HEAD_DIM).astype(np.float32)


def _solve_masks():
    r = np.arange(MIX_W)[:, None]
    c = np.arange(MIX_W)[None, :]
    out = [(r // LEAF == c // LEAF)]
    s = LEAF
    while s < CHUNK:
        out.append((r // (2 * s) == c // (2 * s)) & (r // s != c // s))
        s *= 2
    return jnp.asarray(np.stack(out).astype(np.float32))


def _rope_consts(n_tok):
    t = np.arange(n_tok)
    pos = np.stack([t // GRID_W, t % GRID_W], 0).astype(np.float32)
    lane = np.arange(MIX_W)
    u = lane % 32
    axis = u // 16
    w = u % 16
    first = w < 8
    inv = ROPE_BASE ** (-(w % 8).astype(np.float32) / 8.0)
    ang = pos[axis, :].T * inv[None, :]
    cos = np.cos(ang)
    sin = np.sin(ang) * np.where(first, -1.0, 1.0)[None, :]
    partner = np.where(first, lane + 8, lane - 8)
    perm = np.zeros((MIX_W, MIX_W), np.float32)
    perm[partner, lane] = 1.0
    return jnp.asarray(cos, F32), jnp.asarray(sin, F32), jnp.asarray(perm, BF16)


def _ada_kernel(c_ref, w_ref, b_ref, o_ref):
    c = c_ref[...]
    o_ref[0] = _bdot(_silu(c), w_ref[0]) + b_ref[0]


def _ada(cond, w_ada16, b_ada):
    depth, _, n = w_ada16.shape
    tn = 1536
    return pl.pallas_call(
        _ada_kernel,
        grid=(depth, n // tn),
        in_specs=[pl.BlockSpec((8, D_MODEL), lambda l, j: (0, 0)),
                  pl.BlockSpec((1, D_MODEL, tn), lambda l, j: (l, 0, j)),
                  pl.BlockSpec((1, 1, tn), lambda l, j: (l, 0, j))],
        out_specs=pl.BlockSpec((1, 8, tn), lambda l, j: (l, 0, j)),
        out_shape=jax.ShapeDtypeStruct((depth, 8, n), F32),
        compiler_params=_params(("parallel", "parallel")),
        name="ada_mod",
    )(cond, w_ada16, b_ada.reshape(depth, 1, n))


def _proj_kernel(x_ref, mod_ref, nw_ref, w_ref, o_na, o_gla, o_gdn, o_diff, o_small):
    mod = mod_ref[0]
    h = _rms_rows(x_ref[...], nw_ref[...]) * (1.0 + mod[1:2]) + mod[0:1]
    hb = h.astype(BF16)
    o_na[...] = _dot(hb, w_ref[:, 0:768])
    o_gla[...] = _dot(hb, w_ref[:, 768:1792])
    o_gdn[...] = _dot(hb, w_ref[:, 1792:2816])
    o_diff[...] = _dot(hb, w_ref[:, 2816:3584])
    o_small[...] = _dot(hb, w_ref[:, 3584:3712])


def _proj(x, mod_l, nw, wcat, tm, mod_row):
    t = x.shape[0]
    widths = (768, 1024, 1024, 768, SMALL_W)
    return pl.pallas_call(
        _proj_kernel,
        grid=(t // tm,),
        in_specs=[pl.BlockSpec((tm, D_MODEL), lambda i: (i, 0)),
                  pl.BlockSpec((1, 6, D_MODEL), lambda i: (mod_row(i), 0, 0)),
                  pl.BlockSpec((1, D_MODEL), lambda i: (0, 0)),
                  pl.BlockSpec(wcat.shape, lambda i: (0, 0))],
        out_specs=[pl.BlockSpec((tm, w), lambda i: (i, 0)) for w in widths],
        out_shape=[jax.ShapeDtypeStruct((t, w), F32) for w in widths],
        compiler_params=_params(("parallel",)),
        name="in_proj",
    )(x, mod_l, nw, wcat)


def _attn_core(q, k, v, n_sub, lam):
    kb = k.astype(BF16)
    vb = v.astype(BF16)
    tq = q.shape[0]
    sub_w = HEAD_DIM // n_sub
    scale = sub_w ** -0.5
    out = jnp.zeros(q.shape, F32)
    for h0 in range(0, HEADS, ATTN_HEAD_GROUP):
        maps = [(h, m) for h in range(h0, h0 + ATTN_HEAD_GROUP) for m in range(n_sub)]
        qs = jnp.concatenate(
            [(q * jnp.where(_lane_mask(h * HEAD_DIM + m * sub_w, sub_w), scale, 0.0)).astype(BF16)
             for h, m in maps], axis=0)
        s = lax.dot_general(qs, kb, (((1,), (1,)), ((), ())), preferred_element_type=F32)
        e = jnp.exp(s - jnp.max(s, axis=-1, keepdims=True))
        p = e * (1.0 / jnp.sum(e, axis=-1, keepdims=True))
        if n_sub == 2:
            p = jnp.concatenate([p[(2 * i) * tq:(2 * i + 1) * tq] - lam * p[(2 * i + 1) * tq:(2 * i + 2) * tq]
                                 for i in range(ATTN_HEAD_GROUP)], axis=0)
        o_all = _dot(p.astype(BF16), vb)
        for i in range(ATTN_HEAD_GROUP):
            out = jnp.where(_lane_mask((h0 + i) * HEAD_DIM, HEAD_DIM), o_all[i * tq:(i + 1) * tq], out)
    return out


def _diff_lambda(dl, lam_init):
    a = jnp.sum(dl[0:1] * dl[1:2], axis=-1, keepdims=True)
    b = jnp.sum(dl[2:3] * dl[3:4], axis=-1, keepdims=True)
    return jnp.exp(a) - jnp.exp(b) + lam_init


def _attn_kernel(q_ref, k_ref, v_ref, dl_ref, o_ref, *, n_sub, lam_init):
    lam = _diff_lambda(dl_ref[...], lam_init) if n_sub == 2 else None
    o_ref[...] = _attn_core(q_ref[...], k_ref[...], v_ref[...], n_sub, lam)


def _attention(q_arr, q_col, k_arr, k_col, v_arr, v_col, dl, *, n_sub, lam_init, n_seq, lq, lk,
               q_base, k_base, tq, name):
    nq = lq // tq
    return pl.pallas_call(
        functools.partial(_attn_kernel, n_sub=n_sub, lam_init=lam_init),
        grid=(n_seq, nq),
        in_specs=[pl.BlockSpec((tq, MIX_W), lambda s, j: (q_base // tq + s * nq + j, q_col)),
                  pl.BlockSpec((lk, MIX_W), lambda s, j: (k_base // lk + s, k_col)),
                  pl.BlockSpec((lk, MIX_W), lambda s, j: (k_base // lk + s, v_col)),
                  pl.BlockSpec(dl.shape, lambda s, j: (0, 0))],
        out_specs=pl.BlockSpec((tq, MIX_W), lambda s, j: (s * nq + j, 0)),
        out_shape=jax.ShapeDtypeStruct((n_seq * lq, MIX_W), F32),
        compiler_params=_params(("parallel", "parallel")),
        name=name,
    )(q_arr, k_arr, v_arr, dl)


def _na_lat_kernel(q_ref, k_ref, v_ref, ck_ref, cv_ref, bias_ref, o_ref, *, rows, kh, rps):
    g = pl.program_id(1)
    ck = ck_ref[...].astype(BF16)
    cv = cv_ref[...].astype(BF16)
    scale = HEAD_DIM ** -0.5
    head_scale = [jnp.where(_lane_mask(h * HEAD_DIM, HEAD_DIM), scale, 0.0) for h in range(HEADS)]
    hw = HEADS * GRID_W
    qs = []
    for j in range(rps):
        qj = q_ref[j * GRID_W:(j + 1) * GRID_W, :]
        qs.append(jnp.concatenate([(qj * hs).astype(BF16) for hs in head_scale], axis=0))
    s_ctx_all = lax.dot_general(jnp.concatenate(qs, axis=0), ck, (((1,), (1,)), ((), ())),
                                preferred_element_type=F32)
    p_ctx, o_loc = [], []
    for j in range(rps):
        r = g * rps + j
        start = jnp.clip(r - kh // 2, 0, rows - kh)
        cls = r - start
        ws = pl.multiple_of(start * GRID_W, GRID_W)
        kw = k_ref[pl.ds(ws, kh * GRID_W), :].astype(BF16)
        vw = v_ref[pl.ds(ws, kh * GRID_W), :].astype(BF16)
        bias = jnp.concatenate([bias_ref[h, pl.ds(cls, 1)][0] for h in range(HEADS)], axis=0)
        s_loc = lax.dot_general(qs[j], kw, (((1,), (1,)), ((), ())), preferred_element_type=F32) + bias
        s_ctx = s_ctx_all[j * hw:(j + 1) * hw]
        mx = jnp.maximum(jnp.max(s_loc, axis=-1, keepdims=True), jnp.max(s_ctx, axis=-1, keepdims=True))
        e_loc = jnp.exp(s_loc - mx)
        e_ctx = jnp.exp(s_ctx - mx)
        inv = 1.0 / (jnp.sum(e_loc, axis=-1, keepdims=True) + jnp.sum(e_ctx, axis=-1, keepdims=True))
        p_ctx.append((e_ctx * inv).astype(BF16))
        o_loc.append(_dot((e_loc * inv).astype(BF16), vw))
    o_ctx_all = _dot(jnp.concatenate(p_ctx, axis=0), cv)
    for j in range(rps):
        o_all = o_loc[j] + o_ctx_all[j * hw:(j + 1) * hw]
        out = jnp.zeros((GRID_W, MIX_W), F32)
        for h in range(HEADS):
            out = jnp.where(_lane_mask(h * HEAD_DIM, HEAD_DIM), o_all[h * GRID_W:(h + 1) * GRID_W], out)
        o_ref[j * GRID_W:(j + 1) * GRID_W, :] = out


def _na_bias_table(rpb, kh):
    cidx = np.arange(GRID_W)
    cls = np.arange(kh)
    drow = np.arange(kh)[None, :] - cls[:, None] + NA_WIN_H - 1
    col_start = np.clip(cidx - NA_WIN_W // 2, 0, GRID_W - NA_WIN_W)
    col_ok = (cidx[None, :] >= col_start[:, None]) & (cidx[None, :] < col_start[:, None] + NA_WIN_W)
    dcol = np.clip(cidx[None, :] - cidx[:, None], 1 - NA_WIN_W, NA_WIN_W - 1) + NA_WIN_W - 1
    bias = rpb.astype(F32)[:, drow][..., dcol].transpose(0, 1, 3, 2, 4)
    bias = jnp.where(col_ok[:, None, :], bias, NEG)
    return bias.reshape(HEADS, kh, GRID_W, kh * GRID_W)


def _na_latent(z_na, ck, cv, bias, n_b, n_tok, p_len):
    rows = n_tok // GRID_W
    kh = min(NA_WIN_H, rows)
    rps = NA_ROWS_PER_STEP
    assert rows % rps == 0
    steps = rows // rps
    return pl.pallas_call(
        functools.partial(_na_lat_kernel, rows=rows, kh=kh, rps=rps),
        grid=(n_b, steps),
        in_specs=[pl.BlockSpec((rps * GRID_W, MIX_W), lambda b, r: (b * steps + r, 0)),
                  pl.BlockSpec((n_tok, MIX_W), lambda b, r: (b, 1)),
                  pl.BlockSpec((n_tok, MIX_W), lambda b, r: (b, 2)),
                  pl.BlockSpec((p_len, MIX_W), lambda b, r: (b, 0)),
                  pl.BlockSpec((p_len, MIX_W), lambda b, r: (b, 0)),
                  pl.BlockSpec(bias.shape, lambda b, r: (0, 0, 0, 0))],
        out_specs=pl.BlockSpec((rps * GRID_W, MIX_W), lambda b, r: (b * steps + r, 0)),
        out_shape=jax.ShapeDtypeStruct((n_b * n_tok, MIX_W), F32),
        compiler_params=_params(("parallel", "arbitrary")),
        name="na_latent",
    )(z_na, z_na, z_na, ck, cv, bias)


def _rope_kernel(q_ref, k_ref, cos_ref, sin_ref, p_ref, qo_ref, ko_ref):
    cos = cos_ref[...]
    sin = sin_ref[...]
    p = p_ref[...]
    q = q_ref[...]
    k = k_ref[...]
    qo_ref[...] = q * cos + _dot_rx(q, p) * sin
    ko_ref[...] = k * cos + _dot_rx(k, p) * sin


def _rope(z_diff, cos, sin, perm, n_b, n_tok, tm):
    nt = n_tok // tm
    spec_t = pl.BlockSpec((tm, MIX_W), lambda i: (i % nt, 0))
    return pl.pallas_call(
        _rope_kernel,
        grid=(n_b * nt,),
        in_specs=[pl.BlockSpec((tm, MIX_W), lambda i: (i, 0)),
                  pl.BlockSpec((tm, MIX_W), lambda i: (i, 1)),
                  spec_t, spec_t,
                  pl.BlockSpec(perm.shape, lambda i: (0, 0))],
        out_specs=[pl.BlockSpec((tm, MIX_W), lambda i: (i, 0))] * 2,
        out_shape=[jax.ShapeDtypeStruct((n_b * n_tok, MIX_W), F32)] * 2,
        compiler_params=_params(("parallel",)),
        name="diff_rope",
    )(z_diff, z_diff, cos, sin, perm)


def _gdn_pre_kernel(x_ref, prev_ref, next_ref, edge_ref, w_ref, bm_ref, o_ref, *, tm):
    x = x_ref[...]
    w = w_ref[...]
    edge = edge_ref[...]
    row = lax.broadcasted_iota(jnp.int32, (tm, 1), 0)
    x_prev = jnp.where(row == 0, prev_ref[7:8, :], pltpu.roll(x, 1, 0)) * edge[:, 0:1]
    x_next = jnp.where(row == tm - 1, next_ref[0:1, :], pltpu.roll(x, tm - 1, 0)) * edge[:, 1:2]
    y = _silu(x_prev * w[0:1] + x * w[1:2] + x_next * w[2:3])
    bm16 = bm_ref[...]
    q = y[:, 0:MIX_W]
    k = y[:, MIX_W:2 * MIX_W]
    o_ref[:, 0:MIX_W] = q * lax.rsqrt(_head_sum(q * q, bm16) + EPS) * (HEAD_DIM ** -0.5)
    o_ref[:, MIX_W:2 * MIX_W] = k * lax.rsqrt(_head_sum(k * k, bm16) + EPS)
    o_ref[:, 2 * MIX_W:3 * MIX_W] = y[:, 2 * MIX_W:3 * MIX_W]


def _seq_edges(lat_tokens, n_tok, ctx_tokens, l_ctx):
    pos = np.concatenate([np.arange(lat_tokens) % n_tok, np.arange(ctx_tokens) % l_ctx])
    seq = np.concatenate([np.full(lat_tokens, n_tok), np.full(ctx_tokens, l_ctx)])
    edge = np.zeros((lat_tokens + ctx_tokens, 8), np.float32)
    edge[:, 0] = pos != 0
    edge[:, 1] = pos != seq - 1
    return jnp.asarray(edge)


def _gdn_pre(z_gdn, edge, conv_w, bm16, tm):
    t = z_gdn.shape[0]
    w3 = 3 * MIX_W
    nb8 = t // 8
    return pl.pallas_call(
        functools.partial(_gdn_pre_kernel, tm=tm),
        grid=(t // tm,),
        in_specs=[pl.BlockSpec((tm, w3), lambda i: (i, 0)),
                  pl.BlockSpec((8, w3), lambda i: (jnp.maximum(i * (tm // 8) - 1, 0), 0)),
                  pl.BlockSpec((8, w3), lambda i: (jnp.minimum((i + 1) * (tm // 8), nb8 - 1), 0)),
                  pl.BlockSpec((tm, 8), lambda i: (i, 0)),
                  pl.BlockSpec(conv_w.shape, lambda i: (0, 0)),
                  pl.BlockSpec(bm16.shape, lambda i: (0, 0))],
        out_specs=pl.BlockSpec((tm, w3), lambda i: (i, 0)),
        out_shape=jax.ShapeDtypeStruct((t, w3), F32),
        compiler_params=_params(("parallel",)),
        name="gdn_pre",
    )(z_gdn, z_gdn, z_gdn, edge, conv_w, bm16)


def _hier_scores(q, k, f, m_ref, d, bm):
    out = _bdot_nt(q, _bd_rows(k, bm)) * m_ref[d, _N_LVL]
    for lvl in range(_N_LVL):
        yield
        fq = f[2 * lvl * CHUNK:(2 * lvl + 1) * CHUNK]
        fk = f[(2 * lvl + 1) * CHUNK:(2 * lvl + 2) * CHUNK]
        out = out + _bdot_nt(q * fq, _bd_rows(k * fk, bm)) * m_ref[d, lvl]
    return out


def _gla_chain(q, k, v, zs, gup, gb, l16, m_ref, d, bm, st):
    q = q * (HEAD_DIM ** -0.5)
    gp = _bdot(zs, gup) + gb
    g = (jnp.minimum(gp, 0.0) - jnp.log(1.0 + jnp.exp(-jnp.abs(gp)))) * (1.0 / GATE_NORM)
    yield
    f = jnp.exp(_dot_lx(l16, g))
    yield
    att = yield from _hier_scores(q, k, f, m_ref, d, bm)
    fb = f[_ROW_B:_ROW_B + CHUNK]
    fbl = f[_ROW_BL:_ROW_BL + CHUNK]
    ftot = f[_ROW_TOT:_ROW_TOT + 1]
    yield
    o = _bdot_nt(q * fb, st) + _bdot(att, _bd_rows(v, bm))
    yield
    return o, st * ftot + _bdot_tn(v, k * fbl) * bm


def _tri_inverse(mbd, sm_ref):
    r = lax.broadcasted_iota(jnp.int32, (MIX_W, MIX_W), 0)
    c = lax.broadcasted_iota(jnp.int32, (MIX_W, MIX_W), 1)
    eye = jnp.where(r == c, 1.0, 0.0)
    md = mbd * sm_ref[0]
    m2 = _bdot(md, md)
    yield
    m4 = _bdot(m2, m2)
    t = eye - md
    t = t + _bdot(t, m2)
    yield
    t = t + _bdot(t, m4)
    for lvl in range(1, sm_ref.shape[0]):
        yield
        tm = _bdot(t, mbd * sm_ref[lvl])
        yield
        t = t - _bdot(tm, t)
    return t


def _gdn_chain(q, k, v, zs, ea, eb, alog, dtb, lsm, lbt, pm_ref, d, bm, sm_ref, s):
    g = -jnp.exp(alog) * _softplus(_dot_rx(zs, ea) + dtb)
    beta = jax.nn.sigmoid(_dot_rx(zs, eb))
    yield
    e = _dot_lx(lsm, g)
    dcol = e[0:CHUNK]
    drow = _dot_lx(jnp.ones((8, CHUNK), BF16), g * lbt)[0:1]
    kb = k * beta
    s2 = _bdot_nt(jnp.concatenate([kb, q], axis=0), _bd_rows(k, bm))
    yield
    dec = jnp.exp(jnp.where(pm_ref[d, 1] > 0.0, dcol - drow, NEG))
    m_cat = s2[0:CHUNK] * dec * pm_ref[d, 0]
    att = s2[CHUNK:2 * CHUNK] * dec
    t_inv = yield from _tri_inverse(_bd_rows(m_cat, bm), sm_ref)
    fb = jnp.exp(dcol)
    fbl = jnp.exp(e[CHUNK:2 * CHUNK])
    ftot = jnp.exp(e[2 * CHUNK:2 * CHUNK + 1])
    yield
    u_bd = _bdot(t_inv, _bd_rows(v * beta, bm))
    w_bd = _bdot(t_inv, _bd_rows(kb * fb, bm))
    yield
    v_new = u_bd - _bdot(w_bd, s)
    yield
    o = _bdot(q * fb, s) + _bdot(att, v_new)
    return o, s * ftot + _bdot_tn(_bd_rows(k * fbl, bm), v_new)


def _scan_kernel(*refs, chain, n_const, nb, n_chunks):
    n_tok = nb * 2 * 4
    tok = refs[:n_tok]
    consts = refs[n_tok:n_tok + n_const]
    s0_ref, of_ref, ob_ref, sf_ref, s_scr = refs[n_tok + n_const:]
    n = pl.program_id(1)

    @pl.when(n == 0)
    def _():
        s_scr[...] = s0_ref[...]

    ids = [(j, d) for j in range(nb) for d in range(2)]
    gens = []
    for j, d in ids:
        q_ref, k_ref, v_ref, zs_ref = tok[(j * 2 + d) * 4:(j * 2 + d) * 4 + 4]
        gens.append(chain(q_ref[...], k_ref[...], v_ref[...], zs_ref[...], consts, d, s_scr[j, d]))
    live = list(range(len(gens)))
    while live:
        for c in list(live):
            try:
                next(gens[c])
            except StopIteration as done:
                j, d = ids[c]
                o, s_new = done.value
                (of_ref, ob_ref)[d][j] = o
                s_scr[j, d] = s_new
                live.remove(c)

    @pl.when(n == n_chunks - 1)
    def _():
        sf_ref[...] = s_scr[...]


def _gla_step(q, k, v, zs, consts, d, st):
    gup_ref, gb_ref, l_ref, m_ref, bm_ref = consts
    return _gla_chain(q, k, v, zs, gup_ref[d], gb_ref[d], l_ref[d], m_ref, d, bm_ref[...], st)


def _gdn_step(q, k, v, zs, consts, d, s):
    ea_ref, eb_ref, alog_ref, dtb_ref, lsm_ref, lbt_ref, pm_ref, bm_ref, sm_ref = consts
    return _gdn_chain(q, k, v, zs, ea_ref[d], eb_ref[d], alog_ref[d], dtb_ref[d], lsm_ref[d], lbt_ref[d],
                      pm_ref, d, bm_ref[...], sm_ref, s)


def _scan(chain, qkv, z_small, consts, s0, n_seq, l_seq, base, seqs_per_step, name):
    nc = l_seq // CHUNK
    nb = seqs_per_step if n_seq % seqs_per_step == 0 else 1

    def chunk(n, d):
        return n + d * (nc - 1 - 2 * n)

    in_specs, args = [], []
    for j in range(nb):
        for d in range(2):
            for arr, col, width in ((qkv, 0, MIX_W), (qkv, 1, MIX_W), (qkv, 2, MIX_W), (z_small, 0, SMALL_W)):
                in_specs.append(pl.BlockSpec(
                    (CHUNK, width),
                    functools.partial(lambda g, n, j, d, col: (base // CHUNK + (g * nb + j) * nc + chunk(n, d), col),
                                      j=j, d=d, col=col)))
                args.append(arr)
    for cst in consts:
        in_specs.append(pl.BlockSpec(cst.shape, functools.partial(lambda g, n, nd: (0,) * nd, nd=cst.ndim)))
        args.append(cst)
    state_spec = pl.BlockSpec((nb, 2, MIX_W, MIX_W), lambda g, n: (g, 0, 0, 0))
    in_specs.append(state_spec)
    args.append(s0)
    o_f, o_b, s_f = pl.pallas_call(
        functools.partial(_scan_kernel, chain=chain, n_const=len(consts), nb=nb, n_chunks=nc),
        grid=(n_seq // nb, nc),
        in_specs=in_specs,
        out_specs=[pl.BlockSpec((nb, CHUNK, MIX_W), lambda g, n: (g, chunk(n, 0), 0)),
                   pl.BlockSpec((nb, CHUNK, MIX_W), lambda g, n: (g, chunk(n, 1), 0)),
                   state_spec],
        out_shape=[jax.ShapeDtypeStruct((n_seq, l_seq, MIX_W), F32),
                   jax.ShapeDtypeStruct((n_seq, l_seq, MIX_W), F32),
                   jax.ShapeDtypeStruct((n_seq, 2, MIX_W, MIX_W), F32)],
        scratch_shapes=[pltpu.VMEM((nb, 2, MIX_W, MIX_W), F32)],
        compiler_params=_params(("parallel", "arbitrary")),
        name=name,
    )(*args)
    return o_f.reshape(n_seq * l_seq, MIX_W), o_b.reshape(n_seq * l_seq, MIX_W), s_f


def _merge_kernel(x_ref, mod_ref, n1_ref, na_ref, glaf_ref, glab_ref, glag_ref, gdnf_ref, gdnb_ref, gdng_ref,
                  df_ref, hn_ref, bm_ref, wg_ref, wb_ref, wo_ref, o_ref, *, lam_init):
    mod = mod_ref[0]
    x = x_ref[...]
    h = _rms_rows(x, n1_ref[...]) * (1.0 + mod[1:2]) + mod[0:1]
    hb = h.astype(BF16)
    bm16 = bm_ref[...]

    def head_norm(o, w):
        return o * lax.rsqrt(_head_sum(o * o, bm16) * (1.0 / HEAD_DIM) + EPS) * w

    branches = (
        na_ref[...],
        head_norm(glaf_ref[...] + glab_ref[...], hn_ref[0]) * _silu(glag_ref[...]),
        head_norm(gdnf_ref[...] + gdnb_ref[...], hn_ref[1]) * _silu(gdng_ref[...]),
        head_norm(df_ref[...], hn_ref[2]) * (1.0 - lam_init),
    )
    merged = None
    for n, br in enumerate(branches):
        gate = jax.nn.sigmoid(_dot(hb, wg_ref[:, n * D_MODEL:(n + 1) * D_MODEL]))
        term = gate * _bdot(br, wb_ref[n])
        merged = term if merged is None else merged + term
    o_ref[...] = x + mod[2:3] * _bdot(merged, wo_ref[...])


def _merge(x, mod_l, n1, o_na, gla_f, gla_b, z_gla, gdn_f, gdn_b, z_gdn, o_df, hn, bm16, wg, wb, wo, tm, mod_row,
           lam_init):
    t = x.shape[0]
    tok = lambda col: pl.BlockSpec((tm, MIX_W), lambda i: (i, col))
    full = lambda a: pl.BlockSpec(a.shape, lambda i: (0,) * a.ndim)
    return pl.pallas_call(
        functools.partial(_merge_kernel, lam_init=lam_init),
        grid=(t // tm,),
        in_specs=[pl.BlockSpec((tm, D_MODEL), lambda i: (i, 0)),
                  pl.BlockSpec((1, 6, D_MODEL), lambda i: (mod_row(i), 0, 0)),
                  full(n1), tok(0), tok(0), tok(0), tok(3), tok(0), tok(0), tok(3), tok(0),
                  full(hn), full(bm16), full(wg), full(wb), full(wo)],
        out_specs=pl.BlockSpec((tm, D_MODEL), lambda i: (i, 0)),
        out_shape=jax.ShapeDtypeStruct((t, D_MODEL), F32),
        compiler_params=_params(("parallel",)),
        name="merge_out",
    )(x, mod_l, n1, o_na, gla_f, gla_b, z_gla, gdn_f, gdn_b, z_gdn, o_df, hn, bm16, wg, wb, wo)


def _route(h2, wr_ref, br_ref):
    lane_i = lax.broadcasted_iota(jnp.int32, (1, SMALL_W), 1)
    lane = lane_i.astype(F32)
    gid = lax.shift_right_logical(lane_i, 2).astype(F32)
    s = jax.nn.sigmoid(_dot3(h2, wr_ref[...]))
    sel = s + br_ref[...]
    far = float(SMALL_W)

    def first_max(vals):
        mx = jnp.max(vals, axis=-1, keepdims=True)
        idx = jnp.min(jnp.where(vals == mx, lane, far), axis=-1, keepdims=True)
        return mx, idx

    best = None
    for g in range(N_GROUPS):
        vals = jnp.where(gid == float(g), sel, NEG)
        a, ia = first_max(vals)
        b, _ = first_max(jnp.where(lane == ia, NEG, vals))
        score = a + b
        if best is None:
            best, gi = score, jnp.zeros_like(score)
        else:
            better = score > best
            gi = jnp.where(better, float(g), gi)
            best = jnp.where(better, score, best)
    vals = jnp.where(gid == gi, sel, NEG)
    _, i1 = first_max(vals)
    _, i2 = first_max(jnp.where(lane == i1, NEG, vals))
    s1 = jnp.sum(jnp.where(lane == i1, s, 0.0), axis=-1, keepdims=True)
    s2 = jnp.sum(jnp.where(lane == i2, s, 0.0), axis=-1, keepdims=True)
    inv = 1.0 / (s1 + s2)
    return jnp.where(lane == i1, s1 * inv, 0.0) + jnp.where(lane == i2, s2 * inv, 0.0)


def _moe_kernel(x_ref, mod_ref, n2_ref, wr_ref, br_ref, ex_ref, w1_ref, w3_ref, w2_ref, o_ref,
                h_scr, g_scr, acc_scr):
    g = pl.program_id(1)
    mod = mod_ref[0]

    @pl.when(g == 0)
    def _():
        h2 = _rms_rows(x_ref[...], n2_ref[...]) * (1.0 + mod[4:5]) + mod[3:4]
        h_scr[...] = h2.astype(BF16)
        g_scr[...] = _route(h2, wr_ref, br_ref)
        acc_scr[...] = jnp.zeros_like(acc_scr)

    hb = h_scr[...]
    gexp = _dot_rx(g_scr[...], ex_ref[0])
    he = _silu(_dot(hb, w1_ref[...])) * _dot(hb, w3_ref[...]) * gexp
    acc_scr[...] += _bdot(he, w2_ref[...])

    @pl.when(g == N_GROUPS - 1)
    def _():
        o_ref[...] = x_ref[...] + mod[5:6] * acc_scr[...]


def _moe(x, mod_l, n2, wr, br, expand, w1, w3, w2, tm, mod_row):
    t = x.shape[0]
    gw = GROUP_E * D_FF
    return pl.pallas_call(
        _moe_kernel,
        grid=(t // tm, N_GROUPS),
        in_specs=[pl.BlockSpec((tm, D_MODEL), lambda i, g: (i, 0)),
                  pl.BlockSpec((1, 6, D_MODEL), lambda i, g: (mod_row(i), 0, 0)),
                  pl.BlockSpec(n2.shape, lambda i, g: (0, 0)),
                  pl.BlockSpec(wr.shape, lambda i, g: (0, 0)),
                  pl.BlockSpec(br.shape, lambda i, g: (0, 0)),
                  pl.BlockSpec((1, SMALL_W, gw), lambda i, g: (g, 0, 0)),
                  pl.BlockSpec((D_MODEL, gw), lambda i, g: (0, g)),
                  pl.BlockSpec((D_MODEL, gw), lambda i, g: (0, g)),
                  pl.BlockSpec((gw, D_MODEL), lambda i, g: (g, 0))],
        out_specs=pl.BlockSpec((tm, D_MODEL), lambda i, g: (i, 0)),
        out_shape=jax.ShapeDtypeStruct((t, D_MODEL), F32),
        scratch_shapes=[pltpu.VMEM((tm, D_MODEL), BF16), pltpu.VMEM((tm, SMALL_W), F32),
                        pltpu.VMEM((tm, D_MODEL), F32)],
        compiler_params=_params(("parallel", "arbitrary")),
        name="moe",
    )(x, mod_l, n2, wr, br, expand, w1, w3, w2)


def _final_kernel(x_ref, w_ref, o_ref):
    o_ref[...] = _rms_rows(x_ref[...], w_ref[...])


def _final_norm(x, w, tm):
    t = x.shape[0]
    return pl.pallas_call(
        _final_kernel,
        grid=(t // tm,),
        in_specs=[pl.BlockSpec((tm, D_MODEL), lambda i: (i, 0)), pl.BlockSpec(w.shape, lambda i: (0, 0))],
        out_specs=pl.BlockSpec((tm, D_MODEL), lambda i: (i, 0)),
        out_shape=jax.ShapeDtypeStruct((t, D_MODEL), F32),
        compiler_params=_params(("parallel",)),
        name="final_norm",
    )(x, w)


def _block_diag_state(s, transpose):
    if transpose:
        s = jnp.swapaxes(s, -1, -2)
    eye = jnp.eye(HEADS, dtype=s.dtype)
    out = jnp.einsum('bdhij,hg->bdhigj', s, eye)
    return out.reshape(s.shape[0], 2, MIX_W, MIX_W)


def _diag_blocks(s_bd, transpose):
    b = s_bd.shape[0]
    s = s_bd.reshape(b, 2, HEADS, HEAD_DIM, HEADS, HEAD_DIM)
    s = jnp.stack([s[:, :, h, :, h, :] for h in range(HEADS)], axis=2)
    return jnp.swapaxes(s, -1, -2) if transpose else s


def _pick_tile(*lengths):
    for tm in (512, 256, 128, 64):
        if all(n % tm == 0 for n in lengths):
            return tm
    raise ValueError("token counts must be multiples of 64")


def kernel(x_prompt, x_sample, cache_na_k, cache_na_v, cache_diff_k, cache_diff_v, state_gla, state_gdn, c, c_ctx, w_ada, b_ada, norm1_w, norm2_w, w_in, na_rpb, gla_gate_up, gla_gate_bias, gla_norm_w, gdn_conv_w, gdn_A_log, gdn_dt_bias, gdn_norm_w, diff_lambda, diff_subln_w, w_branch, w_out, w_router, b_router, w_e1, w_e3, w_e2, final_norm_w):
    n_ctx, l_ctx, _ = x_prompt.shape
    n_lat, n_tok, _ = x_sample.shape
    depth = w_ada.shape[0]
    p_len = cache_na_k.shape[2]
    lat_tokens = n_lat * n_tok
    ctx_tokens = n_ctx * l_ctx
    assert n_lat + 1 <= 8 and n_tok % l_ctx == 0 and n_tok % GRID_W == 0 and l_ctx % CHUNK == 0
    tm = _pick_tile(n_tok, ctx_tokens)
    tq_lat = _pick_tile(n_tok) // 4 if _pick_tile(n_tok) >= 512 else 64

    def mod_row(i):
        return jnp.where(i * tm < lat_tokens, (i * tm) // n_tok, n_lat)

    x = jnp.concatenate([x_sample.reshape(lat_tokens, D_MODEL), x_prompt.reshape(ctx_tokens, D_MODEL)], 0)
    cond = jnp.zeros((8, D_MODEL), F32).at[:n_lat].set(c).at[n_lat].set(c_ctx)
    mod = _ada(cond, w_ada.astype(BF16), b_ada).reshape(depth, 8, 6, D_MODEL)

    l_all, m_all = _chunk_consts()
    lsm, lbt, pm = _gdn_consts()
    bm = jnp.asarray(_block_mask(), F32)
    bm16 = bm.astype(BF16)
    sm = _solve_masks()
    cos, sin, perm = _rope_consts(n_tok)
    edge = _seq_edges(lat_tokens, n_tok, ctx_tokens, l_ctx)
    lane = np.arange(MIX_W)
    small_rows = np.arange(SMALL_W)
    def expand_cols(first_col):
        return jnp.asarray((small_rows[:, None] == first_col + lane[None, :] // HEAD_DIM).astype(np.float32), BF16)
    ea = jnp.stack([expand_cols(32), expand_cols(36)])
    eb = jnp.stack([expand_cols(40), expand_cols(44)])
    ex_rows = np.arange(SMALL_W)[:, None]
    ex_cols = np.arange(GROUP_E * D_FF)[None, :]
    expand = jnp.asarray(np.stack([(ex_rows == g * GROUP_E + ex_cols // D_FF) for g in range(N_GROUPS)])
                         .astype(np.float32), BF16)
    wr = jnp.zeros((D_MODEL, SMALL_W), F32).at[:, :N_EXPERTS].set(w_router)
    br = jnp.zeros((1, SMALL_W), F32).at[0, :N_EXPERTS].set(b_router)

    def cols(w, *names):
        return [w[:, _OFF[n][0]:_OFF[n][1]] for n in names]

    zeros_state = jnp.zeros((n_ctx, 2, MIX_W, MIX_W), F32)
    new_kv, new_gla, new_gdn = [], [], []
    for l in range(depth):
        lam_init = 0.8 - 0.6 * math.exp(-0.3 * l)
        wl = w_in[l]
        wcat = jnp.concatenate(
            cols(wl, 'na_q', 'na_k', 'na_v', 'gla_q', 'gla_k', 'gla_v', 'gla_og', 'gdn_qkv', 'gdn_og',
                 'diff_q', 'diff_k', 'diff_v', 'gla_gf', 'gla_gb', 'gdn_af', 'gdn_ab', 'gdn_bf', 'gdn_bb')
            + [jnp.zeros((D_MODEL, SMALL_W - 2 * GATE_RANK - 4 * HEADS), F32)], axis=1).astype(BF16)
        w_gate = wl[:, _OFF['branch_gate'][0]:].astype(BF16)
        z_na, z_gla, z_gdn, z_diff, z_small = _proj(x, mod[l], norm1_w[l][None], wcat, tm, mod_row)

        ck = cache_na_k[:, l].reshape(n_lat * p_len, MIX_W)
        cv = cache_na_v[:, l].reshape(n_lat * p_len, MIX_W)
        dl = diff_lambda[l]
        na_lat = _na_latent(z_na, ck, cv, _na_bias_table(na_rpb[l], min(NA_WIN_H, n_tok // GRID_W)),
                            n_lat, n_tok, p_len)
        na_ctx = _attention(z_na, 0, z_na, 1, z_na, 2, dl, n_sub=1, lam_init=0.0, n_seq=n_ctx, lq=l_ctx,
                            lk=l_ctx, q_base=lat_tokens, k_base=lat_tokens, tq=l_ctx, name="na_ctx")
        o_na = jnp.concatenate([na_lat, na_ctx], 0)

        q_r, k_r = _rope(z_diff, cos, sin, perm, n_lat, n_tok, tm)
        kk = jnp.concatenate([k_r.reshape(n_lat, n_tok, MIX_W), cache_diff_k[:, l].reshape(n_lat, p_len, MIX_W)], 1)
        vv = jnp.concatenate([z_diff[:lat_tokens, 2 * MIX_W:].reshape(n_lat, n_tok, MIX_W),
                              cache_diff_v[:, l].reshape(n_lat, p_len, MIX_W)], 1)
        lkv = n_tok + p_len
        df_lat = _attention(q_r, 0, kk.reshape(n_lat * lkv, MIX_W), 0, vv.reshape(n_lat * lkv, MIX_W), 0, dl,
                            n_sub=2, lam_init=lam_init, n_seq=n_lat, lq=n_tok, lk=lkv, q_base=0, k_base=0,
                            tq=tq_lat, name="diff_latent")
        df_ctx = _attention(z_diff, 0, z_diff, 1, z_diff, 2, dl, n_sub=2, lam_init=lam_init, n_seq=n_ctx,
                            lq=l_ctx, lk=l_ctx, q_base=lat_tokens, k_base=lat_tokens, tq=l_ctx, name="diff_ctx")
        o_df = jnp.concatenate([df_lat, df_ctx], 0)

        gup = jnp.zeros((2, SMALL_W, MIX_W), F32)
        gup = gup.at[0, 0:GATE_RANK].set(gla_gate_up[l, 0]).at[1, GATE_RANK:2 * GATE_RANK].set(gla_gate_up[l, 1])
        gbias = gla_gate_bias[l][:, None, :]
        gla_consts = (gup, gbias, l_all, m_all, bm)
        glf_lat, glb_lat, _ = _scan(_gla_step, z_gla, z_small, gla_consts,
                                    _block_diag_state(state_gla[:, l], True), n_lat, n_tok, 0,
                                    GLA_SEQS_PER_STEP, "gla_scan")
        glf_ctx, glb_ctx, st_ctx = _scan(_gla_step, z_gla, z_small, gla_consts, zeros_state,
                                         n_ctx, l_ctx, lat_tokens, GLA_SEQS_PER_STEP, "gla_scan")
        gla_f = jnp.concatenate([glf_lat, glf_ctx], 0)
        gla_b = jnp.concatenate([glb_lat, glb_ctx], 0)

        qkv_n = _gdn_pre(z_gdn, edge, gdn_conv_w[l], bm16, tm)
        alog = jnp.repeat(gdn_A_log[l], HEAD_DIM, axis=-1)[:, None, :]
        dtb = jnp.repeat(gdn_dt_bias[l], HEAD_DIM, axis=-1)[:, None, :]
        gdn_consts = (ea, eb, alog, dtb, lsm, lbt, pm, bm, sm)
        gdf_lat, gdb_lat, _ = _scan(_gdn_step, qkv_n, z_small, gdn_consts,
                                    _block_diag_state(state_gdn[:, l], False), n_lat, n_tok, 0,
                                    GDN_SEQS_PER_STEP, "gdn_scan")
        gdf_ctx, gdb_ctx, s_ctx = _scan(_gdn_step, qkv_n, z_small, gdn_consts, zeros_state,
                                        n_ctx, l_ctx, lat_tokens, GDN_SEQS_PER_STEP, "gdn_scan")
        gdn_f = jnp.concatenate([gdf_lat, gdf_ctx], 0)
        gdn_b = jnp.concatenate([gdb_lat, gdb_ctx], 0)

        hn = jnp.stack([jnp.tile(gla_norm_w[l], HEADS), jnp.tile(gdn_norm_w[l], HEADS),
                        jnp.tile(diff_subln_w[l], HEADS)])[:, None, :]
        x = _merge(x, mod[l], norm1_w[l][None], o_na, gla_f, gla_b, z_gla, gdn_f, gdn_b, z_gdn, o_df, hn, bm16,
                   w_gate, w_branch[l].astype(BF16), w_out[l].astype(BF16), tm, mod_row, lam_init)
        w1 = w_e1[l].transpose(1, 0, 2).reshape(D_MODEL, N_EXPERTS * D_FF).astype(BF16)
        w3 = w_e3[l].transpose(1, 0, 2).reshape(D_MODEL, N_EXPERTS * D_FF).astype(BF16)
        w2 = w_e2[l].reshape(N_EXPERTS * D_FF, D_MODEL).astype(BF16)
        x = _moe(x, mod[l], norm2_w[l][None], wr, br, expand, w1, w3, w2, tm, mod_row)

        def ctx_heads(z, col):
            return z[lat_tokens:, col * MIX_W:(col + 1) * MIX_W].reshape(n_ctx, l_ctx, HEADS, HEAD_DIM)
        new_kv.append((ctx_heads(z_na, 1), ctx_heads(z_na, 2), ctx_heads(z_diff, 1), ctx_heads(z_diff, 2)))
        new_gla.append(_diag_blocks(st_ctx, True))
        new_gdn.append(_diag_blocks(s_ctx, False))

    y = _final_norm(x, final_norm_w[None], tm)
    y_sample = y[:lat_tokens].reshape(n_lat, n_tok, D_MODEL)
    y_prompt = y[lat_tokens:].reshape(n_ctx, l_ctx, D_MODEL)
    stack = lambda j: jnp.stack([t[j] for t in new_kv], axis=1)
    return (y_prompt, y_sample, stack(0), stack(1), stack(2), stack(3),
            jnp.stack(new_gla, axis=1), jnp.stack(new_gdn, axis=1))
```

```python
import functools
import math

import numpy as np
import jax
import jax.numpy as jnp
from jax import lax
from jax.experimental import pallas as pl
from jax.experimental.pallas import tpu as pltpu

F32 = jnp.float32
BF16 = jnp.bfloat16

D_MODEL = 1024
HEADS = 4
HEAD_DIM = 64
MIX_W = HEADS * HEAD_DIM
GRID_W = 64
NA_WIN_H = 8
NA_WIN_W = 16
GATE_RANK = 16
GATE_NORM = 16.0
GDN_CONV = 3
CHUNK = 64
N_EXPERTS = 16
N_GROUPS = 4
GROUP_E = N_EXPERTS // N_GROUPS
D_FF = D_MODEL // 4
ROPE_BASE = 10000.0
EPS = 1e-6
NEG = -1e30
LOG2E = 1.4426950408889634
SMALL_W = 128
V7X_VMEM_LIMIT = 56 * 1024 * 1024
LEAF = 8
ATTN_HEAD_GROUP = 2
NA_ROWS_PER_STEP = 4
GLA_SEQS_PER_STEP = 4
GDN_SEQS_PER_STEP = 2
MOE_TILE = 512

_IN_SPLITS = (
    ('na_q', MIX_W), ('na_k', MIX_W), ('na_v', MIX_W),
    ('gla_q', MIX_W), ('gla_k', MIX_W), ('gla_v', MIX_W),
    ('gla_gf', GATE_RANK), ('gla_gb', GATE_RANK), ('gla_og', MIX_W),
    ('gdn_qkv', 3 * MIX_W), ('gdn_af', HEADS), ('gdn_ab', HEADS),
    ('gdn_bf', HEADS), ('gdn_bb', HEADS), ('gdn_og', MIX_W),
    ('diff_q', MIX_W), ('diff_k', MIX_W), ('diff_v', MIX_W),
    ('branch_gate', 4 * D_MODEL),
)
_OFF = {}
_o = 0
for _n, _s in _IN_SPLITS:
    _OFF[_n] = (_o, _o + _s)
    _o += _s

_N_LVL = 6
_ROW_B = 2 * _N_LVL * CHUNK
_ROW_BL = _ROW_B + CHUNK
_ROW_TOT = _ROW_BL + CHUNK
_L_ROWS = _ROW_TOT + 16


def _params(sem):
    return pltpu.CompilerParams(dimension_semantics=sem, vmem_limit_bytes=V7X_VMEM_LIMIT)


def _dot(a, b):
    return jnp.dot(a, b, preferred_element_type=F32)


def _bdot(a, b):
    return _dot(a.astype(BF16), b.astype(BF16))


def _bdot_nt(a, b):
    return lax.dot_general(a.astype(BF16), b.astype(BF16), (((1,), (1,)), ((), ())),
                           preferred_element_type=F32)


def _bdot_tn(a, b):
    return lax.dot_general(a.astype(BF16), b.astype(BF16), (((0,), (0,)), ((), ())),
                           preferred_element_type=F32)


def _split(x):
    hi = x.astype(BF16)
    lo = (x - hi.astype(F32)).astype(BF16)
    return hi, lo


def _dot_rx(x, m):
    hi, lo = _split(x)
    return _dot(hi, m) + _dot(lo, m)


def _dot_lx(m, x):
    hi, lo = _split(x)
    return _dot(m, hi) + _dot(m, lo)


def _dot3(a, b):
    ah, al = _split(a)
    bh, bl = _split(b)
    return _dot(ah, bh) + _dot(ah, bl) + _dot(al, bh)


def _silu(x):
    return x * jax.nn.sigmoid(x)


def _softplus(x):
    return jnp.maximum(x, 0.0) + jnp.log(1.0 + jnp.exp(-jnp.abs(x)))


def _rms_rows(x, w):
    return x * lax.rsqrt(jnp.mean(x * x, axis=-1, keepdims=True) + EPS) * w


def _head_sum(x, bm16):
    return _dot_rx(x, bm16)


def _bd_rows(x, bm):
    return jnp.concatenate([x] * HEADS, axis=0) * bm


def _lane_mask(lo, width, n=MIX_W):
    lane = lax.broadcasted_iota(jnp.int32, (1, n), 1)
    return (lane >= lo) & (lane < lo + width)


def _chunk_consts():
    c = CHUNK
    i = np.arange(c)[:, None]
    t = np.arange(c)[None, :]
    blocks, masks = [], []
    for lvl in range(_N_LVL):
        s = 1 << lvl
        upper = ((i // s) % 2) == 1
        lq = upper & (t > (i // s) * s) & (t <= i)
        lk = (~upper) & (t > i) & (t <= ((i // s) + 1) * s)
        blocks += [lq, lk]
        masks.append(((i // (2 * s)) == (t // (2 * s))) & upper & (((t // s) % 2) == 0))
    masks.append(i == t)
    blocks += [t <= i, t > i]
    fwd_l = np.concatenate([b.astype(np.float32) for b in blocks], 0)
    rev_l = np.concatenate([b[::-1, ::-1].astype(np.float32) for b in blocks], 0)
    tot = np.ones((16, c), np.float32)
    l_all = np.stack([np.concatenate([fwd_l, tot], 0), np.concatenate([rev_l, tot], 0)])
    m_fwd = np.stack([np.tile(m.astype(np.float32), (1, HEADS)) for m in masks])
    m_rev = np.stack([np.tile(m[::-1, ::-1].astype(np.float32), (1, HEADS)) for m in masks])
    assert l_all.shape[1] == _L_ROWS
    return jnp.asarray(l_all, BF16), jnp.asarray(np.stack([m_fwd, m_rev]), F32)


def _gdn_consts():
    c = CHUNK
    i = np.arange(c)[:, None]
    t = np.arange(c)[None, :]
    lb = (t <= i).astype(np.float32)
    lbl = (t > i).astype(np.float32)
    tot = np.ones((16, c), np.float32)
    strict = (t < i).astype(np.float32)
    incl = (t <= i).astype(np.float32)
    rev = lambda a: a[::-1, ::-1]
    lsm = np.stack([np.concatenate([lb, lbl, tot], 0), np.concatenate([rev(lb), rev(lbl), tot], 0)])
    lbt = np.stack([np.tile(lb.T, (1, HEADS)), np.tile(rev(lb).T, (1, HEADS))])
    pm = np.stack([np.stack([np.tile(strict, (1, HEADS)), np.tile(incl, (1, HEADS))]),
                   np.stack([np.tile(rev(strict), (1, HEADS)), np.tile(rev(incl), (1, HEADS))])])
    return jnp.asarray(lsm, BF16), jnp.asarray(lbt, F32), jnp.asarray(pm, F32)


def _block_mask():
    r = np.arange(MIX_W)
    return (r[:, None] // HEAD_DIM == r[None, :] // HEAD_DIM).astype(np.float32)


def _solve_masks():
    r = np.arange(MIX_W)[:, None]
    c = np.arange(MIX_W)[None, :]
    out = [(r // LEAF == c // LEAF)]
    s = LEAF
    while s < CHUNK:
        out.append((r // (2 * s) == c // (2 * s)) & (r // s != c // s))
        s *= 2
    return jnp.asarray(np.stack(out).astype(np.float32))


def _rope_consts(n_tok):
    t = np.arange(n_tok)
    pos = np.stack([t // GRID_W, t % GRID_W], 0).astype(np.float32)
    lane = np.arange(MIX_W)
    u = lane % 32
    axis = u // 16
    w = u % 16
    first = w < 8
    inv = ROPE_BASE ** (-(w % 8).astype(np.float32) / 8.0)
    ang = pos[axis, :].T * inv[None, :]
    cos = np.cos(ang)
    sin = np.sin(ang) * np.where(first, -1.0, 1.0)[None, :]
    partner = np.where(first, lane + 8, lane - 8)
    perm = np.zeros((MIX_W, MIX_W), np.float32)
    perm[partner, lane] = 1.0
    return jnp.asarray(cos, F32), jnp.asarray(sin, F32), jnp.asarray(perm, BF16)


def _ada_kernel(c_ref, w_ref, b_ref, o_ref):
    c = c_ref[...]
    o_ref[0] = _bdot(_silu(c), w_ref[0]) + b_ref[0]


def _ada(cond, w_ada, b_ada):
    depth, _, n = w_ada.shape
    tn = 1536
    return pl.pallas_call(
        _ada_kernel,
        grid=(depth, n // tn),
        in_specs=[pl.BlockSpec((8, D_MODEL), lambda l, j: (0, 0)),
                  pl.BlockSpec((1, D_MODEL, tn), lambda l, j: (l, 0, j)),
                  pl.BlockSpec((1, 1, tn), lambda l, j: (l, 0, j))],
        out_specs=pl.BlockSpec((1, 8, tn), lambda l, j: (l, 0, j)),
        out_shape=jax.ShapeDtypeStruct((depth, 8, n), F32),
        compiler_params=_params(("parallel", "parallel")),
        name="ada_mod",
    )(cond, w_ada, b_ada.reshape(depth, 1, n))


def _proj_kernel(xl_ref, xc_ref, mod_ref, nw_ref, w_ref, o_na, o_gla, o_gdn, o_diff, o_small, *, n_lat_tiles):
    mod = mod_ref[0]
    x = jnp.where(pl.program_id(0) < n_lat_tiles, xl_ref[...], xc_ref[...])
    h = _rms_rows(x, nw_ref[...]) * (1.0 + mod[1:2]) + mod[0:1]
    hb = h.astype(BF16)
    o_na[...] = _dot(hb, w_ref[:, 0:768])
    o_gla[...] = _dot(hb, w_ref[:, 768:1792])
    o_gdn[...] = _dot(hb, w_ref[:, 1792:2816])
    o_diff[...] = _dot(hb, w_ref[:, 2816:3584])
    o_small[...] = _dot(hb, w_ref[:, 3584:3712])


def _stream_specs(xs, tm):
    _, _, ctx_first, n_lat_tiles = xs
    return [pl.BlockSpec((tm, D_MODEL), lambda i: (jnp.minimum(i, n_lat_tiles - 1), 0)),
            pl.BlockSpec((tm, D_MODEL), lambda i: (ctx_first + jnp.maximum(i - n_lat_tiles, 0), 0))]


def _proj(xs, t, mod_l, nw, wcat, tm, mod_row):
    widths = (768, 1024, 1024, 768, SMALL_W)
    return pl.pallas_call(
        functools.partial(_proj_kernel, n_lat_tiles=xs[3]),
        grid=(t // tm,),
        in_specs=_stream_specs(xs, tm) + [
                  pl.BlockSpec((1, 6, D_MODEL), lambda i: (mod_row(i), 0, 0)),
                  pl.BlockSpec((1, D_MODEL), lambda i: (0, 0)),
                  pl.BlockSpec(wcat.shape, lambda i: (0, 0))],
        out_specs=[pl.BlockSpec((tm, w), lambda i: (i, 0)) for w in widths],
        out_shape=[jax.ShapeDtypeStruct((t, w), F32) for w in widths],
        compiler_params=_params(("parallel",)),
        name="in_proj",
    )(xs[0], xs[1], mod_l, nw, wcat)


def _attn_core(q, k, v, n_sub, lam):
    kb = k.astype(BF16)
    vb = v.astype(BF16)
    tq = q.shape[0]
    sub_w = HEAD_DIM // n_sub
    scale = sub_w ** -0.5 * LOG2E
    out = jnp.zeros(q.shape, F32)
    for h0 in range(0, HEADS, ATTN_HEAD_GROUP):
        maps = [(h, m) for h in range(h0, h0 + ATTN_HEAD_GROUP) for m in range(n_sub)]
        qs = jnp.concatenate(
            [(q * jnp.where(_lane_mask(h * HEAD_DIM + m * sub_w, sub_w), scale, 0.0)).astype(BF16)
             for h, m in maps], axis=0)
        s = lax.dot_general(qs, kb, (((1,), (1,)), ((), ())), preferred_element_type=F32)
        e = jnp.exp2(s - jnp.max(s, axis=-1, keepdims=True))
        inv = 1.0 / jnp.sum(e, axis=-1, keepdims=True)
        if n_sub == 1:
            p = e * inv
        else:
            p = jnp.concatenate(
                [e[(2 * i) * tq:(2 * i + 1) * tq] * inv[(2 * i) * tq:(2 * i + 1) * tq]
                 - e[(2 * i + 1) * tq:(2 * i + 2) * tq] * (lam * inv[(2 * i + 1) * tq:(2 * i + 2) * tq])
                 for i in range(ATTN_HEAD_GROUP)], axis=0)
        o_all = _dot(p.astype(BF16), vb)
        for i in range(ATTN_HEAD_GROUP):
            out = jnp.where(_lane_mask((h0 + i) * HEAD_DIM, HEAD_DIM), o_all[i * tq:(i + 1) * tq], out)
    return out


def _diff_lambda(dl, lam_init):
    a = jnp.sum(dl[0:1] * dl[1:2], axis=-1, keepdims=True)
    b = jnp.sum(dl[2:3] * dl[3:4], axis=-1, keepdims=True)
    return jnp.exp(a) - jnp.exp(b) + lam_init


def _attn_kernel(q_ref, k_ref, v_ref, dl_ref, o_ref, *, n_sub, lam_init):
    lam = _diff_lambda(dl_ref[...], lam_init) if n_sub == 2 else None
    o_ref[...] = _attn_core(q_ref[...], k_ref[...], v_ref[...], n_sub, lam)


def _attention(q_arr, q_col, k_arr, k_col, v_arr, v_col, dl, *, n_sub, lam_init, n_seq, lq, lk,
               q_base, k_base, tq, name):
    nq = lq // tq
    return pl.pallas_call(
        functools.partial(_attn_kernel, n_sub=n_sub, lam_init=lam_init),
        grid=(n_seq, nq),
        in_specs=[pl.BlockSpec((tq, MIX_W), lambda s, j: (q_base // tq + s * nq + j, q_col)),
                  pl.BlockSpec((lk, MIX_W), lambda s, j: (k_base // lk + s, k_col)),
                  pl.BlockSpec((lk, MIX_W), lambda s, j: (k_base // lk + s, v_col)),
                  pl.BlockSpec(dl.shape, lambda s, j: (0, 0))],
        out_specs=pl.BlockSpec((tq, MIX_W), lambda s, j: (s * nq + j, 0)),
        out_shape=jax.ShapeDtypeStruct((n_seq * lq, MIX_W), F32),
        compiler_params=_params(("parallel", "parallel")),
        name=name,
    )(q_arr, k_arr, v_arr, dl)


def _na_lat_kernel(q_ref, k_ref, v_ref, ck_ref, cv_ref, bias_ref, o_ref, *, rows, kh, rps):
    g = pl.program_id(1)
    ck = ck_ref[...].astype(BF16)
    cv = cv_ref[...].astype(BF16)
    scale = HEAD_DIM ** -0.5 * LOG2E
    head_scale = [jnp.where(_lane_mask(h * HEAD_DIM, HEAD_DIM), scale, 0.0) for h in range(HEADS)]
    hw = HEADS * GRID_W
    qs = []
    for j in range(rps):
        qj = q_ref[j * GRID_W:(j + 1) * GRID_W, :]
        qs.append(jnp.concatenate([(qj * hs).astype(BF16) for hs in head_scale], axis=0))
    s_ctx_all = lax.dot_general(jnp.concatenate(qs, axis=0), ck, (((1,), (1,)), ((), ())),
                                preferred_element_type=F32)
    p_ctx, o_loc = [], []
    for j in range(rps):
        r = g * rps + j
        start = jnp.clip(r - kh // 2, 0, rows - kh)
        cls = r - start
        ws = pl.multiple_of(start * GRID_W, GRID_W)
        kw = k_ref[pl.ds(ws, kh * GRID_W), :].astype(BF16)
        vw = v_ref[pl.ds(ws, kh * GRID_W), :].astype(BF16)
        bias = jnp.concatenate([bias_ref[h, pl.ds(cls, 1)][0] for h in range(HEADS)], axis=0)
        s_loc = lax.dot_general(qs[j], kw, (((1,), (1,)), ((), ())), preferred_element_type=F32) + bias
        s_ctx = s_ctx_all[j * hw:(j + 1) * hw]
        mx = jnp.maximum(jnp.max(s_loc, axis=-1, keepdims=True), jnp.max(s_ctx, axis=-1, keepdims=True))
        e_loc = jnp.exp2(s_loc - mx)
        e_ctx = jnp.exp2(s_ctx - mx)
        inv = 1.0 / (jnp.sum(e_loc, axis=-1, keepdims=True) + jnp.sum(e_ctx, axis=-1, keepdims=True))
        p_ctx.append((e_ctx * inv).astype(BF16))
        o_loc.append(_dot((e_loc * inv).astype(BF16), vw))
    o_ctx_all = _dot(jnp.concatenate(p_ctx, axis=0), cv)
    for j in range(rps):
        o_all = o_loc[j] + o_ctx_all[j * hw:(j + 1) * hw]
        out = jnp.zeros((GRID_W, MIX_W), F32)
        for h in range(HEADS):
            out = jnp.where(_lane_mask(h * HEAD_DIM, HEAD_DIM), o_all[h * GRID_W:(h + 1) * GRID_W], out)
        o_ref[j * GRID_W:(j + 1) * GRID_W, :] = out


def _na_bias_table(rpb, kh):
    cidx = np.arange(GRID_W)
    cls = np.arange(kh)
    drow = np.arange(kh)[None, :] - cls[:, None] + NA_WIN_H - 1
    col_start = np.clip(cidx - NA_WIN_W // 2, 0, GRID_W - NA_WIN_W)
    col_ok = (cidx[None, :] >= col_start[:, None]) & (cidx[None, :] < col_start[:, None] + NA_WIN_W)
    dcol = np.clip(cidx[None, :] - cidx[:, None], 1 - NA_WIN_W, NA_WIN_W - 1) + NA_WIN_W - 1
    toep = rpb.astype(F32)[..., dcol]
    bias = jnp.stack([toep[:, drow[c, 0]:drow[c, 0] + kh] for c in range(kh)], axis=1)
    bias = bias.transpose(0, 1, 3, 2, 4)
    bias = jnp.where(col_ok[:, None, :], bias * LOG2E, NEG)
    return bias.reshape(HEADS, kh, GRID_W, kh * GRID_W)


def _na_latent(z_na, ck, cv, bias, n_b, n_tok, p_len):
    rows = n_tok // GRID_W
    kh = min(NA_WIN_H, rows)
    rps = NA_ROWS_PER_STEP
    assert rows % rps == 0
    steps = rows // rps
    return pl.pallas_call(
        functools.partial(_na_lat_kernel, rows=rows, kh=kh, rps=rps),
        grid=(n_b, steps),
        in_specs=[pl.BlockSpec((rps * GRID_W, MIX_W), lambda b, r: (b * steps + r, 0)),
                  pl.BlockSpec((n_tok, MIX_W), lambda b, r: (b, 1)),
                  pl.BlockSpec((n_tok, MIX_W), lambda b, r: (b, 2)),
                  pl.BlockSpec((p_len, MIX_W), lambda b, r: (b, 0)),
                  pl.BlockSpec((p_len, MIX_W), lambda b, r: (b, 0)),
                  pl.BlockSpec(bias.shape, lambda b, r: (0, 0, 0, 0))],
        out_specs=pl.BlockSpec((rps * GRID_W, MIX_W), lambda b, r: (b * steps + r, 0)),
        out_shape=jax.ShapeDtypeStruct((n_b * n_tok, MIX_W), F32),
        compiler_params=_params(("parallel", "arbitrary")),
        name="na_latent",
    )(z_na, z_na, z_na, ck, cv, bias)


def _rope_kernel(q_ref, k_ref, cos_ref, sin_ref, p_ref, qo_ref, ko_ref):
    cos = cos_ref[...]
    sin = sin_ref[...]
    p = p_ref[...]
    q = q_ref[...]
    k = k_ref[...]
    qo_ref[...] = q * cos + _dot_rx(q, p) * sin
    ko_ref[...] = k * cos + _dot_rx(k, p) * sin


def _rope(z_diff, cos, sin, perm, n_b, n_tok, tm):
    nt = n_tok // tm
    spec_t = pl.BlockSpec((tm, MIX_W), lambda i: (i % nt, 0))
    return pl.pallas_call(
        _rope_kernel,
        grid=(n_b * nt,),
        in_specs=[pl.BlockSpec((tm, MIX_W), lambda i: (i, 0)),
                  pl.BlockSpec((tm, MIX_W), lambda i: (i, 1)),
                  spec_t, spec_t,
                  pl.BlockSpec(perm.shape, lambda i: (0, 0))],
        out_specs=[pl.BlockSpec((tm, MIX_W), lambda i: (i, 0))] * 2,
        out_shape=[jax.ShapeDtypeStruct((n_b * n_tok, MIX_W), F32)] * 2,
        compiler_params=_params(("parallel",)),
        name="diff_rope",
    )(z_diff, z_diff, cos, sin, perm)


def _gdn_pre_kernel(x_ref, prev_ref, next_ref, edge_ref, w_ref, bm_ref, o_ref, *, tm):
    x = x_ref[...]
    w = w_ref[...]
    edge = edge_ref[...]
    row = lax.broadcasted_iota(jnp.int32, (tm, 1), 0)
    x_prev = jnp.where(row == 0, prev_ref[7:8, :], pltpu.roll(x, 1, 0)) * edge[:, 0:1]
    x_next = jnp.where(row == tm - 1, next_ref[0:1, :], pltpu.roll(x, tm - 1, 0)) * edge[:, 1:2]
    y = _silu(x_prev * w[0:1] + x * w[1:2] + x_next * w[2:3])
    bm16 = bm_ref[...]
    q = y[:, 0:MIX_W]
    k = y[:, MIX_W:2 * MIX_W]
    o_ref[:, 0:MIX_W] = q * lax.rsqrt(_head_sum(q * q, bm16) + EPS) * (HEAD_DIM ** -0.5)
    o_ref[:, MIX_W:2 * MIX_W] = k * lax.rsqrt(_head_sum(k * k, bm16) + EPS)
    o_ref[:, 2 * MIX_W:3 * MIX_W] = y[:, 2 * MIX_W:3 * MIX_W]


def _seq_edges(lat_tokens, n_tok, ctx_tokens, l_ctx):
    pos = np.concatenate([np.arange(lat_tokens) % n_tok, np.arange(ctx_tokens) % l_ctx])
    seq = np.concatenate([np.full(lat_tokens, n_tok), np.full(ctx_tokens, l_ctx)])
    edge = np.zeros((lat_tokens + ctx_tokens, 8), np.float32)
    edge[:, 0] = pos != 0
    edge[:, 1] = pos != seq - 1
    return jnp.asarray(edge)


def _gdn_pre(z_gdn, edge, conv_w, bm16, tm):
    t = z_gdn.shape[0]
    w3 = 3 * MIX_W
    nb8 = t // 8
    return pl.pallas_call(
        functools.partial(_gdn_pre_kernel, tm=tm),
        grid=(t // tm,),
        in_specs=[pl.BlockSpec((tm, w3), lambda i: (i, 0)),
                  pl.BlockSpec((8, w3), lambda i: (jnp.maximum(i * (tm // 8) - 1, 0), 0)),
                  pl.BlockSpec((8, w3), lambda i: (jnp.minimum((i + 1) * (tm // 8), nb8 - 1), 0)),
                  pl.BlockSpec((tm, 8), lambda i: (i, 0)),
                  pl.BlockSpec(conv_w.shape, lambda i: (0, 0)),
                  pl.BlockSpec(bm16.shape, lambda i: (0, 0))],
        out_specs=pl.BlockSpec((tm, w3), lambda i: (i, 0)),
        out_shape=jax.ShapeDtypeStruct((t, w3), F32),
        compiler_params=_params(("parallel",)),
        name="gdn_pre",
    )(z_gdn, z_gdn, z_gdn, edge, conv_w, bm16)


def _hier_scores(q, k, f, m_ref, d, bm):
    out = _bdot_nt(q, _bd_rows(k, bm)) * m_ref[d, _N_LVL]
    for lvl in range(_N_LVL):
        yield
        fq = f[2 * lvl * CHUNK:(2 * lvl + 1) * CHUNK]
        fk = f[(2 * lvl + 1) * CHUNK:(2 * lvl + 2) * CHUNK]
        out = out + _bdot_nt(q * fq, _bd_rows(k * fk, bm)) * m_ref[d, lvl]
    return out


def _gla_chain(q, k, v, zs, gup, gb, l16, m_ref, d, bm, st):
    q = q * (HEAD_DIM ** -0.5)
    gp = _bdot(zs, gup) + gb
    g = (jnp.minimum(gp, 0.0) - jnp.log(1.0 + jnp.exp(-jnp.abs(gp)))) * (1.0 / GATE_NORM)
    yield
    f = jnp.exp(_dot_lx(l16, g))
    yield
    att = yield from _hier_scores(q, k, f, m_ref, d, bm)
    fb = f[_ROW_B:_ROW_B + CHUNK]
    fbl = f[_ROW_BL:_ROW_BL + CHUNK]
    ftot = f[_ROW_TOT:_ROW_TOT + 1]
    yield
    o = _bdot_nt(q * fb, st) + _bdot(att, _bd_rows(v, bm))
    yield
    return o, st * ftot + _bdot_tn(v, k * fbl) * bm


def _tri_inverse(mbd, sm_ref):
    r = lax.broadcasted_iota(jnp.int32, (MIX_W, MIX_W), 0)
    c = lax.broadcasted_iota(jnp.int32, (MIX_W, MIX_W), 1)
    eye = jnp.where(r == c, 1.0, 0.0)
    md = mbd * sm_ref[0]
    m2 = _bdot(md, md)
    yield
    m4 = _bdot(m2, m2)
    t = eye - md
    t = t + _bdot(t, m2)
    yield
    t = t + _bdot(t, m4)
    for lvl in range(1, sm_ref.shape[0]):
        yield
        tm = _bdot(t, mbd * sm_ref[lvl])
        yield
        t = t - _bdot(tm, t)
    return t


def _gdn_chain(q, k, v, zs, ea, eb, alog, dtb, lsm, lbt, pm_ref, d, bm, sm_ref, s):
    g = -jnp.exp(alog) * _softplus(_dot_rx(zs, ea) + dtb)
    beta = jax.nn.sigmoid(_dot_rx(zs, eb))
    yield
    e = _dot_lx(lsm, g)
    dcol = e[0:CHUNK]
    drow = _dot_lx(jnp.ones((8, CHUNK), BF16), g * lbt)[0:1]
    kb = k * beta
    s2 = _bdot_nt(jnp.concatenate([kb, q], axis=0), _bd_rows(k, bm))
    yield
    dec = jnp.exp(jnp.where(pm_ref[d, 1] > 0.0, dcol - drow, NEG))
    m_cat = s2[0:CHUNK] * dec * pm_ref[d, 0]
    att = s2[CHUNK:2 * CHUNK] * dec
    t_inv = yield from _tri_inverse(_bd_rows(m_cat, bm), sm_ref)
    fb = jnp.exp(dcol)
    fbl = jnp.exp(e[CHUNK:2 * CHUNK])
    ftot = jnp.exp(e[2 * CHUNK:2 * CHUNK + 1])
    yield
    u_bd = _bdot(t_inv, _bd_rows(v * beta, bm))
    w_bd = _bdot(t_inv, _bd_rows(kb * fb, bm))
    yield
    v_new = u_bd - _bdot(w_bd, s)
    yield
    o = _bdot(q * fb, s) + _bdot(att, v_new)
    return o, s * ftot + _bdot_tn(_bd_rows(k * fbl, bm), v_new)


def _scan_kernel(*refs, chain, n_const, nb, n_chunks):
    n_tok = nb * 2 * 4
    tok = refs[:n_tok]
    consts = refs[n_tok:n_tok + n_const]
    s0_ref, of_ref, ob_ref, sf_ref, s_scr = refs[n_tok + n_const:]
    n = pl.program_id(1)

    @pl.when(n == 0)
    def _():
        s_scr[...] = s0_ref[...]

    ids = [(j, d) for j in range(nb) for d in range(2)]
    gens = []
    for j, d in ids:
        q_ref, k_ref, v_ref, zs_ref = tok[(j * 2 + d) * 4:(j * 2 + d) * 4 + 4]
        gens.append(chain(q_ref[...], k_ref[...], v_ref[...], zs_ref[...], consts, d, s_scr[j, d]))
    live = list(range(len(gens)))
    while live:
        for c in list(live):
            try:
                next(gens[c])
            except StopIteration as done:
                j, d = ids[c]
                o, s_new = done.value
                (of_ref, ob_ref)[d][j] = o
                s_scr[j, d] = s_new
                live.remove(c)

    @pl.when(n == n_chunks - 1)
    def _():
        sf_ref[...] = s_scr[...]


def _gla_step(q, k, v, zs, consts, d, st):
    gup_ref, gb_ref, l_ref, m_ref, bm_ref = consts
    return _gla_chain(q, k, v, zs, gup_ref[d], gb_ref[d], l_ref[d], m_ref, d, bm_ref[...], st)


def _gdn_step(q, k, v, zs, consts, d, s):
    ea_ref, eb_ref, alog_ref, dtb_ref, lsm_ref, lbt_ref, pm_ref, bm_ref, sm_ref = consts
    return _gdn_chain(q, k, v, zs, ea_ref[d], eb_ref[d], alog_ref[d], dtb_ref[d], lsm_ref[d], lbt_ref[d],
                      pm_ref, d, bm_ref[...], sm_ref, s)


def _scan(chain, qkv, z_small, consts, s0, n_seq, l_seq, base, seqs_per_step, name):
    nc = l_seq // CHUNK
    nb = seqs_per_step if n_seq % seqs_per_step == 0 else 1

    def chunk(n, d):
        return n + d * (nc - 1 - 2 * n)

    in_specs, args = [], []
    for j in range(nb):
        for d in range(2):
            for arr, col, width in ((qkv, 0, MIX_W), (qkv, 1, MIX_W), (qkv, 2, MIX_W), (z_small, 0, SMALL_W)):
                in_specs.append(pl.BlockSpec(
                    (CHUNK, width),
                    functools.partial(lambda g, n, j, d, col: (base // CHUNK + (g * nb + j) * nc + chunk(n, d), col),
                                      j=j, d=d, col=col)))
                args.append(arr)
    for cst in consts:
        in_specs.append(pl.BlockSpec(cst.shape, functools.partial(lambda g, n, nd: (0,) * nd, nd=cst.ndim)))
        args.append(cst)
    state_spec = pl.BlockSpec((nb, 2, MIX_W, MIX_W), lambda g, n: (g, 0, 0, 0))
    in_specs.append(state_spec)
    args.append(s0)
    o_f, o_b, s_f = pl.pallas_call(
        functools.partial(_scan_kernel, chain=chain, n_const=len(consts), nb=nb, n_chunks=nc),
        grid=(n_seq // nb, nc),
        in_specs=in_specs,
        out_specs=[pl.BlockSpec((nb, CHUNK, MIX_W), lambda g, n: (g, chunk(n, 0), 0)),
                   pl.BlockSpec((nb, CHUNK, MIX_W), lambda g, n: (g, chunk(n, 1), 0)),
                   state_spec],
        out_shape=[jax.ShapeDtypeStruct((n_seq, l_seq, MIX_W), F32),
                   jax.ShapeDtypeStruct((n_seq, l_seq, MIX_W), F32),
                   jax.ShapeDtypeStruct((n_seq, 2, MIX_W, MIX_W), F32)],
        scratch_shapes=[pltpu.VMEM((nb, 2, MIX_W, MIX_W), F32)],
        compiler_params=_params(("parallel", "arbitrary")),
        name=name,
    )(*args)
    return o_f.reshape(n_seq * l_seq, MIX_W), o_b.reshape(n_seq * l_seq, MIX_W), s_f


def _merge_kernel(xl_ref, xc_ref, mod_ref, n1_ref, na_l, na_c, glaf_l, glaf_c, glab_l, glab_c, glag_ref, gdnf_l, gdnf_c,
                  gdnb_l, gdnb_c, gdng_ref, df_l, df_c, hn_ref, bm_ref, wg_ref, wb_ref, wo_ref, o_ref, *,
                  lam_init, n_lat_tiles):
    is_lat = pl.program_id(0) < n_lat_tiles

    def pick(lat_ref, ctx_ref):
        return jnp.where(is_lat, lat_ref[...], ctx_ref[...])

    mod = mod_ref[0]
    x = pick(xl_ref, xc_ref)
    h = _rms_rows(x, n1_ref[...]) * (1.0 + mod[1:2]) + mod[0:1]
    hb = h.astype(BF16)
    bm16 = bm_ref[...]

    def head_norm(o, w):
        return o * lax.rsqrt(_head_sum(o * o, bm16) * (1.0 / HEAD_DIM) + EPS) * w

    branches = (
        pick(na_l, na_c),
        head_norm(pick(glaf_l, glaf_c) + pick(glab_l, glab_c), hn_ref[0]) * _silu(glag_ref[...]),
        head_norm(pick(gdnf_l, gdnf_c) + pick(gdnb_l, gdnb_c), hn_ref[1]) * _silu(gdng_ref[...]),
        head_norm(pick(df_l, df_c), hn_ref[2]) * (1.0 - lam_init),
    )
    merged = None
    for n, br in enumerate(branches):
        gate = jax.nn.sigmoid(_dot(hb, wg_ref[:, n * D_MODEL:(n + 1) * D_MODEL]))
        term = gate * _bdot(br, wb_ref[n])
        merged = term if merged is None else merged + term
    o_ref[...] = x + mod[2:3] * _bdot(merged, wo_ref[...])


def _merge(xs, t, mod_l, n1, na, gla_f, gla_b, z_gla, gdn_f, gdn_b, z_gdn, df, hn, bm16, wg, wb, wo, tm, mod_row,
           lam_init, n_lat_tiles):
    tok = lambda col: pl.BlockSpec((tm, MIX_W), lambda i: (i, col))
    lat = pl.BlockSpec((tm, MIX_W), lambda i: (jnp.minimum(i, n_lat_tiles - 1), 0))
    ctx = pl.BlockSpec((tm, MIX_W), lambda i: (jnp.maximum(i - n_lat_tiles, 0), 0))
    full = lambda a: pl.BlockSpec(a.shape, lambda i: (0,) * a.ndim)
    return pl.pallas_call(
        functools.partial(_merge_kernel, lam_init=lam_init, n_lat_tiles=n_lat_tiles),
        grid=(t // tm,),
        in_specs=_stream_specs(xs, tm) + [
                  pl.BlockSpec((1, 6, D_MODEL), lambda i: (mod_row(i), 0, 0)),
                  full(n1), lat, ctx, lat, ctx, lat, ctx, tok(3), lat, ctx, lat, ctx, tok(3), lat, ctx,
                  full(hn), full(bm16), full(wg), full(wb), full(wo)],
        out_specs=pl.BlockSpec((tm, D_MODEL), lambda i: (i, 0)),
        out_shape=jax.ShapeDtypeStruct((t, D_MODEL), F32),
        compiler_params=_params(("parallel",)),
        name="merge_out",
    )(xs[0], xs[1], mod_l, n1, *na, *gla_f, *gla_b, z_gla, *gdn_f, *gdn_b, z_gdn, *df, hn, bm16, wg, wb, wo)


def _route(h2, wr_ref, br_ref):
    lane_i = lax.broadcasted_iota(jnp.int32, (1, SMALL_W), 1)
    lane = lane_i.astype(F32)
    gid = lax.shift_right_logical(lane_i, 2).astype(F32)
    s = jax.nn.sigmoid(_dot3(h2, wr_ref[...]))
    sel = s + br_ref[...]
    far = float(SMALL_W)

    def first_max(vals):
        mx = jnp.max(vals, axis=-1, keepdims=True)
        idx = jnp.min(jnp.where(vals == mx, lane, far), axis=-1, keepdims=True)
        return mx, idx

    best = None
    for g in range(N_GROUPS):
        vals = jnp.where(gid == float(g), sel, NEG)
        a, ia = first_max(vals)
        b, _ = first_max(jnp.where(lane == ia, NEG, vals))
        score = a + b
        if best is None:
            best, gi = score, jnp.zeros_like(score)
        else:
            better = score > best
            gi = jnp.where(better, float(g), gi)
            best = jnp.where(better, score, best)
    vals = jnp.where(gid == gi, sel, NEG)
    _, i1 = first_max(vals)
    _, i2 = first_max(jnp.where(lane == i1, NEG, vals))
    s1 = jnp.sum(jnp.where(lane == i1, s, 0.0), axis=-1, keepdims=True)
    s2 = jnp.sum(jnp.where(lane == i2, s, 0.0), axis=-1, keepdims=True)
    inv = 1.0 / (s1 + s2)
    return jnp.where(lane == i1, s1 * inv, 0.0) + jnp.where(lane == i2, s2 * inv, 0.0)


def _moe_kernel(x_ref, mod_ref, n2_ref, wr_ref, br_ref, w1_ref, w3_ref, w2_ref, o_ref,
                h_scr, g_scr, acc_scr):
    g = pl.program_id(1)
    mod = mod_ref[0]

    @pl.when(g == 0)
    def _():
        h2 = _rms_rows(x_ref[...], n2_ref[...]) * (1.0 + mod[4:5]) + mod[3:4]
        h_scr[...] = h2.astype(BF16)
        g_scr[...] = _route(h2, wr_ref, br_ref)
        acc_scr[...] = jnp.zeros_like(acc_scr)

    hb = h_scr[...]
    gates = g_scr[...]
    lane = lax.broadcasted_iota(jnp.int32, (1, SMALL_W), 1)
    he = jnp.concatenate(
        [_silu(_dot(hb, w1_ref[e])) * _dot(hb, w3_ref[e])
         * jnp.sum(jnp.where(lane == g * GROUP_E + e, gates, 0.0), axis=-1, keepdims=True)
         for e in range(GROUP_E)], axis=1)
    acc_scr[...] += _dot(he.astype(BF16), w2_ref[...])

    @pl.when(g == N_GROUPS - 1)
    def _():
        o_ref[...] = x_ref[...] + mod[5:6] * acc_scr[...]


def _moe(x, mod_l, n2, wr, br, w1, w3, w2, tm, mod_row):
    t = x.shape[0]
    gw = GROUP_E * D_FF
    return pl.pallas_call(
        _moe_kernel,
        grid=(t // tm, N_GROUPS),
        in_specs=[pl.BlockSpec((tm, D_MODEL), lambda i, g: (i, 0)),
                  pl.BlockSpec((1, 6, D_MODEL), lambda i, g: (mod_row(i), 0, 0)),
                  pl.BlockSpec(n2.shape, lambda i, g: (0, 0)),
                  pl.BlockSpec(wr.shape, lambda i, g: (0, 0)),
                  pl.BlockSpec(br.shape, lambda i, g: (0, 0)),
                  pl.BlockSpec((GROUP_E, D_MODEL, D_FF), lambda i, g: (g, 0, 0)),
                  pl.BlockSpec((GROUP_E, D_MODEL, D_FF), lambda i, g: (g, 0, 0)),
                  pl.BlockSpec((gw, D_MODEL), lambda i, g: (g, 0))],
        out_specs=pl.BlockSpec((tm, D_MODEL), lambda i, g: (i, 0)),
        out_shape=jax.ShapeDtypeStruct((t, D_MODEL), F32),
        scratch_shapes=[pltpu.VMEM((tm, D_MODEL), BF16), pltpu.VMEM((tm, SMALL_W), F32),
                        pltpu.VMEM((tm, D_MODEL), F32)],
        compiler_params=_params(("parallel", "arbitrary")),
        name="moe",
    )(x, mod_l, n2, wr, br, w1, w3, w2)


def _final_kernel(x_ref, w_ref, o_ref):
    o_ref[...] = _rms_rows(x_ref[...], w_ref[...])


def _final_norm(x, w, tm, first_tile, n_tiles):
    return pl.pallas_call(
        _final_kernel,
        grid=(n_tiles,),
        in_specs=[pl.BlockSpec((tm, D_MODEL), lambda i: (first_tile + i, 0)),
                  pl.BlockSpec(w.shape, lambda i: (0, 0))],
        out_specs=pl.BlockSpec((tm, D_MODEL), lambda i: (i, 0)),
        out_shape=jax.ShapeDtypeStruct((n_tiles * tm, D_MODEL), F32),
        compiler_params=_params(("parallel",)),
        name="final_norm",
    )(x, w)


def _block_diag_state(s, transpose):
    if transpose:
        s = jnp.swapaxes(s, -1, -2)
    eye = jnp.eye(HEADS, dtype=s.dtype)
    out = jnp.einsum('bdhij,hg->bdhigj', s, eye)
    return out.reshape(s.shape[0], 2, MIX_W, MIX_W)


def _diag_blocks(s_bd, transpose):
    b = s_bd.shape[0]
    s = s_bd.reshape(b, 2, HEADS, HEAD_DIM, HEADS, HEAD_DIM)
    s = jnp.stack([s[:, :, h, :, h, :] for h in range(HEADS)], axis=2)
    return jnp.swapaxes(s, -1, -2) if transpose else s


def _pick_tile(*lengths):
    for tm in (512, 256, 128, 64):
        if all(n % tm == 0 for n in lengths):
            return tm
    raise ValueError("token counts must be multiples of 64")


def kernel(x_prompt, x_sample, cache_na_k, cache_na_v, cache_diff_k, cache_diff_v, state_gla, state_gdn, c, c_ctx, w_ada, b_ada, norm1_w, norm2_w, w_in, na_rpb, gla_gate_up, gla_gate_bias, gla_norm_w, gdn_conv_w, gdn_A_log, gdn_dt_bias, gdn_norm_w, diff_lambda, diff_subln_w, w_branch, w_out, w_router, b_router, w_e1, w_e3, w_e2, final_norm_w):
    n_ctx, l_ctx, _ = x_prompt.shape
    n_lat, n_tok, _ = x_sample.shape
    depth = w_ada.shape[0]
    p_len = cache_na_k.shape[2]
    lat_tokens = n_lat * n_tok
    ctx_tokens = n_ctx * l_ctx
    assert n_lat + 1 <= 8 and n_tok % l_ctx == 0 and n_tok % GRID_W == 0 and l_ctx % CHUNK == 0
    tm = _pick_tile(n_tok, ctx_tokens)
    tq_lat = _pick_tile(n_tok) // 4 if _pick_tile(n_tok) >= 512 else 64

    def mod_row(i):
        return jnp.where(i * tm < lat_tokens, (i * tm) // n_tok, n_lat)

    tm_moe = MOE_TILE if (n_tok % MOE_TILE == 0 and ctx_tokens % MOE_TILE == 0) else tm

    def mod_row_moe(i):
        return jnp.where(i * tm_moe < lat_tokens, (i * tm_moe) // n_tok, n_lat)

    tokens = lat_tokens + ctx_tokens
    xs = (x_sample.reshape(lat_tokens, D_MODEL), x_prompt.reshape(ctx_tokens, D_MODEL), 0, lat_tokens // tm)
    cond = jnp.zeros((8, D_MODEL), F32).at[:n_lat].set(c).at[n_lat].set(c_ctx)
    mod = _ada(cond, w_ada, b_ada).reshape(depth, 8, 6, D_MODEL)

    l_all, m_all = _chunk_consts()
    lsm, lbt, pm = _gdn_consts()
    bm = jnp.asarray(_block_mask(), F32)
    bm16 = bm.astype(BF16)
    sm = _solve_masks()
    cos, sin, perm = _rope_consts(n_tok)
    edge = _seq_edges(lat_tokens, n_tok, ctx_tokens, l_ctx)
    lane = np.arange(MIX_W)
    small_rows = np.arange(SMALL_W)
    def expand_cols(first_col):
        return jnp.asarray((small_rows[:, None] == first_col + lane[None, :] // HEAD_DIM).astype(np.float32), BF16)
    ea = jnp.stack([expand_cols(32), expand_cols(36)])
    eb = jnp.stack([expand_cols(40), expand_cols(44)])
    wr = jnp.zeros((D_MODEL, SMALL_W), F32).at[:, :N_EXPERTS].set(w_router)
    br = jnp.zeros((1, SMALL_W), F32).at[0, :N_EXPERTS].set(b_router)

    def cols(w, *names):
        return [w[:, _OFF[n][0]:_OFF[n][1]] for n in names]

    zeros_state = jnp.zeros((n_ctx, 2, MIX_W, MIX_W), F32)
    new_kv, new_gla, new_gdn = [], [], []
    for l in range(depth):
        lam_init = 0.8 - 0.6 * math.exp(-0.3 * l)
        wl = w_in[l]
        wcat = jnp.concatenate(
            cols(wl, 'na_q', 'na_k', 'na_v', 'gla_q', 'gla_k', 'gla_v', 'gla_og', 'gdn_qkv', 'gdn_og',
                 'diff_q', 'diff_k', 'diff_v', 'gla_gf', 'gla_gb', 'gdn_af', 'gdn_ab', 'gdn_bf', 'gdn_bb')
            + [jnp.zeros((D_MODEL, SMALL_W - 2 * GATE_RANK - 4 * HEADS), F32)], axis=1).astype(BF16)
        w_gate = wl[:, _OFF['branch_gate'][0]:].astype(BF16)
        z_na, z_gla, z_gdn, z_diff, z_small = _proj(xs, tokens, mod[l], norm1_w[l][None], wcat, tm, mod_row)

        ck = cache_na_k[:, l].reshape(n_lat * p_len, MIX_W)
        cv = cache_na_v[:, l].reshape(n_lat * p_len, MIX_W)
        dl = diff_lambda[l]
        na_lat = _na_latent(z_na, ck, cv, _na_bias_table(na_rpb[l], min(NA_WIN_H, n_tok // GRID_W)),
                            n_lat, n_tok, p_len)
        na_ctx = _attention(z_na, 0, z_na, 1, z_na, 2, dl, n_sub=1, lam_init=0.0, n_seq=n_ctx, lq=l_ctx,
                            lk=l_ctx, q_base=lat_tokens, k_base=lat_tokens, tq=l_ctx, name="na_ctx")

        q_r, k_r = _rope(z_diff, cos, sin, perm, n_lat, n_tok, tm)
        kk = jnp.concatenate([k_r.reshape(n_lat, n_tok, MIX_W), cache_diff_k[:, l].reshape(n_lat, p_len, MIX_W)], 1)
        vv = jnp.concatenate([z_diff[:lat_tokens, 2 * MIX_W:].reshape(n_lat, n_tok, MIX_W),
                              cache_diff_v[:, l].reshape(n_lat, p_len, MIX_W)], 1)
        lkv = n_tok + p_len
        df_lat = _attention(q_r, 0, kk.reshape(n_lat * lkv, MIX_W), 0, vv.reshape(n_lat * lkv, MIX_W), 0, dl,
                            n_sub=2, lam_init=lam_init, n_seq=n_lat, lq=n_tok, lk=lkv, q_base=0, k_base=0,
                            tq=tq_lat, name="diff_latent")
        df_ctx = _attention(z_diff, 0, z_diff, 1, z_diff, 2, dl, n_sub=2, lam_init=lam_init, n_seq=n_ctx,
                            lq=l_ctx, lk=l_ctx, q_base=lat_tokens, k_base=lat_tokens, tq=l_ctx, name="diff_ctx")

        gup = jnp.zeros((2, SMALL_W, MIX_W), F32)
        gup = gup.at[0, 0:GATE_RANK].set(gla_gate_up[l, 0]).at[1, GATE_RANK:2 * GATE_RANK].set(gla_gate_up[l, 1])
        gbias = gla_gate_bias[l][:, None, :]
        gla_consts = (gup, gbias, l_all, m_all, bm)
        glf_lat, glb_lat, _ = _scan(_gla_step, z_gla, z_small, gla_consts,
                                    _block_diag_state(state_gla[:, l], True), n_lat, n_tok, 0,
                                    GLA_SEQS_PER_STEP, "gla_scan")
        glf_ctx, glb_ctx, st_ctx = _scan(_gla_step, z_gla, z_small, gla_consts, zeros_state,
                                         n_ctx, l_ctx, lat_tokens, GLA_SEQS_PER_STEP, "gla_scan")

        qkv_n = _gdn_pre(z_gdn, edge, gdn_conv_w[l], bm16, tm)
        alog = jnp.repeat(gdn_A_log[l], HEAD_DIM, axis=-1)[:, None, :]
        dtb = jnp.repeat(gdn_dt_bias[l], HEAD_DIM, axis=-1)[:, None, :]
        gdn_consts = (ea, eb, alog, dtb, lsm, lbt, pm, bm, sm)
        gdf_lat, gdb_lat, _ = _scan(_gdn_step, qkv_n, z_small, gdn_consts,
                                    _block_diag_state(state_gdn[:, l], False), n_lat, n_tok, 0,
                                    GDN_SEQS_PER_STEP, "gdn_scan")
        gdf_ctx, gdb_ctx, s_ctx = _scan(_gdn_step, qkv_n, z_small, gdn_consts, zeros_state,
                                        n_ctx, l_ctx, lat_tokens, GDN_SEQS_PER_STEP, "gdn_scan")

        hn = jnp.stack([jnp.tile(gla_norm_w[l], HEADS), jnp.tile(gdn_norm_w[l], HEADS),
                        jnp.tile(diff_subln_w[l], HEADS)])[:, None, :]
        x = _merge(xs, tokens, mod[l], norm1_w[l][None], (na_lat, na_ctx), (glf_lat, glf_ctx), (glb_lat, glb_ctx), z_gla,
                   (gdf_lat, gdf_ctx), (gdb_lat, gdb_ctx), z_gdn, (df_lat, df_ctx), hn, bm16,
                   w_gate, w_branch[l].astype(BF16), w_out[l].astype(BF16), tm, mod_row, lam_init,
                   lat_tokens // tm)
        w1 = w_e1[l].astype(BF16)
        w3 = w_e3[l].astype(BF16)
        w2 = w_e2[l].reshape(N_EXPERTS * D_FF, D_MODEL).astype(BF16)
        x = _moe(x, mod[l], norm2_w[l][None], wr, br, w1, w3, w2, tm_moe, mod_row_moe)
        xs = (x, x, lat_tokens // tm, lat_tokens // tm)

        def ctx_heads(z, col):
            return z[lat_tokens:, col * MIX_W:(col + 1) * MIX_W].reshape(n_ctx, l_ctx, HEADS, HEAD_DIM)
        new_kv.append((ctx_heads(z_na, 1), ctx_heads(z_na, 2), ctx_heads(z_diff, 1), ctx_heads(z_diff, 2)))
        new_gla.append(_diag_blocks(st_ctx, True))
        new_gdn.append(_diag_blocks(s_ctx, False))

    y_sample = _final_norm(x, final_norm_w[None], tm, 0, lat_tokens // tm).reshape(n_lat, n_tok, D_MODEL)
    y_prompt = _final_norm(x, final_norm_w[None], tm, lat_tokens // tm, ctx_tokens // tm).reshape(
        n_ctx, l_ctx, D_MODEL)
    stack = lambda j: jnp.stack([t[j] for t in new_kv], axis=1)
    return (y_prompt, y_sample, stack(0), stack(1), stack(2), stack(3),
            jnp.stack(new_gla, axis=1), jnp.stack(new_gdn, axis=1))
```

```python
import functools
import math

import numpy as np
import jax
import jax.numpy as jnp
from jax import lax
from jax.experimental import pallas as pl
from jax.experimental.pallas import tpu as pltpu

F32 = jnp.float32
BF16 = jnp.bfloat16

D_MODEL = 1024
HEADS = 4
HEAD_DIM = 64
MIX_W = HEADS * HEAD_DIM
GRID_W = 64
NA_WIN_H = 8
NA_WIN_W = 16
GATE_RANK = 16
GATE_NORM = 16.0
GDN_CONV = 3
CHUNK = 64
N_EXPERTS = 16
N_GROUPS = 4
GROUP_E = N_EXPERTS // N_GROUPS
D_FF = D_MODEL // 4
ROPE_BASE = 10000.0
EPS = 1e-6
NEG = -1e30
LOG2E = 1.4426950408889634
SMALL_W = 128
V7X_VMEM_LIMIT = 56 * 1024 * 1024
LEAF = 8
ATTN_HEAD_GROUP = 2
NA_ROWS_PER_STEP = 4
GLA_SEQS_PER_STEP = 4
GDN_SEQS_PER_STEP = 2
MOE_TILE = 512

_IN_SPLITS = (
    ('na_q', MIX_W), ('na_k', MIX_W), ('na_v', MIX_W),
    ('gla_q', MIX_W), ('gla_k', MIX_W), ('gla_v', MIX_W),
    ('gla_gf', GATE_RANK), ('gla_gb', GATE_RANK), ('gla_og', MIX_W),
    ('gdn_qkv', 3 * MIX_W), ('gdn_af', HEADS), ('gdn_ab', HEADS),
    ('gdn_bf', HEADS), ('gdn_bb', HEADS), ('gdn_og', MIX_W),
    ('diff_q', MIX_W), ('diff_k', MIX_W), ('diff_v', MIX_W),
    ('branch_gate', 4 * D_MODEL),
)
_OFF = {}
_o = 0
for _n, _s in _IN_SPLITS:
    _OFF[_n] = (_o, _o + _s)
    _o += _s

def _params(sem):
    return pltpu.CompilerParams(dimension_semantics=sem, vmem_limit_bytes=V7X_VMEM_LIMIT)


def _dot(a, b):
    return jnp.dot(a, b, preferred_element_type=F32)


def _bdot(a, b):
    return _dot(a.astype(BF16), b.astype(BF16))


def _bdot_nt(a, b):
    return lax.dot_general(a.astype(BF16), b.astype(BF16), (((1,), (1,)), ((), ())),
                           preferred_element_type=F32)


def _bdot_tn(a, b):
    return lax.dot_general(a.astype(BF16), b.astype(BF16), (((0,), (0,)), ((), ())),
                           preferred_element_type=F32)


def _split(x):
    hi = x.astype(BF16)
    lo = (x - hi.astype(F32)).astype(BF16)
    return hi, lo


def _dot_rx(x, m):
    hi, lo = _split(x)
    return _dot(hi, m) + _dot(lo, m)


def _dot_lx(m, x):
    hi, lo = _split(x)
    return _dot(m, hi) + _dot(m, lo)


def _dot3(a, b):
    ah, al = _split(a)
    bh, bl = _split(b)
    return _dot(ah, bh) + _dot(ah, bl) + _dot(al, bh)


def _silu(x):
    return x * jax.nn.sigmoid(x)


def _softplus(x):
    return jnp.maximum(x, 0.0) + jnp.log(1.0 + jnp.exp(-jnp.abs(x)))


def _rms_rows(x, w):
    return x * lax.rsqrt(jnp.mean(x * x, axis=-1, keepdims=True) + EPS) * w


def _head_sum(x, bm16):
    return _dot_rx(x, bm16)


def _bd_rows(x, bm):
    return jnp.concatenate([x] * HEADS, axis=0) * bm


def _lane_mask(lo, width, n=MIX_W):
    lane = lax.broadcasted_iota(jnp.int32, (1, n), 1)
    return (lane >= lo) & (lane < lo + width)


def _gdn_consts():
    c = CHUNK
    i = np.arange(c)[:, None]
    t = np.arange(c)[None, :]
    lb = (t <= i).astype(np.float32)
    lbl = (t > i).astype(np.float32)
    tot = np.ones((16, c), np.float32)
    strict = (t < i).astype(np.float32)
    incl = (t <= i).astype(np.float32)
    rev = lambda a: a[::-1, ::-1]
    lsm = np.stack([np.concatenate([lb, lbl, tot], 0), np.concatenate([rev(lb), rev(lbl), tot], 0)])
    lbt = np.stack([np.tile(lb.T, (1, HEADS)), np.tile(rev(lb).T, (1, HEADS))])
    pm = np.stack([np.stack([np.tile(strict, (1, HEADS)), np.tile(incl, (1, HEADS))]),
                   np.stack([np.tile(rev(strict), (1, HEADS)), np.tile(rev(incl), (1, HEADS))])])
    return jnp.asarray(lsm, BF16), jnp.asarray(lbt, F32), jnp.asarray(pm, F32)


def _gla_pair_masks():
    i = np.arange(CHUNK)[:, None]
    j = np.arange(CHUNK)[None, :]
    nblk = CHUNK // LEAF
    out = []
    for d in range(2):
        later = (i // LEAF > j // LEAF) if d == 0 else (i // LEAF < j // LEAF)
        order = (j <= i) if d == 0 else (j >= i)
        far = [(j // LEAF == blk) & later for blk in range(nblk)]
        near = [(i // LEAF == j // LEAF) & (j % LEAF == r) & order for r in range(LEAF)]
        out.append(np.stack([np.tile(m.astype(np.float32), (1, HEADS)) for m in far + near]))
    return jnp.asarray(np.stack(out), F32)


def _block_mask():
    r = np.arange(MIX_W)
    return (r[:, None] // HEAD_DIM == r[None, :] // HEAD_DIM).astype(np.float32)


def _solve_masks():
    r = np.arange(MIX_W)[:, None]
    c = np.arange(MIX_W)[None, :]
    out = [(r // LEAF == c // LEAF)]
    s = LEAF
    while s < CHUNK:
        out.append((r // (2 * s) == c // (2 * s)) & (r // s != c // s))
        s *= 2
    return jnp.asarray(np.stack(out).astype(np.float32))


def _rope_consts(n_tok):
    t = np.arange(n_tok)
    pos = np.stack([t // GRID_W, t % GRID_W], 0).astype(np.float32)
    lane = np.arange(MIX_W)
    u = lane % 32
    axis = u // 16
    w = u % 16
    first = w < 8
    inv = ROPE_BASE ** (-(w % 8).astype(np.float32) / 8.0)
    ang = pos[axis, :].T * inv[None, :]
    cos = np.cos(ang)
    sin = np.sin(ang) * np.where(first, -1.0, 1.0)[None, :]
    partner = np.where(first, lane + 8, lane - 8)
    perm = np.zeros((MIX_W, MIX_W), np.float32)
    perm[partner, lane] = 1.0
    return jnp.asarray(cos, F32), jnp.asarray(sin, F32), jnp.asarray(perm, BF16)


def _ada_kernel(c_ref, w_ref, b_ref, o_ref):
    c = c_ref[...]
    o_ref[0] = _bdot(_silu(c), w_ref[0]) + b_ref[0]


def _ada(cond, w_ada, b_ada):
    depth, _, n = w_ada.shape
    tn = 1536
    return pl.pallas_call(
        _ada_kernel,
        grid=(depth, n // tn),
        in_specs=[pl.BlockSpec((8, D_MODEL), lambda l, j: (0, 0)),
                  pl.BlockSpec((1, D_MODEL, tn), lambda l, j: (l, 0, j)),
                  pl.BlockSpec((1, 1, tn), lambda l, j: (l, 0, j))],
        out_specs=pl.BlockSpec((1, 8, tn), lambda l, j: (l, 0, j)),
        out_shape=jax.ShapeDtypeStruct((depth, 8, n), F32),
        compiler_params=_params(("parallel", "parallel")),
        name="ada_mod",
    )(cond, w_ada, b_ada.reshape(depth, 1, n))


def _proj_kernel(xl_ref, xc_ref, mod_ref, nw_ref, w_ref, o_na, o_gla, o_gdn, o_diff, o_small, *, n_lat_tiles):
    mod = mod_ref[0]
    x = jnp.where(pl.program_id(0) < n_lat_tiles, xl_ref[...], xc_ref[...])
    h = _rms_rows(x, nw_ref[...]) * (1.0 + mod[1:2]) + mod[0:1]
    hb = h.astype(BF16)
    o_na[...] = _dot(hb, w_ref[:, 0:768])
    o_gla[...] = _dot(hb, w_ref[:, 768:1792])
    o_gdn[...] = _dot(hb, w_ref[:, 1792:2816])
    o_diff[...] = _dot(hb, w_ref[:, 2816:3584])
    o_small[...] = _dot(hb, w_ref[:, 3584:3712])


def _stream_specs(xs, tm):
    _, _, ctx_first, n_lat_tiles = xs
    return [pl.BlockSpec((tm, D_MODEL), lambda i: (jnp.minimum(i, n_lat_tiles - 1), 0)),
            pl.BlockSpec((tm, D_MODEL), lambda i: (ctx_first + jnp.maximum(i - n_lat_tiles, 0), 0))]


def _proj(xs, t, mod_l, nw, wcat, tm, mod_row):
    widths = (768, 1024, 1024, 768, SMALL_W)
    return pl.pallas_call(
        functools.partial(_proj_kernel, n_lat_tiles=xs[3]),
        grid=(t // tm,),
        in_specs=_stream_specs(xs, tm) + [
                  pl.BlockSpec((1, 6, D_MODEL), lambda i: (mod_row(i), 0, 0)),
                  pl.BlockSpec((1, D_MODEL), lambda i: (0, 0)),
                  pl.BlockSpec(wcat.shape, lambda i: (0, 0))],
        out_specs=[pl.BlockSpec((tm, w), lambda i: (i, 0)) for w in widths],
        out_shape=[jax.ShapeDtypeStruct((t, w), F32) for w in widths],
        compiler_params=_params(("parallel",)),
        name="in_proj",
    )(xs[0], xs[1], mod_l, nw, wcat)


def _attn_core(q, k, v, n_sub, lam):
    kb = k.astype(BF16)
    vb = v.astype(BF16)
    tq = q.shape[0]
    sub_w = HEAD_DIM // n_sub
    scale = sub_w ** -0.5 * LOG2E
    out = jnp.zeros(q.shape, F32)
    for h0 in range(0, HEADS, ATTN_HEAD_GROUP):
        maps = [(h, m) for h in range(h0, h0 + ATTN_HEAD_GROUP) for m in range(n_sub)]
        qs = jnp.concatenate(
            [(q * jnp.where(_lane_mask(h * HEAD_DIM + m * sub_w, sub_w), scale, 0.0)).astype(BF16)
             for h, m in maps], axis=0)
        s = lax.dot_general(qs, kb, (((1,), (1,)), ((), ())), preferred_element_type=F32)
        e = jnp.exp2(s - jnp.max(s, axis=-1, keepdims=True))
        inv = 1.0 / jnp.sum(e, axis=-1, keepdims=True)
        if n_sub == 1:
            p = e * inv
        else:
            p = jnp.concatenate(
                [e[(2 * i) * tq:(2 * i + 1) * tq] * inv[(2 * i) * tq:(2 * i + 1) * tq]
                 - e[(2 * i + 1) * tq:(2 * i + 2) * tq] * (lam * inv[(2 * i + 1) * tq:(2 * i + 2) * tq])
                 for i in range(ATTN_HEAD_GROUP)], axis=0)
        o_all = _dot(p.astype(BF16), vb)
        for i in range(ATTN_HEAD_GROUP):
            out = jnp.where(_lane_mask((h0 + i) * HEAD_DIM, HEAD_DIM), o_all[i * tq:(i + 1) * tq], out)
    return out


def _diff_lambda(dl, lam_init):
    a = jnp.sum(dl[0:1] * dl[1:2], axis=-1, keepdims=True)
    b = jnp.sum(dl[2:3] * dl[3:4], axis=-1, keepdims=True)
    return jnp.exp(a) - jnp.exp(b) + lam_init


def _attn_kernel(q_ref, k_ref, v_ref, dl_ref, o_ref, *, n_sub, lam_init):
    lam = _diff_lambda(dl_ref[...], lam_init) if n_sub == 2 else None
    o_ref[...] = _attn_core(q_ref[...], k_ref[...], v_ref[...], n_sub, lam)


def _attention(q_arr, q_col, k_arr, k_col, v_arr, v_col, dl, *, n_sub, lam_init, n_seq, lq, lk,
               q_base, k_base, tq, name):
    nq = lq // tq
    return pl.pallas_call(
        functools.partial(_attn_kernel, n_sub=n_sub, lam_init=lam_init),
        grid=(n_seq, nq),
        in_specs=[pl.BlockSpec((tq, MIX_W), lambda s, j: (q_base // tq + s * nq + j, q_col)),
                  pl.BlockSpec((lk, MIX_W), lambda s, j: (k_base // lk + s, k_col)),
                  pl.BlockSpec((lk, MIX_W), lambda s, j: (k_base // lk + s, v_col)),
                  pl.BlockSpec(dl.shape, lambda s, j: (0, 0))],
        out_specs=pl.BlockSpec((tq, MIX_W), lambda s, j: (s * nq + j, 0)),
        out_shape=jax.ShapeDtypeStruct((n_seq * lq, MIX_W), F32),
        compiler_params=_params(("parallel", "parallel")),
        name=name,
    )(q_arr, k_arr, v_arr, dl)


def _na_lat_kernel(q_ref, k_ref, v_ref, ck_ref, cv_ref, bias_ref, o_ref, *, rows, kh, rps):
    g = pl.program_id(1)
    ck = ck_ref[...].astype(BF16)
    cv = cv_ref[...].astype(BF16)
    scale = HEAD_DIM ** -0.5 * LOG2E
    head_scale = [jnp.where(_lane_mask(h * HEAD_DIM, HEAD_DIM), scale, 0.0) for h in range(HEADS)]
    hw = HEADS * GRID_W
    qs = []
    for j in range(rps):
        qj = q_ref[j * GRID_W:(j + 1) * GRID_W, :]
        qs.append(jnp.concatenate([(qj * hs).astype(BF16) for hs in head_scale], axis=0))
    s_ctx_all = lax.dot_general(jnp.concatenate(qs, axis=0), ck, (((1,), (1,)), ((), ())),
                                preferred_element_type=F32)
    p_ctx, o_loc = [], []
    for j in range(rps):
        r = g * rps + j
        start = jnp.clip(r - kh // 2, 0, rows - kh)
        cls = r - start
        ws = pl.multiple_of(start * GRID_W, GRID_W)
        kw = k_ref[pl.ds(ws, kh * GRID_W), :].astype(BF16)
        vw = v_ref[pl.ds(ws, kh * GRID_W), :].astype(BF16)
        bias = jnp.concatenate([bias_ref[h, pl.ds(cls, 1)][0] for h in range(HEADS)], axis=0)
        s_loc = lax.dot_general(qs[j], kw, (((1,), (1,)), ((), ())), preferred_element_type=F32) + bias
        s_ctx = s_ctx_all[j * hw:(j + 1) * hw]
        mx = jnp.maximum(jnp.max(s_loc, axis=-1, keepdims=True), jnp.max(s_ctx, axis=-1, keepdims=True))
        e_loc = jnp.exp2(s_loc - mx)
        e_ctx = jnp.exp2(s_ctx - mx)
        inv = 1.0 / (jnp.sum(e_loc, axis=-1, keepdims=True) + jnp.sum(e_ctx, axis=-1, keepdims=True))
        p_ctx.append((e_ctx * inv).astype(BF16))
        o_loc.append(_dot((e_loc * inv).astype(BF16), vw))
    o_ctx_all = _dot(jnp.concatenate(p_ctx, axis=0), cv)
    for j in range(rps):
        o_all = o_loc[j] + o_ctx_all[j * hw:(j + 1) * hw]
        out = jnp.zeros((GRID_W, MIX_W), F32)
        for h in range(HEADS):
            out = jnp.where(_lane_mask(h * HEAD_DIM, HEAD_DIM), o_all[h * GRID_W:(h + 1) * GRID_W], out)
        o_ref[j * GRID_W:(j + 1) * GRID_W, :] = out


def _na_bias_table(rpb, kh):
    cidx = np.arange(GRID_W)
    cls = np.arange(kh)
    drow = np.arange(kh)[None, :] - cls[:, None] + NA_WIN_H - 1
    col_start = np.clip(cidx - NA_WIN_W // 2, 0, GRID_W - NA_WIN_W)
    col_ok = (cidx[None, :] >= col_start[:, None]) & (cidx[None, :] < col_start[:, None] + NA_WIN_W)
    dcol = np.clip(cidx[None, :] - cidx[:, None], 1 - NA_WIN_W, NA_WIN_W - 1) + NA_WIN_W - 1
    toep = rpb.astype(F32)[..., dcol]
    bias = jnp.stack([toep[:, drow[c, 0]:drow[c, 0] + kh] for c in range(kh)], axis=1)
    bias = bias.transpose(0, 1, 3, 2, 4)
    bias = jnp.where(col_ok[:, None, :], bias * LOG2E, NEG)
    return bias.reshape(HEADS, kh, GRID_W, kh * GRID_W)


def _na_latent(z_na, ck, cv, bias, n_b, n_tok, p_len):
    rows = n_tok // GRID_W
    kh = min(NA_WIN_H, rows)
    rps = NA_ROWS_PER_STEP
    assert rows % rps == 0
    steps = rows // rps
    return pl.pallas_call(
        functools.partial(_na_lat_kernel, rows=rows, kh=kh, rps=rps),
        grid=(n_b, steps),
        in_specs=[pl.BlockSpec((rps * GRID_W, MIX_W), lambda b, r: (b * steps + r, 0)),
                  pl.BlockSpec((n_tok, MIX_W), lambda b, r: (b, 1)),
                  pl.BlockSpec((n_tok, MIX_W), lambda b, r: (b, 2)),
                  pl.BlockSpec((p_len, MIX_W), lambda b, r: (b, 0)),
                  pl.BlockSpec((p_len, MIX_W), lambda b, r: (b, 0)),
                  pl.BlockSpec(bias.shape, lambda b, r: (0, 0, 0, 0))],
        out_specs=pl.BlockSpec((rps * GRID_W, MIX_W), lambda b, r: (b * steps + r, 0)),
        out_shape=jax.ShapeDtypeStruct((n_b * n_tok, MIX_W), F32),
        compiler_params=_params(("parallel", "arbitrary")),
        name="na_latent",
    )(z_na, z_na, z_na, ck, cv, bias)


def _rope_kernel(q_ref, k_ref, cos_ref, sin_ref, p_ref, qo_ref, ko_ref):
    cos = cos_ref[...]
    sin = sin_ref[...]
    p = p_ref[...]
    q = q_ref[...]
    k = k_ref[...]
    qo_ref[...] = q * cos + _dot_rx(q, p) * sin
    ko_ref[...] = k * cos + _dot_rx(k, p) * sin


def _rope(z_diff, cos, sin, perm, n_b, n_tok, tm):
    nt = n_tok // tm
    spec_t = pl.BlockSpec((tm, MIX_W), lambda i: (i % nt, 0))
    return pl.pallas_call(
        _rope_kernel,
        grid=(n_b * nt,),
        in_specs=[pl.BlockSpec((tm, MIX_W), lambda i: (i, 0)),
                  pl.BlockSpec((tm, MIX_W), lambda i: (i, 1)),
                  spec_t, spec_t,
                  pl.BlockSpec(perm.shape, lambda i: (0, 0))],
        out_specs=[pl.BlockSpec((tm, MIX_W), lambda i: (i, 0))] * 2,
        out_shape=[jax.ShapeDtypeStruct((n_b * n_tok, MIX_W), F32)] * 2,
        compiler_params=_params(("parallel",)),
        name="diff_rope",
    )(z_diff, z_diff, cos, sin, perm)


def _gdn_pre_kernel(x_ref, prev_ref, next_ref, edge_ref, w_ref, bm_ref, o_ref, *, tm):
    x = x_ref[...]
    w = w_ref[...]
    edge = edge_ref[...]
    row = lax.broadcasted_iota(jnp.int32, (tm, 1), 0)
    x_prev = jnp.where(row == 0, prev_ref[7:8, :], pltpu.roll(x, 1, 0)) * edge[:, 0:1]
    x_next = jnp.where(row == tm - 1, next_ref[0:1, :], pltpu.roll(x, tm - 1, 0)) * edge[:, 1:2]
    y = _silu(x_prev * w[0:1] + x * w[1:2] + x_next * w[2:3])
    bm16 = bm_ref[...]
    q = y[:, 0:MIX_W]
    k = y[:, MIX_W:2 * MIX_W]
    o_ref[:, 0:MIX_W] = q * lax.rsqrt(_head_sum(q * q, bm16) + EPS) * (HEAD_DIM ** -0.5)
    o_ref[:, MIX_W:2 * MIX_W] = k * lax.rsqrt(_head_sum(k * k, bm16) + EPS)
    o_ref[:, 2 * MIX_W:3 * MIX_W] = y[:, 2 * MIX_W:3 * MIX_W]


def _seq_edges(lat_tokens, n_tok, ctx_tokens, l_ctx):
    pos = np.concatenate([np.arange(lat_tokens) % n_tok, np.arange(ctx_tokens) % l_ctx])
    seq = np.concatenate([np.full(lat_tokens, n_tok), np.full(ctx_tokens, l_ctx)])
    edge = np.zeros((lat_tokens + ctx_tokens, 8), np.float32)
    edge[:, 0] = pos != 0
    edge[:, 1] = pos != seq - 1
    return jnp.asarray(edge)


def _gdn_pre(z_gdn, edge, conv_w, bm16, tm):
    t = z_gdn.shape[0]
    w3 = 3 * MIX_W
    nb8 = t // 8
    return pl.pallas_call(
        functools.partial(_gdn_pre_kernel, tm=tm),
        grid=(t // tm,),
        in_specs=[pl.BlockSpec((tm, w3), lambda i: (i, 0)),
                  pl.BlockSpec((8, w3), lambda i: (jnp.maximum(i * (tm // 8) - 1, 0), 0)),
                  pl.BlockSpec((8, w3), lambda i: (jnp.minimum((i + 1) * (tm // 8), nb8 - 1), 0)),
                  pl.BlockSpec((tm, 8), lambda i: (i, 0)),
                  pl.BlockSpec(conv_w.shape, lambda i: (0, 0)),
                  pl.BlockSpec(bm16.shape, lambda i: (0, 0))],
        out_specs=pl.BlockSpec((tm, w3), lambda i: (i, 0)),
        out_shape=jax.ShapeDtypeStruct((t, w3), F32),
        compiler_params=_params(("parallel",)),
        name="gdn_pre",
    )(z_gdn, z_gdn, z_gdn, edge, conv_w, bm16)


def _bd_rows16(x, bm16):
    return jnp.concatenate([x.astype(BF16)] * HEADS, axis=0) * bm16


def _gla_chain(q, k, v, zs, gup, gb, lsm, pm_ref, d, bm, bm16, st):
    nblk = CHUNK // LEAF
    q = q * (HEAD_DIM ** -0.5)
    gp = _bdot(zs, gup) + gb
    g = (jnp.minimum(gp, 0.0) - jnp.log(1.0 + jnp.exp(-jnp.abs(gp)))) * (LOG2E / GATE_NORM)
    yield
    e = _dot_lx(lsm, g)
    b = e[0:CHUNK]
    fb = jnp.exp2(b)
    fbl = jnp.exp2(e[CHUNK:2 * CHUNK])
    ftot = jnp.exp2(e[2 * CHUNK:2 * CHUNK + 1])
    yield
    edge = LEAF - 1 if d == 0 else 0
    blocks = range(nblk - 1) if d == 0 else range(1, nblk)
    b3 = b.reshape(nblk, LEAF, MIX_W)
    b_edge = jnp.broadcast_to(b3[:, edge:edge + 1, :], b3.shape).reshape(CHUNK, MIX_W)
    kf = jnp.exp2(jnp.minimum(b_edge - b, 0.0))
    q_far = jnp.concatenate(
        [(q * jnp.exp2(jnp.minimum(b - b[blk * LEAF + edge:blk * LEAF + edge + 1, :], 0.0))).astype(BF16)
         for blk in blocks], axis=0)
    s_far = lax.dot_general(q_far, _bd_rows16(k * kf, bm16), (((1,), (1,)), ((), ())),
                            preferred_element_type=F32)
    yield
    q_near = jnp.concatenate(
        [(q * jnp.exp2(jnp.minimum(
            b - jnp.broadcast_to(b3[:, r:r + 1, :], b3.shape).reshape(CHUNK, MIX_W), 0.0))).astype(BF16)
         for r in range(LEAF)], axis=0)
    s_near = lax.dot_general(q_near, _bd_rows16(k, bm16), (((1,), (1,)), ((), ())),
                             preferred_element_type=F32)
    yield
    att = None
    for n, blk in enumerate(blocks):
        term = s_far[n * CHUNK:(n + 1) * CHUNK] * pm_ref[d, blk]
        att = term if att is None else att + term
    for r in range(LEAF):
        att = att + s_near[r * CHUNK:(r + 1) * CHUNK] * pm_ref[d, nblk + r]
    yield
    o = _bdot_nt(q * fb, st) + _dot(att.astype(BF16), _bd_rows16(v, bm16))
    yield
    return o, st * ftot + _bdot_tn(v, k * fbl) * bm


def _tri_inverse(mbd, sm_ref):
    r = lax.broadcasted_iota(jnp.int32, (MIX_W, MIX_W), 0)
    c = lax.broadcasted_iota(jnp.int32, (MIX_W, MIX_W), 1)
    eye = jnp.where(r == c, 1.0, 0.0)
    md = mbd * sm_ref[0]
    m2 = _bdot(md, md)
    yield
    m4 = _bdot(m2, m2)
    t = eye - md
    t = t + _bdot(t, m2)
    yield
    t = t + _bdot(t, m4)
    for lvl in range(1, sm_ref.shape[0]):
        yield
        tm = _bdot(t, mbd * sm_ref[lvl])
        yield
        t = t - _bdot(tm, t)
    return t


def _gdn_chain(q, k, v, zs, ea, eb, alog, dtb, lsm, lbt, pm_ref, d, bm, bm16, sm_ref, s):
    g = -jnp.exp(alog) * _softplus(_dot_rx(zs, ea) + dtb)
    beta = jax.nn.sigmoid(_dot_rx(zs, eb))
    yield
    e = _dot_lx(lsm, g)
    dcol = e[0:CHUNK]
    drow = _dot_lx(jnp.ones((8, CHUNK), BF16), g * lbt)[0:1]
    kb = k * beta
    s2 = lax.dot_general(jnp.concatenate([kb, q], axis=0).astype(BF16), _bd_rows16(k, bm16),
                         (((1,), (1,)), ((), ())), preferred_element_type=F32)
    yield
    dec = jnp.exp(jnp.where(pm_ref[d, 1] > 0.0, dcol - drow, NEG))
    m_cat = s2[0:CHUNK] * dec * pm_ref[d, 0]
    att = s2[CHUNK:2 * CHUNK] * dec
    t_inv = yield from _tri_inverse(_bd_rows(m_cat, bm), sm_ref)
    t16 = t_inv.astype(BF16)
    fb = jnp.exp(dcol)
    fbl = jnp.exp(e[CHUNK:2 * CHUNK])
    ftot = jnp.exp(e[2 * CHUNK:2 * CHUNK + 1])
    yield
    u_bd = _dot(t16, _bd_rows16(v * beta, bm16))
    w_bd = _dot(t16, _bd_rows16(kb * fb, bm16))
    yield
    ws = _dot(jnp.concatenate([w_bd.astype(BF16), (q * fb).astype(BF16)], axis=0), s.astype(BF16))
    v_new = u_bd - ws[0:MIX_W]
    yield
    o = ws[MIX_W:MIX_W + CHUNK] + _bdot(att, v_new)
    return o, s * ftot + lax.dot_general(_bd_rows16(k * fbl, bm16), v_new.astype(BF16), (((0,), (0,)), ((), ())),
                                         preferred_element_type=F32)


def _scan_kernel(*refs, chain, n_const, nb, n_chunks):
    n_tok = nb * 2 * 4
    tok = refs[:n_tok]
    consts = refs[n_tok:n_tok + n_const]
    s0_ref, of_ref, ob_ref, sf_ref, s_scr = refs[n_tok + n_const:]
    n = pl.program_id(1)

    @pl.when(n == 0)
    def _():
        s_scr[...] = s0_ref[...]

    ids = [(j, d) for j in range(nb) for d in range(2)]
    gens = []
    for j, d in ids:
        q_ref, k_ref, v_ref, zs_ref = tok[(j * 2 + d) * 4:(j * 2 + d) * 4 + 4]
        gens.append(chain(q_ref[...], k_ref[...], v_ref[...], zs_ref[...], consts, d, s_scr[j, d]))
    live = list(range(len(gens)))
    while live:
        for c in list(live):
            try:
                next(gens[c])
            except StopIteration as done:
                j, d = ids[c]
                o, s_new = done.value
                (of_ref, ob_ref)[d][j] = o
                s_scr[j, d] = s_new
                live.remove(c)

    @pl.when(n == n_chunks - 1)
    def _():
        sf_ref[...] = s_scr[...]


def _gla_step(q, k, v, zs, consts, d, st):
    gup_ref, gb_ref, lsm_ref, pm_ref, bm_ref, bm16_ref = consts
    return _gla_chain(q, k, v, zs, gup_ref[d], gb_ref[d], lsm_ref[d], pm_ref, d, bm_ref[...], bm16_ref[...], st)


def _gdn_step(q, k, v, zs, consts, d, s):
    ea_ref, eb_ref, alog_ref, dtb_ref, lsm_ref, lbt_ref, pm_ref, bm_ref, bm16_ref, sm_ref = consts
    return _gdn_chain(q, k, v, zs, ea_ref[d], eb_ref[d], alog_ref[d], dtb_ref[d], lsm_ref[d], lbt_ref[d],
                      pm_ref, d, bm_ref[...], bm16_ref[...], sm_ref, s)


def _scan(chain, qkv, z_small, consts, s0, n_seq, l_seq, base, seqs_per_step, name):
    nc = l_seq // CHUNK
    nb = seqs_per_step if n_seq % seqs_per_step == 0 else 1

    def chunk(n, d):
        return n + d * (nc - 1 - 2 * n)

    in_specs, args = [], []
    for j in range(nb):
        for d in range(2):
            for arr, col, width in ((qkv, 0, MIX_W), (qkv, 1, MIX_W), (qkv, 2, MIX_W), (z_small, 0, SMALL_W)):
                in_specs.append(pl.BlockSpec(
                    (CHUNK, width),
                    functools.partial(lambda g, n, j, d, col: (base // CHUNK + (g * nb + j) * nc + chunk(n, d), col),
                                      j=j, d=d, col=col)))
                args.append(arr)
    for cst in consts:
        in_specs.append(pl.BlockSpec(cst.shape, functools.partial(lambda g, n, nd: (0,) * nd, nd=cst.ndim)))
        args.append(cst)
    state_spec = pl.BlockSpec((nb, 2, MIX_W, MIX_W), lambda g, n: (g, 0, 0, 0))
    in_specs.append(state_spec)
    args.append(s0)
    o_f, o_b, s_f = pl.pallas_call(
        functools.partial(_scan_kernel, chain=chain, n_const=len(consts), nb=nb, n_chunks=nc),
        grid=(n_seq // nb, nc),
        in_specs=in_specs,
        out_specs=[pl.BlockSpec((nb, CHUNK, MIX_W), lambda g, n: (g, chunk(n, 0), 0)),
                   pl.BlockSpec((nb, CHUNK, MIX_W), lambda g, n: (g, chunk(n, 1), 0)),
                   state_spec],
        out_shape=[jax.ShapeDtypeStruct((n_seq, l_seq, MIX_W), F32),
                   jax.ShapeDtypeStruct((n_seq, l_seq, MIX_W), F32),
                   jax.ShapeDtypeStruct((n_seq, 2, MIX_W, MIX_W), F32)],
        scratch_shapes=[pltpu.VMEM((nb, 2, MIX_W, MIX_W), F32)],
        compiler_params=_params(("parallel", "arbitrary")),
        name=name,
    )(*args)
    return o_f.reshape(n_seq * l_seq, MIX_W), o_b.reshape(n_seq * l_seq, MIX_W), s_f


def _merge_kernel(xl_ref, xc_ref, mod_ref, n1_ref, na_l, na_c, glaf_l, glaf_c, glab_l, glab_c, glag_ref, gdnf_l, gdnf_c,
                  gdnb_l, gdnb_c, gdng_ref, df_l, df_c, hn_ref, bm_ref, wg_ref, wb_ref, wo_ref, o_ref, *,
                  lam_init, n_lat_tiles):
    is_lat = pl.program_id(0) < n_lat_tiles

    def pick(lat_ref, ctx_ref):
        return jnp.where(is_lat, lat_ref[...], ctx_ref[...])

    mod = mod_ref[0]
    x = pick(xl_ref, xc_ref)
    h = _rms_rows(x, n1_ref[...]) * (1.0 + mod[1:2]) + mod[0:1]
    hb = h.astype(BF16)
    bm16 = bm_ref[...]

    def head_norm(o, w):
        return o * lax.rsqrt(_head_sum(o * o, bm16) * (1.0 / HEAD_DIM) + EPS) * w

    branches = (
        pick(na_l, na_c),
        head_norm(pick(glaf_l, glaf_c) + pick(glab_l, glab_c), hn_ref[0]) * _silu(glag_ref[...]),
        head_norm(pick(gdnf_l, gdnf_c) + pick(gdnb_l, gdnb_c), hn_ref[1]) * _silu(gdng_ref[...]),
        head_norm(pick(df_l, df_c), hn_ref[2]) * (1.0 - lam_init),
    )
    merged = None
    for n, br in enumerate(branches):
        gate = jax.nn.sigmoid(_dot(hb, wg_ref[:, n * D_MODEL:(n + 1) * D_MODEL]))
        term = gate * _bdot(br, wb_ref[n])
        merged = term if merged is None else merged + term
    o_ref[...] = x + mod[2:3] * _bdot(merged, wo_ref[...])


def _merge(xs, t, mod_l, n1, na, gla_f, gla_b, z_gla, gdn_f, gdn_b, z_gdn, df, hn, bm16, wg, wb, wo, tm, mod_row,
           lam_init, n_lat_tiles):
    tok = lambda col: pl.BlockSpec((tm, MIX_W), lambda i: (i, col))
    lat = pl.BlockSpec((tm, MIX_W), lambda i: (jnp.minimum(i, n_lat_tiles - 1), 0))
    ctx = pl.BlockSpec((tm, MIX_W), lambda i: (jnp.maximum(i - n_lat_tiles, 0), 0))
    full = lambda a: pl.BlockSpec(a.shape, lambda i: (0,) * a.ndim)
    return pl.pallas_call(
        functools.partial(_merge_kernel, lam_init=lam_init, n_lat_tiles=n_lat_tiles),
        grid=(t // tm,),
        in_specs=_stream_specs(xs, tm) + [
                  pl.BlockSpec((1, 6, D_MODEL), lambda i: (mod_row(i), 0, 0)),
                  full(n1), lat, ctx, lat, ctx, lat, ctx, tok(3), lat, ctx, lat, ctx, tok(3), lat, ctx,
                  full(hn), full(bm16), full(wg), full(wb), full(wo)],
        out_specs=pl.BlockSpec((tm, D_MODEL), lambda i: (i, 0)),
        out_shape=jax.ShapeDtypeStruct((t, D_MODEL), F32),
        compiler_params=_params(("parallel",)),
        name="merge_out",
    )(xs[0], xs[1], mod_l, n1, *na, *gla_f, *gla_b, z_gla, *gdn_f, *gdn_b, z_gdn, *df, hn, bm16, wg, wb, wo)


def _route(h2, wr_ref, br_ref):
    lane_i = lax.broadcasted_iota(jnp.int32, (1, SMALL_W), 1)
    lane = lane_i.astype(F32)
    gid = lax.shift_right_logical(lane_i, 2).astype(F32)
    s = jax.nn.sigmoid(_dot3(h2, wr_ref[...]))
    sel = s + br_ref[...]
    far = float(SMALL_W)

    def first_max(vals):
        mx = jnp.max(vals, axis=-1, keepdims=True)
        idx = jnp.min(jnp.where(vals == mx, lane, far), axis=-1, keepdims=True)
        return mx, idx

    best = None
    for g in range(N_GROUPS):
        vals = jnp.where(gid == float(g), sel, NEG)
        a, ia = first_max(vals)
        b, _ = first_max(jnp.where(lane == ia, NEG, vals))
        score = a + b
        if best is None:
            best, gi = score, jnp.zeros_like(score)
        else:
            better = score > best
            gi = jnp.where(better, float(g), gi)
            best = jnp.where(better, score, best)
    vals = jnp.where(gid == gi, sel, NEG)
    _, i1 = first_max(vals)
    _, i2 = first_max(jnp.where(lane == i1, NEG, vals))
    s1 = jnp.sum(jnp.where(lane == i1, s, 0.0), axis=-1, keepdims=True)
    s2 = jnp.sum(jnp.where(lane == i2, s, 0.0), axis=-1, keepdims=True)
    inv = 1.0 / (s1 + s2)
    return jnp.where(lane == i1, s1 * inv, 0.0) + jnp.where(lane == i2, s2 * inv, 0.0)


def _moe_kernel(x_ref, mod_ref, n2_ref, wr_ref, br_ref, w1_ref, w3_ref, w2_ref, o_ref,
                h_scr, g_scr, acc_scr):
    g = pl.program_id(1)
    mod = mod_ref[0]

    @pl.when(g == 0)
    def _():
        h2 = _rms_rows(x_ref[...], n2_ref[...]) * (1.0 + mod[4:5]) + mod[3:4]
        h_scr[...] = h2.astype(BF16)
        g_scr[...] = _route(h2, wr_ref, br_ref)
        acc_scr[...] = jnp.zeros_like(acc_scr)

    hb = h_scr[...]
    gates = g_scr[...]
    lane = lax.broadcasted_iota(jnp.int32, (1, SMALL_W), 1)
    he = jnp.concatenate(
        [_silu(_dot(hb, w1_ref[e])) * _dot(hb, w3_ref[e])
         * jnp.sum(jnp.where(lane == g * GROUP_E + e, gates, 0.0), axis=-1, keepdims=True)
         for e in range(GROUP_E)], axis=1)
    acc_scr[...] += _dot(he.astype(BF16), w2_ref[...])

    @pl.when(g == N_GROUPS - 1)
    def _():
        o_ref[...] = x_ref[...] + mod[5:6] * acc_scr[...]


def _moe(x, mod_l, n2, wr, br, w1, w3, w2, tm, mod_row):
    t = x.shape[0]
    gw = GROUP_E * D_FF
    return pl.pallas_call(
        _moe_kernel,
        grid=(t // tm, N_GROUPS),
        in_specs=[pl.BlockSpec((tm, D_MODEL), lambda i, g: (i, 0)),
                  pl.BlockSpec((1, 6, D_MODEL), lambda i, g: (mod_row(i), 0, 0)),
                  pl.BlockSpec(n2.shape, lambda i, g: (0, 0)),
                  pl.BlockSpec(wr.shape, lambda i, g: (0, 0)),
                  pl.BlockSpec(br.shape, lambda i, g: (0, 0)),
                  pl.BlockSpec((GROUP_E, D_MODEL, D_FF), lambda i, g: (g, 0, 0)),
                  pl.BlockSpec((GROUP_E, D_MODEL, D_FF), lambda i, g: (g, 0, 0)),
                  pl.BlockSpec((gw, D_MODEL), lambda i, g: (g, 0))],
        out_specs=pl.BlockSpec((tm, D_MODEL), lambda i, g: (i, 0)),
        out_shape=jax.ShapeDtypeStruct((t, D_MODEL), F32),
        scratch_shapes=[pltpu.VMEM((tm, D_MODEL), BF16), pltpu.VMEM((tm, SMALL_W), F32),
                        pltpu.VMEM((tm, D_MODEL), F32)],
        compiler_params=_params(("parallel", "arbitrary")),
        name="moe",
    )(x, mod_l, n2, wr, br, w1, w3, w2)


def _final_kernel(x_ref, w_ref, o_ref):
    o_ref[...] = _rms_rows(x_ref[...], w_ref[...])


def _final_norm(x, w, tm, first_tile, n_tiles):
    return pl.pallas_call(
        _final_kernel,
        grid=(n_tiles,),
        in_specs=[pl.BlockSpec((tm, D_MODEL), lambda i: (first_tile + i, 0)),
                  pl.BlockSpec(w.shape, lambda i: (0, 0))],
        out_specs=pl.BlockSpec((tm, D_MODEL), lambda i: (i, 0)),
        out_shape=jax.ShapeDtypeStruct((n_tiles * tm, D_MODEL), F32),
        compiler_params=_params(("parallel",)),
        name="final_norm",
    )(x, w)


def _block_diag_state(s, transpose):
    if transpose:
        s = jnp.swapaxes(s, -1, -2)
    eye = jnp.eye(HEADS, dtype=s.dtype)
    out = jnp.einsum('bdhij,hg->bdhigj', s, eye)
    return out.reshape(s.shape[0], 2, MIX_W, MIX_W)


def _diag_blocks(s_bd, transpose):
    b = s_bd.shape[0]
    s = s_bd.reshape(b, 2, HEADS, HEAD_DIM, HEADS, HEAD_DIM)
    s = jnp.stack([s[:, :, h, :, h, :] for h in range(HEADS)], axis=2)
    return jnp.swapaxes(s, -1, -2) if transpose else s


def _pick_tile(*lengths):
    for tm in (512, 256, 128, 64):
        if all(n % tm == 0 for n in lengths):
            return tm
    raise ValueError("token counts must be multiples of 64")


def kernel(x_prompt, x_sample, cache_na_k, cache_na_v, cache_diff_k, cache_diff_v, state_gla, state_gdn, c, c_ctx, w_ada, b_ada, norm1_w, norm2_w, w_in, na_rpb, gla_gate_up, gla_gate_bias, gla_norm_w, gdn_conv_w, gdn_A_log, gdn_dt_bias, gdn_norm_w, diff_lambda, diff_subln_w, w_branch, w_out, w_router, b_router, w_e1, w_e3, w_e2, final_norm_w):
    n_ctx, l_ctx, _ = x_prompt.shape
    n_lat, n_tok, _ = x_sample.shape
    depth = w_ada.shape[0]
    p_len = cache_na_k.shape[2]
    lat_tokens = n_lat * n_tok
    ctx_tokens = n_ctx * l_ctx
    assert n_lat + 1 <= 8 and n_tok % l_ctx == 0 and n_tok % GRID_W == 0 and l_ctx % CHUNK == 0
    tm = _pick_tile(n_tok, ctx_tokens)
    tq_lat = _pick_tile(n_tok) // 4 if _pick_tile(n_tok) >= 512 else 64

    def mod_row(i):
        return jnp.where(i * tm < lat_tokens, (i * tm) // n_tok, n_lat)

    tm_moe = MOE_TILE if (n_tok % MOE_TILE == 0 and ctx_tokens % MOE_TILE == 0) else tm

    def mod_row_moe(i):
        return jnp.where(i * tm_moe < lat_tokens, (i * tm_moe) // n_tok, n_lat)

    tokens = lat_tokens + ctx_tokens
    xs = (x_sample.reshape(lat_tokens, D_MODEL), x_prompt.reshape(ctx_tokens, D_MODEL), 0, lat_tokens // tm)
    cond = jnp.zeros((8, D_MODEL), F32).at[:n_lat].set(c).at[n_lat].set(c_ctx)
    mod = _ada(cond, w_ada, b_ada).reshape(depth, 8, 6, D_MODEL)

    gla_pm = _gla_pair_masks()
    lsm, lbt, pm = _gdn_consts()
    bm = jnp.asarray(_block_mask(), F32)
    bm16 = bm.astype(BF16)
    sm = _solve_masks()
    cos, sin, perm = _rope_consts(n_tok)
    edge = _seq_edges(lat_tokens, n_tok, ctx_tokens, l_ctx)
    lane = np.arange(MIX_W)
    small_rows = np.arange(SMALL_W)
    def expand_cols(first_col):
        return jnp.asarray((small_rows[:, None] == first_col + lane[None, :] // HEAD_DIM).astype(np.float32), BF16)
    ea = jnp.stack([expand_cols(32), expand_cols(36)])
    eb = jnp.stack([expand_cols(40), expand_cols(44)])
    wr = jnp.zeros((D_MODEL, SMALL_W), F32).at[:, :N_EXPERTS].set(w_router)
    br = jnp.zeros((1, SMALL_W), F32).at[0, :N_EXPERTS].set(b_router)

    def cols(w, *names):
        return [w[:, _OFF[n][0]:_OFF[n][1]] for n in names]

    zeros_state = jnp.zeros((n_ctx, 2, MIX_W, MIX_W), F32)
    new_kv, new_gla, new_gdn = [], [], []
    for l in range(depth):
        lam_init = 0.8 - 0.6 * math.exp(-0.3 * l)
        wl = w_in[l]
        wcat = jnp.concatenate(
            cols(wl, 'na_q', 'na_k', 'na_v', 'gla_q', 'gla_k', 'gla_v', 'gla_og', 'gdn_qkv', 'gdn_og',
                 'diff_q', 'diff_k', 'diff_v', 'gla_gf', 'gla_gb', 'gdn_af', 'gdn_ab', 'gdn_bf', 'gdn_bb')
            + [jnp.zeros((D_MODEL, SMALL_W - 2 * GATE_RANK - 4 * HEADS), F32)], axis=1).astype(BF16)
        w_gate = wl[:, _OFF['branch_gate'][0]:].astype(BF16)
        z_na, z_gla, z_gdn, z_diff, z_small = _proj(xs, tokens, mod[l], norm1_w[l][None], wcat, tm, mod_row)

        ck = cache_na_k[:, l].reshape(n_lat * p_len, MIX_W)
        cv = cache_na_v[:, l].reshape(n_lat * p_len, MIX_W)
        dl = diff_lambda[l]
        na_lat = _na_latent(z_na, ck, cv, _na_bias_table(na_rpb[l], min(NA_WIN_H, n_tok // GRID_W)),
                            n_lat, n_tok, p_len)
        na_ctx = _attention(z_na, 0, z_na, 1, z_na, 2, dl, n_sub=1, lam_init=0.0, n_seq=n_ctx, lq=l_ctx,
                            lk=l_ctx, q_base=lat_tokens, k_base=lat_tokens, tq=l_ctx, name="na_ctx")

        q_r, k_r = _rope(z_diff, cos, sin, perm, n_lat, n_tok, tm)
        kk = jnp.concatenate([k_r.reshape(n_lat, n_tok, MIX_W), cache_diff_k[:, l].reshape(n_lat, p_len, MIX_W)], 1)
        vv = jnp.concatenate([z_diff[:lat_tokens, 2 * MIX_W:].reshape(n_lat, n_tok, MIX_W),
                              cache_diff_v[:, l].reshape(n_lat, p_len, MIX_W)], 1)
        lkv = n_tok + p_len
        df_lat = _attention(q_r, 0, kk.reshape(n_lat * lkv, MIX_W), 0, vv.reshape(n_lat * lkv, MIX_W), 0, dl,
                            n_sub=2, lam_init=lam_init, n_seq=n_lat, lq=n_tok, lk=lkv, q_base=0, k_base=0,
                            tq=tq_lat, name="diff_latent")
        df_ctx = _attention(z_diff, 0, z_diff, 1, z_diff, 2, dl, n_sub=2, lam_init=lam_init, n_seq=n_ctx,
                            lq=l_ctx, lk=l_ctx, q_base=lat_tokens, k_base=lat_tokens, tq=l_ctx, name="diff_ctx")

        gup = jnp.zeros((2, SMALL_W, MIX_W), F32)
        gup = gup.at[0, 0:GATE_RANK].set(gla_gate_up[l, 0]).at[1, GATE_RANK:2 * GATE_RANK].set(gla_gate_up[l, 1])
        gbias = gla_gate_bias[l][:, None, :]
        gla_consts = (gup, gbias, lsm, gla_pm, bm, bm16)
        glf_lat, glb_lat, _ = _scan(_gla_step, z_gla, z_small, gla_consts,
                                    _block_diag_state(state_gla[:, l], True), n_lat, n_tok, 0,
                                    GLA_SEQS_PER_STEP, "gla_scan")
        glf_ctx, glb_ctx, st_ctx = _scan(_gla_step, z_gla, z_small, gla_consts, zeros_state,
                                         n_ctx, l_ctx, lat_tokens, GLA_SEQS_PER_STEP, "gla_scan")

        qkv_n = _gdn_pre(z_gdn, edge, gdn_conv_w[l], bm16, tm)
        alog = jnp.repeat(gdn_A_log[l], HEAD_DIM, axis=-1)[:, None, :]
        dtb = jnp.repeat(gdn_dt_bias[l], HEAD_DIM, axis=-1)[:, None, :]
        gdn_consts = (ea, eb, alog, dtb, lsm, lbt, pm, bm, bm16, sm)
        gdf_lat, gdb_lat, _ = _scan(_gdn_step, qkv_n, z_small, gdn_consts,
                                    _block_diag_state(state_gdn[:, l], False), n_lat, n_tok, 0,
                                    GDN_SEQS_PER_STEP, "gdn_scan")
        gdf_ctx, gdb_ctx, s_ctx = _scan(_gdn_step, qkv_n, z_small, gdn_consts, zeros_state,
                                        n_ctx, l_ctx, lat_tokens, GDN_SEQS_PER_STEP, "gdn_scan")

        hn = jnp.stack([jnp.tile(gla_norm_w[l], HEADS), jnp.tile(gdn_norm_w[l], HEADS),
                        jnp.tile(diff_subln_w[l], HEADS)])[:, None, :]
        x = _merge(xs, tokens, mod[l], norm1_w[l][None], (na_lat, na_ctx), (glf_lat, glf_ctx), (glb_lat, glb_ctx), z_gla,
                   (gdf_lat, gdf_ctx), (gdb_lat, gdb_ctx), z_gdn, (df_lat, df_ctx), hn, bm16,
                   w_gate, w_branch[l].astype(BF16), w_out[l].astype(BF16), tm, mod_row, lam_init,
                   lat_tokens // tm)
        w1 = w_e1[l].astype(BF16)
        w3 = w_e3[l].astype(BF16)
        w2 = w_e2[l].reshape(N_EXPERTS * D_FF, D_MODEL).astype(BF16)
        x = _moe(x, mod[l], norm2_w[l][None], wr, br, w1, w3, w2, tm_moe, mod_row_moe)
        xs = (x, x, lat_tokens // tm, lat_tokens // tm)

        def ctx_heads(z, col):
            return z[lat_tokens:, col * MIX_W:(col + 1) * MIX_W].reshape(n_ctx, l_ctx, HEADS, HEAD_DIM)
        new_kv.append((ctx_heads(z_na, 1), ctx_heads(z_na, 2), ctx_heads(z_diff, 1), ctx_heads(z_diff, 2)))
        new_gla.append(_diag_blocks(st_ctx, True))
        new_gdn.append(_diag_blocks(s_ctx, False))

    y_sample = _final_norm(x, final_norm_w[None], tm, 0, lat_tokens // tm).reshape(n_lat, n_tok, D_MODEL)
    y_prompt = _final_norm(x, final_norm_w[None], tm, lat_tokens // tm, ctx_tokens // tm).reshape(
        n_ctx, l_ctx, D_MODEL)
    stack = lambda j: jnp.stack([t[j] for t in new_kv], axis=1)
    return (y_prompt, y_sample, stack(0), stack(1), stack(2), stack(3),
            jnp.stack(new_gla, axis=1), jnp.stack(new_gdn, axis=1))
```

```python
import functools
import math

import numpy as np
import jax
import jax.numpy as jnp
from jax import lax
from jax.experimental import pallas as pl
from jax.experimental.pallas import tpu as pltpu

F32 = jnp.float32
BF16 = jnp.bfloat16

D_MODEL = 1024
HEADS = 4
HEAD_DIM = 64
MIX_W = HEADS * HEAD_DIM
GRID_W = 64
NA_WIN_H = 8
NA_WIN_W = 16
GATE_RANK = 16
GATE_NORM = 16.0
GDN_CONV = 3
CHUNK = 64
N_EXPERTS = 16
N_GROUPS = 4
GROUP_E = N_EXPERTS // N_GROUPS
D_FF = D_MODEL // 4
ROPE_BASE = 10000.0
EPS = 1e-6
NEG = -1e30
LOG2E = 1.4426950408889634
SMALL_W = 128
V7X_VMEM_LIMIT = 56 * 1024 * 1024
LEAF = 8
ATTN_HEAD_GROUP = 2
NA_ROWS_PER_STEP = 4
GLA_SEQS_PER_STEP = 4
GDN_SEQS_PER_STEP = 2
MOE_TILE = 512

_IN_SPLITS = (
    ('na_q', MIX_W), ('na_k', MIX_W), ('na_v', MIX_W),
    ('gla_q', MIX_W), ('gla_k', MIX_W), ('gla_v', MIX_W),
    ('gla_gf', GATE_RANK), ('gla_gb', GATE_RANK), ('gla_og', MIX_W),
    ('gdn_qkv', 3 * MIX_W), ('gdn_af', HEADS), ('gdn_ab', HEADS),
    ('gdn_bf', HEADS), ('gdn_bb', HEADS), ('gdn_og', MIX_W),
    ('diff_q', MIX_W), ('diff_k', MIX_W), ('diff_v', MIX_W),
    ('branch_gate', 4 * D_MODEL),
)
_OFF = {}
_o = 0
for _n, _s in _IN_SPLITS:
    _OFF[_n] = (_o, _o + _s)
    _o += _s

def _params(sem):
    return pltpu.CompilerParams(dimension_semantics=sem, vmem_limit_bytes=V7X_VMEM_LIMIT)


def _dot(a, b):
    return jnp.dot(a, b, preferred_element_type=F32)


def _bdot(a, b):
    return _dot(a.astype(BF16), b.astype(BF16))


def _bdot_nt(a, b):
    return lax.dot_general(a.astype(BF16), b.astype(BF16), (((1,), (1,)), ((), ())),
                           preferred_element_type=F32)


def _bdot_tn(a, b):
    return lax.dot_general(a.astype(BF16), b.astype(BF16), (((0,), (0,)), ((), ())),
                           preferred_element_type=F32)


def _split(x):
    hi = x.astype(BF16)
    lo = (x - hi.astype(F32)).astype(BF16)
    return hi, lo


def _dot_rx(x, m):
    hi, lo = _split(x)
    return _dot(hi, m) + _dot(lo, m)


def _dot_lx(m, x):
    hi, lo = _split(x)
    return _dot(m, hi) + _dot(m, lo)


def _dot3(a, b):
    ah, al = _split(a)
    bh, bl = _split(b)
    return _dot(ah, bh) + _dot(ah, bl) + _dot(al, bh)


def _silu(x):
    return x * jax.nn.sigmoid(x)


def _softplus(x):
    return jnp.maximum(x, 0.0) + jnp.log(1.0 + jnp.exp(-jnp.abs(x)))


def _rms_rows(x, w):
    return x * lax.rsqrt(jnp.mean(x * x, axis=-1, keepdims=True) + EPS) * w


def _head_sum(x, bm16):
    return _dot_rx(x, bm16)


def _bd_rows(x, bm):
    return jnp.concatenate([x] * HEADS, axis=0) * bm


def _lane_mask(lo, width, n=MIX_W):
    lane = lax.broadcasted_iota(jnp.int32, (1, n), 1)
    return (lane >= lo) & (lane < lo + width)


def _gdn_consts():
    c = CHUNK
    i = np.arange(c)[:, None]
    t = np.arange(c)[None, :]
    lb = (t <= i).astype(np.float32)
    lbl = (t > i).astype(np.float32)
    tot = np.ones((16, c), np.float32)
    strict = (t < i).astype(np.float32)
    incl = (t <= i).astype(np.float32)
    rev = lambda a: a[::-1, ::-1]
    lsm = np.stack([np.concatenate([lb, lbl, tot], 0), np.concatenate([rev(lb), rev(lbl), tot], 0)])
    lbt = np.stack([np.tile(lb.T, (1, HEADS)), np.tile(rev(lb).T, (1, HEADS))])
    pm = np.stack([np.stack([np.tile(strict, (1, HEADS)), np.tile(incl, (1, HEADS))]),
                   np.stack([np.tile(rev(strict), (1, HEADS)), np.tile(rev(incl), (1, HEADS))])])
    return jnp.asarray(lsm, BF16), jnp.asarray(lbt, F32), jnp.asarray(pm, F32)


def _gla_pair_masks():
    i = np.arange(CHUNK)[:, None]
    j = np.arange(CHUNK)[None, :]
    nblk = CHUNK // LEAF
    out = []
    for d in range(2):
        later = (i // LEAF > j // LEAF) if d == 0 else (i // LEAF < j // LEAF)
        order = (j <= i) if d == 0 else (j >= i)
        far = [(j // LEAF == blk) & later for blk in range(nblk)]
        near = [(i // LEAF == j // LEAF) & (j % LEAF == r) & order for r in range(LEAF)]
        out.append(np.stack([np.tile(m.astype(np.float32), (1, HEADS)) for m in far + near]))
    return jnp.asarray(np.stack(out), F32)


def _block_mask():
    r = np.arange(MIX_W)
    return (r[:, None] // HEAD_DIM == r[None, :] // HEAD_DIM).astype(np.float32)


def _solve_masks():
    r = np.arange(MIX_W)[:, None]
    c = np.arange(MIX_W)[None, :]
    out = [(r // LEAF == c // LEAF)]
    s = LEAF
    while s < CHUNK:
        out.append((r // (2 * s) == c // (2 * s)) & (r // s != c // s))
        s *= 2
    return jnp.asarray(np.stack(out).astype(np.float32))


def _rope_consts(n_tok):
    t = np.arange(n_tok)
    pos = np.stack([t // GRID_W, t % GRID_W], 0).astype(np.float32)
    lane = np.arange(MIX_W)
    u = lane % 32
    axis = u // 16
    w = u % 16
    first = w < 8
    inv = ROPE_BASE ** (-(w % 8).astype(np.float32) / 8.0)
    ang = pos[axis, :].T * inv[None, :]
    cos = np.cos(ang)
    sin = np.sin(ang) * np.where(first, -1.0, 1.0)[None, :]
    partner = np.where(first, lane + 8, lane - 8)
    perm = np.zeros((MIX_W, MIX_W), np.float32)
    perm[partner, lane] = 1.0
    return jnp.asarray(cos, F32), jnp.asarray(sin, F32), jnp.asarray(perm, BF16)


def _ada_kernel(c_ref, w_ref, b_ref, o_ref):
    c = c_ref[...]
    o_ref[0] = _bdot(_silu(c), w_ref[0]) + b_ref[0]


def _ada(cond, w_ada, b_ada):
    depth, _, n = w_ada.shape
    tn = 1536
    return pl.pallas_call(
        _ada_kernel,
        grid=(depth, n // tn),
        in_specs=[pl.BlockSpec((8, D_MODEL), lambda l, j: (0, 0)),
                  pl.BlockSpec((1, D_MODEL, tn), lambda l, j: (l, 0, j)),
                  pl.BlockSpec((1, 1, tn), lambda l, j: (l, 0, j))],
        out_specs=pl.BlockSpec((1, 8, tn), lambda l, j: (l, 0, j)),
        out_shape=jax.ShapeDtypeStruct((depth, 8, n), F32),
        compiler_params=_params(("parallel", "parallel")),
        name="ada_mod",
    )(cond, w_ada, b_ada.reshape(depth, 1, n))


def _proj_kernel(xl_ref, xc_ref, mod_ref, nw_ref, w_ref, o_na, o_gla, o_gdn, o_diff, o_small, *, n_lat_tiles):
    mod = mod_ref[0]
    x = jnp.where(pl.program_id(0) < n_lat_tiles, xl_ref[...], xc_ref[...])
    h = _rms_rows(x, nw_ref[...]) * (1.0 + mod[1:2]) + mod[0:1]
    hb = h.astype(BF16)
    o_na[...] = _dot(hb, w_ref[:, 0:768])
    o_gla[...] = _dot(hb, w_ref[:, 768:1792])
    o_gdn[...] = _dot(hb, w_ref[:, 1792:2816])
    o_diff[...] = _dot(hb, w_ref[:, 2816:3584])
    o_small[...] = _dot(hb, w_ref[:, 3584:3712])


def _stream_specs(xs, tm):
    _, _, ctx_first, n_lat_tiles = xs
    return [pl.BlockSpec((tm, D_MODEL), lambda i: (jnp.minimum(i, n_lat_tiles - 1), 0)),
            pl.BlockSpec((tm, D_MODEL), lambda i: (ctx_first + jnp.maximum(i - n_lat_tiles, 0), 0))]


def _proj(xs, t, mod_l, nw, wcat, tm, mod_row):
    widths = (768, 1024, 1024, 768, SMALL_W)
    return pl.pallas_call(
        functools.partial(_proj_kernel, n_lat_tiles=xs[3]),
        grid=(t // tm,),
        in_specs=_stream_specs(xs, tm) + [
                  pl.BlockSpec((1, 6, D_MODEL), lambda i: (mod_row(i), 0, 0)),
                  pl.BlockSpec((1, D_MODEL), lambda i: (0, 0)),
                  pl.BlockSpec(wcat.shape, lambda i: (0, 0))],
        out_specs=[pl.BlockSpec((tm, w), lambda i: (i, 0)) for w in widths],
        out_shape=[jax.ShapeDtypeStruct((t, w), F32) for w in widths],
        compiler_params=_params(("parallel",)),
        name="in_proj",
    )(xs[0], xs[1], mod_l, nw, wcat)


def _attn_core(q, k, v, n_sub, lam):
    kb = k.astype(BF16)
    vb = v.astype(BF16)
    tq = q.shape[0]
    sub_w = HEAD_DIM // n_sub
    scale = sub_w ** -0.5 * LOG2E
    out = jnp.zeros(q.shape, F32)
    for h0 in range(0, HEADS, ATTN_HEAD_GROUP):
        maps = [(h, m) for h in range(h0, h0 + ATTN_HEAD_GROUP) for m in range(n_sub)]
        qs = jnp.concatenate(
            [(q * jnp.where(_lane_mask(h * HEAD_DIM + m * sub_w, sub_w), scale, 0.0)).astype(BF16)
             for h, m in maps], axis=0)
        s = lax.dot_general(qs, kb, (((1,), (1,)), ((), ())), preferred_element_type=F32)
        e = jnp.exp2(s - jnp.max(s, axis=-1, keepdims=True))
        inv = 1.0 / jnp.sum(e, axis=-1, keepdims=True)
        if n_sub == 1:
            p = e * inv
        else:
            p = jnp.concatenate(
                [e[(2 * i) * tq:(2 * i + 1) * tq] * inv[(2 * i) * tq:(2 * i + 1) * tq]
                 - e[(2 * i + 1) * tq:(2 * i + 2) * tq] * (lam * inv[(2 * i + 1) * tq:(2 * i + 2) * tq])
                 for i in range(ATTN_HEAD_GROUP)], axis=0)
        o_all = _dot(p.astype(BF16), vb)
        for i in range(ATTN_HEAD_GROUP):
            out = jnp.where(_lane_mask((h0 + i) * HEAD_DIM, HEAD_DIM), o_all[i * tq:(i + 1) * tq], out)
    return out


def _diff_lambda(dl, lam_init):
    a = jnp.sum(dl[0:1] * dl[1:2], axis=-1, keepdims=True)
    b = jnp.sum(dl[2:3] * dl[3:4], axis=-1, keepdims=True)
    return jnp.exp(a) - jnp.exp(b) + lam_init


def _attn_kernel(q_ref, k_ref, v_ref, dl_ref, o_ref, *, n_sub, lam_init):
    lam = _diff_lambda(dl_ref[...], lam_init) if n_sub == 2 else None
    o_ref[...] = _attn_core(q_ref[...], k_ref[...], v_ref[...], n_sub, lam)


def _attention(q_arr, q_col, k_arr, k_col, v_arr, v_col, dl, *, n_sub, lam_init, n_seq, lq, lk,
               q_base, k_base, tq, name):
    nq = lq // tq
    return pl.pallas_call(
        functools.partial(_attn_kernel, n_sub=n_sub, lam_init=lam_init),
        grid=(n_seq, nq),
        in_specs=[pl.BlockSpec((tq, MIX_W), lambda s, j: (q_base // tq + s * nq + j, q_col)),
                  pl.BlockSpec((lk, MIX_W), lambda s, j: (k_base // lk + s, k_col)),
                  pl.BlockSpec((lk, MIX_W), lambda s, j: (k_base // lk + s, v_col)),
                  pl.BlockSpec(dl.shape, lambda s, j: (0, 0))],
        out_specs=pl.BlockSpec((tq, MIX_W), lambda s, j: (s * nq + j, 0)),
        out_shape=jax.ShapeDtypeStruct((n_seq * lq, MIX_W), F32),
        compiler_params=_params(("parallel", "parallel")),
        name=name,
    )(q_arr, k_arr, v_arr, dl)


DIFF_KEY_CHUNK = 256
DIFF_HEAD_GROUP = 1
DIFF_Q_TILE = 256


def _lane_fold(x, op):
    out = x[:, 0:128]
    for t in range(1, x.shape[1] // 128):
        out = op(out, x[:, t * 128:(t + 1) * 128])
    return out


def _diff_group(h0, q, kt_ref, v_ref, lam, tkc):
    tq = q.shape[0]
    nck = kt_ref.shape[1] // tkc
    sub_w = HEAD_DIM // 2
    scale = sub_w ** -0.5 * LOG2E
    maps = [(h, m) for h in range(h0, h0 + DIFF_HEAD_GROUP) for m in range(2)]
    qs = jnp.concatenate(
        [(q * jnp.where(_lane_mask(h * HEAD_DIM + m * sub_w, sub_w), scale, 0.0)).astype(BF16) for h, m in maps],
        axis=0)
    s, mvec = [], None
    for c in range(nck):
        sc = _dot(qs, kt_ref[:, c * tkc:(c + 1) * tkc])
        s.append(sc)
        mc = _lane_fold(sc, jnp.maximum)
        mvec = mc if mvec is None else jnp.maximum(mvec, mc)
        yield
    mrun = jnp.max(mvec, axis=-1, keepdims=True)
    e, lvec = [], None
    for c in range(nck):
        ec = jnp.exp2(s[c] - mrun)
        e.append(ec)
        lc = _lane_fold(ec, jnp.add)
        lvec = lc if lvec is None else lvec + lc
        yield
    inv = 1.0 / jnp.sum(lvec, axis=-1, keepdims=True)
    acc = None
    for c in range(nck):
        a = jnp.concatenate(
            [e[c][(2 * i) * tq:(2 * i + 1) * tq] * inv[(2 * i) * tq:(2 * i + 1) * tq]
             - e[c][(2 * i + 1) * tq:(2 * i + 2) * tq] * (lam * inv[(2 * i + 1) * tq:(2 * i + 2) * tq])
             for i in range(DIFF_HEAD_GROUP)], axis=0)
        o = _dot(a.astype(BF16), v_ref[c * tkc:(c + 1) * tkc, :])
        acc = o if acc is None else acc + o
        if c < nck - 1:
            yield
    return acc


_DIFF_PASSES = 3


def _diff_lat_kernel(q_ref, kt_ref, v_ref, dl_ref, o_ref, *, lam_init, tkc):
    lam = _diff_lambda(dl_ref[...], lam_init)
    q = q_ref[...]
    tq = q.shape[0]
    nck = kt_ref.shape[1] // tkc
    n_grp = HEADS // DIFF_HEAD_GROUP
    groups = [_diff_group(g * DIFF_HEAD_GROUP, q, kt_ref, v_ref, lam, tkc) for g in range(n_grp)]
    res = [None] * n_grp
    for t in range(n_grp + _DIFF_PASSES - 1):
        active = [g for g in range(n_grp) if 0 <= t - g < _DIFF_PASSES]
        for _ in range(nck):
            for g in active:
                try:
                    next(groups[g])
                except StopIteration as done:
                    res[g] = done.value
    out = jnp.zeros(q.shape, F32)
    for h in range(HEADS):
        g, i = divmod(h, DIFF_HEAD_GROUP)
        out = jnp.where(_lane_mask(h * HEAD_DIM, HEAD_DIM), res[g][i * tq:(i + 1) * tq], out)
    o_ref[...] = out


def _diff_latent(q_r, kk_t, vv16, dl, lam_init, n_b, n_tok, lk, tq):
    tkc = DIFF_KEY_CHUNK if lk % DIFF_KEY_CHUNK == 0 else 128
    assert lk % tkc == 0
    nq = n_tok // tq
    return pl.pallas_call(
        functools.partial(_diff_lat_kernel, lam_init=lam_init, tkc=tkc),
        grid=(n_b, nq),
        in_specs=[pl.BlockSpec((tq, MIX_W), lambda b, j: (b * nq + j, 0)),
                  pl.BlockSpec((MIX_W, lk), lambda b, j: (b, 0)),
                  pl.BlockSpec((lk, MIX_W), lambda b, j: (b, 0)),
                  pl.BlockSpec(dl.shape, lambda b, j: (0, 0))],
        out_specs=pl.BlockSpec((tq, MIX_W), lambda b, j: (b * nq + j, 0)),
        out_shape=jax.ShapeDtypeStruct((n_b * n_tok, MIX_W), F32),
        compiler_params=_params(("parallel", "arbitrary")),
        name="diff_latent",
    )(q_r, kk_t, vv16, dl)


def _na_lat_kernel(q_ref, k_ref, v_ref, ck_ref, cv_ref, bias_ref, o_ref, *, rows, kh, rps):
    g = pl.program_id(1)
    ck = ck_ref[...].astype(BF16)
    cv = cv_ref[...].astype(BF16)
    scale = HEAD_DIM ** -0.5 * LOG2E
    head_scale = [jnp.where(_lane_mask(h * HEAD_DIM, HEAD_DIM), scale, 0.0) for h in range(HEADS)]
    hw = HEADS * GRID_W
    qs = []
    for j in range(rps):
        qj = q_ref[j * GRID_W:(j + 1) * GRID_W, :]
        qs.append(jnp.concatenate([(qj * hs).astype(BF16) for hs in head_scale], axis=0))
    s_ctx_all = lax.dot_general(jnp.concatenate(qs, axis=0), ck, (((1,), (1,)), ((), ())),
                                preferred_element_type=F32)
    p_ctx, o_loc = [], []
    for j in range(rps):
        r = g * rps + j
        start = jnp.clip(r - kh // 2, 0, rows - kh)
        cls = r - start
        ws = pl.multiple_of(start * GRID_W, GRID_W)
        kw = k_ref[pl.ds(ws, kh * GRID_W), :].astype(BF16)
        vw = v_ref[pl.ds(ws, kh * GRID_W), :].astype(BF16)
        bias = jnp.concatenate([bias_ref[h, pl.ds(cls, 1)][0] for h in range(HEADS)], axis=0)
        s_loc = lax.dot_general(qs[j], kw, (((1,), (1,)), ((), ())), preferred_element_type=F32) + bias
        s_ctx = s_ctx_all[j * hw:(j + 1) * hw]
        mx = jnp.maximum(jnp.max(s_loc, axis=-1, keepdims=True), jnp.max(s_ctx, axis=-1, keepdims=True))
        e_loc = jnp.exp2(s_loc - mx)
        e_ctx = jnp.exp2(s_ctx - mx)
        inv = 1.0 / (jnp.sum(e_loc, axis=-1, keepdims=True) + jnp.sum(e_ctx, axis=-1, keepdims=True))
        p_ctx.append((e_ctx * inv).astype(BF16))
        o_loc.append(_dot((e_loc * inv).astype(BF16), vw))
    o_ctx_all = _dot(jnp.concatenate(p_ctx, axis=0), cv)
    for j in range(rps):
        o_all = o_loc[j] + o_ctx_all[j * hw:(j + 1) * hw]
        out = jnp.zeros((GRID_W, MIX_W), F32)
        for h in range(HEADS):
            out = jnp.where(_lane_mask(h * HEAD_DIM, HEAD_DIM), o_all[h * GRID_W:(h + 1) * GRID_W], out)
        o_ref[j * GRID_W:(j + 1) * GRID_W, :] = out


def _na_bias_table(rpb, kh):
    cidx = np.arange(GRID_W)
    cls = np.arange(kh)
    drow = np.arange(kh)[None, :] - cls[:, None] + NA_WIN_H - 1
    col_start = np.clip(cidx - NA_WIN_W // 2, 0, GRID_W - NA_WIN_W)
    col_ok = (cidx[None, :] >= col_start[:, None]) & (cidx[None, :] < col_start[:, None] + NA_WIN_W)
    dcol = np.clip(cidx[None, :] - cidx[:, None], 1 - NA_WIN_W, NA_WIN_W - 1) + NA_WIN_W - 1
    toep = rpb.astype(F32)[..., dcol]
    bias = jnp.stack([toep[:, drow[c, 0]:drow[c, 0] + kh] for c in range(kh)], axis=1)
    bias = bias.transpose(0, 1, 3, 2, 4)
    bias = jnp.where(col_ok[:, None, :], bias * LOG2E, NEG)
    return bias.reshape(HEADS, kh, GRID_W, kh * GRID_W)


def _na_latent(z_na, ck, cv, bias, n_b, n_tok, p_len):
    rows = n_tok // GRID_W
    kh = min(NA_WIN_H, rows)
    rps = NA_ROWS_PER_STEP
    assert rows % rps == 0
    steps = rows // rps
    return pl.pallas_call(
        functools.partial(_na_lat_kernel, rows=rows, kh=kh, rps=rps),
        grid=(n_b, steps),
        in_specs=[pl.BlockSpec((rps * GRID_W, MIX_W), lambda b, r: (b * steps + r, 0)),
                  pl.BlockSpec((n_tok, MIX_W), lambda b, r: (b, 1)),
                  pl.BlockSpec((n_tok, MIX_W), lambda b, r: (b, 2)),
                  pl.BlockSpec((p_len, MIX_W), lambda b, r: (b, 0)),
                  pl.BlockSpec((p_len, MIX_W), lambda b, r: (b, 0)),
                  pl.BlockSpec(bias.shape, lambda b, r: (0, 0, 0, 0))],
        out_specs=pl.BlockSpec((rps * GRID_W, MIX_W), lambda b, r: (b * steps + r, 0)),
        out_shape=jax.ShapeDtypeStruct((n_b * n_tok, MIX_W), F32),
        compiler_params=_params(("parallel", "arbitrary")),
        name="na_latent",
    )(z_na, z_na, z_na, ck, cv, bias)


def _rope_kernel(q_ref, k_ref, cos_ref, sin_ref, p_ref, qo_ref, ko_ref):
    cos = cos_ref[...]
    sin = sin_ref[...]
    p = p_ref[...]
    q = q_ref[...]
    k = k_ref[...]
    qo_ref[...] = q * cos + _dot_rx(q, p) * sin
    ko_ref[...] = k * cos + _dot_rx(k, p) * sin


def _rope(z_diff, cos, sin, perm, n_b, n_tok, tm):
    nt = n_tok // tm
    spec_t = pl.BlockSpec((tm, MIX_W), lambda i: (i % nt, 0))
    return pl.pallas_call(
        _rope_kernel,
        grid=(n_b * nt,),
        in_specs=[pl.BlockSpec((tm, MIX_W), lambda i: (i, 0)),
                  pl.BlockSpec((tm, MIX_W), lambda i: (i, 1)),
                  spec_t, spec_t,
                  pl.BlockSpec(perm.shape, lambda i: (0, 0))],
        out_specs=[pl.BlockSpec((tm, MIX_W), lambda i: (i, 0))] * 2,
        out_shape=[jax.ShapeDtypeStruct((n_b * n_tok, MIX_W), F32)] * 2,
        compiler_params=_params(("parallel",)),
        name="diff_rope",
    )(z_diff, z_diff, cos, sin, perm)


def _gdn_pre_kernel(x_ref, prev_ref, next_ref, edge_ref, w_ref, bm_ref, o_ref, *, tm):
    x = x_ref[...]
    w = w_ref[...]
    edge = edge_ref[...]
    row = lax.broadcasted_iota(jnp.int32, (tm, 1), 0)
    x_prev = jnp.where(row == 0, prev_ref[7:8, :], pltpu.roll(x, 1, 0)) * edge[:, 0:1]
    x_next = jnp.where(row == tm - 1, next_ref[0:1, :], pltpu.roll(x, tm - 1, 0)) * edge[:, 1:2]
    y = _silu(x_prev * w[0:1] + x * w[1:2] + x_next * w[2:3])
    bm16 = bm_ref[...]
    q = y[:, 0:MIX_W]
    k = y[:, MIX_W:2 * MIX_W]
    o_ref[:, 0:MIX_W] = q * lax.rsqrt(_head_sum(q * q, bm16) + EPS) * (HEAD_DIM ** -0.5)
    o_ref[:, MIX_W:2 * MIX_W] = k * lax.rsqrt(_head_sum(k * k, bm16) + EPS)
    o_ref[:, 2 * MIX_W:3 * MIX_W] = y[:, 2 * MIX_W:3 * MIX_W]


def _seq_edges(lat_tokens, n_tok, ctx_tokens, l_ctx):
    pos = np.concatenate([np.arange(lat_tokens) % n_tok, np.arange(ctx_tokens) % l_ctx])
    seq = np.concatenate([np.full(lat_tokens, n_tok), np.full(ctx_tokens, l_ctx)])
    edge = np.zeros((lat_tokens + ctx_tokens, 8), np.float32)
    edge[:, 0] = pos != 0
    edge[:, 1] = pos != seq - 1
    return jnp.asarray(edge)


def _gdn_pre(z_gdn, edge, conv_w, bm16, tm):
    t = z_gdn.shape[0]
    w3 = 3 * MIX_W
    nb8 = t // 8
    return pl.pallas_call(
        functools.partial(_gdn_pre_kernel, tm=tm),
        grid=(t // tm,),
        in_specs=[pl.BlockSpec((tm, w3), lambda i: (i, 0)),
                  pl.BlockSpec((8, w3), lambda i: (jnp.maximum(i * (tm // 8) - 1, 0), 0)),
                  pl.BlockSpec((8, w3), lambda i: (jnp.minimum((i + 1) * (tm // 8), nb8 - 1), 0)),
                  pl.BlockSpec((tm, 8), lambda i: (i, 0)),
                  pl.BlockSpec(conv_w.shape, lambda i: (0, 0)),
                  pl.BlockSpec(bm16.shape, lambda i: (0, 0))],
        out_specs=pl.BlockSpec((tm, w3), lambda i: (i, 0)),
        out_shape=jax.ShapeDtypeStruct((t, w3), F32),
        compiler_params=_params(("parallel",)),
        name="gdn_pre",
    )(z_gdn, z_gdn, z_gdn, edge, conv_w, bm16)


def _bd_rows16(x, bm16):
    return jnp.concatenate([x.astype(BF16)] * HEADS, axis=0) * bm16


def _gla_chain(q, k, v, zs, gup, gb, lsm, pm_ref, d, bm, bm16, st):
    nblk = CHUNK // LEAF
    q = q * (HEAD_DIM ** -0.5)
    gp = _bdot(zs, gup) + gb
    g = (jnp.minimum(gp, 0.0) - jnp.log(1.0 + jnp.exp(-jnp.abs(gp)))) * (LOG2E / GATE_NORM)
    yield
    e = _dot_lx(lsm, g)
    b = e[0:CHUNK]
    fb = jnp.exp2(b)
    fbl = jnp.exp2(e[CHUNK:2 * CHUNK])
    ftot = jnp.exp2(e[2 * CHUNK:2 * CHUNK + 1])
    yield
    edge = LEAF - 1 if d == 0 else 0
    blocks = range(nblk - 1) if d == 0 else range(1, nblk)
    b3 = b.reshape(nblk, LEAF, MIX_W)
    b_edge = jnp.broadcast_to(b3[:, edge:edge + 1, :], b3.shape).reshape(CHUNK, MIX_W)
    kf = jnp.exp2(jnp.minimum(b_edge - b, 0.0))
    q_far = jnp.concatenate(
        [(q * jnp.exp2(jnp.minimum(b - b[blk * LEAF + edge:blk * LEAF + edge + 1, :], 0.0))).astype(BF16)
         for blk in blocks], axis=0)
    s_far = lax.dot_general(q_far, _bd_rows16(k * kf, bm16), (((1,), (1,)), ((), ())),
                            preferred_element_type=F32)
    yield
    q_near = jnp.concatenate(
        [(q * jnp.exp2(jnp.minimum(
            b - jnp.broadcast_to(b3[:, r:r + 1, :], b3.shape).reshape(CHUNK, MIX_W), 0.0))).astype(BF16)
         for r in range(LEAF)], axis=0)
    s_near = lax.dot_general(q_near, _bd_rows16(k, bm16), (((1,), (1,)), ((), ())),
                             preferred_element_type=F32)
    yield
    att = None
    for n, blk in enumerate(blocks):
        term = s_far[n * CHUNK:(n + 1) * CHUNK] * pm_ref[d, blk]
        att = term if att is None else att + term
    for r in range(LEAF):
        att = att + s_near[r * CHUNK:(r + 1) * CHUNK] * pm_ref[d, nblk + r]
    yield
    o = _bdot_nt(q * fb, st) + _dot(att.astype(BF16), _bd_rows16(v, bm16))
    yield
    return o, st * ftot + _bdot_tn(v, k * fbl) * bm


def _tri_inverse(mbd, sm_ref):
    r = lax.broadcasted_iota(jnp.int32, (MIX_W, MIX_W), 0)
    c = lax.broadcasted_iota(jnp.int32, (MIX_W, MIX_W), 1)
    eye = jnp.where(r == c, 1.0, 0.0)
    md = mbd * sm_ref[0]
    m2 = _bdot(md, md)
    yield
    m4 = _bdot(m2, m2)
    t = eye - md
    t = t + _bdot(t, m2)
    yield
    t = t + _bdot(t, m4)
    for lvl in range(1, sm_ref.shape[0]):
        yield
        tm = _bdot(t, mbd * sm_ref[lvl])
        yield
        t = t - _bdot(tm, t)
    return t


def _gdn_chain(q, k, v, zs, ea, eb, alog, dtb, lsm, lbt, pm_ref, d, bm, bm16, sm_ref, s):
    g = -jnp.exp(alog) * _softplus(_dot_rx(zs, ea) + dtb)
    beta = jax.nn.sigmoid(_dot_rx(zs, eb))
    yield
    e = _dot_lx(lsm, g)
    dcol = e[0:CHUNK]
    drow = _dot_lx(jnp.ones((8, CHUNK), BF16), g * lbt)[0:1]
    kb = k * beta
    s2 = lax.dot_general(jnp.concatenate([kb, q], axis=0).astype(BF16), _bd_rows16(k, bm16),
                         (((1,), (1,)), ((), ())), preferred_element_type=F32)
    yield
    dec = jnp.exp(jnp.where(pm_ref[d, 1] > 0.0, dcol - drow, NEG))
    m_cat = s2[0:CHUNK] * dec * pm_ref[d, 0]
    att = s2[CHUNK:2 * CHUNK] * dec
    t_inv = yield from _tri_inverse(_bd_rows(m_cat, bm), sm_ref)
    t16 = (t_inv[0:CHUNK] + t_inv[CHUNK:2 * CHUNK] + t_inv[2 * CHUNK:3 * CHUNK] + t_inv[3 * CHUNK:]).astype(BF16)
    fb = jnp.exp(dcol)
    fbl = jnp.exp(e[CHUNK:2 * CHUNK])
    ftot = jnp.exp(e[2 * CHUNK:2 * CHUNK + 1])
    yield
    u = _dot(t16, _bd_rows16(v * beta, bm16))
    w = _dot(t16, _bd_rows16(kb * fb, bm16))
    yield
    ws = _dot(jnp.concatenate([w.astype(BF16), (q * fb).astype(BF16)], axis=0), s.astype(BF16))
    v_new = u - ws[0:CHUNK]
    yield
    o = ws[CHUNK:2 * CHUNK] + _dot(att.astype(BF16), _bd_rows16(v_new, bm16))
    return o, s * ftot + lax.dot_general((k * fbl).astype(BF16), v_new.astype(BF16), (((0,), (0,)), ((), ())),
                                         preferred_element_type=F32) * bm


def _scan_kernel(*refs, chain, n_const, nb, n_chunks):
    n_tok = nb * 2 * 4
    tok = refs[:n_tok]
    consts = refs[n_tok:n_tok + n_const]
    s0_ref, of_ref, ob_ref, sf_ref, s_scr = refs[n_tok + n_const:]
    n = pl.program_id(1)

    @pl.when(n == 0)
    def _():
        s_scr[...] = s0_ref[...]

    ids = [(j, d) for j in range(nb) for d in range(2)]
    gens = []
    for j, d in ids:
        q_ref, k_ref, v_ref, zs_ref = tok[(j * 2 + d) * 4:(j * 2 + d) * 4 + 4]
        gens.append(chain(q_ref[...], k_ref[...], v_ref[...], zs_ref[...], consts, d, s_scr[j, d]))
    live = list(range(len(gens)))
    while live:
        for c in list(live):
            try:
                next(gens[c])
            except StopIteration as done:
                j, d = ids[c]
                o, s_new = done.value
                (of_ref, ob_ref)[d][j] = o
                s_scr[j, d] = s_new
                live.remove(c)

    @pl.when(n == n_chunks - 1)
    def _():
        sf_ref[...] = s_scr[...]


def _gla_step(q, k, v, zs, consts, d, st):
    gup_ref, gb_ref, lsm_ref, pm_ref, bm_ref, bm16_ref = consts
    return _gla_chain(q, k, v, zs, gup_ref[d], gb_ref[d], lsm_ref[d], pm_ref, d, bm_ref[...], bm16_ref[...], st)


def _gdn_step(q, k, v, zs, consts, d, s):
    ea_ref, eb_ref, alog_ref, dtb_ref, lsm_ref, lbt_ref, pm_ref, bm_ref, bm16_ref, sm_ref = consts
    return _gdn_chain(q, k, v, zs, ea_ref[d], eb_ref[d], alog_ref[d], dtb_ref[d], lsm_ref[d], lbt_ref[d],
                      pm_ref, d, bm_ref[...], bm16_ref[...], sm_ref, s)


def _scan(chain, qkv, z_small, consts, s0, n_seq, l_seq, base, seqs_per_step, name):
    nc = l_seq // CHUNK
    nb = seqs_per_step if n_seq % seqs_per_step == 0 else 1

    def chunk(n, d):
        return n + d * (nc - 1 - 2 * n)

    in_specs, args = [], []
    for j in range(nb):
        for d in range(2):
            for arr, col, width in ((qkv, 0, MIX_W), (qkv, 1, MIX_W), (qkv, 2, MIX_W), (z_small, 0, SMALL_W)):
                in_specs.append(pl.BlockSpec(
                    (CHUNK, width),
                    functools.partial(lambda g, n, j, d, col: (base // CHUNK + (g * nb + j) * nc + chunk(n, d), col),
                                      j=j, d=d, col=col)))
                args.append(arr)
    for cst in consts:
        in_specs.append(pl.BlockSpec(cst.shape, functools.partial(lambda g, n, nd: (0,) * nd, nd=cst.ndim)))
        args.append(cst)
    state_spec = pl.BlockSpec((nb, 2, MIX_W, MIX_W), lambda g, n: (g, 0, 0, 0))
    in_specs.append(state_spec)
    args.append(s0)
    o_f, o_b, s_f = pl.pallas_call(
        functools.partial(_scan_kernel, chain=chain, n_const=len(consts), nb=nb, n_chunks=nc),
        grid=(n_seq // nb, nc),
        in_specs=in_specs,
        out_specs=[pl.BlockSpec((nb, CHUNK, MIX_W), lambda g, n: (g, chunk(n, 0), 0)),
                   pl.BlockSpec((nb, CHUNK, MIX_W), lambda g, n: (g, chunk(n, 1), 0)),
                   state_spec],
        out_shape=[jax.ShapeDtypeStruct((n_seq, l_seq, MIX_W), F32),
                   jax.ShapeDtypeStruct((n_seq, l_seq, MIX_W), F32),
                   jax.ShapeDtypeStruct((n_seq, 2, MIX_W, MIX_W), F32)],
        scratch_shapes=[pltpu.VMEM((nb, 2, MIX_W, MIX_W), F32)],
        compiler_params=_params(("parallel", "arbitrary")),
        name=name,
    )(*args)
    return o_f.reshape(n_seq * l_seq, MIX_W), o_b.reshape(n_seq * l_seq, MIX_W), s_f


def _merge_kernel(xl_ref, xc_ref, mod_ref, n1_ref, na_l, na_c, glaf_l, glaf_c, glab_l, glab_c, glag_ref, gdnf_l, gdnf_c,
                  gdnb_l, gdnb_c, gdng_ref, df_l, df_c, hn_ref, bm_ref, wg_ref, wb_ref, wo_ref, o_ref, *,
                  lam_init, n_lat_tiles):
    is_lat = pl.program_id(0) < n_lat_tiles

    def pick(lat_ref, ctx_ref):
        return jnp.where(is_lat, lat_ref[...], ctx_ref[...])

    mod = mod_ref[0]
    x = pick(xl_ref, xc_ref)
    h = _rms_rows(x, n1_ref[...]) * (1.0 + mod[1:2]) + mod[0:1]
    hb = h.astype(BF16)
    bm16 = bm_ref[...]

    def head_norm(o, w):
        return o * lax.rsqrt(_head_sum(o * o, bm16) * (1.0 / HEAD_DIM) + EPS) * w

    branches = (
        pick(na_l, na_c),
        head_norm(pick(glaf_l, glaf_c) + pick(glab_l, glab_c), hn_ref[0]) * _silu(glag_ref[...]),
        head_norm(pick(gdnf_l, gdnf_c) + pick(gdnb_l, gdnb_c), hn_ref[1]) * _silu(gdng_ref[...]),
        head_norm(pick(df_l, df_c), hn_ref[2]) * (1.0 - lam_init),
    )
    merged = None
    for n, br in enumerate(branches):
        gate = jax.nn.sigmoid(_dot(hb, wg_ref[:, n * D_MODEL:(n + 1) * D_MODEL]))
        term = gate * _bdot(br, wb_ref[n])
        merged = term if merged is None else merged + term
    o_ref[...] = x + mod[2:3] * _bdot(merged, wo_ref[...])


def _merge(xs, t, mod_l, n1, na, gla_f, gla_b, z_gla, gdn_f, gdn_b, z_gdn, df, hn, bm16, wg, wb, wo, tm, mod_row,
           lam_init, n_lat_tiles):
    tok = lambda col: pl.BlockSpec((tm, MIX_W), lambda i: (i, col))
    lat = pl.BlockSpec((tm, MIX_W), lambda i: (jnp.minimum(i, n_lat_tiles - 1), 0))
    ctx = pl.BlockSpec((tm, MIX_W), lambda i: (jnp.maximum(i - n_lat_tiles, 0), 0))
    full = lambda a: pl.BlockSpec(a.shape, lambda i: (0,) * a.ndim)
    return pl.pallas_call(
        functools.partial(_merge_kernel, lam_init=lam_init, n_lat_tiles=n_lat_tiles),
        grid=(t // tm,),
        in_specs=_stream_specs(xs, tm) + [
                  pl.BlockSpec((1, 6, D_MODEL), lambda i: (mod_row(i), 0, 0)),
                  full(n1), lat, ctx, lat, ctx, lat, ctx, tok(3), lat, ctx, lat, ctx, tok(3), lat, ctx,
                  full(hn), full(bm16), full(wg), full(wb), full(wo)],
        out_specs=pl.BlockSpec((tm, D_MODEL), lambda i: (i, 0)),
        out_shape=jax.ShapeDtypeStruct((t, D_MODEL), F32),
        compiler_params=_params(("parallel",)),
        name="merge_out",
    )(xs[0], xs[1], mod_l, n1, *na, *gla_f, *gla_b, z_gla, *gdn_f, *gdn_b, z_gdn, *df, hn, bm16, wg, wb, wo)


def _route(h2, wr_ref, br_ref):
    lane_i = lax.broadcasted_iota(jnp.int32, (1, SMALL_W), 1)
    lane = lane_i.astype(F32)
    gid = lax.shift_right_logical(lane_i, 2).astype(F32)
    s = jax.nn.sigmoid(_dot3(h2, wr_ref[...]))
    sel = s + br_ref[...]
    far = float(SMALL_W)

    def first_max(vals):
        mx = jnp.max(vals, axis=-1, keepdims=True)
        idx = jnp.min(jnp.where(vals == mx, lane, far), axis=-1, keepdims=True)
        return mx, idx

    best = None
    for g in range(N_GROUPS):
        vals = jnp.where(gid == float(g), sel, NEG)
        a, ia = first_max(vals)
        b, _ = first_max(jnp.where(lane == ia, NEG, vals))
        score = a + b
        if best is None:
            best, gi = score, jnp.zeros_like(score)
        else:
            better = score > best
            gi = jnp.where(better, float(g), gi)
            best = jnp.where(better, score, best)
    vals = jnp.where(gid == gi, sel, NEG)
    _, i1 = first_max(vals)
    _, i2 = first_max(jnp.where(lane == i1, NEG, vals))
    s1 = jnp.sum(jnp.where(lane == i1, s, 0.0), axis=-1, keepdims=True)
    s2 = jnp.sum(jnp.where(lane == i2, s, 0.0), axis=-1, keepdims=True)
    inv = 1.0 / (s1 + s2)
    return jnp.where(lane == i1, s1 * inv, 0.0) + jnp.where(lane == i2, s2 * inv, 0.0)


def _moe_kernel(x_ref, mod_ref, n2_ref, wr_ref, br_ref, w1_ref, w3_ref, w2_ref, o_ref,
                h_scr, g_scr, acc_scr):
    g = pl.program_id(1)
    mod = mod_ref[0]

    @pl.when(g == 0)
    def _():
        h2 = _rms_rows(x_ref[...], n2_ref[...]) * (1.0 + mod[4:5]) + mod[3:4]
        h_scr[...] = h2.astype(BF16)
        g_scr[...] = _route(h2, wr_ref, br_ref)
        acc_scr[...] = jnp.zeros_like(acc_scr)

    hb = h_scr[...]
    gates = g_scr[...]
    lane = lax.broadcasted_iota(jnp.int32, (1, SMALL_W), 1)
    he = jnp.concatenate(
        [_silu(_dot(hb, w1_ref[e])) * _dot(hb, w3_ref[e])
         * jnp.sum(jnp.where(lane == g * GROUP_E + e, gates, 0.0), axis=-1, keepdims=True)
         for e in range(GROUP_E)], axis=1)
    acc_scr[...] += _dot(he.astype(BF16), w2_ref[...])

    @pl.when(g == N_GROUPS - 1)
    def _():
        o_ref[...] = x_ref[...] + mod[5:6] * acc_scr[...]


def _moe(x, mod_l, n2, wr, br, w1, w3, w2, tm, mod_row):
    t = x.shape[0]
    gw = GROUP_E * D_FF
    return pl.pallas_call(
        _moe_kernel,
        grid=(t // tm, N_GROUPS),
        in_specs=[pl.BlockSpec((tm, D_MODEL), lambda i, g: (i, 0)),
                  pl.BlockSpec((1, 6, D_MODEL), lambda i, g: (mod_row(i), 0, 0)),
                  pl.BlockSpec(n2.shape, lambda i, g: (0, 0)),
                  pl.BlockSpec(wr.shape, lambda i, g: (0, 0)),
                  pl.BlockSpec(br.shape, lambda i, g: (0, 0)),
                  pl.BlockSpec((GROUP_E, D_MODEL, D_FF), lambda i, g: (g, 0, 0)),
                  pl.BlockSpec((GROUP_E, D_MODEL, D_FF), lambda i, g: (g, 0, 0)),
                  pl.BlockSpec((gw, D_MODEL), lambda i, g: (g, 0))],
        out_specs=pl.BlockSpec((tm, D_MODEL), lambda i, g: (i, 0)),
        out_shape=jax.ShapeDtypeStruct((t, D_MODEL), F32),
        scratch_shapes=[pltpu.VMEM((tm, D_MODEL), BF16), pltpu.VMEM((tm, SMALL_W), F32),
                        pltpu.VMEM((tm, D_MODEL), F32)],
        compiler_params=_params(("parallel", "arbitrary")),
        name="moe",
    )(x, mod_l, n2, wr, br, w1, w3, w2)


def _final_kernel(x_ref, w_ref, o_ref):
    o_ref[...] = _rms_rows(x_ref[...], w_ref[...])


def _final_norm(x, w, tm, first_tile, n_tiles):
    return pl.pallas_call(
        _final_kernel,
        grid=(n_tiles,),
        in_specs=[pl.BlockSpec((tm, D_MODEL), lambda i: (first_tile + i, 0)),
                  pl.BlockSpec(w.shape, lambda i: (0, 0))],
        out_specs=pl.BlockSpec((tm, D_MODEL), lambda i: (i, 0)),
        out_shape=jax.ShapeDtypeStruct((n_tiles * tm, D_MODEL), F32),
        compiler_params=_params(("parallel",)),
        name="final_norm",
    )(x, w)


def _block_diag_state(s, transpose):
    if transpose:
        s = jnp.swapaxes(s, -1, -2)
    eye = jnp.eye(HEADS, dtype=s.dtype)
    out = jnp.einsum('bdhij,hg->bdhigj', s, eye)
    return out.reshape(s.shape[0], 2, MIX_W, MIX_W)


def _diag_blocks(s_bd, transpose):
    b = s_bd.shape[0]
    s = s_bd.reshape(b, 2, HEADS, HEAD_DIM, HEADS, HEAD_DIM)
    s = jnp.stack([s[:, :, h, :, h, :] for h in range(HEADS)], axis=2)
    return jnp.swapaxes(s, -1, -2) if transpose else s


def _pick_tile(*lengths):
    for tm in (512, 256, 128, 64):
        if all(n % tm == 0 for n in lengths):
            return tm
    raise ValueError("token counts must be multiples of 64")


def kernel(x_prompt, x_sample, cache_na_k, cache_na_v, cache_diff_k, cache_diff_v, state_gla, state_gdn, c, c_ctx, w_ada, b_ada, norm1_w, norm2_w, w_in, na_rpb, gla_gate_up, gla_gate_bias, gla_norm_w, gdn_conv_w, gdn_A_log, gdn_dt_bias, gdn_norm_w, diff_lambda, diff_subln_w, w_branch, w_out, w_router, b_router, w_e1, w_e3, w_e2, final_norm_w):
    n_ctx, l_ctx, _ = x_prompt.shape
    n_lat, n_tok, _ = x_sample.shape
    depth = w_ada.shape[0]
    p_len = cache_na_k.shape[2]
    lat_tokens = n_lat * n_tok
    ctx_tokens = n_ctx * l_ctx
    assert n_lat + 1 <= 8 and n_tok % l_ctx == 0 and n_tok % GRID_W == 0 and l_ctx % CHUNK == 0
    tm = _pick_tile(n_tok, ctx_tokens)
    tq_lat = _pick_tile(n_tok) // 4 if _pick_tile(n_tok) >= 512 else 64

    def mod_row(i):
        return jnp.where(i * tm < lat_tokens, (i * tm) // n_tok, n_lat)

    tm_moe = MOE_TILE if (n_tok % MOE_TILE == 0 and ctx_tokens % MOE_TILE == 0) else tm

    def mod_row_moe(i):
        return jnp.where(i * tm_moe < lat_tokens, (i * tm_moe) // n_tok, n_lat)

    tokens = lat_tokens + ctx_tokens
    xs = (x_sample.reshape(lat_tokens, D_MODEL), x_prompt.reshape(ctx_tokens, D_MODEL), 0, lat_tokens // tm)
    cond = jnp.zeros((8, D_MODEL), F32).at[:n_lat].set(c).at[n_lat].set(c_ctx)
    mod = _ada(cond, w_ada, b_ada).reshape(depth, 8, 6, D_MODEL)

    gla_pm = _gla_pair_masks()
    lsm, lbt, pm = _gdn_consts()
    bm = jnp.asarray(_block_mask(), F32)
    bm16 = bm.astype(BF16)
    sm = _solve_masks()
    cos, sin, perm = _rope_consts(n_tok)
    edge = _seq_edges(lat_tokens, n_tok, ctx_tokens, l_ctx)
    lane = np.arange(MIX_W)
    small_rows = np.arange(SMALL_W)
    def expand_cols(first_col):
        return jnp.asarray((small_rows[:, None] == first_col + lane[None, :] // HEAD_DIM).astype(np.float32), BF16)
    ea = jnp.stack([expand_cols(32), expand_cols(36)])
    eb = jnp.stack([expand_cols(40), expand_cols(44)])
    wr = jnp.zeros((D_MODEL, SMALL_W), F32).at[:, :N_EXPERTS].set(w_router)
    br = jnp.zeros((1, SMALL_W), F32).at[0, :N_EXPERTS].set(b_router)

    def cols(w, *names):
        return [w[:, _OFF[n][0]:_OFF[n][1]] for n in names]

    zeros_state = jnp.zeros((n_ctx, 2, MIX_W, MIX_W), F32)
    new_kv, new_gla, new_gdn = [], [], []
    for l in range(depth):
        lam_init = 0.8 - 0.6 * math.exp(-0.3 * l)
        wl = w_in[l]
        wcat = jnp.concatenate(
            cols(wl, 'na_q', 'na_k', 'na_v', 'gla_q', 'gla_k', 'gla_v', 'gla_og', 'gdn_qkv', 'gdn_og',
                 'diff_q', 'diff_k', 'diff_v', 'gla_gf', 'gla_gb', 'gdn_af', 'gdn_ab', 'gdn_bf', 'gdn_bb')
            + [jnp.zeros((D_MODEL, SMALL_W - 2 * GATE_RANK - 4 * HEADS), F32)], axis=1).astype(BF16)
        w_gate = wl[:, _OFF['branch_gate'][0]:].astype(BF16)
        z_na, z_gla, z_gdn, z_diff, z_small = _proj(xs, tokens, mod[l], norm1_w[l][None], wcat, tm, mod_row)

        ck = cache_na_k[:, l].reshape(n_lat * p_len, MIX_W)
        cv = cache_na_v[:, l].reshape(n_lat * p_len, MIX_W)
        dl = diff_lambda[l]
        na_lat = _na_latent(z_na, ck, cv, _na_bias_table(na_rpb[l], min(NA_WIN_H, n_tok // GRID_W)),
                            n_lat, n_tok, p_len)
        na_ctx = _attention(z_na, 0, z_na, 1, z_na, 2, dl, n_sub=1, lam_init=0.0, n_seq=n_ctx, lq=l_ctx,
                            lk=l_ctx, q_base=lat_tokens, k_base=lat_tokens, tq=l_ctx, name="na_ctx")

        q_r, k_r = _rope(z_diff, cos, sin, perm, n_lat, n_tok, tm)
        kk = jnp.concatenate([k_r.reshape(n_lat, n_tok, MIX_W), cache_diff_k[:, l].reshape(n_lat, p_len, MIX_W)], 1)
        vv = jnp.concatenate([z_diff[:lat_tokens, 2 * MIX_W:].reshape(n_lat, n_tok, MIX_W),
                              cache_diff_v[:, l].reshape(n_lat, p_len, MIX_W)], 1)
        lkv = n_tok + p_len
        kk_t = kk.astype(BF16).transpose(0, 2, 1).reshape(n_lat * MIX_W, lkv)
        df_lat = _diff_latent(q_r, kk_t, vv.astype(BF16).reshape(n_lat * lkv, MIX_W), dl, lam_init,
                              n_lat, n_tok, lkv, DIFF_Q_TILE if n_tok % DIFF_Q_TILE == 0 else tq_lat)
        df_ctx = _attention(z_diff, 0, z_diff, 1, z_diff, 2, dl, n_sub=2, lam_init=lam_init, n_seq=n_ctx,
                            lq=l_ctx, lk=l_ctx, q_base=lat_tokens, k_base=lat_tokens, tq=l_ctx, name="diff_ctx")

        gup = jnp.zeros((2, SMALL_W, MIX_W), F32)
        gup = gup.at[0, 0:GATE_RANK].set(gla_gate_up[l, 0]).at[1, GATE_RANK:2 * GATE_RANK].set(gla_gate_up[l, 1])
        gbias = gla_gate_bias[l][:, None, :]
        gla_consts = (gup, gbias, lsm, gla_pm, bm, bm16)
        glf_lat, glb_lat, _ = _scan(_gla_step, z_gla, z_small, gla_consts,
                                    _block_diag_state(state_gla[:, l], True), n_lat, n_tok, 0,
                                    GLA_SEQS_PER_STEP, "gla_scan")
        glf_ctx, glb_ctx, st_ctx = _scan(_gla_step, z_gla, z_small, gla_consts, zeros_state,
                                         n_ctx, l_ctx, lat_tokens, GLA_SEQS_PER_STEP, "gla_scan")

        qkv_n = _gdn_pre(z_gdn, edge, gdn_conv_w[l], bm16, tm)
        alog = jnp.repeat(gdn_A_log[l], HEAD_DIM, axis=-1)[:, None, :]
        dtb = jnp.repeat(gdn_dt_bias[l], HEAD_DIM, axis=-1)[:, None, :]
        gdn_consts = (ea, eb, alog, dtb, lsm, lbt, pm, bm, bm16, sm)
        gdf_lat, gdb_lat, _ = _scan(_gdn_step, qkv_n, z_small, gdn_consts,
                                    _block_diag_state(state_gdn[:, l], False), n_lat, n_tok, 0,
                                    GDN_SEQS_PER_STEP, "gdn_scan")
        gdf_ctx, gdb_ctx, s_ctx = _scan(_gdn_step, qkv_n, z_small, gdn_consts, zeros_state,
                                        n_ctx, l_ctx, lat_tokens, GDN_SEQS_PER_STEP, "gdn_scan")

        hn = jnp.stack([jnp.tile(gla_norm_w[l], HEADS), jnp.tile(gdn_norm_w[l], HEADS),
                        jnp.tile(diff_subln_w[l], HEADS)])[:, None, :]
        x = _merge(xs, tokens, mod[l], norm1_w[l][None], (na_lat, na_ctx), (glf_lat, glf_ctx), (glb_lat, glb_ctx), z_gla,
                   (gdf_lat, gdf_ctx), (gdb_lat, gdb_ctx), z_gdn, (df_lat, df_ctx), hn, bm16,
                   w_gate, w_branch[l].astype(BF16), w_out[l].astype(BF16), tm, mod_row, lam_init,
                   lat_tokens // tm)
        w1 = w_e1[l].astype(BF16)
        w3 = w_e3[l].astype(BF16)
        w2 = w_e2[l].reshape(N_EXPERTS * D_FF, D_MODEL).astype(BF16)
        x = _moe(x, mod[l], norm2_w[l][None], wr, br, w1, w3, w2, tm_moe, mod_row_moe)
        xs = (x, x, lat_tokens // tm, lat_tokens // tm)

        def ctx_heads(z, col):
            return z[lat_tokens:, col * MIX_W:(col + 1) * MIX_W].reshape(n_ctx, l_ctx, HEADS, HEAD_DIM)
        new_kv.append((ctx_heads(z_na, 1), ctx_heads(z_na, 2), ctx_heads(z_diff, 1), ctx_heads(z_diff, 2)))
        new_gla.append(_diag_blocks(st_ctx, True))
        new_gdn.append(_diag_blocks(s_ctx, False))

    y_sample = _final_norm(x, final_norm_w[None], tm, 0, lat_tokens // tm).reshape(n_lat, n_tok, D_MODEL)
    y_prompt = _final_norm(x, final_norm_w[None], tm, lat_tokens // tm, ctx_tokens // tm).reshape(
        n_ctx, l_ctx, D_MODEL)
    stack = lambda j: jnp.stack([t[j] for t in new_kv], axis=1)
    return (y_prompt, y_sample, stack(0), stack(1), stack(2), stack(3),
            jnp.stack(new_gla, axis=1), jnp.stack(new_gdn, axis=1))
```

```python
import functools
import math

import numpy as np
import jax
import jax.numpy as jnp
from jax import lax
from jax.experimental import pallas as pl
from jax.experimental.pallas import tpu as pltpu

F32 = jnp.float32
BF16 = jnp.bfloat16

D_MODEL = 1024
HEADS = 4
HEAD_DIM = 64
MIX_W = HEADS * HEAD_DIM
GRID_W = 64
NA_WIN_H = 8
NA_WIN_W = 16
GATE_RANK = 16
GATE_NORM = 16.0
GDN_CONV = 3
CHUNK = 64
N_EXPERTS = 16
N_GROUPS = 4
GROUP_E = N_EXPERTS // N_GROUPS
D_FF = D_MODEL // 4
ROPE_BASE = 10000.0
EPS = 1e-6
NEG = -1e30
LOG2E = 1.4426950408889634
SMALL_W = 128
V7X_VMEM_LIMIT = 56 * 1024 * 1024
LEAF = 8
ATTN_HEAD_GROUP = 2
NA_ROWS_PER_STEP = 4
GLA_SEQS_PER_STEP = 4
GDN_SEQS_PER_STEP = 4
MOE_TILE = 512

_IN_SPLITS = (
    ('na_q', MIX_W), ('na_k', MIX_W), ('na_v', MIX_W),
    ('gla_q', MIX_W), ('gla_k', MIX_W), ('gla_v', MIX_W),
    ('gla_gf', GATE_RANK), ('gla_gb', GATE_RANK), ('gla_og', MIX_W),
    ('gdn_qkv', 3 * MIX_W), ('gdn_af', HEADS), ('gdn_ab', HEADS),
    ('gdn_bf', HEADS), ('gdn_bb', HEADS), ('gdn_og', MIX_W),
    ('diff_q', MIX_W), ('diff_k', MIX_W), ('diff_v', MIX_W),
    ('branch_gate', 4 * D_MODEL),
)
_OFF = {}
_o = 0
for _n, _s in _IN_SPLITS:
    _OFF[_n] = (_o, _o + _s)
    _o += _s

def _params(sem):
    return pltpu.CompilerParams(dimension_semantics=sem, vmem_limit_bytes=V7X_VMEM_LIMIT)


def _dot(a, b):
    return jnp.dot(a, b, preferred_element_type=F32)


def _bdot(a, b):
    return _dot(a.astype(BF16), b.astype(BF16))


def _bdot_nt(a, b):
    return lax.dot_general(a.astype(BF16), b.astype(BF16), (((1,), (1,)), ((), ())),
                           preferred_element_type=F32)


def _bdot_tn(a, b):
    return lax.dot_general(a.astype(BF16), b.astype(BF16), (((0,), (0,)), ((), ())),
                           preferred_element_type=F32)


def _split(x):
    hi = x.astype(BF16)
    lo = (x - hi.astype(F32)).astype(BF16)
    return hi, lo


def _dot_rx(x, m):
    hi, lo = _split(x)
    return _dot(hi, m) + _dot(lo, m)


def _dot_lx(m, x):
    hi, lo = _split(x)
    return _dot(m, hi) + _dot(m, lo)


def _dot3(a, b):
    ah, al = _split(a)
    bh, bl = _split(b)
    return _dot(ah, bh) + _dot(ah, bl) + _dot(al, bh)


def _silu(x):
    return x * jax.nn.sigmoid(x)


def _softplus(x):
    return jnp.maximum(x, 0.0) + jnp.log(1.0 + jnp.exp(-jnp.abs(x)))


def _rms_rows(x, w):
    return x * lax.rsqrt(jnp.mean(x * x, axis=-1, keepdims=True) + EPS) * w


def _head_sum(x, bm16):
    return _dot_rx(x, bm16)


def _bd_rows(x, bm):
    return jnp.concatenate([x] * HEADS, axis=0) * bm


def _lane_mask(lo, width, n=MIX_W):
    lane = lax.broadcasted_iota(jnp.int32, (1, n), 1)
    return (lane >= lo) & (lane < lo + width)


def _gdn_consts():
    c = CHUNK
    i = np.arange(c)[:, None]
    t = np.arange(c)[None, :]
    lb = (t <= i).astype(np.float32)
    lbl = (t > i).astype(np.float32)
    tot = np.ones((16, c), np.float32)
    strict = (t < i).astype(np.float32)
    incl = (t <= i).astype(np.float32)
    rev = lambda a: a[::-1, ::-1]
    lsm = np.stack([np.concatenate([lb, lbl, tot], 0), np.concatenate([rev(lb), rev(lbl), tot], 0)])
    lbt = np.stack([np.tile(lb.T, (1, HEADS)), np.tile(rev(lb).T, (1, HEADS))])
    pm = np.stack([np.stack([np.tile(strict, (1, HEADS)), np.tile(incl, (1, HEADS))]),
                   np.stack([np.tile(rev(strict), (1, HEADS)), np.tile(rev(incl), (1, HEADS))])])
    return jnp.asarray(lsm, BF16), jnp.asarray(lbt, F32), jnp.asarray(pm, F32)


def _gla_pair_masks():
    i = np.arange(CHUNK)[:, None]
    j = np.arange(CHUNK)[None, :]
    nblk = CHUNK // LEAF
    out = []
    for d in range(2):
        later = (i // LEAF > j // LEAF) if d == 0 else (i // LEAF < j // LEAF)
        order = (j <= i) if d == 0 else (j >= i)
        far = [(j // LEAF == blk) & later for blk in range(nblk)]
        near = [(i // LEAF == j // LEAF) & (j % LEAF == r) & order for r in range(LEAF)]
        out.append(np.stack([np.tile(m.astype(np.float32), (1, HEADS)) for m in far + near]))
    return jnp.asarray(np.stack(out), F32)


def _block_mask():
    r = np.arange(MIX_W)
    return (r[:, None] // HEAD_DIM == r[None, :] // HEAD_DIM).astype(np.float32)


def _solve_masks():
    r = np.arange(MIX_W)[:, None]
    c = np.arange(MIX_W)[None, :]
    out = [(r // LEAF == c // LEAF)]
    s = LEAF
    while s < CHUNK:
        out.append((r // (2 * s) == c // (2 * s)) & (r // s != c // s))
        s *= 2
    return jnp.asarray(np.stack(out).astype(np.float32))


def _rope_consts(n_tok):
    t = np.arange(n_tok)
    pos = np.stack([t // GRID_W, t % GRID_W], 0).astype(np.float32)
    lane = np.arange(MIX_W)
    u = lane % 32
    axis = u // 16
    w = u % 16
    first = w < 8
    inv = ROPE_BASE ** (-(w % 8).astype(np.float32) / 8.0)
    ang = pos[axis, :].T * inv[None, :]
    cos = np.cos(ang)
    sin = np.sin(ang) * np.where(first, -1.0, 1.0)[None, :]
    partner = np.where(first, lane + 8, lane - 8)
    perm = np.zeros((MIX_W, MIX_W), np.float32)
    perm[partner, lane] = 1.0
    return jnp.asarray(cos, F32), jnp.asarray(sin, F32), jnp.asarray(perm, BF16)


def _ada_kernel(c_ref, w_ref, b_ref, o_ref):
    c = c_ref[...]
    o_ref[0] = _bdot(_silu(c), w_ref[0]) + b_ref[0]


def _ada(cond, w_ada, b_ada):
    depth, _, n = w_ada.shape
    tn = 1536
    return pl.pallas_call(
        _ada_kernel,
        grid=(depth, n // tn),
        in_specs=[pl.BlockSpec((8, D_MODEL), lambda l, j: (0, 0)),
                  pl.BlockSpec((1, D_MODEL, tn), lambda l, j: (l, 0, j)),
                  pl.BlockSpec((1, 1, tn), lambda l, j: (l, 0, j))],
        out_specs=pl.BlockSpec((1, 8, tn), lambda l, j: (l, 0, j)),
        out_shape=jax.ShapeDtypeStruct((depth, 8, n), F32),
        compiler_params=_params(("parallel", "parallel")),
        name="ada_mod",
    )(cond, w_ada, b_ada.reshape(depth, 1, n))


def _proj_kernel(xl_ref, xc_ref, mod_ref, nw_ref, w_ref, o_na, o_gla, o_gdn, o_diff, o_small, *, n_lat_tiles):
    mod = mod_ref[0]
    x = jnp.where(pl.program_id(0) < n_lat_tiles, xl_ref[...], xc_ref[...])
    h = _rms_rows(x, nw_ref[...]) * (1.0 + mod[1:2]) + mod[0:1]
    hb = h.astype(BF16)
    o_na[...] = _dot(hb, w_ref[:, 0:768])
    o_gla[...] = _dot(hb, w_ref[:, 768:1792])
    o_gdn[...] = _dot(hb, w_ref[:, 1792:2816])
    o_diff[...] = _dot(hb, w_ref[:, 2816:3584])
    o_small[...] = _dot(hb, w_ref[:, 3584:3712])


def _stream_specs(xs, tm):
    _, _, ctx_first, n_lat_tiles = xs
    return [pl.BlockSpec((tm, D_MODEL), lambda i: (jnp.minimum(i, n_lat_tiles - 1), 0)),
            pl.BlockSpec((tm, D_MODEL), lambda i: (ctx_first + jnp.maximum(i - n_lat_tiles, 0), 0))]


def _proj(xs, t, mod_l, nw, wcat, tm, mod_row):
    widths = (768, 1024, 1024, 768, SMALL_W)
    return pl.pallas_call(
        functools.partial(_proj_kernel, n_lat_tiles=xs[3]),
        grid=(t // tm,),
        in_specs=_stream_specs(xs, tm) + [
                  pl.BlockSpec((1, 6, D_MODEL), lambda i: (mod_row(i), 0, 0)),
                  pl.BlockSpec((1, D_MODEL), lambda i: (0, 0)),
                  pl.BlockSpec(wcat.shape, lambda i: (0, 0))],
        out_specs=[pl.BlockSpec((tm, w), lambda i: (i, 0)) for w in widths],
        out_shape=[jax.ShapeDtypeStruct((t, w), F32) for w in widths],
        compiler_params=_params(("parallel",)),
        name="in_proj",
    )(xs[0], xs[1], mod_l, nw, wcat)


def _attn_core(q, k, v, n_sub, lam):
    kb = k.astype(BF16)
    vb = v.astype(BF16)
    tq = q.shape[0]
    sub_w = HEAD_DIM // n_sub
    scale = sub_w ** -0.5 * LOG2E
    out = jnp.zeros(q.shape, F32)
    for h0 in range(0, HEADS, ATTN_HEAD_GROUP):
        maps = [(h, m) for h in range(h0, h0 + ATTN_HEAD_GROUP) for m in range(n_sub)]
        qs = jnp.concatenate(
            [(q * jnp.where(_lane_mask(h * HEAD_DIM + m * sub_w, sub_w), scale, 0.0)).astype(BF16)
             for h, m in maps], axis=0)
        s = lax.dot_general(qs, kb, (((1,), (1,)), ((), ())), preferred_element_type=F32)
        e = jnp.exp2(s - jnp.max(s, axis=-1, keepdims=True))
        inv = 1.0 / jnp.sum(e, axis=-1, keepdims=True)
        if n_sub == 1:
            p = e * inv
        else:
            p = jnp.concatenate(
                [e[(2 * i) * tq:(2 * i + 1) * tq] * inv[(2 * i) * tq:(2 * i + 1) * tq]
                 - e[(2 * i + 1) * tq:(2 * i + 2) * tq] * (lam * inv[(2 * i + 1) * tq:(2 * i + 2) * tq])
                 for i in range(ATTN_HEAD_GROUP)], axis=0)
        o_all = _dot(p.astype(BF16), vb)
        for i in range(ATTN_HEAD_GROUP):
            out = jnp.where(_lane_mask((h0 + i) * HEAD_DIM, HEAD_DIM), o_all[i * tq:(i + 1) * tq], out)
    return out


def _diff_lambda(dl, lam_init):
    a = jnp.sum(dl[0:1] * dl[1:2], axis=-1, keepdims=True)
    b = jnp.sum(dl[2:3] * dl[3:4], axis=-1, keepdims=True)
    return jnp.exp(a) - jnp.exp(b) + lam_init


def _attn_kernel(q_ref, k_ref, v_ref, dl_ref, o_ref, *, n_sub, lam_init):
    lam = _diff_lambda(dl_ref[...], lam_init) if n_sub == 2 else None
    o_ref[...] = _attn_core(q_ref[...], k_ref[...], v_ref[...], n_sub, lam)


def _attention(q_arr, q_col, k_arr, k_col, v_arr, v_col, dl, *, n_sub, lam_init, n_seq, lq, lk,
               q_base, k_base, tq, name):
    nq = lq // tq
    return pl.pallas_call(
        functools.partial(_attn_kernel, n_sub=n_sub, lam_init=lam_init),
        grid=(n_seq, nq),
        in_specs=[pl.BlockSpec((tq, MIX_W), lambda s, j: (q_base // tq + s * nq + j, q_col)),
                  pl.BlockSpec((lk, MIX_W), lambda s, j: (k_base // lk + s, k_col)),
                  pl.BlockSpec((lk, MIX_W), lambda s, j: (k_base // lk + s, v_col)),
                  pl.BlockSpec(dl.shape, lambda s, j: (0, 0))],
        out_specs=pl.BlockSpec((tq, MIX_W), lambda s, j: (s * nq + j, 0)),
        out_shape=jax.ShapeDtypeStruct((n_seq * lq, MIX_W), F32),
        compiler_params=_params(("parallel", "parallel")),
        name=name,
    )(q_arr, k_arr, v_arr, dl)


DIFF_KEY_CHUNK = 256
DIFF_HEAD_GROUP = 1
DIFF_Q_TILE = 256


def _lane_fold(x, op):
    out = x[:, 0:128]
    for t in range(1, x.shape[1] // 128):
        out = op(out, x[:, t * 128:(t + 1) * 128])
    return out


def _diff_group(h0, q, kt_ref, v_ref, lam, tkc):
    tq = q.shape[0]
    nck = kt_ref.shape[1] // tkc
    sub_w = HEAD_DIM // 2
    scale = sub_w ** -0.5 * LOG2E
    maps = [(h, m) for h in range(h0, h0 + DIFF_HEAD_GROUP) for m in range(2)]
    qs = jnp.concatenate(
        [(q * jnp.where(_lane_mask(h * HEAD_DIM + m * sub_w, sub_w), scale, 0.0)).astype(BF16) for h, m in maps],
        axis=0)
    s, mvec = [], None
    for c in range(nck):
        sc = _dot(qs, kt_ref[:, c * tkc:(c + 1) * tkc])
        s.append(sc)
        mc = _lane_fold(sc, jnp.maximum)
        mvec = mc if mvec is None else jnp.maximum(mvec, mc)
        yield
    mrun = jnp.max(mvec, axis=-1, keepdims=True)
    e, lvec = [], None
    for c in range(nck):
        ec = jnp.exp2(s[c] - mrun)
        e.append(ec)
        lc = _lane_fold(ec, jnp.add)
        lvec = lc if lvec is None else lvec + lc
        yield
    inv = 1.0 / jnp.sum(lvec, axis=-1, keepdims=True)
    acc = None
    for c in range(nck):
        a = jnp.concatenate(
            [e[c][(2 * i) * tq:(2 * i + 1) * tq] * inv[(2 * i) * tq:(2 * i + 1) * tq]
             - e[c][(2 * i + 1) * tq:(2 * i + 2) * tq] * (lam * inv[(2 * i + 1) * tq:(2 * i + 2) * tq])
             for i in range(DIFF_HEAD_GROUP)], axis=0)
        o = _dot(a.astype(BF16), v_ref[c * tkc:(c + 1) * tkc, :])
        acc = o if acc is None else acc + o
        if c < nck - 1:
            yield
    return acc


_DIFF_PASSES = 3


def _diff_lat_kernel(q_ref, kt_ref, v_ref, dl_ref, o_ref, *, lam_init, tkc):
    lam = _diff_lambda(dl_ref[...], lam_init)
    q = q_ref[...]
    tq = q.shape[0]
    nck = kt_ref.shape[1] // tkc
    n_grp = HEADS // DIFF_HEAD_GROUP
    groups = [_diff_group(g * DIFF_HEAD_GROUP, q, kt_ref, v_ref, lam, tkc) for g in range(n_grp)]
    res = [None] * n_grp
    for t in range(n_grp + _DIFF_PASSES - 1):
        active = [g for g in range(n_grp) if 0 <= t - g < _DIFF_PASSES]
        for _ in range(nck):
            for g in active:
                try:
                    next(groups[g])
                except StopIteration as done:
                    res[g] = done.value
    out = jnp.zeros(q.shape, F32)
    for h in range(HEADS):
        g, i = divmod(h, DIFF_HEAD_GROUP)
        out = jnp.where(_lane_mask(h * HEAD_DIM, HEAD_DIM), res[g][i * tq:(i + 1) * tq], out)
    o_ref[...] = out


def _diff_latent(q_r, kk_t, vv16, dl, lam_init, n_b, n_tok, lk, tq):
    tkc = DIFF_KEY_CHUNK if lk % DIFF_KEY_CHUNK == 0 else 128
    assert lk % tkc == 0
    nq = n_tok // tq
    return pl.pallas_call(
        functools.partial(_diff_lat_kernel, lam_init=lam_init, tkc=tkc),
        grid=(n_b, nq),
        in_specs=[pl.BlockSpec((tq, MIX_W), lambda b, j: (b * nq + j, 0)),
                  pl.BlockSpec((MIX_W, lk), lambda b, j: (b, 0)),
                  pl.BlockSpec((lk, MIX_W), lambda b, j: (b, 0)),
                  pl.BlockSpec(dl.shape, lambda b, j: (0, 0))],
        out_specs=pl.BlockSpec((tq, MIX_W), lambda b, j: (b * nq + j, 0)),
        out_shape=jax.ShapeDtypeStruct((n_b * n_tok, MIX_W), F32),
        compiler_params=_params(("parallel", "arbitrary")),
        name="diff_latent",
    )(q_r, kk_t, vv16, dl)


def _na_lat_kernel(q_ref, k_ref, v_ref, ck_ref, cv_ref, bias_ref, o_ref, *, rows, kh, rps):
    g = pl.program_id(1)
    ck = ck_ref[...].astype(BF16)
    cv = cv_ref[...].astype(BF16)
    scale = HEAD_DIM ** -0.5 * LOG2E
    head_scale = [jnp.where(_lane_mask(h * HEAD_DIM, HEAD_DIM), scale, 0.0) for h in range(HEADS)]
    hw = HEADS * GRID_W
    qs = []
    for j in range(rps):
        qj = q_ref[j * GRID_W:(j + 1) * GRID_W, :]
        qs.append(jnp.concatenate([(qj * hs).astype(BF16) for hs in head_scale], axis=0))
    s_ctx_all = lax.dot_general(jnp.concatenate(qs, axis=0), ck, (((1,), (1,)), ((), ())),
                                preferred_element_type=F32)
    p_ctx, o_loc = [], []
    for j in range(rps):
        r = g * rps + j
        start = jnp.clip(r - kh // 2, 0, rows - kh)
        cls = r - start
        ws = pl.multiple_of(start * GRID_W, GRID_W)
        kw = k_ref[pl.ds(ws, kh * GRID_W), :].astype(BF16)
        vw = v_ref[pl.ds(ws, kh * GRID_W), :].astype(BF16)
        bias = jnp.concatenate([bias_ref[h, pl.ds(cls, 1)][0] for h in range(HEADS)], axis=0)
        s_loc = lax.dot_general(qs[j], kw, (((1,), (1,)), ((), ())), preferred_element_type=F32) + bias
        s_ctx = s_ctx_all[j * hw:(j + 1) * hw]
        mx = jnp.maximum(jnp.max(s_loc, axis=-1, keepdims=True), jnp.max(s_ctx, axis=-1, keepdims=True))
        e_loc = jnp.exp2(s_loc - mx)
        e_ctx = jnp.exp2(s_ctx - mx)
        inv = 1.0 / (jnp.sum(e_loc, axis=-1, keepdims=True) + jnp.sum(e_ctx, axis=-1, keepdims=True))
        p_ctx.append((e_ctx * inv).astype(BF16))
        o_loc.append(_dot((e_loc * inv).astype(BF16), vw))
    o_ctx_all = _dot(jnp.concatenate(p_ctx, axis=0), cv)
    for j in range(rps):
        o_all = o_loc[j] + o_ctx_all[j * hw:(j + 1) * hw]
        out = jnp.zeros((GRID_W, MIX_W), F32)
        for h in range(HEADS):
            out = jnp.where(_lane_mask(h * HEAD_DIM, HEAD_DIM), o_all[h * GRID_W:(h + 1) * GRID_W], out)
        o_ref[j * GRID_W:(j + 1) * GRID_W, :] = out


def _na_bias_table(rpb, kh):
    cidx = np.arange(GRID_W)
    cls = np.arange(kh)
    drow = np.arange(kh)[None, :] - cls[:, None] + NA_WIN_H - 1
    col_start = np.clip(cidx - NA_WIN_W // 2, 0, GRID_W - NA_WIN_W)
    col_ok = (cidx[None, :] >= col_start[:, None]) & (cidx[None, :] < col_start[:, None] + NA_WIN_W)
    dcol = np.clip(cidx[None, :] - cidx[:, None], 1 - NA_WIN_W, NA_WIN_W - 1) + NA_WIN_W - 1
    toep = rpb.astype(F32)[..., dcol]
    bias = jnp.stack([toep[:, drow[c, 0]:drow[c, 0] + kh] for c in range(kh)], axis=1)
    bias = bias.transpose(0, 1, 3, 2, 4)
    bias = jnp.where(col_ok[:, None, :], bias * LOG2E, NEG)
    return bias.reshape(HEADS, kh, GRID_W, kh * GRID_W)


def _na_latent(z_na, ck, cv, bias, n_b, n_tok, p_len):
    rows = n_tok // GRID_W
    kh = min(NA_WIN_H, rows)
    rps = NA_ROWS_PER_STEP
    assert rows % rps == 0
    steps = rows // rps
    return pl.pallas_call(
        functools.partial(_na_lat_kernel, rows=rows, kh=kh, rps=rps),
        grid=(n_b, steps),
        in_specs=[pl.BlockSpec((rps * GRID_W, MIX_W), lambda b, r: (b * steps + r, 0)),
                  pl.BlockSpec((n_tok, MIX_W), lambda b, r: (b, 1)),
                  pl.BlockSpec((n_tok, MIX_W), lambda b, r: (b, 2)),
                  pl.BlockSpec((p_len, MIX_W), lambda b, r: (b, 0)),
                  pl.BlockSpec((p_len, MIX_W), lambda b, r: (b, 0)),
                  pl.BlockSpec(bias.shape, lambda b, r: (0, 0, 0, 0))],
        out_specs=pl.BlockSpec((rps * GRID_W, MIX_W), lambda b, r: (b * steps + r, 0)),
        out_shape=jax.ShapeDtypeStruct((n_b * n_tok, MIX_W), F32),
        compiler_params=_params(("parallel", "arbitrary")),
        name="na_latent",
    )(z_na, z_na, z_na, ck, cv, bias)


def _rope_kernel(q_ref, k_ref, cos_ref, sin_ref, p_ref, qo_ref, ko_ref):
    cos = cos_ref[...]
    sin = sin_ref[...]
    p = p_ref[...]
    q = q_ref[...]
    k = k_ref[...]
    qo_ref[...] = q * cos + _dot_rx(q, p) * sin
    ko_ref[...] = k * cos + _dot_rx(k, p) * sin


def _rope(z_diff, cos, sin, perm, n_b, n_tok, tm):
    nt = n_tok // tm
    spec_t = pl.BlockSpec((tm, MIX_W), lambda i: (i % nt, 0))
    return pl.pallas_call(
        _rope_kernel,
        grid=(n_b * nt,),
        in_specs=[pl.BlockSpec((tm, MIX_W), lambda i: (i, 0)),
                  pl.BlockSpec((tm, MIX_W), lambda i: (i, 1)),
                  spec_t, spec_t,
                  pl.BlockSpec(perm.shape, lambda i: (0, 0))],
        out_specs=[pl.BlockSpec((tm, MIX_W), lambda i: (i, 0))] * 2,
        out_shape=[jax.ShapeDtypeStruct((n_b * n_tok, MIX_W), F32)] * 2,
        compiler_params=_params(("parallel",)),
        name="diff_rope",
    )(z_diff, z_diff, cos, sin, perm)


def _gdn_pre_kernel(x_ref, prev_ref, next_ref, edge_ref, w_ref, bm_ref, o_ref, *, tm):
    x = x_ref[...]
    w = w_ref[...]
    edge = edge_ref[...]
    row = lax.broadcasted_iota(jnp.int32, (tm, 1), 0)
    x_prev = jnp.where(row == 0, prev_ref[7:8, :], pltpu.roll(x, 1, 0)) * edge[:, 0:1]
    x_next = jnp.where(row == tm - 1, next_ref[0:1, :], pltpu.roll(x, tm - 1, 0)) * edge[:, 1:2]
    y = _silu(x_prev * w[0:1] + x * w[1:2] + x_next * w[2:3])
    bm16 = bm_ref[...]
    q = y[:, 0:MIX_W]
    k = y[:, MIX_W:2 * MIX_W]
    o_ref[:, 0:MIX_W] = q * lax.rsqrt(_head_sum(q * q, bm16) + EPS) * (HEAD_DIM ** -0.5)
    o_ref[:, MIX_W:2 * MIX_W] = k * lax.rsqrt(_head_sum(k * k, bm16) + EPS)
    o_ref[:, 2 * MIX_W:3 * MIX_W] = y[:, 2 * MIX_W:3 * MIX_W]


def _seq_edges(lat_tokens, n_tok, ctx_tokens, l_ctx):
    pos = np.concatenate([np.arange(lat_tokens) % n_tok, np.arange(ctx_tokens) % l_ctx])
    seq = np.concatenate([np.full(lat_tokens, n_tok), np.full(ctx_tokens, l_ctx)])
    edge = np.zeros((lat_tokens + ctx_tokens, 8), np.float32)
    edge[:, 0] = pos != 0
    edge[:, 1] = pos != seq - 1
    return jnp.asarray(edge)


def _gdn_pre(z_gdn, edge, conv_w, bm16, tm):
    t = z_gdn.shape[0]
    w3 = 3 * MIX_W
    nb8 = t // 8
    return pl.pallas_call(
        functools.partial(_gdn_pre_kernel, tm=tm),
        grid=(t // tm,),
        in_specs=[pl.BlockSpec((tm, w3), lambda i: (i, 0)),
                  pl.BlockSpec((8, w3), lambda i: (jnp.maximum(i * (tm // 8) - 1, 0), 0)),
                  pl.BlockSpec((8, w3), lambda i: (jnp.minimum((i + 1) * (tm // 8), nb8 - 1), 0)),
                  pl.BlockSpec((tm, 8), lambda i: (i, 0)),
                  pl.BlockSpec(conv_w.shape, lambda i: (0, 0)),
                  pl.BlockSpec(bm16.shape, lambda i: (0, 0))],
        out_specs=pl.BlockSpec((tm, w3), lambda i: (i, 0)),
        out_shape=jax.ShapeDtypeStruct((t, w3), F32),
        compiler_params=_params(("parallel",)),
        name="gdn_pre",
    )(z_gdn, z_gdn, z_gdn, edge, conv_w, bm16)


def _bd_rows16(x, bm16):
    return jnp.concatenate([x.astype(BF16)] * HEADS, axis=0) * bm16


def _gla_chain(q, k, v, zs, gup, gb, lsm, pm_ref, d, bm, bm16, st):
    nblk = CHUNK // LEAF
    q = q * (HEAD_DIM ** -0.5)
    gp = _bdot(zs, gup) + gb
    g = (jnp.minimum(gp, 0.0) - jnp.log(1.0 + jnp.exp(-jnp.abs(gp)))) * (LOG2E / GATE_NORM)
    yield
    e = _dot_lx(lsm, g)
    b = e[0:CHUNK]
    fb = jnp.exp2(b)
    fbl = jnp.exp2(e[CHUNK:2 * CHUNK])
    ftot = jnp.exp2(e[2 * CHUNK:2 * CHUNK + 1])
    yield
    edge = LEAF - 1 if d == 0 else 0
    blocks = range(nblk - 1) if d == 0 else range(1, nblk)
    b3 = b.reshape(nblk, LEAF, MIX_W)
    b_edge = jnp.broadcast_to(b3[:, edge:edge + 1, :], b3.shape).reshape(CHUNK, MIX_W)
    kf = jnp.exp2(jnp.minimum(b_edge - b, 0.0))
    q_far = jnp.concatenate(
        [(q * jnp.exp2(jnp.minimum(b - b[blk * LEAF + edge:blk * LEAF + edge + 1, :], 0.0))).astype(BF16)
         for blk in blocks], axis=0)
    s_far = lax.dot_general(q_far, _bd_rows16(k * kf, bm16), (((1,), (1,)), ((), ())),
                            preferred_element_type=F32)
    yield
    q_near = jnp.concatenate(
        [(q * jnp.exp2(jnp.minimum(
            b - jnp.broadcast_to(b3[:, r:r + 1, :], b3.shape).reshape(CHUNK, MIX_W), 0.0))).astype(BF16)
         for r in range(LEAF)], axis=0)
    s_near = lax.dot_general(q_near, _bd_rows16(k, bm16), (((1,), (1,)), ((), ())),
                             preferred_element_type=F32)
    yield
    att = None
    for n, blk in enumerate(blocks):
        term = s_far[n * CHUNK:(n + 1) * CHUNK] * pm_ref[d, blk]
        att = term if att is None else att + term
    for r in range(LEAF):
        att = att + s_near[r * CHUNK:(r + 1) * CHUNK] * pm_ref[d, nblk + r]
    yield
    o = _bdot_nt(q * fb, st) + _dot(att.astype(BF16), _bd_rows16(v, bm16))
    yield
    return o, st * ftot + _bdot_tn(v, k * fbl) * bm


def _row_pieces(s, d):
    out = []
    for h in range(HEADS):
        for blk in range(CHUNK // s):
            lo = h * CHUNK + blk * s
            out.append((lo, lo + s, blk % 2 == 1 - d))
    return out


def _tri_inverse(mbd, sm_ref, d):
    r = lax.broadcasted_iota(jnp.int32, (MIX_W, MIX_W), 0)
    c = lax.broadcasted_iota(jnp.int32, (MIX_W, MIX_W), 1)
    eye = jnp.where(r == c, 1.0, 0.0)
    md = mbd * sm_ref[0]
    m2 = _bdot(md, md)
    yield
    m4 = _bdot(m2, m2)
    t = eye - md
    t = t + _bdot(t, m2)
    yield
    t = t + _bdot(t, m4)
    s = LEAF
    for lvl in range(1, sm_ref.shape[0]):
        pieces = _row_pieces(s, d)
        take = lambda x: jnp.concatenate([x[lo:hi] for lo, hi, upd in pieces if upd], axis=0)
        yield
        t16 = t.astype(BF16)
        a = _dot(take(mbd * sm_ref[lvl]).astype(BF16), t16).astype(BF16)
        rows, n = [], 0
        for lo, hi, upd in pieces:
            rows.append(a[n:n + s] if upd else jnp.zeros((s, MIX_W), BF16))
            n += s if upd else 0
        yield
        b = _dot(take(t16), jnp.concatenate(rows, axis=0))
        rows, n = [], 0
        for lo, hi, upd in pieces:
            rows.append(t[lo:hi] - b[n:n + s] if upd else t[lo:hi])
            n += s if upd else 0
        t = jnp.concatenate(rows, axis=0)
        s *= 2
    return t


def _gdn_chain(q, k, v, zs, ea, eb, alog, dtb, lsm, lbt, pm_ref, d, bm, bm16, sm_ref, s):
    g = -jnp.exp(alog) * _softplus(_dot_rx(zs, ea) + dtb)
    beta = jax.nn.sigmoid(_dot_rx(zs, eb))
    yield
    e = _dot_lx(lsm, g)
    dcol = e[0:CHUNK]
    drow = _dot_lx(jnp.ones((8, CHUNK), BF16), g * lbt)[0:1]
    kb = k * beta
    s2 = lax.dot_general(jnp.concatenate([kb, q], axis=0).astype(BF16), _bd_rows16(k, bm16),
                         (((1,), (1,)), ((), ())), preferred_element_type=F32)
    yield
    dec = jnp.exp(jnp.where(pm_ref[d, 1] > 0.0, dcol - drow, NEG))
    m_cat = s2[0:CHUNK] * dec * pm_ref[d, 0]
    att = s2[CHUNK:2 * CHUNK] * dec
    t_inv = yield from _tri_inverse(_bd_rows(m_cat, bm), sm_ref, d)
    t16 = (t_inv[0:CHUNK] + t_inv[CHUNK:2 * CHUNK] + t_inv[2 * CHUNK:3 * CHUNK] + t_inv[3 * CHUNK:]).astype(BF16)
    fb = jnp.exp(dcol)
    fbl = jnp.exp(e[CHUNK:2 * CHUNK])
    ftot = jnp.exp(e[2 * CHUNK:2 * CHUNK + 1])
    yield
    u = _dot(t16, _bd_rows16(v * beta, bm16))
    w = _dot(t16, _bd_rows16(kb * fb, bm16))
    yield
    ws = _dot(jnp.concatenate([w.astype(BF16), (q * fb).astype(BF16)], axis=0), s.astype(BF16))
    v_new = u - ws[0:CHUNK]
    yield
    o = ws[CHUNK:2 * CHUNK] + _dot(att.astype(BF16), _bd_rows16(v_new, bm16))
    return o, s * ftot + lax.dot_general((k * fbl).astype(BF16), v_new.astype(BF16), (((0,), (0,)), ((), ())),
                                         preferred_element_type=F32) * bm


def _scan_kernel(*refs, chain, n_const, nb, n_chunks):
    n_tok = nb * 2 * 4
    tok = refs[:n_tok]
    consts = refs[n_tok:n_tok + n_const]
    s0_ref, of_ref, ob_ref, sf_ref, s_scr = refs[n_tok + n_const:]
    n = pl.program_id(1)

    @pl.when(n == 0)
    def _():
        s_scr[...] = s0_ref[...]

    ids = [(j, d) for j in range(nb) for d in range(2)]
    gens = []
    for j, d in ids:
        q_ref, k_ref, v_ref, zs_ref = tok[(j * 2 + d) * 4:(j * 2 + d) * 4 + 4]
        gens.append(chain(q_ref[...], k_ref[...], v_ref[...], zs_ref[...], consts, d, s_scr[j, d]))
    live = list(range(len(gens)))
    while live:
        for c in list(live):
            try:
                next(gens[c])
            except StopIteration as done:
                j, d = ids[c]
                o, s_new = done.value
                (of_ref, ob_ref)[d][j] = o
                s_scr[j, d] = s_new
                live.remove(c)

    @pl.when(n == n_chunks - 1)
    def _():
        sf_ref[...] = s_scr[...]


def _gla_step(q, k, v, zs, consts, d, st):
    gup_ref, gb_ref, lsm_ref, pm_ref, bm_ref, bm16_ref = consts
    return _gla_chain(q, k, v, zs, gup_ref[d], gb_ref[d], lsm_ref[d], pm_ref, d, bm_ref[...], bm16_ref[...], st)


def _gdn_step(q, k, v, zs, consts, d, s):
    ea_ref, eb_ref, alog_ref, dtb_ref, lsm_ref, lbt_ref, pm_ref, bm_ref, bm16_ref, sm_ref = consts
    return _gdn_chain(q, k, v, zs, ea_ref[d], eb_ref[d], alog_ref[d], dtb_ref[d], lsm_ref[d], lbt_ref[d],
                      pm_ref, d, bm_ref[...], bm16_ref[...], sm_ref, s)


def _scan(chain, qkv, z_small, consts, s0, n_seq, l_seq, base, seqs_per_step, name):
    nc = l_seq // CHUNK
    nb = seqs_per_step if n_seq % seqs_per_step == 0 else 1

    def chunk(n, d):
        return n + d * (nc - 1 - 2 * n)

    in_specs, args = [], []
    for j in range(nb):
        for d in range(2):
            for arr, col, width in ((qkv, 0, MIX_W), (qkv, 1, MIX_W), (qkv, 2, MIX_W), (z_small, 0, SMALL_W)):
                in_specs.append(pl.BlockSpec(
                    (CHUNK, width),
                    functools.partial(lambda g, n, j, d, col: (base // CHUNK + (g * nb + j) * nc + chunk(n, d), col),
                                      j=j, d=d, col=col)))
                args.append(arr)
    for cst in consts:
        in_specs.append(pl.BlockSpec(cst.shape, functools.partial(lambda g, n, nd: (0,) * nd, nd=cst.ndim)))
        args.append(cst)
    state_spec = pl.BlockSpec((nb, 2, MIX_W, MIX_W), lambda g, n: (g, 0, 0, 0))
    in_specs.append(state_spec)
    args.append(s0)
    o_f, o_b, s_f = pl.pallas_call(
        functools.partial(_scan_kernel, chain=chain, n_const=len(consts), nb=nb, n_chunks=nc),
        grid=(n_seq // nb, nc),
        in_specs=in_specs,
        out_specs=[pl.BlockSpec((nb, CHUNK, MIX_W), lambda g, n: (g, chunk(n, 0), 0)),
                   pl.BlockSpec((nb, CHUNK, MIX_W), lambda g, n: (g, chunk(n, 1), 0)),
                   state_spec],
        out_shape=[jax.ShapeDtypeStruct((n_seq, l_seq, MIX_W), F32),
                   jax.ShapeDtypeStruct((n_seq, l_seq, MIX_W), F32),
                   jax.ShapeDtypeStruct((n_seq, 2, MIX_W, MIX_W), F32)],
        scratch_shapes=[pltpu.VMEM((nb, 2, MIX_W, MIX_W), F32)],
        compiler_params=_params(("parallel", "arbitrary")),
        name=name,
    )(*args)
    return o_f.reshape(n_seq * l_seq, MIX_W), o_b.reshape(n_seq * l_seq, MIX_W), s_f


def _merge_kernel(xl_ref, xc_ref, mod_ref, n1_ref, na_l, na_c, glaf_l, glaf_c, glab_l, glab_c, glag_ref, gdnf_l, gdnf_c,
                  gdnb_l, gdnb_c, gdng_ref, df_l, df_c, hn_ref, bm_ref, wg_ref, wb_ref, wo_ref, o_ref, *,
                  lam_init, n_lat_tiles):
    is_lat = pl.program_id(0) < n_lat_tiles

    def pick(lat_ref, ctx_ref):
        return jnp.where(is_lat, lat_ref[...], ctx_ref[...])

    mod = mod_ref[0]
    x = pick(xl_ref, xc_ref)
    h = _rms_rows(x, n1_ref[...]) * (1.0 + mod[1:2]) + mod[0:1]
    hb = h.astype(BF16)
    bm16 = bm_ref[...]

    def head_norm(o, w):
        return o * lax.rsqrt(_head_sum(o * o, bm16) * (1.0 / HEAD_DIM) + EPS) * w

    branches = (
        pick(na_l, na_c),
        head_norm(pick(glaf_l, glaf_c) + pick(glab_l, glab_c), hn_ref[0]) * _silu(glag_ref[...]),
        head_norm(pick(gdnf_l, gdnf_c) + pick(gdnb_l, gdnb_c), hn_ref[1]) * _silu(gdng_ref[...]),
        head_norm(pick(df_l, df_c), hn_ref[2]) * (1.0 - lam_init),
    )
    merged = None
    for n, br in enumerate(branches):
        gate = jax.nn.sigmoid(_dot(hb, wg_ref[:, n * D_MODEL:(n + 1) * D_MODEL]))
        term = gate * _bdot(br, wb_ref[n])
        merged = term if merged is None else merged + term
    o_ref[...] = x + mod[2:3] * _bdot(merged, wo_ref[...])


def _merge(xs, t, mod_l, n1, na, gla_f, gla_b, z_gla, gdn_f, gdn_b, z_gdn, df, hn, bm16, wg, wb, wo, tm, mod_row,
           lam_init, n_lat_tiles):
    tok = lambda col: pl.BlockSpec((tm, MIX_W), lambda i: (i, col))
    lat = pl.BlockSpec((tm, MIX_W), lambda i: (jnp.minimum(i, n_lat_tiles - 1), 0))
    ctx = pl.BlockSpec((tm, MIX_W), lambda i: (jnp.maximum(i - n_lat_tiles, 0), 0))
    full = lambda a: pl.BlockSpec(a.shape, lambda i: (0,) * a.ndim)
    return pl.pallas_call(
        functools.partial(_merge_kernel, lam_init=lam_init, n_lat_tiles=n_lat_tiles),
        grid=(t // tm,),
        in_specs=_stream_specs(xs, tm) + [
                  pl.BlockSpec((1, 6, D_MODEL), lambda i: (mod_row(i), 0, 0)),
                  full(n1), lat, ctx, lat, ctx, lat, ctx, tok(3), lat, ctx, lat, ctx, tok(3), lat, ctx,
                  full(hn), full(bm16), full(wg), full(wb), full(wo)],
        out_specs=pl.BlockSpec((tm, D_MODEL), lambda i: (i, 0)),
        out_shape=jax.ShapeDtypeStruct((t, D_MODEL), F32),
        compiler_params=_params(("parallel",)),
        name="merge_out",
    )(xs[0], xs[1], mod_l, n1, *na, *gla_f, *gla_b, z_gla, *gdn_f, *gdn_b, z_gdn, *df, hn, bm16, wg, wb, wo)


def _route(h2, wr_ref, br_ref):
    lane_i = lax.broadcasted_iota(jnp.int32, (1, SMALL_W), 1)
    lane = lane_i.astype(F32)
    gid = lax.shift_right_logical(lane_i, 2).astype(F32)
    s = jax.nn.sigmoid(_dot3(h2, wr_ref[...]))
    sel = s + br_ref[...]
    far = float(SMALL_W)

    def first_max(vals):
        mx = jnp.max(vals, axis=-1, keepdims=True)
        idx = jnp.min(jnp.where(vals == mx, lane, far), axis=-1, keepdims=True)
        return mx, idx

    best = None
    for g in range(N_GROUPS):
        vals = jnp.where(gid == float(g), sel, NEG)
        a, ia = first_max(vals)
        b, _ = first_max(jnp.where(lane == ia, NEG, vals))
        score = a + b
        if best is None:
            best, gi = score, jnp.zeros_like(score)
        else:
            better = score > best
            gi = jnp.where(better, float(g), gi)
            best = jnp.where(better, score, best)
    vals = jnp.where(gid == gi, sel, NEG)
    _, i1 = first_max(vals)
    _, i2 = first_max(jnp.where(lane == i1, NEG, vals))
    s1 = jnp.sum(jnp.where(lane == i1, s, 0.0), axis=-1, keepdims=True)
    s2 = jnp.sum(jnp.where(lane == i2, s, 0.0), axis=-1, keepdims=True)
    inv = 1.0 / (s1 + s2)
    return jnp.where(lane == i1, s1 * inv, 0.0) + jnp.where(lane == i2, s2 * inv, 0.0)


def _moe_kernel(x_ref, mod_ref, n2_ref, wr_ref, br_ref, w1_ref, w3_ref, w2_ref, o_ref,
                h_scr, g_scr, acc_scr):
    g = pl.program_id(1)
    mod = mod_ref[0]

    @pl.when(g == 0)
    def _():
        h2 = _rms_rows(x_ref[...], n2_ref[...]) * (1.0 + mod[4:5]) + mod[3:4]
        h_scr[...] = h2.astype(BF16)
        g_scr[...] = _route(h2, wr_ref, br_ref)
        acc_scr[...] = jnp.zeros_like(acc_scr)

    hb = h_scr[...]
    gates = g_scr[...]
    lane = lax.broadcasted_iota(jnp.int32, (1, SMALL_W), 1)
    he = jnp.concatenate(
        [_silu(_dot(hb, w1_ref[e])) * _dot(hb, w3_ref[e])
         * jnp.sum(jnp.where(lane == g * GROUP_E + e, gates, 0.0), axis=-1, keepdims=True)
         for e in range(GROUP_E)], axis=1)
    acc_scr[...] += _dot(he.astype(BF16), w2_ref[...])

    @pl.when(g == N_GROUPS - 1)
    def _():
        o_ref[...] = x_ref[...] + mod[5:6] * acc_scr[...]


def _moe(x, mod_l, n2, wr, br, w1, w3, w2, tm, mod_row):
    t = x.shape[0]
    gw = GROUP_E * D_FF
    return pl.pallas_call(
        _moe_kernel,
        grid=(t // tm, N_GROUPS),
        in_specs=[pl.BlockSpec((tm, D_MODEL), lambda i, g: (i, 0)),
                  pl.BlockSpec((1, 6, D_MODEL), lambda i, g: (mod_row(i), 0, 0)),
                  pl.BlockSpec(n2.shape, lambda i, g: (0, 0)),
                  pl.BlockSpec(wr.shape, lambda i, g: (0, 0)),
                  pl.BlockSpec(br.shape, lambda i, g: (0, 0)),
                  pl.BlockSpec((GROUP_E, D_MODEL, D_FF), lambda i, g: (g, 0, 0)),
                  pl.BlockSpec((GROUP_E, D_MODEL, D_FF), lambda i, g: (g, 0, 0)),
                  pl.BlockSpec((gw, D_MODEL), lambda i, g: (g, 0))],
        out_specs=pl.BlockSpec((tm, D_MODEL), lambda i, g: (i, 0)),
        out_shape=jax.ShapeDtypeStruct((t, D_MODEL), F32),
        scratch_shapes=[pltpu.VMEM((tm, D_MODEL), BF16), pltpu.VMEM((tm, SMALL_W), F32),
                        pltpu.VMEM((tm, D_MODEL), F32)],
        compiler_params=_params(("parallel", "arbitrary")),
        name="moe",
    )(x, mod_l, n2, wr, br, w1, w3, w2)


def _final_kernel(x_ref, w_ref, o_ref):
    o_ref[...] = _rms_rows(x_ref[...], w_ref[...])


def _final_norm(x, w, tm, first_tile, n_tiles):
    return pl.pallas_call(
        _final_kernel,
        grid=(n_tiles,),
        in_specs=[pl.BlockSpec((tm, D_MODEL), lambda i: (first_tile + i, 0)),
                  pl.BlockSpec(w.shape, lambda i: (0, 0))],
        out_specs=pl.BlockSpec((tm, D_MODEL), lambda i: (i, 0)),
        out_shape=jax.ShapeDtypeStruct((n_tiles * tm, D_MODEL), F32),
        compiler_params=_params(("parallel",)),
        name="final_norm",
    )(x, w)


def _block_diag_state(s, transpose):
    if transpose:
        s = jnp.swapaxes(s, -1, -2)
    eye = jnp.eye(HEADS, dtype=s.dtype)
    out = jnp.einsum('bdhij,hg->bdhigj', s, eye)
    return out.reshape(s.shape[0], 2, MIX_W, MIX_W)


def _diag_blocks(s_bd, transpose):
    b = s_bd.shape[0]
    s = s_bd.reshape(b, 2, HEADS, HEAD_DIM, HEADS, HEAD_DIM)
    s = jnp.stack([s[:, :, h, :, h, :] for h in range(HEADS)], axis=2)
    return jnp.swapaxes(s, -1, -2) if transpose else s


def _pick_tile(*lengths):
    for tm in (512, 256, 128, 64):
        if all(n % tm == 0 for n in lengths):
            return tm
    raise ValueError("token counts must be multiples of 64")


def kernel(x_prompt, x_sample, cache_na_k, cache_na_v, cache_diff_k, cache_diff_v, state_gla, state_gdn, c, c_ctx, w_ada, b_ada, norm1_w, norm2_w, w_in, na_rpb, gla_gate_up, gla_gate_bias, gla_norm_w, gdn_conv_w, gdn_A_log, gdn_dt_bias, gdn_norm_w, diff_lambda, diff_subln_w, w_branch, w_out, w_router, b_router, w_e1, w_e3, w_e2, final_norm_w):
    n_ctx, l_ctx, _ = x_prompt.shape
    n_lat, n_tok, _ = x_sample.shape
    depth = w_ada.shape[0]
    p_len = cache_na_k.shape[2]
    lat_tokens = n_lat * n_tok
    ctx_tokens = n_ctx * l_ctx
    assert n_lat + 1 <= 8 and n_tok % l_ctx == 0 and n_tok % GRID_W == 0 and l_ctx % CHUNK == 0
    tm = _pick_tile(n_tok, ctx_tokens)
    tq_lat = _pick_tile(n_tok) // 4 if _pick_tile(n_tok) >= 512 else 64

    def mod_row(i):
        return jnp.where(i * tm < lat_tokens, (i * tm) // n_tok, n_lat)

    tm_moe = MOE_TILE if (n_tok % MOE_TILE == 0 and ctx_tokens % MOE_TILE == 0) else tm

    def mod_row_moe(i):
        return jnp.where(i * tm_moe < lat_tokens, (i * tm_moe) // n_tok, n_lat)

    tokens = lat_tokens + ctx_tokens
    xs = (x_sample.reshape(lat_tokens, D_MODEL), x_prompt.reshape(ctx_tokens, D_MODEL), 0, lat_tokens // tm)
    cond = jnp.zeros((8, D_MODEL), F32).at[:n_lat].set(c).at[n_lat].set(c_ctx)
    mod = _ada(cond, w_ada, b_ada).reshape(depth, 8, 6, D_MODEL)

    gla_pm = _gla_pair_masks()
    lsm, lbt, pm = _gdn_consts()
    bm = jnp.asarray(_block_mask(), F32)
    bm16 = bm.astype(BF16)
    sm = _solve_masks()
    cos, sin, perm = _rope_consts(n_tok)
    edge = _seq_edges(lat_tokens, n_tok, ctx_tokens, l_ctx)
    lane = np.arange(MIX_W)
    small_rows = np.arange(SMALL_W)
    def expand_cols(first_col):
        return jnp.asarray((small_rows[:, None] == first_col + lane[None, :] // HEAD_DIM).astype(np.float32), BF16)
    ea = jnp.stack([expand_cols(32), expand_cols(36)])
    eb = jnp.stack([expand_cols(40), expand_cols(44)])
    wr = jnp.zeros((D_MODEL, SMALL_W), F32).at[:, :N_EXPERTS].set(w_router)
    br = jnp.zeros((1, SMALL_W), F32).at[0, :N_EXPERTS].set(b_router)

    def cols(w, *names):
        return [w[:, _OFF[n][0]:_OFF[n][1]] for n in names]

    zeros_state = jnp.zeros((n_ctx, 2, MIX_W, MIX_W), F32)
    new_kv, new_gla, new_gdn = [], [], []
    for l in range(depth):
        lam_init = 0.8 - 0.6 * math.exp(-0.3 * l)
        wl = w_in[l]
        wcat = jnp.concatenate(
            cols(wl, 'na_q', 'na_k', 'na_v', 'gla_q', 'gla_k', 'gla_v', 'gla_og', 'gdn_qkv', 'gdn_og',
                 'diff_q', 'diff_k', 'diff_v', 'gla_gf', 'gla_gb', 'gdn_af', 'gdn_ab', 'gdn_bf', 'gdn_bb')
            + [jnp.zeros((D_MODEL, SMALL_W - 2 * GATE_RANK - 4 * HEADS), F32)], axis=1).astype(BF16)
        w_gate = wl[:, _OFF['branch_gate'][0]:].astype(BF16)
        z_na, z_gla, z_gdn, z_diff, z_small = _proj(xs, tokens, mod[l], norm1_w[l][None], wcat, tm, mod_row)

        ck = cache_na_k[:, l].reshape(n_lat * p_len, MIX_W)
        cv = cache_na_v[:, l].reshape(n_lat * p_len, MIX_W)
        dl = diff_lambda[l]
        na_lat = _na_latent(z_na, ck, cv, _na_bias_table(na_rpb[l], min(NA_WIN_H, n_tok // GRID_W)),
                            n_lat, n_tok, p_len)
        na_ctx = _attention(z_na, 0, z_na, 1, z_na, 2, dl, n_sub=1, lam_init=0.0, n_seq=n_ctx, lq=l_ctx,
                            lk=l_ctx, q_base=lat_tokens, k_base=lat_tokens, tq=l_ctx, name="na_ctx")

        q_r, k_r = _rope(z_diff, cos, sin, perm, n_lat, n_tok, tm)
        kk = jnp.concatenate([k_r.reshape(n_lat, n_tok, MIX_W), cache_diff_k[:, l].reshape(n_lat, p_len, MIX_W)], 1)
        vv = jnp.concatenate([z_diff[:lat_tokens, 2 * MIX_W:].reshape(n_lat, n_tok, MIX_W),
                              cache_diff_v[:, l].reshape(n_lat, p_len, MIX_W)], 1)
        lkv = n_tok + p_len
        kk_t = kk.astype(BF16).transpose(0, 2, 1).reshape(n_lat * MIX_W, lkv)
        df_lat = _diff_latent(q_r, kk_t, vv.astype(BF16).reshape(n_lat * lkv, MIX_W), dl, lam_init,
                              n_lat, n_tok, lkv, DIFF_Q_TILE if n_tok % DIFF_Q_TILE == 0 else tq_lat)
        df_ctx = _attention(z_diff, 0, z_diff, 1, z_diff, 2, dl, n_sub=2, lam_init=lam_init, n_seq=n_ctx,
                            lq=l_ctx, lk=l_ctx, q_base=lat_tokens, k_base=lat_tokens, tq=l_ctx, name="diff_ctx")

        gup = jnp.zeros((2, SMALL_W, MIX_W), F32)
        gup = gup.at[0, 0:GATE_RANK].set(gla_gate_up[l, 0]).at[1, GATE_RANK:2 * GATE_RANK].set(gla_gate_up[l, 1])
        gbias = gla_gate_bias[l][:, None, :]
        gla_consts = (gup, gbias, lsm, gla_pm, bm, bm16)
        glf_lat, glb_lat, _ = _scan(_gla_step, z_gla, z_small, gla_consts,
                                    _block_diag_state(state_gla[:, l], True), n_lat, n_tok, 0,
                                    GLA_SEQS_PER_STEP, "gla_scan")
        glf_ctx, glb_ctx, st_ctx = _scan(_gla_step, z_gla, z_small, gla_consts, zeros_state,
                                         n_ctx, l_ctx, lat_tokens, GLA_SEQS_PER_STEP, "gla_scan")

        qkv_n = _gdn_pre(z_gdn, edge, gdn_conv_w[l], bm16, tm)
        alog = jnp.repeat(gdn_A_log[l], HEAD_DIM, axis=-1)[:, None, :]
        dtb = jnp.repeat(gdn_dt_bias[l], HEAD_DIM, axis=-1)[:, None, :]
        gdn_consts = (ea, eb, alog, dtb, lsm, lbt, pm, bm, bm16, sm)
        gdf_lat, gdb_lat, _ = _scan(_gdn_step, qkv_n, z_small, gdn_consts,
                                    _block_diag_state(state_gdn[:, l], False), n_lat, n_tok, 0,
                                    GDN_SEQS_PER_STEP, "gdn_scan")
        gdf_ctx, gdb_ctx, s_ctx = _scan(_gdn_step, qkv_n, z_small, gdn_consts, zeros_state,
                                        n_ctx, l_ctx, lat_tokens, GDN_SEQS_PER_STEP, "gdn_scan")

        hn = jnp.stack([jnp.tile(gla_norm_w[l], HEADS), jnp.tile(gdn_norm_w[l], HEADS),
                        jnp.tile(diff_subln_w[l], HEADS)])[:, None, :]
        x = _merge(xs, tokens, mod[l], norm1_w[l][None], (na_lat, na_ctx), (glf_lat, glf_ctx), (glb_lat, glb_ctx), z_gla,
                   (gdf_lat, gdf_ctx), (gdb_lat, gdb_ctx), z_gdn, (df_lat, df_ctx), hn, bm16,
                   w_gate, w_branch[l].astype(BF16), w_out[l].astype(BF16), tm, mod_row, lam_init,
                   lat_tokens // tm)
        w1 = w_e1[l].astype(BF16)
        w3 = w_e3[l].astype(BF16)
        w2 = w_e2[l].reshape(N_EXPERTS * D_FF, D_MODEL).astype(BF16)
        x = _moe(x, mod[l], norm2_w[l][None], wr, br, w1, w3, w2, tm_moe, mod_row_moe)
        xs = (x, x, lat_tokens // tm, lat_tokens // tm)

        def ctx_heads(z, col):
            return z[lat_tokens:, col * MIX_W:(col + 1) * MIX_W].reshape(n_ctx, l_ctx, HEADS, HEAD_DIM)
        new_kv.append((ctx_heads(z_na, 1), ctx_heads(z_na, 2), ctx_heads(z_diff, 1), ctx_heads(z_diff, 2)))
        new_gla.append(_diag_blocks(st_ctx, True))
        new_gdn.append(_diag_blocks(s_ctx, False))

    y_sample = _final_norm(x, final_norm_w[None], tm, 0, lat_tokens // tm).reshape(n_lat, n_tok, D_MODEL)
    y_prompt = _final_norm(x, final_norm_w[None], tm, lat_tokens // tm, ctx_tokens // tm).reshape(
        n_ctx, l_ctx, D_MODEL)
    stack = lambda j: jnp.stack([t[j] for t in new_kv], axis=1)
    return (y_prompt, y_sample, stack(0), stack(1), stack(2), stack(3),
            jnp.stack(new_gla, axis=1), jnp.stack(new_gdn, axis=1))
```

```python
import functools
import math

import numpy as np
import jax
import jax.numpy as jnp
from jax import lax
from jax.experimental import pallas as pl
from jax.experimental.pallas import tpu as pltpu

F32 = jnp.float32
BF16 = jnp.bfloat16

D_MODEL = 1024
HEADS = 4
HEAD_DIM = 64
MIX_W = HEADS * HEAD_DIM
GRID_W = 64
NA_WIN_H = 8
NA_WIN_W = 16
GATE_RANK = 16
GATE_NORM = 16.0
GDN_CONV = 3
CHUNK = 64
N_EXPERTS = 16
N_GROUPS = 4
GROUP_E = N_EXPERTS // N_GROUPS
D_FF = D_MODEL // 4
ROPE_BASE = 10000.0
EPS = 1e-6
NEG = -1e30
LOG2E = 1.4426950408889634
SMALL_W = 128
V7X_VMEM_LIMIT = 56 * 1024 * 1024
LEAF = 8
ATTN_HEAD_GROUP = 2
NA_ROWS_PER_STEP = 4
GLA_SEQS_PER_STEP = 4
GDN_SEQS_PER_STEP = 4
MOE_TILE = 512

_IN_SPLITS = (
    ('na_q', MIX_W), ('na_k', MIX_W), ('na_v', MIX_W),
    ('gla_q', MIX_W), ('gla_k', MIX_W), ('gla_v', MIX_W),
    ('gla_gf', GATE_RANK), ('gla_gb', GATE_RANK), ('gla_og', MIX_W),
    ('gdn_qkv', 3 * MIX_W), ('gdn_af', HEADS), ('gdn_ab', HEADS),
    ('gdn_bf', HEADS), ('gdn_bb', HEADS), ('gdn_og', MIX_W),
    ('diff_q', MIX_W), ('diff_k', MIX_W), ('diff_v', MIX_W),
    ('branch_gate', 4 * D_MODEL),
)
_OFF = {}
_o = 0
for _n, _s in _IN_SPLITS:
    _OFF[_n] = (_o, _o + _s)
    _o += _s

def _params(sem):
    return pltpu.CompilerParams(dimension_semantics=sem, vmem_limit_bytes=V7X_VMEM_LIMIT)


def _dot(a, b):
    return jnp.dot(a, b, preferred_element_type=F32)


def _bdot(a, b):
    return _dot(a.astype(BF16), b.astype(BF16))


def _bdot_nt(a, b):
    return lax.dot_general(a.astype(BF16), b.astype(BF16), (((1,), (1,)), ((), ())),
                           preferred_element_type=F32)


def _bdot_tn(a, b):
    return lax.dot_general(a.astype(BF16), b.astype(BF16), (((0,), (0,)), ((), ())),
                           preferred_element_type=F32)


def _split(x):
    hi = x.astype(BF16)
    lo = (x - hi.astype(F32)).astype(BF16)
    return hi, lo


def _dot_rx(x, m):
    hi, lo = _split(x)
    return _dot(hi, m) + _dot(lo, m)


def _dot_lx(m, x):
    hi, lo = _split(x)
    return _dot(m, hi) + _dot(m, lo)


def _dot3(a, b):
    ah, al = _split(a)
    bh, bl = _split(b)
    return _dot(ah, bh) + _dot(ah, bl) + _dot(al, bh)


def _silu(x):
    return x * jax.nn.sigmoid(x)


def _softplus(x):
    return jnp.maximum(x, 0.0) + jnp.log(1.0 + jnp.exp(-jnp.abs(x)))


def _rms_rows(x, w):
    return x * lax.rsqrt(jnp.mean(x * x, axis=-1, keepdims=True) + EPS) * w


def _head_sum(x, bm16):
    return _dot_rx(x, bm16)


def _bd_rows(x, bm):
    return jnp.concatenate([x] * HEADS, axis=0) * bm


def _lane_mask(lo, width, n=MIX_W):
    lane = lax.broadcasted_iota(jnp.int32, (1, n), 1)
    return (lane >= lo) & (lane < lo + width)


def _gdn_consts():
    c = CHUNK
    i = np.arange(c)[:, None]
    t = np.arange(c)[None, :]
    lb = (t <= i).astype(np.float32)
    lbl = (t > i).astype(np.float32)
    tot = np.ones((16, c), np.float32)
    strict = (t < i).astype(np.float32)
    incl = (t <= i).astype(np.float32)
    rev = lambda a: a[::-1, ::-1]
    lsm = np.stack([np.concatenate([lb, lbl, tot], 0), np.concatenate([rev(lb), rev(lbl), tot], 0)])
    lbt = np.stack([np.tile(lb.T, (1, HEADS)), np.tile(rev(lb).T, (1, HEADS))])
    pm = np.stack([np.stack([np.tile(strict, (1, HEADS)), np.tile(incl, (1, HEADS))]),
                   np.stack([np.tile(rev(strict), (1, HEADS)), np.tile(rev(incl), (1, HEADS))])])
    return jnp.asarray(lsm, BF16), jnp.asarray(lbt, F32), jnp.asarray(pm, F32)


def _gla_pair_masks():
    i = np.arange(CHUNK)[:, None]
    j = np.arange(CHUNK)[None, :]
    nblk = CHUNK // LEAF
    out = []
    for d in range(2):
        later = (i // LEAF > j // LEAF) if d == 0 else (i // LEAF < j // LEAF)
        order = (j <= i) if d == 0 else (j >= i)
        far = [(j // LEAF == blk) & later for blk in range(nblk)]
        near = [(i // LEAF == j // LEAF) & (j % LEAF == r) & order for r in range(LEAF)]
        out.append(np.stack([np.tile(m.astype(np.float32), (1, HEADS)) for m in far + near]))
    return jnp.asarray(np.stack(out), F32)


def _block_mask():
    r = np.arange(MIX_W)
    return (r[:, None] // HEAD_DIM == r[None, :] // HEAD_DIM).astype(np.float32)


def _solve_masks():
    r = np.arange(MIX_W)[:, None]
    c = np.arange(MIX_W)[None, :]
    out = [(r // LEAF == c // LEAF)]
    s = LEAF
    while s < CHUNK:
        out.append((r // (2 * s) == c // (2 * s)) & (r // s != c // s))
        s *= 2
    return jnp.asarray(np.stack(out).astype(np.float32))


def _rope_consts(n_tok):
    t = np.arange(n_tok)
    pos = np.stack([t // GRID_W, t % GRID_W], 0).astype(np.float32)
    lane = np.arange(MIX_W)
    u = lane % 32
    axis = u // 16
    w = u % 16
    first = w < 8
    inv = ROPE_BASE ** (-(w % 8).astype(np.float32) / 8.0)
    ang = pos[axis, :].T * inv[None, :]
    cos = np.cos(ang)
    sin = np.sin(ang) * np.where(first, -1.0, 1.0)[None, :]
    partner = np.where(first, lane + 8, lane - 8)
    perm = np.zeros((MIX_W, MIX_W), np.float32)
    perm[partner, lane] = 1.0
    return jnp.asarray(cos, F32), jnp.asarray(sin, F32), jnp.asarray(perm, BF16)


def _ada_kernel(c_ref, w_ref, b_ref, o_ref):
    c = c_ref[...]
    o_ref[0] = _bdot(_silu(c), w_ref[0]) + b_ref[0]


def _ada(cond, w_ada, b_ada):
    depth, _, n = w_ada.shape
    tn = 1536
    return pl.pallas_call(
        _ada_kernel,
        grid=(depth, n // tn),
        in_specs=[pl.BlockSpec((8, D_MODEL), lambda l, j: (0, 0)),
                  pl.BlockSpec((1, D_MODEL, tn), lambda l, j: (l, 0, j)),
                  pl.BlockSpec((1, 1, tn), lambda l, j: (l, 0, j))],
        out_specs=pl.BlockSpec((1, 8, tn), lambda l, j: (l, 0, j)),
        out_shape=jax.ShapeDtypeStruct((depth, 8, n), F32),
        compiler_params=_params(("parallel", "parallel")),
        name="ada_mod",
    )(cond, w_ada, b_ada.reshape(depth, 1, n))


def _proj_kernel(xl_ref, xc_ref, mod_ref, nw_ref, w_ref, o_na, o_gla, o_gdn, o_diff, o_small, *, n_lat_tiles):
    mod = mod_ref[0]
    x = jnp.where(pl.program_id(0) < n_lat_tiles, xl_ref[...], xc_ref[...])
    h = _rms_rows(x, nw_ref[...]) * (1.0 + mod[1:2]) + mod[0:1]
    hb = h.astype(BF16)
    o_na[...] = _dot(hb, w_ref[:, 0:768])
    o_gla[...] = _dot(hb, w_ref[:, 768:1792])
    o_gdn[...] = _dot(hb, w_ref[:, 1792:2816])
    o_diff[...] = _dot(hb, w_ref[:, 2816:3584])
    o_small[...] = _dot(hb, w_ref[:, 3584:3712])


def _stream_specs(xs, tm):
    _, _, ctx_first, n_lat_tiles = xs
    return [pl.BlockSpec((tm, D_MODEL), lambda i: (jnp.minimum(i, n_lat_tiles - 1), 0)),
            pl.BlockSpec((tm, D_MODEL), lambda i: (ctx_first + jnp.maximum(i - n_lat_tiles, 0), 0))]


def _proj(xs, t, mod_l, nw, wcat, tm, mod_row):
    widths = (768, 1024, 1024, 768, SMALL_W)
    return pl.pallas_call(
        functools.partial(_proj_kernel, n_lat_tiles=xs[3]),
        grid=(t // tm,),
        in_specs=_stream_specs(xs, tm) + [
                  pl.BlockSpec((1, 6, D_MODEL), lambda i: (mod_row(i), 0, 0)),
                  pl.BlockSpec((1, D_MODEL), lambda i: (0, 0)),
                  pl.BlockSpec(wcat.shape, lambda i: (0, 0))],
        out_specs=[pl.BlockSpec((tm, w), lambda i: (i, 0)) for w in widths],
        out_shape=[jax.ShapeDtypeStruct((t, w), F32) for w in widths],
        compiler_params=_params(("parallel",)),
        name="in_proj",
    )(xs[0], xs[1], mod_l, nw, wcat)


def _attn_core(q, k, v, n_sub, lam):
    kb = k.astype(BF16)
    vb = v.astype(BF16)
    tq = q.shape[0]
    sub_w = HEAD_DIM // n_sub
    scale = sub_w ** -0.5 * LOG2E
    out = jnp.zeros(q.shape, F32)
    for h0 in range(0, HEADS, ATTN_HEAD_GROUP):
        maps = [(h, m) for h in range(h0, h0 + ATTN_HEAD_GROUP) for m in range(n_sub)]
        qs = jnp.concatenate(
            [(q * jnp.where(_lane_mask(h * HEAD_DIM + m * sub_w, sub_w), scale, 0.0)).astype(BF16)
             for h, m in maps], axis=0)
        s = lax.dot_general(qs, kb, (((1,), (1,)), ((), ())), preferred_element_type=F32)
        e = jnp.exp2(s - jnp.max(s, axis=-1, keepdims=True))
        inv = 1.0 / jnp.sum(e, axis=-1, keepdims=True)
        if n_sub == 1:
            p = e * inv
        else:
            p = jnp.concatenate(
                [e[(2 * i) * tq:(2 * i + 1) * tq] * inv[(2 * i) * tq:(2 * i + 1) * tq]
                 - e[(2 * i + 1) * tq:(2 * i + 2) * tq] * (lam * inv[(2 * i + 1) * tq:(2 * i + 2) * tq])
                 for i in range(ATTN_HEAD_GROUP)], axis=0)
        o_all = _dot(p.astype(BF16), vb)
        for i in range(ATTN_HEAD_GROUP):
            out = jnp.where(_lane_mask((h0 + i) * HEAD_DIM, HEAD_DIM), o_all[i * tq:(i + 1) * tq], out)
    return out


def _diff_lambda(dl, lam_init):
    a = jnp.sum(dl[0:1] * dl[1:2], axis=-1, keepdims=True)
    b = jnp.sum(dl[2:3] * dl[3:4], axis=-1, keepdims=True)
    return jnp.exp(a) - jnp.exp(b) + lam_init


def _attn_kernel(q_ref, k_ref, v_ref, dl_ref, o_ref, *, n_sub, lam_init):
    lam = _diff_lambda(dl_ref[...], lam_init) if n_sub == 2 else None
    o_ref[...] = _attn_core(q_ref[...], k_ref[...], v_ref[...], n_sub, lam)


def _attention(q_arr, q_col, k_arr, k_col, v_arr, v_col, dl, *, n_sub, lam_init, n_seq, lq, lk,
               q_base, k_base, tq, name):
    nq = lq // tq
    return pl.pallas_call(
        functools.partial(_attn_kernel, n_sub=n_sub, lam_init=lam_init),
        grid=(n_seq, nq),
        in_specs=[pl.BlockSpec((tq, MIX_W), lambda s, j: (q_base // tq + s * nq + j, q_col)),
                  pl.BlockSpec((lk, MIX_W), lambda s, j: (k_base // lk + s, k_col)),
                  pl.BlockSpec((lk, MIX_W), lambda s, j: (k_base // lk + s, v_col)),
                  pl.BlockSpec(dl.shape, lambda s, j: (0, 0))],
        out_specs=pl.BlockSpec((tq, MIX_W), lambda s, j: (s * nq + j, 0)),
        out_shape=jax.ShapeDtypeStruct((n_seq * lq, MIX_W), F32),
        compiler_params=_params(("parallel", "parallel")),
        name=name,
    )(q_arr, k_arr, v_arr, dl)


DIFF_KEY_CHUNK = 256
DIFF_HEAD_GROUP = 1
DIFF_Q_TILE = 256


def _lane_fold(x, op):
    out = x[:, 0:128]
    for t in range(1, x.shape[1] // 128):
        out = op(out, x[:, t * 128:(t + 1) * 128])
    return out


def _diff_group(h0, q, kt_ref, v_ref, lam, tkc):
    tq = q.shape[0]
    nck = kt_ref.shape[1] // tkc
    sub_w = HEAD_DIM // 2
    scale = sub_w ** -0.5 * LOG2E
    maps = [(h, m) for h in range(h0, h0 + DIFF_HEAD_GROUP) for m in range(2)]
    qs = jnp.concatenate(
        [(q * jnp.where(_lane_mask(h * HEAD_DIM + m * sub_w, sub_w), scale, 0.0)).astype(BF16) for h, m in maps],
        axis=0)
    s, mvec = [], None
    for c in range(nck):
        sc = _dot(qs, kt_ref[:, c * tkc:(c + 1) * tkc])
        s.append(sc)
        mc = _lane_fold(sc, jnp.maximum)
        mvec = mc if mvec is None else jnp.maximum(mvec, mc)
        yield
    mrun = jnp.max(mvec, axis=-1, keepdims=True)
    e, lvec = [], None
    for c in range(nck):
        ec = jnp.exp2(s[c] - mrun)
        e.append(ec)
        lc = _lane_fold(ec, jnp.add)
        lvec = lc if lvec is None else lvec + lc
        yield
    inv = 1.0 / jnp.sum(lvec, axis=-1, keepdims=True)
    acc = None
    for c in range(nck):
        a = jnp.concatenate(
            [e[c][(2 * i) * tq:(2 * i + 1) * tq] * inv[(2 * i) * tq:(2 * i + 1) * tq]
             - e[c][(2 * i + 1) * tq:(2 * i + 2) * tq] * (lam * inv[(2 * i + 1) * tq:(2 * i + 2) * tq])
             for i in range(DIFF_HEAD_GROUP)], axis=0)
        o = _dot(a.astype(BF16), v_ref[c * tkc:(c + 1) * tkc, :])
        acc = o if acc is None else acc + o
        if c < nck - 1:
            yield
    return acc


_DIFF_PASSES = 3


def _diff_lat_kernel(q_ref, kt_ref, v_ref, dl_ref, o_ref, *, lam_init, tkc):
    lam = _diff_lambda(dl_ref[...], lam_init)
    q = q_ref[...]
    tq = q.shape[0]
    nck = kt_ref.shape[1] // tkc
    n_grp = HEADS // DIFF_HEAD_GROUP
    groups = [_diff_group(g * DIFF_HEAD_GROUP, q, kt_ref, v_ref, lam, tkc) for g in range(n_grp)]
    res = [None] * n_grp
    for t in range(n_grp + _DIFF_PASSES - 1):
        active = [g for g in range(n_grp) if 0 <= t - g < _DIFF_PASSES]
        for _ in range(nck):
            for g in active:
                try:
                    next(groups[g])
                except StopIteration as done:
                    res[g] = done.value
    out = jnp.zeros(q.shape, F32)
    for h in range(HEADS):
        g, i = divmod(h, DIFF_HEAD_GROUP)
        out = jnp.where(_lane_mask(h * HEAD_DIM, HEAD_DIM), res[g][i * tq:(i + 1) * tq], out)
    o_ref[...] = out


def _diff_latent(q_r, kk_t, vv16, dl, lam_init, n_b, n_tok, lk, tq):
    tkc = DIFF_KEY_CHUNK if lk % DIFF_KEY_CHUNK == 0 else 128
    assert lk % tkc == 0
    nq = n_tok // tq
    return pl.pallas_call(
        functools.partial(_diff_lat_kernel, lam_init=lam_init, tkc=tkc),
        grid=(n_b, nq),
        in_specs=[pl.BlockSpec((tq, MIX_W), lambda b, j: (b * nq + j, 0)),
                  pl.BlockSpec((MIX_W, lk), lambda b, j: (b, 0)),
                  pl.BlockSpec((lk, MIX_W), lambda b, j: (b, 0)),
                  pl.BlockSpec(dl.shape, lambda b, j: (0, 0))],
        out_specs=pl.BlockSpec((tq, MIX_W), lambda b, j: (b * nq + j, 0)),
        out_shape=jax.ShapeDtypeStruct((n_b * n_tok, MIX_W), F32),
        compiler_params=_params(("parallel", "arbitrary")),
        name="diff_latent",
    )(q_r, kk_t, vv16, dl)


def _na_lat_kernel(q_ref, k_ref, v_ref, ck_ref, cv_ref, bias_ref, o_ref, *, rows, kh, rps):
    g = pl.program_id(1)
    ck = ck_ref[...].astype(BF16)
    cv = cv_ref[...].astype(BF16)
    scale = HEAD_DIM ** -0.5 * LOG2E
    head_scale = [jnp.where(_lane_mask(h * HEAD_DIM, HEAD_DIM), scale, 0.0) for h in range(HEADS)]
    hw = HEADS * GRID_W
    qs = []
    for j in range(rps):
        qj = q_ref[j * GRID_W:(j + 1) * GRID_W, :]
        qs.append(jnp.concatenate([(qj * hs).astype(BF16) for hs in head_scale], axis=0))
    s_ctx_all = lax.dot_general(jnp.concatenate(qs, axis=0), ck, (((1,), (1,)), ((), ())),
                                preferred_element_type=F32)
    p_ctx, o_loc = [], []
    for j in range(rps):
        r = g * rps + j
        start = jnp.clip(r - kh // 2, 0, rows - kh)
        cls = r - start
        ws = pl.multiple_of(start * GRID_W, GRID_W)
        kw = k_ref[pl.ds(ws, kh * GRID_W), :].astype(BF16)
        vw = v_ref[pl.ds(ws, kh * GRID_W), :].astype(BF16)
        bias = jnp.concatenate([bias_ref[h, pl.ds(cls, 1)][0] for h in range(HEADS)], axis=0)
        s_loc = lax.dot_general(qs[j], kw, (((1,), (1,)), ((), ())), preferred_element_type=F32) + bias
        s_ctx = s_ctx_all[j * hw:(j + 1) * hw]
        mx = jnp.maximum(jnp.max(s_loc, axis=-1, keepdims=True), jnp.max(s_ctx, axis=-1, keepdims=True))
        e_loc = jnp.exp2(s_loc - mx)
        e_ctx = jnp.exp2(s_ctx - mx)
        inv = 1.0 / (jnp.sum(e_loc, axis=-1, keepdims=True) + jnp.sum(e_ctx, axis=-1, keepdims=True))
        p_ctx.append((e_ctx * inv).astype(BF16))
        o_loc.append(_dot((e_loc * inv).astype(BF16), vw))
    o_ctx_all = _dot(jnp.concatenate(p_ctx, axis=0), cv)
    for j in range(rps):
        o_all = o_loc[j] + o_ctx_all[j * hw:(j + 1) * hw]
        out = jnp.zeros((GRID_W, MIX_W), F32)
        for h in range(HEADS):
            out = jnp.where(_lane_mask(h * HEAD_DIM, HEAD_DIM), o_all[h * GRID_W:(h + 1) * GRID_W], out)
        o_ref[j * GRID_W:(j + 1) * GRID_W, :] = out


def _na_bias_table(rpb, kh):
    cidx = np.arange(GRID_W)
    cls = np.arange(kh)
    drow = np.arange(kh)[None, :] - cls[:, None] + NA_WIN_H - 1
    col_start = np.clip(cidx - NA_WIN_W // 2, 0, GRID_W - NA_WIN_W)
    col_ok = (cidx[None, :] >= col_start[:, None]) & (cidx[None, :] < col_start[:, None] + NA_WIN_W)
    dcol = np.clip(cidx[None, :] - cidx[:, None], 1 - NA_WIN_W, NA_WIN_W - 1) + NA_WIN_W - 1
    onehot = (dcol.reshape(1, -1) == np.arange(2 * NA_WIN_W - 1)[:, None]).astype(np.float32)
    toep = jnp.dot(rpb.astype(F32).reshape(-1, 2 * NA_WIN_W - 1), onehot, precision=lax.Precision.HIGHEST)
    toep = toep.reshape(HEADS, 2 * NA_WIN_H - 1, GRID_W, GRID_W)
    bias = jnp.stack([toep[:, drow[c, 0]:drow[c, 0] + kh] for c in range(kh)], axis=1)
    bias = bias.transpose(0, 1, 3, 2, 4)
    bias = jnp.where(col_ok[:, None, :], bias * LOG2E, NEG)
    return bias.reshape(HEADS, kh, GRID_W, kh * GRID_W)


def _na_latent(z_na, ck, cv, bias, n_b, n_tok, p_len):
    rows = n_tok // GRID_W
    kh = min(NA_WIN_H, rows)
    rps = NA_ROWS_PER_STEP
    assert rows % rps == 0
    steps = rows // rps
    return pl.pallas_call(
        functools.partial(_na_lat_kernel, rows=rows, kh=kh, rps=rps),
        grid=(n_b, steps),
        in_specs=[pl.BlockSpec((rps * GRID_W, MIX_W), lambda b, r: (b * steps + r, 0)),
                  pl.BlockSpec((n_tok, MIX_W), lambda b, r: (b, 1)),
                  pl.BlockSpec((n_tok, MIX_W), lambda b, r: (b, 2)),
                  pl.BlockSpec((p_len, MIX_W), lambda b, r: (b, 0)),
                  pl.BlockSpec((p_len, MIX_W), lambda b, r: (b, 0)),
                  pl.BlockSpec(bias.shape, lambda b, r: (0, 0, 0, 0))],
        out_specs=pl.BlockSpec((rps * GRID_W, MIX_W), lambda b, r: (b * steps + r, 0)),
        out_shape=jax.ShapeDtypeStruct((n_b * n_tok, MIX_W), F32),
        compiler_params=_params(("parallel", "arbitrary")),
        name="na_latent",
    )(z_na, z_na, z_na, ck, cv, bias)


def _rope_kernel(q_ref, k_ref, cos_ref, sin_ref, p_ref, qo_ref, ko_ref):
    cos = cos_ref[...]
    sin = sin_ref[...]
    p = p_ref[...]
    q = q_ref[...]
    k = k_ref[...]
    qo_ref[...] = q * cos + _dot_rx(q, p) * sin
    ko_ref[...] = k * cos + _dot_rx(k, p) * sin


def _rope(z_diff, cos, sin, perm, n_b, n_tok, tm):
    nt = n_tok // tm
    spec_t = pl.BlockSpec((tm, MIX_W), lambda i: (i % nt, 0))
    return pl.pallas_call(
        _rope_kernel,
        grid=(n_b * nt,),
        in_specs=[pl.BlockSpec((tm, MIX_W), lambda i: (i, 0)),
                  pl.BlockSpec((tm, MIX_W), lambda i: (i, 1)),
                  spec_t, spec_t,
                  pl.BlockSpec(perm.shape, lambda i: (0, 0))],
        out_specs=[pl.BlockSpec((tm, MIX_W), lambda i: (i, 0))] * 2,
        out_shape=[jax.ShapeDtypeStruct((n_b * n_tok, MIX_W), F32)] * 2,
        compiler_params=_params(("parallel",)),
        name="diff_rope",
    )(z_diff, z_diff, cos, sin, perm)


def _gdn_pre_kernel(x_ref, prev_ref, next_ref, edge_ref, w_ref, bm_ref, o_ref, *, tm):
    x = x_ref[...]
    w = w_ref[...]
    edge = edge_ref[...]
    row = lax.broadcasted_iota(jnp.int32, (tm, 1), 0)
    x_prev = jnp.where(row == 0, prev_ref[7:8, :], pltpu.roll(x, 1, 0)) * edge[:, 0:1]
    x_next = jnp.where(row == tm - 1, next_ref[0:1, :], pltpu.roll(x, tm - 1, 0)) * edge[:, 1:2]
    y = _silu(x_prev * w[0:1] + x * w[1:2] + x_next * w[2:3])
    bm16 = bm_ref[...]
    q = y[:, 0:MIX_W]
    k = y[:, MIX_W:2 * MIX_W]
    o_ref[:, 0:MIX_W] = q * lax.rsqrt(_head_sum(q * q, bm16) + EPS) * (HEAD_DIM ** -0.5)
    o_ref[:, MIX_W:2 * MIX_W] = k * lax.rsqrt(_head_sum(k * k, bm16) + EPS)
    o_ref[:, 2 * MIX_W:3 * MIX_W] = y[:, 2 * MIX_W:3 * MIX_W]


def _seq_edges(lat_tokens, n_tok, ctx_tokens, l_ctx):
    pos = np.concatenate([np.arange(lat_tokens) % n_tok, np.arange(ctx_tokens) % l_ctx])
    seq = np.concatenate([np.full(lat_tokens, n_tok), np.full(ctx_tokens, l_ctx)])
    edge = np.zeros((lat_tokens + ctx_tokens, 8), np.float32)
    edge[:, 0] = pos != 0
    edge[:, 1] = pos != seq - 1
    return jnp.asarray(edge)


def _gdn_pre(z_gdn, edge, conv_w, bm16, tm):
    t = z_gdn.shape[0]
    w3 = 3 * MIX_W
    nb8 = t // 8
    return pl.pallas_call(
        functools.partial(_gdn_pre_kernel, tm=tm),
        grid=(t // tm,),
        in_specs=[pl.BlockSpec((tm, w3), lambda i: (i, 0)),
                  pl.BlockSpec((8, w3), lambda i: (jnp.maximum(i * (tm // 8) - 1, 0), 0)),
                  pl.BlockSpec((8, w3), lambda i: (jnp.minimum((i + 1) * (tm // 8), nb8 - 1), 0)),
                  pl.BlockSpec((tm, 8), lambda i: (i, 0)),
                  pl.BlockSpec(conv_w.shape, lambda i: (0, 0)),
                  pl.BlockSpec(bm16.shape, lambda i: (0, 0))],
        out_specs=pl.BlockSpec((tm, w3), lambda i: (i, 0)),
        out_shape=jax.ShapeDtypeStruct((t, w3), F32),
        compiler_params=_params(("parallel",)),
        name="gdn_pre",
    )(z_gdn, z_gdn, z_gdn, edge, conv_w, bm16)


def _bd_rows16(x, bm16):
    return jnp.concatenate([x.astype(BF16)] * HEADS, axis=0) * bm16


def _gla_chain(q, k, v, zs, gup, gb, lsm, pm_ref, d, bm, bm16, st):
    nblk = CHUNK // LEAF
    q = q * (HEAD_DIM ** -0.5)
    gp = _bdot(zs, gup) + gb
    g = (jnp.minimum(gp, 0.0) - jnp.log(1.0 + jnp.exp(-jnp.abs(gp)))) * (LOG2E / GATE_NORM)
    yield
    e = _dot_lx(lsm, g)
    b = e[0:CHUNK]
    fb = jnp.exp2(b)
    fbl = jnp.exp2(e[CHUNK:2 * CHUNK])
    ftot = jnp.exp2(e[2 * CHUNK:2 * CHUNK + 1])
    yield
    edge = LEAF - 1 if d == 0 else 0
    blocks = range(nblk - 1) if d == 0 else range(1, nblk)
    b3 = b.reshape(nblk, LEAF, MIX_W)
    b_edge = jnp.broadcast_to(b3[:, edge:edge + 1, :], b3.shape).reshape(CHUNK, MIX_W)
    kf = jnp.exp2(jnp.minimum(b_edge - b, 0.0))
    q_far = jnp.concatenate(
        [(q * jnp.exp2(jnp.minimum(b - b[blk * LEAF + edge:blk * LEAF + edge + 1, :], 0.0))).astype(BF16)
         for blk in blocks], axis=0)
    s_far = lax.dot_general(q_far, _bd_rows16(k * kf, bm16), (((1,), (1,)), ((), ())),
                            preferred_element_type=F32)
    yield
    q_near = jnp.concatenate(
        [(q * jnp.exp2(jnp.minimum(
            b - jnp.broadcast_to(b3[:, r:r + 1, :], b3.shape).reshape(CHUNK, MIX_W), 0.0))).astype(BF16)
         for r in range(LEAF)], axis=0)
    s_near = lax.dot_general(q_near, _bd_rows16(k, bm16), (((1,), (1,)), ((), ())),
                             preferred_element_type=F32)
    yield
    att = None
    for n, blk in enumerate(blocks):
        term = s_far[n * CHUNK:(n + 1) * CHUNK] * pm_ref[d, blk]
        att = term if att is None else att + term
    for r in range(LEAF):
        att = att + s_near[r * CHUNK:(r + 1) * CHUNK] * pm_ref[d, nblk + r]
    yield
    o = _bdot_nt(q * fb, st) + _dot(att.astype(BF16), _bd_rows16(v, bm16))
    yield
    return o, st * ftot + _bdot_tn(v, k * fbl) * bm


def _row_pieces(s, d):
    out = []
    for h in range(HEADS):
        for blk in range(CHUNK // s):
            lo = h * CHUNK + blk * s
            out.append((lo, lo + s, blk % 2 == 1 - d))
    return out


def _tri_inverse(mbd, sm_ref, d):
    r = lax.broadcasted_iota(jnp.int32, (MIX_W, MIX_W), 0)
    c = lax.broadcasted_iota(jnp.int32, (MIX_W, MIX_W), 1)
    eye = jnp.where(r == c, 1.0, 0.0)
    md = mbd * sm_ref[0]
    m2 = _bdot(md, md)
    yield
    m4 = _bdot(m2, m2)
    t = eye - md
    t = t + _bdot(t, m2)
    yield
    t = t + _bdot(t, m4)
    s = LEAF
    for lvl in range(1, sm_ref.shape[0]):
        pieces = _row_pieces(s, d)
        take = lambda x: jnp.concatenate([x[lo:hi] for lo, hi, upd in pieces if upd], axis=0)
        yield
        t16 = t.astype(BF16)
        a = _dot(take(mbd * sm_ref[lvl]).astype(BF16), t16).astype(BF16)
        rows, n = [], 0
        for lo, hi, upd in pieces:
            rows.append(a[n:n + s] if upd else jnp.zeros((s, MIX_W), BF16))
            n += s if upd else 0
        yield
        b = _dot(take(t16), jnp.concatenate(rows, axis=0))
        rows, n = [], 0
        for lo, hi, upd in pieces:
            rows.append(t[lo:hi] - b[n:n + s] if upd else t[lo:hi])
            n += s if upd else 0
        t = jnp.concatenate(rows, axis=0)
        s *= 2
    return t


def _gdn_chain(q, k, v, zs, ea, eb, alog, dtb, lsm, lbt, pm_ref, d, bm, bm16, sm_ref, s):
    g = -jnp.exp(alog) * _softplus(_dot_rx(zs, ea) + dtb)
    beta = jax.nn.sigmoid(_dot_rx(zs, eb))
    yield
    e = _dot_lx(lsm, g)
    dcol = e[0:CHUNK]
    drow = _dot_lx(jnp.ones((8, CHUNK), BF16), g * lbt)[0:1]
    kb = k * beta
    s2 = lax.dot_general(jnp.concatenate([kb, q], axis=0).astype(BF16), _bd_rows16(k, bm16),
                         (((1,), (1,)), ((), ())), preferred_element_type=F32)
    yield
    dec = jnp.exp(jnp.where(pm_ref[d, 1] > 0.0, dcol - drow, NEG))
    m_cat = s2[0:CHUNK] * dec * pm_ref[d, 0]
    att = s2[CHUNK:2 * CHUNK] * dec
    t_inv = yield from _tri_inverse(_bd_rows(m_cat, bm), sm_ref, d)
    t16 = (t_inv[0:CHUNK] + t_inv[CHUNK:2 * CHUNK] + t_inv[2 * CHUNK:3 * CHUNK] + t_inv[3 * CHUNK:]).astype(BF16)
    fb = jnp.exp(dcol)
    fbl = jnp.exp(e[CHUNK:2 * CHUNK])
    ftot = jnp.exp(e[2 * CHUNK:2 * CHUNK + 1])
    yield
    u = _dot(t16, _bd_rows16(v * beta, bm16))
    w = _dot(t16, _bd_rows16(kb * fb, bm16))
    yield
    ws = _dot(jnp.concatenate([w.astype(BF16), (q * fb).astype(BF16)], axis=0), s.astype(BF16))
    v_new = u - ws[0:CHUNK]
    yield
    o = ws[CHUNK:2 * CHUNK] + _dot(att.astype(BF16), _bd_rows16(v_new, bm16))
    return o, s * ftot + lax.dot_general((k * fbl).astype(BF16), v_new.astype(BF16), (((0,), (0,)), ((), ())),
                                         preferred_element_type=F32) * bm


def _scan_kernel(*refs, chain, n_const, nb, n_chunks):
    n_tok = nb * 2 * 4
    tok = refs[:n_tok]
    consts = refs[n_tok:n_tok + n_const]
    s0_ref, of_ref, ob_ref, sf_ref, s_scr = refs[n_tok + n_const:]
    n = pl.program_id(1)

    @pl.when(n == 0)
    def _():
        s_scr[...] = s0_ref[...]

    ids = [(j, d) for j in range(nb) for d in range(2)]
    gens = []
    for j, d in ids:
        q_ref, k_ref, v_ref, zs_ref = tok[(j * 2 + d) * 4:(j * 2 + d) * 4 + 4]
        gens.append(chain(q_ref[...], k_ref[...], v_ref[...], zs_ref[...], consts, d, s_scr[j, d]))
    live = list(range(len(gens)))
    while live:
        for c in list(live):
            try:
                next(gens[c])
            except StopIteration as done:
                j, d = ids[c]
                o, s_new = done.value
                (of_ref, ob_ref)[d][j] = o
                s_scr[j, d] = s_new
                live.remove(c)

    @pl.when(n == n_chunks - 1)
    def _():
        sf_ref[...] = s_scr[...]


def _gla_step(q, k, v, zs, consts, d, st):
    gup_ref, gb_ref, lsm_ref, pm_ref, bm_ref, bm16_ref = consts
    return _gla_chain(q, k, v, zs, gup_ref[d], gb_ref[d], lsm_ref[d], pm_ref, d, bm_ref[...], bm16_ref[...], st)


def _gdn_step(q, k, v, zs, consts, d, s):
    ea_ref, eb_ref, alog_ref, dtb_ref, lsm_ref, lbt_ref, pm_ref, bm_ref, bm16_ref, sm_ref = consts
    return _gdn_chain(q, k, v, zs, ea_ref[d], eb_ref[d], alog_ref[d], dtb_ref[d], lsm_ref[d], lbt_ref[d],
                      pm_ref, d, bm_ref[...], bm16_ref[...], sm_ref, s)


def _scan(chain, qkv, z_small, consts, s0, n_seq, l_seq, base, seqs_per_step, name):
    nc = l_seq // CHUNK
    nb = seqs_per_step if n_seq % seqs_per_step == 0 else 1

    def chunk(n, d):
        return n + d * (nc - 1 - 2 * n)

    in_specs, args = [], []
    for j in range(nb):
        for d in range(2):
            for arr, col, width in ((qkv, 0, MIX_W), (qkv, 1, MIX_W), (qkv, 2, MIX_W), (z_small, 0, SMALL_W)):
                in_specs.append(pl.BlockSpec(
                    (CHUNK, width),
                    functools.partial(lambda g, n, j, d, col: (base // CHUNK + (g * nb + j) * nc + chunk(n, d), col),
                                      j=j, d=d, col=col)))
                args.append(arr)
    for cst in consts:
        in_specs.append(pl.BlockSpec(cst.shape, functools.partial(lambda g, n, nd: (0,) * nd, nd=cst.ndim)))
        args.append(cst)
    state_spec = pl.BlockSpec((nb, 2, MIX_W, MIX_W), lambda g, n: (g, 0, 0, 0))
    in_specs.append(state_spec)
    args.append(s0)
    o_f, o_b, s_f = pl.pallas_call(
        functools.partial(_scan_kernel, chain=chain, n_const=len(consts), nb=nb, n_chunks=nc),
        grid=(n_seq // nb, nc),
        in_specs=in_specs,
        out_specs=[pl.BlockSpec((nb, CHUNK, MIX_W), lambda g, n: (g, chunk(n, 0), 0)),
                   pl.BlockSpec((nb, CHUNK, MIX_W), lambda g, n: (g, chunk(n, 1), 0)),
                   state_spec],
        out_shape=[jax.ShapeDtypeStruct((n_seq, l_seq, MIX_W), F32),
                   jax.ShapeDtypeStruct((n_seq, l_seq, MIX_W), F32),
                   jax.ShapeDtypeStruct((n_seq, 2, MIX_W, MIX_W), F32)],
        scratch_shapes=[pltpu.VMEM((nb, 2, MIX_W, MIX_W), F32)],
        compiler_params=_params(("parallel", "arbitrary")),
        name=name,
    )(*args)
    return o_f.reshape(n_seq * l_seq, MIX_W), o_b.reshape(n_seq * l_seq, MIX_W), s_f


def _merge_kernel(xl_ref, xc_ref, mod_ref, n1_ref, na_l, na_c, glaf_l, glaf_c, glab_l, glab_c, glag_ref, gdnf_l, gdnf_c,
                  gdnb_l, gdnb_c, gdng_ref, df_l, df_c, hn_ref, bm_ref, wg_ref, wb_ref, wo_ref, o_ref, *,
                  lam_init, n_lat_tiles):
    is_lat = pl.program_id(0) < n_lat_tiles

    def pick(lat_ref, ctx_ref):
        return jnp.where(is_lat, lat_ref[...], ctx_ref[...])

    mod = mod_ref[0]
    x = pick(xl_ref, xc_ref)
    h = _rms_rows(x, n1_ref[...]) * (1.0 + mod[1:2]) + mod[0:1]
    hb = h.astype(BF16)
    bm16 = bm_ref[...]

    def head_norm(o, w):
        return o * lax.rsqrt(_head_sum(o * o, bm16) * (1.0 / HEAD_DIM) + EPS) * w

    branches = (
        pick(na_l, na_c),
        head_norm(pick(glaf_l, glaf_c) + pick(glab_l, glab_c), hn_ref[0]) * _silu(glag_ref[...]),
        head_norm(pick(gdnf_l, gdnf_c) + pick(gdnb_l, gdnb_c), hn_ref[1]) * _silu(gdng_ref[...]),
        head_norm(pick(df_l, df_c), hn_ref[2]) * (1.0 - lam_init),
    )
    merged = None
    for n, br in enumerate(branches):
        gate = jax.nn.sigmoid(_dot(hb, wg_ref[:, n * D_MODEL:(n + 1) * D_MODEL]))
        term = gate * _bdot(br, wb_ref[n])
        merged = term if merged is None else merged + term
    o_ref[...] = x + mod[2:3] * _bdot(merged, wo_ref[...])


def _merge(xs, t, mod_l, n1, na, gla_f, gla_b, z_gla, gdn_f, gdn_b, z_gdn, df, hn, bm16, wg, wb, wo, tm, mod_row,
           lam_init, n_lat_tiles):
    tok = lambda col: pl.BlockSpec((tm, MIX_W), lambda i: (i, col))
    lat = pl.BlockSpec((tm, MIX_W), lambda i: (jnp.minimum(i, n_lat_tiles - 1), 0))
    ctx = pl.BlockSpec((tm, MIX_W), lambda i: (jnp.maximum(i - n_lat_tiles, 0), 0))
    full = lambda a: pl.BlockSpec(a.shape, lambda i: (0,) * a.ndim)
    return pl.pallas_call(
        functools.partial(_merge_kernel, lam_init=lam_init, n_lat_tiles=n_lat_tiles),
        grid=(t // tm,),
        in_specs=_stream_specs(xs, tm) + [
                  pl.BlockSpec((1, 6, D_MODEL), lambda i: (mod_row(i), 0, 0)),
                  full(n1), lat, ctx, lat, ctx, lat, ctx, tok(3), lat, ctx, lat, ctx, tok(3), lat, ctx,
                  full(hn), full(bm16), full(wg), full(wb), full(wo)],
        out_specs=pl.BlockSpec((tm, D_MODEL), lambda i: (i, 0)),
        out_shape=jax.ShapeDtypeStruct((t, D_MODEL), F32),
        compiler_params=_params(("parallel",)),
        name="merge_out",
    )(xs[0], xs[1], mod_l, n1, *na, *gla_f, *gla_b, z_gla, *gdn_f, *gdn_b, z_gdn, *df, hn, bm16, wg, wb, wo)


def _route(h2, wr_ref, br_ref):
    s_t = jax.nn.sigmoid(_dot3(h2, wr_ref[...])).T
    s = [s_t[e:e + 1] for e in range(N_EXPERTS)]
    sel = [s[e] + br_ref[e] for e in range(N_EXPERTS)]
    best = None
    for g in range(N_GROUPS):
        v = sel[g * GROUP_E:(g + 1) * GROUP_E]
        score = None
        for i in range(GROUP_E):
            for j in range(i + 1, GROUP_E):
                score = v[i] + v[j] if score is None else jnp.maximum(score, v[i] + v[j])
        if best is None:
            best, gi = score, jnp.zeros_like(score)
        else:
            better = score > best
            gi = jnp.where(better, float(g), gi)
            best = jnp.where(better, score, best)
    vals = [jnp.where(gi == float(e // GROUP_E), sel[e], NEG) for e in range(N_EXPERTS)]

    def first_max(v):
        mx, idx = v[0], jnp.zeros_like(v[0])
        for e in range(1, N_EXPERTS):
            better = v[e] > mx
            idx = jnp.where(better, float(e), idx)
            mx = jnp.where(better, v[e], mx)
        return idx

    i1 = first_max(vals)
    i2 = first_max([jnp.where(i1 == float(e), NEG, vals[e]) for e in range(N_EXPERTS)])
    s1 = sum(jnp.where(i1 == float(e), s[e], 0.0) for e in range(N_EXPERTS))
    s2 = sum(jnp.where(i2 == float(e), s[e], 0.0) for e in range(N_EXPERTS))
    inv = 1.0 / (s1 + s2)
    rows = [jnp.where(i1 == float(e), s1 * inv, 0.0) + jnp.where(i2 == float(e), s2 * inv, 0.0)
            for e in range(N_EXPERTS)]
    gates_t = jnp.concatenate(rows + [jnp.zeros((SMALL_W - N_EXPERTS, h2.shape[0]), F32)], axis=0)
    return gates_t.T


def _moe_kernel(x_ref, mod_ref, n2_ref, wr_ref, br_ref, w1_ref, w3_ref, w2_ref, *rest, n_lat_tiles):
    if n_lat_tiles is None:
        o_ref, h_scr, g_scr, acc_scr = rest
    else:
        fw_ref, ol_ref, oc_ref, h_scr, g_scr, acc_scr = rest
    g = pl.program_id(1)
    mod = mod_ref[0]

    @pl.when(g == 0)
    def _():
        h2 = _rms_rows(x_ref[...], n2_ref[...]) * (1.0 + mod[4:5]) + mod[3:4]
        h_scr[...] = h2.astype(BF16)
        g_scr[...] = _route(h2, wr_ref, br_ref)
        acc_scr[...] = jnp.zeros_like(acc_scr)

    hb = h_scr[...]
    gates = g_scr[...]
    lane = lax.broadcasted_iota(jnp.int32, (1, SMALL_W), 1)
    he = jnp.concatenate(
        [_silu(_dot(hb, w1_ref[e])) * _dot(hb, w3_ref[e])
         * jnp.sum(jnp.where(lane == g * GROUP_E + e, gates, 0.0), axis=-1, keepdims=True)
         for e in range(GROUP_E)], axis=1)
    acc_scr[...] += _dot(he.astype(BF16), w2_ref[...])

    if n_lat_tiles is None:
        @pl.when(g == N_GROUPS - 1)
        def _():
            o_ref[...] = x_ref[...] + mod[5:6] * acc_scr[...]
    else:
        is_lat = pl.program_id(0) < n_lat_tiles

        @pl.when((g == N_GROUPS - 1) & is_lat)
        def _():
            ol_ref[...] = _rms_rows(x_ref[...] + mod[5:6] * acc_scr[...], fw_ref[...])

        @pl.when((g == N_GROUPS - 1) & jnp.logical_not(is_lat))
        def _():
            oc_ref[...] = _rms_rows(x_ref[...] + mod[5:6] * acc_scr[...], fw_ref[...])


def _moe(x, mod_l, n2, wr, br, w1, w3, w2, tm, mod_row, final_w=None, n_lat_tiles=None):
    t = x.shape[0]
    gw = GROUP_E * D_FF
    in_specs = [pl.BlockSpec((tm, D_MODEL), lambda i, g: (i, 0)),
                pl.BlockSpec((1, 6, D_MODEL), lambda i, g: (mod_row(i), 0, 0)),
                pl.BlockSpec(n2.shape, lambda i, g: (0, 0)),
                pl.BlockSpec(wr.shape, lambda i, g: (0, 0)),
                pl.BlockSpec(memory_space=pltpu.SMEM),
                pl.BlockSpec((GROUP_E, D_MODEL, D_FF), lambda i, g: (g, 0, 0)),
                pl.BlockSpec((GROUP_E, D_MODEL, D_FF), lambda i, g: (g, 0, 0)),
                pl.BlockSpec((gw, D_MODEL), lambda i, g: (g, 0))]
    args = [x, mod_l, n2, wr, br, w1, w3, w2]
    if final_w is None:
        out_specs = pl.BlockSpec((tm, D_MODEL), lambda i, g: (i, 0))
        out_shape = jax.ShapeDtypeStruct((t, D_MODEL), F32)
    else:
        in_specs.append(pl.BlockSpec(final_w.shape, lambda i, g: (0, 0)))
        args.append(final_w)
        out_specs = [pl.BlockSpec((tm, D_MODEL), lambda i, g: (jnp.minimum(i, n_lat_tiles - 1), 0)),
                     pl.BlockSpec((tm, D_MODEL), lambda i, g: (jnp.maximum(i - n_lat_tiles, 0), 0))]
        out_shape = [jax.ShapeDtypeStruct((n_lat_tiles * tm, D_MODEL), F32),
                     jax.ShapeDtypeStruct((t - n_lat_tiles * tm, D_MODEL), F32)]
    return pl.pallas_call(
        functools.partial(_moe_kernel, n_lat_tiles=n_lat_tiles),
        grid=(t // tm, N_GROUPS),
        in_specs=in_specs,
        out_specs=out_specs,
        out_shape=out_shape,
        scratch_shapes=[pltpu.VMEM((tm, D_MODEL), BF16), pltpu.VMEM((tm, SMALL_W), F32),
                        pltpu.VMEM((tm, D_MODEL), F32)],
        compiler_params=_params(("arbitrary", "arbitrary")),
        name="moe",
    )(*args)


def _block_diag_state(s, transpose):
    if transpose:
        s = jnp.swapaxes(s, -1, -2)
    eye = jnp.eye(HEADS, dtype=s.dtype)
    out = jnp.einsum('bdhij,hg->bdhigj', s, eye)
    return out.reshape(s.shape[0], 2, MIX_W, MIX_W)


def _diag_blocks(s_bd, transpose):
    b = s_bd.shape[0]
    s = s_bd.reshape(b, 2, HEADS, HEAD_DIM, HEADS, HEAD_DIM)
    s = jnp.stack([s[:, :, h, :, h, :] for h in range(HEADS)], axis=2)
    return jnp.swapaxes(s, -1, -2) if transpose else s


def _pick_tile(*lengths):
    for tm in (512, 256, 128, 64):
        if all(n % tm == 0 for n in lengths):
            return tm
    raise ValueError("token counts must be multiples of 64")


def kernel(x_prompt, x_sample, cache_na_k, cache_na_v, cache_diff_k, cache_diff_v, state_gla, state_gdn, c, c_ctx, w_ada, b_ada, norm1_w, norm2_w, w_in, na_rpb, gla_gate_up, gla_gate_bias, gla_norm_w, gdn_conv_w, gdn_A_log, gdn_dt_bias, gdn_norm_w, diff_lambda, diff_subln_w, w_branch, w_out, w_router, b_router, w_e1, w_e3, w_e2, final_norm_w):
    n_ctx, l_ctx, _ = x_prompt.shape
    n_lat, n_tok, _ = x_sample.shape
    depth = w_ada.shape[0]
    p_len = cache_na_k.shape[2]
    lat_tokens = n_lat * n_tok
    ctx_tokens = n_ctx * l_ctx
    assert n_lat + 1 <= 8 and n_tok % l_ctx == 0 and n_tok % GRID_W == 0 and l_ctx % CHUNK == 0
    tm = _pick_tile(n_tok, ctx_tokens)
    tq_lat = _pick_tile(n_tok) // 4 if _pick_tile(n_tok) >= 512 else 64

    def mod_row(i):
        return jnp.where(i * tm < lat_tokens, (i * tm) // n_tok, n_lat)

    tm_moe = MOE_TILE if (n_tok % MOE_TILE == 0 and ctx_tokens % MOE_TILE == 0) else tm

    def mod_row_moe(i):
        return jnp.where(i * tm_moe < lat_tokens, (i * tm_moe) // n_tok, n_lat)

    tokens = lat_tokens + ctx_tokens
    xs = (x_sample.reshape(lat_tokens, D_MODEL), x_prompt.reshape(ctx_tokens, D_MODEL), 0, lat_tokens // tm)
    cond = jnp.zeros((8, D_MODEL), F32).at[:n_lat].set(c).at[n_lat].set(c_ctx)
    mod = _ada(cond, w_ada, b_ada).reshape(depth, 8, 6, D_MODEL)

    gla_pm = _gla_pair_masks()
    lsm, lbt, pm = _gdn_consts()
    bm = jnp.asarray(_block_mask(), F32)
    bm16 = bm.astype(BF16)
    sm = _solve_masks()
    cos, sin, perm = _rope_consts(n_tok)
    edge = _seq_edges(lat_tokens, n_tok, ctx_tokens, l_ctx)
    lane = np.arange(MIX_W)
    small_rows = np.arange(SMALL_W)
    def expand_cols(first_col):
        return jnp.asarray((small_rows[:, None] == first_col + lane[None, :] // HEAD_DIM).astype(np.float32), BF16)
    ea = jnp.stack([expand_cols(32), expand_cols(36)])
    eb = jnp.stack([expand_cols(40), expand_cols(44)])
    wr = jnp.zeros((D_MODEL, SMALL_W), F32).at[:, :N_EXPERTS].set(w_router)
    br = b_router.astype(F32)

    def cols(w, *names):
        return [w[:, _OFF[n][0]:_OFF[n][1]] for n in names]

    zeros_state = jnp.zeros((n_ctx, 2, MIX_W, MIX_W), F32)
    new_kv, new_gla, new_gdn = [], [], []
    for l in range(depth):
        lam_init = 0.8 - 0.6 * math.exp(-0.3 * l)
        wl = w_in[l]
        wcat = jnp.concatenate(
            cols(wl, 'na_q', 'na_k', 'na_v', 'gla_q', 'gla_k', 'gla_v', 'gla_og', 'gdn_qkv', 'gdn_og',
                 'diff_q', 'diff_k', 'diff_v', 'gla_gf', 'gla_gb', 'gdn_af', 'gdn_ab', 'gdn_bf', 'gdn_bb')
            + [jnp.zeros((D_MODEL, SMALL_W - 2 * GATE_RANK - 4 * HEADS), F32)], axis=1).astype(BF16)
        w_gate = wl[:, _OFF['branch_gate'][0]:].astype(BF16)
        z_na, z_gla, z_gdn, z_diff, z_small = _proj(xs, tokens, mod[l], norm1_w[l][None], wcat, tm, mod_row)

        ck = cache_na_k[:, l].reshape(n_lat * p_len, MIX_W)
        cv = cache_na_v[:, l].reshape(n_lat * p_len, MIX_W)
        dl = diff_lambda[l]
        na_lat = _na_latent(z_na, ck, cv, _na_bias_table(na_rpb[l], min(NA_WIN_H, n_tok // GRID_W)),
                            n_lat, n_tok, p_len)
        na_ctx = _attention(z_na, 0, z_na, 1, z_na, 2, dl, n_sub=1, lam_init=0.0, n_seq=n_ctx, lq=l_ctx,
                            lk=l_ctx, q_base=lat_tokens, k_base=lat_tokens, tq=l_ctx, name="na_ctx")

        q_r, k_r = _rope(z_diff, cos, sin, perm, n_lat, n_tok, tm)
        kk = jnp.concatenate([k_r.reshape(n_lat, n_tok, MIX_W), cache_diff_k[:, l].reshape(n_lat, p_len, MIX_W)], 1)
        vv = jnp.concatenate([z_diff[:lat_tokens, 2 * MIX_W:].reshape(n_lat, n_tok, MIX_W),
                              cache_diff_v[:, l].reshape(n_lat, p_len, MIX_W)], 1)
        lkv = n_tok + p_len
        kk_t = kk.astype(BF16).transpose(0, 2, 1).reshape(n_lat * MIX_W, lkv)
        df_lat = _diff_latent(q_r, kk_t, vv.astype(BF16).reshape(n_lat * lkv, MIX_W), dl, lam_init,
                              n_lat, n_tok, lkv, DIFF_Q_TILE if n_tok % DIFF_Q_TILE == 0 else tq_lat)
        df_ctx = _attention(z_diff, 0, z_diff, 1, z_diff, 2, dl, n_sub=2, lam_init=lam_init, n_seq=n_ctx,
                            lq=l_ctx, lk=l_ctx, q_base=lat_tokens, k_base=lat_tokens, tq=l_ctx, name="diff_ctx")

        gup = jnp.zeros((2, SMALL_W, MIX_W), F32)
        gup = gup.at[0, 0:GATE_RANK].set(gla_gate_up[l, 0]).at[1, GATE_RANK:2 * GATE_RANK].set(gla_gate_up[l, 1])
        gbias = gla_gate_bias[l][:, None, :]
        gla_consts = (gup, gbias, lsm, gla_pm, bm, bm16)
        glf_lat, glb_lat, _ = _scan(_gla_step, z_gla, z_small, gla_consts,
                                    _block_diag_state(state_gla[:, l], True), n_lat, n_tok, 0,
                                    GLA_SEQS_PER_STEP, "gla_scan")
        glf_ctx, glb_ctx, st_ctx = _scan(_gla_step, z_gla, z_small, gla_consts, zeros_state,
                                         n_ctx, l_ctx, lat_tokens, GLA_SEQS_PER_STEP, "gla_scan")

        qkv_n = _gdn_pre(z_gdn, edge, gdn_conv_w[l], bm16, tm)
        alog = jnp.repeat(gdn_A_log[l], HEAD_DIM, axis=-1)[:, None, :]
        dtb = jnp.repeat(gdn_dt_bias[l], HEAD_DIM, axis=-1)[:, None, :]
        gdn_consts = (ea, eb, alog, dtb, lsm, lbt, pm, bm, bm16, sm)
        gdf_lat, gdb_lat, _ = _scan(_gdn_step, qkv_n, z_small, gdn_consts,
                                    _block_diag_state(state_gdn[:, l], False), n_lat, n_tok, 0,
                                    GDN_SEQS_PER_STEP, "gdn_scan")
        gdf_ctx, gdb_ctx, s_ctx = _scan(_gdn_step, qkv_n, z_small, gdn_consts, zeros_state,
                                        n_ctx, l_ctx, lat_tokens, GDN_SEQS_PER_STEP, "gdn_scan")

        hn = jnp.stack([jnp.tile(gla_norm_w[l], HEADS), jnp.tile(gdn_norm_w[l], HEADS),
                        jnp.tile(diff_subln_w[l], HEADS)])[:, None, :]
        x = _merge(xs, tokens, mod[l], norm1_w[l][None], (na_lat, na_ctx), (glf_lat, glf_ctx), (glb_lat, glb_ctx), z_gla,
                   (gdf_lat, gdf_ctx), (gdb_lat, gdb_ctx), z_gdn, (df_lat, df_ctx), hn, bm16,
                   w_gate, w_branch[l].astype(BF16), w_out[l].astype(BF16), tm, mod_row, lam_init,
                   lat_tokens // tm)
        w1 = w_e1[l].astype(BF16)
        w3 = w_e3[l].astype(BF16)
        w2 = w_e2[l].reshape(N_EXPERTS * D_FF, D_MODEL).astype(BF16)
        if l < depth - 1:
            x = _moe(x, mod[l], norm2_w[l][None], wr, br, w1, w3, w2, tm_moe, mod_row_moe)
            xs = (x, x, lat_tokens // tm, lat_tokens // tm)
        else:
            y_lat, y_ctx = _moe(x, mod[l], norm2_w[l][None], wr, br, w1, w3, w2, tm_moe, mod_row_moe,
                                final_norm_w[None], lat_tokens // tm_moe)

        def ctx_heads(z, col):
            return z[lat_tokens:, col * MIX_W:(col + 1) * MIX_W].reshape(n_ctx, l_ctx, HEADS, HEAD_DIM)
        new_kv.append((ctx_heads(z_na, 1), ctx_heads(z_na, 2), ctx_heads(z_diff, 1), ctx_heads(z_diff, 2)))
        new_gla.append(_diag_blocks(st_ctx, True))
        new_gdn.append(_diag_blocks(s_ctx, False))

    y_sample = y_lat.reshape(n_lat, n_tok, D_MODEL)
    y_prompt = y_ctx.reshape(n_ctx, l_ctx, D_MODEL)
    stack = lambda j: jnp.stack([t[j] for t in new_kv], axis=1)
    return (y_prompt, y_sample, stack(0), stack(1), stack(2), stack(3),
            jnp.stack(new_gla, axis=1), jnp.stack(new_gdn, axis=1))
```

```python
import functools
import math

import numpy as np
import jax
import jax.numpy as jnp
from jax import lax
from jax.experimental import pallas as pl
from jax.experimental.pallas import tpu as pltpu

F32 = jnp.float32
BF16 = jnp.bfloat16

D_MODEL = 1024
HEADS = 4
HEAD_DIM = 64
MIX_W = HEADS * HEAD_DIM
GRID_W = 64
NA_WIN_H = 8
NA_WIN_W = 16
GATE_RANK = 16
GATE_NORM = 16.0
CHUNK = 64
N_EXPERTS = 16
N_GROUPS = 4
GROUP_E = N_EXPERTS // N_GROUPS
D_FF = D_MODEL // 4
ROPE_BASE = 10000.0
EPS = 1e-6
NEG = -1e30
LOG2E = 1.4426950408889634
SMALL_W = 128
V7X_VMEM_LIMIT = 56 * 1024 * 1024
LEAF = 8
ATTN_HEAD_GROUP = 2
NA_ROWS_PER_STEP = 4
GLA_SEQS_PER_STEP = 4
GDN_SEQS_PER_STEP = 4
MOE_TILE = 512

_IN_SPLITS = (
    ('na_q', MIX_W), ('na_k', MIX_W), ('na_v', MIX_W),
    ('gla_q', MIX_W), ('gla_k', MIX_W), ('gla_v', MIX_W),
    ('gla_gf', GATE_RANK), ('gla_gb', GATE_RANK), ('gla_og', MIX_W),
    ('gdn_qkv', 3 * MIX_W), ('gdn_af', HEADS), ('gdn_ab', HEADS),
    ('gdn_bf', HEADS), ('gdn_bb', HEADS), ('gdn_og', MIX_W),
    ('diff_q', MIX_W), ('diff_k', MIX_W), ('diff_v', MIX_W),
    ('branch_gate', 4 * D_MODEL),
)
_OFF = {}
_o = 0
for _n, _s in _IN_SPLITS:
    _OFF[_n] = (_o, _o + _s)
    _o += _s

def _params(sem):
    return pltpu.CompilerParams(dimension_semantics=sem, vmem_limit_bytes=V7X_VMEM_LIMIT)


def _dot(a, b):
    return jnp.dot(a, b, preferred_element_type=F32)


def _bdot(a, b):
    return _dot(a.astype(BF16), b.astype(BF16))


def _bdot_nt(a, b):
    return lax.dot_general(a.astype(BF16), b.astype(BF16), (((1,), (1,)), ((), ())),
                           preferred_element_type=F32)


def _bdot_tn(a, b):
    return lax.dot_general(a.astype(BF16), b.astype(BF16), (((0,), (0,)), ((), ())),
                           preferred_element_type=F32)


def _split(x):
    hi = x.astype(BF16)
    lo = (x - hi.astype(F32)).astype(BF16)
    return hi, lo


def _dot_rx(x, m):
    hi, lo = _split(x)
    return _dot(hi, m) + _dot(lo, m)


def _dot_lx(m, x):
    hi, lo = _split(x)
    return _dot(m, hi) + _dot(m, lo)


def _dot3(a, b):
    ah, al = _split(a)
    bh, bl = _split(b)
    return _dot(ah, bh) + _dot(ah, bl) + _dot(al, bh)


def _silu(x):
    return x * jax.nn.sigmoid(x)


def _softplus(x):
    return jnp.maximum(x, 0.0) + jnp.log(1.0 + jnp.exp(-jnp.abs(x)))


def _rms_rows(x, w):
    return x * lax.rsqrt(jnp.mean(x * x, axis=-1, keepdims=True) + EPS) * w


def _head_sum(x, bm16):
    return _dot_rx(x, bm16)


def _bd_rows(x, bm):
    return jnp.concatenate([x] * HEADS, axis=0) * bm


def _lane_mask(lo, width, n=MIX_W):
    lane = lax.broadcasted_iota(jnp.int32, (1, n), 1)
    return (lane >= lo) & (lane < lo + width)


def _gdn_consts():
    c = CHUNK
    i = np.arange(c)[:, None]
    t = np.arange(c)[None, :]
    lb = (t <= i).astype(np.float32)
    lbl = (t > i).astype(np.float32)
    tot = np.ones((16, c), np.float32)
    strict = (t < i).astype(np.float32)
    incl = (t <= i).astype(np.float32)
    rev = lambda a: a[::-1, ::-1]
    lsm = np.stack([np.concatenate([lb, lbl, tot], 0), np.concatenate([rev(lb), rev(lbl), tot], 0)])
    lbt = np.stack([np.tile(lb.T, (1, HEADS)), np.tile(rev(lb).T, (1, HEADS))])
    pm = np.stack([np.stack([np.tile(strict, (1, HEADS)), np.tile(incl, (1, HEADS))]),
                   np.stack([np.tile(rev(strict), (1, HEADS)), np.tile(rev(incl), (1, HEADS))])])
    return jnp.asarray(lsm, BF16), jnp.asarray(lbt, F32), jnp.asarray(pm, F32)


def _gla_pair_masks():
    i = np.arange(CHUNK)[:, None]
    j = np.arange(CHUNK)[None, :]
    nblk = CHUNK // LEAF
    out = []
    for d in range(2):
        later = (i // LEAF > j // LEAF) if d == 0 else (i // LEAF < j // LEAF)
        order = (j <= i) if d == 0 else (j >= i)
        far = [(j // LEAF == blk) & later for blk in range(nblk)]
        near = [(i // LEAF == j // LEAF) & (j % LEAF == r) & order for r in range(LEAF)]
        out.append(np.stack([np.tile(m.astype(np.float32), (1, HEADS)) for m in far + near]))
    return jnp.asarray(np.stack(out), F32)


def _block_mask():
    r = np.arange(MIX_W)
    return (r[:, None] // HEAD_DIM == r[None, :] // HEAD_DIM).astype(np.float32)


def _solve_masks():
    r = np.arange(MIX_W)[:, None]
    c = np.arange(MIX_W)[None, :]
    out = [(r // LEAF == c // LEAF)]
    s = LEAF
    while s < CHUNK:
        out.append((r // (2 * s) == c // (2 * s)) & (r // s != c // s))
        s *= 2
    return jnp.asarray(np.stack(out).astype(np.float32))


def _rope_consts(n_tok):
    t = np.arange(n_tok)
    pos = np.stack([t // GRID_W, t % GRID_W], 0).astype(np.float32)
    lane = np.arange(MIX_W)
    u = lane % 32
    axis = u // 16
    w = u % 16
    first = w < 8
    inv = ROPE_BASE ** (-(w % 8).astype(np.float32) / 8.0)
    ang = pos[axis, :].T * inv[None, :]
    cos = np.cos(ang)
    sin = np.sin(ang) * np.where(first, -1.0, 1.0)[None, :]
    partner = np.where(first, lane + 8, lane - 8)
    perm = np.zeros((MIX_W, MIX_W), np.float32)
    perm[partner, lane] = 1.0
    return jnp.asarray(cos, F32), jnp.asarray(sin, F32), jnp.asarray(perm, BF16)


def _ada_kernel(c_ref, w_ref, b_ref, o_ref):
    c = c_ref[...]
    o_ref[0] = _bdot(_silu(c), w_ref[0]) + b_ref[0]


def _ada(cond, w_ada, b_ada):
    depth, _, n = w_ada.shape
    tn = 1536
    return pl.pallas_call(
        _ada_kernel,
        grid=(depth, n // tn),
        in_specs=[pl.BlockSpec((8, D_MODEL), lambda l, j: (0, 0)),
                  pl.BlockSpec((1, D_MODEL, tn), lambda l, j: (l, 0, j)),
                  pl.BlockSpec((1, 1, tn), lambda l, j: (l, 0, j))],
        out_specs=pl.BlockSpec((1, 8, tn), lambda l, j: (l, 0, j)),
        out_shape=jax.ShapeDtypeStruct((depth, 8, n), F32),
        compiler_params=_params(("parallel", "parallel")),
        name="ada_mod",
    )(cond, w_ada, b_ada.reshape(depth, 1, n))


def _proj_kernel(xl_ref, xc_ref, mod_ref, nw_ref, w_ref, o_na, o_gla, o_gdn, o_diff, o_small, *, n_lat_tiles):
    mod = mod_ref[0]
    x = jnp.where(pl.program_id(0) < n_lat_tiles, xl_ref[...], xc_ref[...])
    h = _rms_rows(x, nw_ref[...]) * (1.0 + mod[1:2]) + mod[0:1]
    hb = h.astype(BF16)
    o_na[...] = _dot(hb, w_ref[:, 0:768])
    o_gla[...] = _dot(hb, w_ref[:, 768:1792])
    o_gdn[...] = _dot(hb, w_ref[:, 1792:2816])
    o_diff[...] = _dot(hb, w_ref[:, 2816:3584])
    o_small[...] = _dot(hb, w_ref[:, 3584:3712])


def _stream_specs(xs, tm):
    _, _, ctx_first, n_lat_tiles = xs
    return [pl.BlockSpec((tm, D_MODEL), lambda i: (jnp.minimum(i, n_lat_tiles - 1), 0)),
            pl.BlockSpec((tm, D_MODEL), lambda i: (ctx_first + jnp.maximum(i - n_lat_tiles, 0), 0))]


def _proj(xs, t, mod_l, nw, wcat, tm, mod_row):
    widths = (768, 1024, 1024, 768, SMALL_W)
    return pl.pallas_call(
        functools.partial(_proj_kernel, n_lat_tiles=xs[3]),
        grid=(t // tm,),
        in_specs=_stream_specs(xs, tm) + [
                  pl.BlockSpec((1, 6, D_MODEL), lambda i: (mod_row(i), 0, 0)),
                  pl.BlockSpec((1, D_MODEL), lambda i: (0, 0)),
                  pl.BlockSpec(wcat.shape, lambda i: (0, 0))],
        out_specs=[pl.BlockSpec((tm, w), lambda i: (i, 0)) for w in widths],
        out_shape=[jax.ShapeDtypeStruct((t, w), F32) for w in widths],
        compiler_params=_params(("parallel",)),
        name="in_proj",
    )(xs[0], xs[1], mod_l, nw, wcat)


def _attn_core(q, k, v, n_sub, lam):
    kb = k.astype(BF16)
    vb = v.astype(BF16)
    tq = q.shape[0]
    sub_w = HEAD_DIM // n_sub
    scale = sub_w ** -0.5 * LOG2E
    out = jnp.zeros(q.shape, F32)
    for h0 in range(0, HEADS, ATTN_HEAD_GROUP):
        maps = [(h, m) for h in range(h0, h0 + ATTN_HEAD_GROUP) for m in range(n_sub)]
        qs = jnp.concatenate(
            [(q * jnp.where(_lane_mask(h * HEAD_DIM + m * sub_w, sub_w), scale, 0.0)).astype(BF16)
             for h, m in maps], axis=0)
        s = lax.dot_general(qs, kb, (((1,), (1,)), ((), ())), preferred_element_type=F32)
        e = jnp.exp2(s - jnp.max(s, axis=-1, keepdims=True))
        inv = 1.0 / jnp.sum(e, axis=-1, keepdims=True)
        if n_sub == 1:
            p = e * inv
        else:
            p = jnp.concatenate(
                [e[(2 * i) * tq:(2 * i + 1) * tq] * inv[(2 * i) * tq:(2 * i + 1) * tq]
                 - e[(2 * i + 1) * tq:(2 * i + 2) * tq] * (lam * inv[(2 * i + 1) * tq:(2 * i + 2) * tq])
                 for i in range(ATTN_HEAD_GROUP)], axis=0)
        o_all = _dot(p.astype(BF16), vb)
        for i in range(ATTN_HEAD_GROUP):
            out = jnp.where(_lane_mask((h0 + i) * HEAD_DIM, HEAD_DIM), o_all[i * tq:(i + 1) * tq], out)
    return out


def _diff_lambda(dl, lam_init):
    a = jnp.sum(dl[0:1] * dl[1:2], axis=-1, keepdims=True)
    b = jnp.sum(dl[2:3] * dl[3:4], axis=-1, keepdims=True)
    return jnp.exp(a) - jnp.exp(b) + lam_init


def _attn_kernel(q_ref, k_ref, v_ref, dl_ref, o_ref, *, n_sub, lam_init):
    lam = _diff_lambda(dl_ref[...], lam_init) if n_sub == 2 else None
    o_ref[...] = _attn_core(q_ref[...], k_ref[...], v_ref[...], n_sub, lam)


def _attention(q_arr, q_col, k_arr, k_col, v_arr, v_col, dl, *, n_sub, lam_init, n_seq, lq, lk,
               q_base, k_base, tq, name):
    nq = lq // tq
    return pl.pallas_call(
        functools.partial(_attn_kernel, n_sub=n_sub, lam_init=lam_init),
        grid=(n_seq, nq),
        in_specs=[pl.BlockSpec((tq, MIX_W), lambda s, j: (q_base // tq + s * nq + j, q_col)),
                  pl.BlockSpec((lk, MIX_W), lambda s, j: (k_base // lk + s, k_col)),
                  pl.BlockSpec((lk, MIX_W), lambda s, j: (k_base // lk + s, v_col)),
                  pl.BlockSpec(dl.shape, lambda s, j: (0, 0))],
        out_specs=pl.BlockSpec((tq, MIX_W), lambda s, j: (s * nq + j, 0)),
        out_shape=jax.ShapeDtypeStruct((n_seq * lq, MIX_W), F32),
        compiler_params=_params(("parallel", "parallel")),
        name=name,
    )(q_arr, k_arr, v_arr, dl)


DIFF_KEY_CHUNK = 256
DIFF_HEAD_GROUP = 1
DIFF_Q_TILE = 256


def _lane_fold(x, op):
    out = x[:, 0:128]
    for t in range(1, x.shape[1] // 128):
        out = op(out, x[:, t * 128:(t + 1) * 128])
    return out


def _diff_group(h0, q, kt_ref, v_ref, lam, tkc):
    tq = q.shape[0]
    nck = kt_ref.shape[1] // tkc
    sub_w = HEAD_DIM // 2
    scale = sub_w ** -0.5 * LOG2E
    maps = [(h, m) for h in range(h0, h0 + DIFF_HEAD_GROUP) for m in range(2)]
    qs = jnp.concatenate(
        [(q * jnp.where(_lane_mask(h * HEAD_DIM + m * sub_w, sub_w), scale, 0.0)).astype(BF16) for h, m in maps],
        axis=0)
    s, mvec = [], None
    for c in range(nck):
        sc = _dot(qs, kt_ref[:, c * tkc:(c + 1) * tkc])
        s.append(sc)
        mc = _lane_fold(sc, jnp.maximum)
        mvec = mc if mvec is None else jnp.maximum(mvec, mc)
        yield
    mrun = jnp.max(mvec, axis=-1, keepdims=True)
    e, lvec = [], None
    for c in range(nck):
        ec = jnp.exp2(s[c] - mrun)
        e.append(ec)
        lc = _lane_fold(ec, jnp.add)
        lvec = lc if lvec is None else lvec + lc
        yield
    inv = 1.0 / jnp.sum(lvec, axis=-1, keepdims=True)
    acc = None
    for c in range(nck):
        a = jnp.concatenate(
            [e[c][(2 * i) * tq:(2 * i + 1) * tq] * inv[(2 * i) * tq:(2 * i + 1) * tq]
             - e[c][(2 * i + 1) * tq:(2 * i + 2) * tq] * (lam * inv[(2 * i + 1) * tq:(2 * i + 2) * tq])
             for i in range(DIFF_HEAD_GROUP)], axis=0)
        o = _dot(a.astype(BF16), v_ref[c * tkc:(c + 1) * tkc, :])
        acc = o if acc is None else acc + o
        if c < nck - 1:
            yield
    return acc


_DIFF_PASSES = 3


def _diff_lat_kernel(q_ref, kt_ref, v_ref, dl_ref, o_ref, *, lam_init, tkc):
    lam = _diff_lambda(dl_ref[...], lam_init)
    q = q_ref[...]
    tq = q.shape[0]
    nck = kt_ref.shape[1] // tkc
    n_grp = HEADS // DIFF_HEAD_GROUP
    groups = [_diff_group(g * DIFF_HEAD_GROUP, q, kt_ref, v_ref, lam, tkc) for g in range(n_grp)]
    res = [None] * n_grp
    for t in range(n_grp + _DIFF_PASSES - 1):
        active = [g for g in range(n_grp) if 0 <= t - g < _DIFF_PASSES]
        for _ in range(nck):
            for g in active:
                try:
                    next(groups[g])
                except StopIteration as done:
                    res[g] = done.value
    out = jnp.zeros(q.shape, F32)
    for h in range(HEADS):
        g, i = divmod(h, DIFF_HEAD_GROUP)
        out = jnp.where(_lane_mask(h * HEAD_DIM, HEAD_DIM), res[g][i * tq:(i + 1) * tq], out)
    o_ref[...] = out


def _diff_latent(q_r, kk_t, vv16, dl, lam_init, n_b, n_tok, lk, tq):
    tkc = DIFF_KEY_CHUNK if lk % DIFF_KEY_CHUNK == 0 else 128
    assert lk % tkc == 0
    nq = n_tok // tq
    return pl.pallas_call(
        functools.partial(_diff_lat_kernel, lam_init=lam_init, tkc=tkc),
        grid=(n_b, nq),
        in_specs=[pl.BlockSpec((tq, MIX_W), lambda b, j: (b * nq + j, 0)),
                  pl.BlockSpec((MIX_W, lk), lambda b, j: (b, 0)),
                  pl.BlockSpec((lk, MIX_W), lambda b, j: (b, 0)),
                  pl.BlockSpec(dl.shape, lambda b, j: (0, 0))],
        out_specs=pl.BlockSpec((tq, MIX_W), lambda b, j: (b * nq + j, 0)),
        out_shape=jax.ShapeDtypeStruct((n_b * n_tok, MIX_W), F32),
        compiler_params=_params(("parallel", "arbitrary")),
        name="diff_latent",
    )(q_r, kk_t, vv16, dl)


def _na_lat_kernel(q_ref, k_ref, v_ref, ck_ref, cv_ref, bias_ref, o_ref, *, rows, kh, rps):
    g = pl.program_id(1)
    ck = ck_ref[...].astype(BF16)
    cv = cv_ref[...].astype(BF16)
    scale = HEAD_DIM ** -0.5 * LOG2E
    head_scale = [jnp.where(_lane_mask(h * HEAD_DIM, HEAD_DIM), scale, 0.0) for h in range(HEADS)]
    hw = HEADS * GRID_W
    qs = []
    for j in range(rps):
        qj = q_ref[j * GRID_W:(j + 1) * GRID_W, :]
        qs.append(jnp.concatenate([(qj * hs).astype(BF16) for hs in head_scale], axis=0))
    s_ctx_all = lax.dot_general(jnp.concatenate(qs, axis=0), ck, (((1,), (1,)), ((), ())),
                                preferred_element_type=F32)
    p_ctx, o_loc = [], []
    for j in range(rps):
        r = g * rps + j
        start = jnp.clip(r - kh // 2, 0, rows - kh)
        cls = r - start
        ws = pl.multiple_of(start * GRID_W, GRID_W)
        kw = k_ref[pl.ds(ws, kh * GRID_W), :].astype(BF16)
        vw = v_ref[pl.ds(ws, kh * GRID_W), :].astype(BF16)
        bias = jnp.concatenate([bias_ref[h, pl.ds(cls, 1)][0] for h in range(HEADS)], axis=0)
        s_loc = lax.dot_general(qs[j], kw, (((1,), (1,)), ((), ())), preferred_element_type=F32) + bias
        s_ctx = s_ctx_all[j * hw:(j + 1) * hw]
        mx = jnp.maximum(jnp.max(s_loc, axis=-1, keepdims=True), jnp.max(s_ctx, axis=-1, keepdims=True))
        e_loc = jnp.exp2(s_loc - mx)
        e_ctx = jnp.exp2(s_ctx - mx)
        inv = 1.0 / (jnp.sum(e_loc, axis=-1, keepdims=True) + jnp.sum(e_ctx, axis=-1, keepdims=True))
        p_ctx.append((e_ctx * inv).astype(BF16))
        o_loc.append(_dot((e_loc * inv).astype(BF16), vw))
    o_ctx_all = _dot(jnp.concatenate(p_ctx, axis=0), cv)
    for j in range(rps):
        o_all = o_loc[j] + o_ctx_all[j * hw:(j + 1) * hw]
        out = jnp.zeros((GRID_W, MIX_W), F32)
        for h in range(HEADS):
            out = jnp.where(_lane_mask(h * HEAD_DIM, HEAD_DIM), o_all[h * GRID_W:(h + 1) * GRID_W], out)
        o_ref[j * GRID_W:(j + 1) * GRID_W, :] = out


def _na_bias_table(rpb, kh):
    cidx = np.arange(GRID_W)
    cls = np.arange(kh)
    drow = np.arange(kh)[None, :] - cls[:, None] + NA_WIN_H - 1
    col_start = np.clip(cidx - NA_WIN_W // 2, 0, GRID_W - NA_WIN_W)
    col_ok = (cidx[None, :] >= col_start[:, None]) & (cidx[None, :] < col_start[:, None] + NA_WIN_W)
    dcol = np.clip(cidx[None, :] - cidx[:, None], 1 - NA_WIN_W, NA_WIN_W - 1) + NA_WIN_W - 1
    onehot = (dcol.reshape(1, -1) == np.arange(2 * NA_WIN_W - 1)[:, None]).astype(np.float32)
    toep = jnp.dot(rpb.astype(F32).reshape(-1, 2 * NA_WIN_W - 1), onehot, precision=lax.Precision.HIGHEST)
    toep = toep.reshape(-1, 2 * NA_WIN_H - 1, GRID_W, GRID_W)
    bias = jnp.stack([toep[:, drow[c, 0]:drow[c, 0] + kh] for c in range(kh)], axis=1)
    bias = bias.transpose(0, 1, 3, 2, 4)
    bias = jnp.where(col_ok[:, None, :], bias * LOG2E, NEG)
    return bias.reshape(-1, kh, GRID_W, kh * GRID_W)


def _na_latent(z_na, ck, cv, bias, n_b, n_tok, p_len):
    rows = n_tok // GRID_W
    kh = min(NA_WIN_H, rows)
    rps = NA_ROWS_PER_STEP
    assert rows % rps == 0
    steps = rows // rps
    return pl.pallas_call(
        functools.partial(_na_lat_kernel, rows=rows, kh=kh, rps=rps),
        grid=(n_b, steps),
        in_specs=[pl.BlockSpec((rps * GRID_W, MIX_W), lambda b, r: (b * steps + r, 0)),
                  pl.BlockSpec((n_tok, MIX_W), lambda b, r: (b, 1)),
                  pl.BlockSpec((n_tok, MIX_W), lambda b, r: (b, 2)),
                  pl.BlockSpec((p_len, MIX_W), lambda b, r: (b, 0)),
                  pl.BlockSpec((p_len, MIX_W), lambda b, r: (b, 0)),
                  pl.BlockSpec(bias.shape, lambda b, r: (0, 0, 0, 0))],
        out_specs=pl.BlockSpec((rps * GRID_W, MIX_W), lambda b, r: (b * steps + r, 0)),
        out_shape=jax.ShapeDtypeStruct((n_b * n_tok, MIX_W), F32),
        compiler_params=_params(("parallel", "arbitrary")),
        name="na_latent",
    )(z_na, z_na, z_na, ck, cv, bias)


def _rope_kernel(q_ref, k_ref, cos_ref, sin_ref, p_ref, qo_ref, ko_ref):
    cos = cos_ref[...]
    sin = sin_ref[...]
    p = p_ref[...]
    q = q_ref[...]
    k = k_ref[...]
    qo_ref[...] = q * cos + _dot_rx(q, p) * sin
    ko_ref[...] = (k * cos + _dot_rx(k, p) * sin).T.astype(BF16)


def _rope(z_diff, cos, sin, perm, n_b, n_tok, tm):
    nt = n_tok // tm
    spec_t = pl.BlockSpec((tm, MIX_W), lambda i: (i % nt, 0))
    return pl.pallas_call(
        _rope_kernel,
        grid=(n_b * nt,),
        in_specs=[pl.BlockSpec((tm, MIX_W), lambda i: (i, 0)),
                  pl.BlockSpec((tm, MIX_W), lambda i: (i, 1)),
                  spec_t, spec_t,
                  pl.BlockSpec(perm.shape, lambda i: (0, 0))],
        out_specs=[pl.BlockSpec((tm, MIX_W), lambda i: (i, 0)),
                   pl.BlockSpec((MIX_W, tm), lambda i: (i // nt, i % nt))],
        out_shape=[jax.ShapeDtypeStruct((n_b * n_tok, MIX_W), F32),
                   jax.ShapeDtypeStruct((n_b * MIX_W, n_tok), BF16)],
        compiler_params=_params(("parallel",)),
        name="diff_rope",
    )(z_diff, z_diff, cos, sin, perm)


def _gdn_pre_kernel(x_ref, prev_ref, next_ref, edge_ref, w_ref, bm_ref, o_ref, *, tm):
    x = x_ref[...]
    w = w_ref[...]
    edge = edge_ref[...]
    row = lax.broadcasted_iota(jnp.int32, (tm, 1), 0)
    x_prev = jnp.where(row == 0, prev_ref[7:8, :], pltpu.roll(x, 1, 0)) * edge[:, 0:1]
    x_next = jnp.where(row == tm - 1, next_ref[0:1, :], pltpu.roll(x, tm - 1, 0)) * edge[:, 1:2]
    y = _silu(x_prev * w[0:1] + x * w[1:2] + x_next * w[2:3])
    bm16 = bm_ref[...]
    q = y[:, 0:MIX_W]
    k = y[:, MIX_W:2 * MIX_W]
    o_ref[:, 0:MIX_W] = q * lax.rsqrt(_head_sum(q * q, bm16) + EPS) * (HEAD_DIM ** -0.5)
    o_ref[:, MIX_W:2 * MIX_W] = k * lax.rsqrt(_head_sum(k * k, bm16) + EPS)
    o_ref[:, 2 * MIX_W:3 * MIX_W] = y[:, 2 * MIX_W:3 * MIX_W]


def _seq_edges(lat_tokens, n_tok, ctx_tokens, l_ctx):
    pos = np.concatenate([np.arange(lat_tokens) % n_tok, np.arange(ctx_tokens) % l_ctx])
    seq = np.concatenate([np.full(lat_tokens, n_tok), np.full(ctx_tokens, l_ctx)])
    edge = np.zeros((lat_tokens + ctx_tokens, 8), np.float32)
    edge[:, 0] = pos != 0
    edge[:, 1] = pos != seq - 1
    return jnp.asarray(edge)


def _gdn_pre(z_gdn, edge, conv_w, bm16, tm):
    t = z_gdn.shape[0]
    w3 = 3 * MIX_W
    nb8 = t // 8
    return pl.pallas_call(
        functools.partial(_gdn_pre_kernel, tm=tm),
        grid=(t // tm,),
        in_specs=[pl.BlockSpec((tm, w3), lambda i: (i, 0)),
                  pl.BlockSpec((8, w3), lambda i: (jnp.maximum(i * (tm // 8) - 1, 0), 0)),
                  pl.BlockSpec((8, w3), lambda i: (jnp.minimum((i + 1) * (tm // 8), nb8 - 1), 0)),
                  pl.BlockSpec((tm, 8), lambda i: (i, 0)),
                  pl.BlockSpec(conv_w.shape, lambda i: (0, 0)),
                  pl.BlockSpec(bm16.shape, lambda i: (0, 0))],
        out_specs=pl.BlockSpec((tm, w3), lambda i: (i, 0)),
        out_shape=jax.ShapeDtypeStruct((t, w3), F32),
        compiler_params=_params(("parallel",)),
        name="gdn_pre",
    )(z_gdn, z_gdn, z_gdn, edge, conv_w, bm16)


def _bd_rows16(x, bm16):
    return jnp.concatenate([x.astype(BF16)] * HEADS, axis=0) * bm16


def _gla_chain(q, k, v, zs, gup, gb, lsm, pm_ref, d, bm, bm16, st):
    nblk = CHUNK // LEAF
    q = q * (HEAD_DIM ** -0.5)
    gp = _bdot(zs, gup) + gb
    g = (jnp.minimum(gp, 0.0) - jnp.log(1.0 + jnp.exp(-jnp.abs(gp)))) * (LOG2E / GATE_NORM)
    yield
    e = _dot_lx(lsm, g)
    b = e[0:CHUNK]
    fb = jnp.exp2(b)
    fbl = jnp.exp2(e[CHUNK:2 * CHUNK])
    ftot = jnp.exp2(e[2 * CHUNK:2 * CHUNK + 1])
    yield
    edge = LEAF - 1 if d == 0 else 0
    blocks = range(nblk - 1) if d == 0 else range(1, nblk)
    b3 = b.reshape(nblk, LEAF, MIX_W)
    b_edge = jnp.broadcast_to(b3[:, edge:edge + 1, :], b3.shape).reshape(CHUNK, MIX_W)
    kf = jnp.exp2(jnp.minimum(b_edge - b, 0.0))
    q_far = jnp.concatenate(
        [(q * jnp.exp2(jnp.minimum(b - b[blk * LEAF + edge:blk * LEAF + edge + 1, :], 0.0))).astype(BF16)
         for blk in blocks], axis=0)
    s_far = lax.dot_general(q_far, _bd_rows16(k * kf, bm16), (((1,), (1,)), ((), ())),
                            preferred_element_type=F32)
    yield
    q_near = jnp.concatenate(
        [(q * jnp.exp2(jnp.minimum(
            b - jnp.broadcast_to(b3[:, r:r + 1, :], b3.shape).reshape(CHUNK, MIX_W), 0.0))).astype(BF16)
         for r in range(LEAF)], axis=0)
    s_near = lax.dot_general(q_near, _bd_rows16(k, bm16), (((1,), (1,)), ((), ())),
                             preferred_element_type=F32)
    yield
    att = None
    for n, blk in enumerate(blocks):
        term = s_far[n * CHUNK:(n + 1) * CHUNK] * pm_ref[d, blk]
        att = term if att is None else att + term
    for r in range(LEAF):
        att = att + s_near[r * CHUNK:(r + 1) * CHUNK] * pm_ref[d, nblk + r]
    yield
    o = _bdot_nt(q * fb, st) + _dot(att.astype(BF16), _bd_rows16(v, bm16))
    yield
    return o, st * ftot + _bdot_tn(v, k * fbl) * bm


def _row_pieces(s, d):
    out = []
    for h in range(HEADS):
        for blk in range(CHUNK // s):
            lo = h * CHUNK + blk * s
            out.append((lo, lo + s, blk % 2 == 1 - d))
    return out


def _tri_inverse(mbd, sm_ref, d):
    r = lax.broadcasted_iota(jnp.int32, (MIX_W, MIX_W), 0)
    c = lax.broadcasted_iota(jnp.int32, (MIX_W, MIX_W), 1)
    eye = jnp.where(r == c, 1.0, 0.0)
    md = mbd * sm_ref[0]
    m2 = _bdot(md, md)
    yield
    m4 = _bdot(m2, m2)
    t = eye - md
    t = t + _bdot(t, m2)
    yield
    t = t + _bdot(t, m4)
    s = LEAF
    for lvl in range(1, sm_ref.shape[0]):
        pieces = _row_pieces(s, d)
        take = lambda x: jnp.concatenate([x[lo:hi] for lo, hi, upd in pieces if upd], axis=0)
        yield
        t16 = t.astype(BF16)
        a = _dot(take(mbd * sm_ref[lvl]).astype(BF16), t16).astype(BF16)
        rows, n = [], 0
        for lo, hi, upd in pieces:
            rows.append(a[n:n + s] if upd else jnp.zeros((s, MIX_W), BF16))
            n += s if upd else 0
        yield
        b = _dot(take(t16), jnp.concatenate(rows, axis=0))
        rows, n = [], 0
        for lo, hi, upd in pieces:
            rows.append(t[lo:hi] - b[n:n + s] if upd else t[lo:hi])
            n += s if upd else 0
        t = jnp.concatenate(rows, axis=0)
        s *= 2
    return t


def _gdn_chain(q, k, v, zs, ea, eb, alog, dtb, lsm, lbt, pm_ref, d, bm, bm16, sm_ref, s):
    g = -jnp.exp(alog) * _softplus(_dot_rx(zs, ea) + dtb)
    beta = jax.nn.sigmoid(_dot_rx(zs, eb))
    yield
    e = _dot_lx(lsm, g)
    dcol = e[0:CHUNK]
    drow = _dot_lx(jnp.ones((8, CHUNK), BF16), g * lbt)[0:1]
    kb = k * beta
    s2 = lax.dot_general(jnp.concatenate([kb, q], axis=0).astype(BF16), _bd_rows16(k, bm16),
                         (((1,), (1,)), ((), ())), preferred_element_type=F32)
    yield
    dec = jnp.exp(jnp.where(pm_ref[d, 1] > 0.0, dcol - drow, NEG))
    m_cat = s2[0:CHUNK] * dec * pm_ref[d, 0]
    att = s2[CHUNK:2 * CHUNK] * dec
    t_inv = yield from _tri_inverse(_bd_rows(m_cat, bm), sm_ref, d)
    t16 = (t_inv[0:CHUNK] + t_inv[CHUNK:2 * CHUNK] + t_inv[2 * CHUNK:3 * CHUNK] + t_inv[3 * CHUNK:]).astype(BF16)
    fb = jnp.exp(dcol)
    fbl = jnp.exp(e[CHUNK:2 * CHUNK])
    ftot = jnp.exp(e[2 * CHUNK:2 * CHUNK + 1])
    yield
    u = _dot(t16, _bd_rows16(v * beta, bm16))
    w = _dot(t16, _bd_rows16(kb * fb, bm16))
    yield
    ws = _dot(jnp.concatenate([w.astype(BF16), (q * fb).astype(BF16)], axis=0), s.astype(BF16))
    v_new = u - ws[0:CHUNK]
    yield
    o = ws[CHUNK:2 * CHUNK] + _dot(att.astype(BF16), _bd_rows16(v_new, bm16))
    return o, s * ftot + lax.dot_general((k * fbl).astype(BF16), v_new.astype(BF16), (((0,), (0,)), ((), ())),
                                         preferred_element_type=F32) * bm


def _scan_kernel(*refs, chain, n_const, nb, n_chunks):
    n_tok = nb * 2 * 4
    tok = refs[:n_tok]
    consts = refs[n_tok:n_tok + n_const]
    s0_ref, of_ref, ob_ref, sf_ref, s_scr = refs[n_tok + n_const:]
    n = pl.program_id(1)

    @pl.when(n == 0)
    def _():
        s_scr[...] = s0_ref[...]

    ids = [(j, d) for j in range(nb) for d in range(2)]
    gens = []
    for j, d in ids:
        q_ref, k_ref, v_ref, zs_ref = tok[(j * 2 + d) * 4:(j * 2 + d) * 4 + 4]
        gens.append(chain(q_ref[...], k_ref[...], v_ref[...], zs_ref[...], consts, d, s_scr[j, d]))
    live = list(range(len(gens)))
    while live:
        for c in list(live):
            try:
                next(gens[c])
            except StopIteration as done:
                j, d = ids[c]
                o, s_new = done.value
                (of_ref, ob_ref)[d][j] = o
                s_scr[j, d] = s_new
                live.remove(c)

    @pl.when(n == n_chunks - 1)
    def _():
        sf_ref[...] = s_scr[...]


def _gla_step(q, k, v, zs, consts, d, st):
    gup_ref, gb_ref, lsm_ref, pm_ref, bm_ref, bm16_ref = consts
    return _gla_chain(q, k, v, zs, gup_ref[d], gb_ref[d], lsm_ref[d], pm_ref, d, bm_ref[...], bm16_ref[...], st)


def _gdn_step(q, k, v, zs, consts, d, s):
    ea_ref, eb_ref, alog_ref, dtb_ref, lsm_ref, lbt_ref, pm_ref, bm_ref, bm16_ref, sm_ref = consts
    return _gdn_chain(q, k, v, zs, ea_ref[d], eb_ref[d], alog_ref[d], dtb_ref[d], lsm_ref[d], lbt_ref[d],
                      pm_ref, d, bm_ref[...], bm16_ref[...], sm_ref, s)


def _scan(chain, qkv, z_small, consts, s0, n_seq, l_seq, base, seqs_per_step, name):
    nc = l_seq // CHUNK
    nb = seqs_per_step if n_seq % seqs_per_step == 0 else 1

    def chunk(n, d):
        return n + d * (nc - 1 - 2 * n)

    in_specs, args = [], []
    for j in range(nb):
        for d in range(2):
            for arr, col, width in ((qkv, 0, MIX_W), (qkv, 1, MIX_W), (qkv, 2, MIX_W), (z_small, 0, SMALL_W)):
                in_specs.append(pl.BlockSpec(
                    (CHUNK, width),
                    functools.partial(lambda g, n, j, d, col: (base // CHUNK + (g * nb + j) * nc + chunk(n, d), col),
                                      j=j, d=d, col=col)))
                args.append(arr)
    for cst in consts:
        in_specs.append(pl.BlockSpec(cst.shape, functools.partial(lambda g, n, nd: (0,) * nd, nd=cst.ndim)))
        args.append(cst)
    state_spec = pl.BlockSpec((nb, 2, MIX_W, MIX_W), lambda g, n: (g, 0, 0, 0))
    in_specs.append(state_spec)
    args.append(s0)
    o_f, o_b, s_f = pl.pallas_call(
        functools.partial(_scan_kernel, chain=chain, n_const=len(consts), nb=nb, n_chunks=nc),
        grid=(n_seq // nb, nc),
        in_specs=in_specs,
        out_specs=[pl.BlockSpec((nb, CHUNK, MIX_W), lambda g, n: (g, chunk(n, 0), 0)),
                   pl.BlockSpec((nb, CHUNK, MIX_W), lambda g, n: (g, chunk(n, 1), 0)),
                   state_spec],
        out_shape=[jax.ShapeDtypeStruct((n_seq, l_seq, MIX_W), F32),
                   jax.ShapeDtypeStruct((n_seq, l_seq, MIX_W), F32),
                   jax.ShapeDtypeStruct((n_seq, 2, MIX_W, MIX_W), F32)],
        scratch_shapes=[pltpu.VMEM((nb, 2, MIX_W, MIX_W), F32)],
        compiler_params=_params(("parallel", "arbitrary")),
        name=name,
    )(*args)
    return o_f.reshape(n_seq * l_seq, MIX_W), o_b.reshape(n_seq * l_seq, MIX_W), s_f


def _merge_kernel(xl_ref, xc_ref, mod_ref, n1_ref, na_l, na_c, glaf_l, glaf_c, glab_l, glab_c, glag_ref, gdnf_l, gdnf_c,
                  gdnb_l, gdnb_c, gdng_ref, df_l, df_c, hn_ref, bm_ref, wg_ref, wb_ref, wo_ref, o_ref, *,
                  lam_init, n_lat_tiles):
    is_lat = pl.program_id(0) < n_lat_tiles

    def pick(lat_ref, ctx_ref):
        return jnp.where(is_lat, lat_ref[...], ctx_ref[...])

    mod = mod_ref[0]
    x = pick(xl_ref, xc_ref)
    h = _rms_rows(x, n1_ref[...]) * (1.0 + mod[1:2]) + mod[0:1]
    hb = h.astype(BF16)
    bm16 = bm_ref[...]

    def head_norm(o, w):
        return o * lax.rsqrt(_head_sum(o * o, bm16) * (1.0 / HEAD_DIM) + EPS) * w

    branches = (
        pick(na_l, na_c),
        head_norm(pick(glaf_l, glaf_c) + pick(glab_l, glab_c), hn_ref[0]) * _silu(glag_ref[...]),
        head_norm(pick(gdnf_l, gdnf_c) + pick(gdnb_l, gdnb_c), hn_ref[1]) * _silu(gdng_ref[...]),
        head_norm(pick(df_l, df_c), hn_ref[2]) * (1.0 - lam_init),
    )
    merged = None
    for n, br in enumerate(branches):
        gate = jax.nn.sigmoid(_dot(hb, wg_ref[:, n * D_MODEL:(n + 1) * D_MODEL]))
        term = gate * _bdot(br, wb_ref[n])
        merged = term if merged is None else merged + term
    o_ref[...] = x + mod[2:3] * _bdot(merged, wo_ref[...])


def _merge(xs, t, mod_l, n1, na, gla_f, gla_b, z_gla, gdn_f, gdn_b, z_gdn, df, hn, bm16, wg, wb, wo, tm, mod_row,
           lam_init, n_lat_tiles):
    tok = lambda col: pl.BlockSpec((tm, MIX_W), lambda i: (i, col))
    lat = pl.BlockSpec((tm, MIX_W), lambda i: (jnp.minimum(i, n_lat_tiles - 1), 0))
    ctx = pl.BlockSpec((tm, MIX_W), lambda i: (jnp.maximum(i - n_lat_tiles, 0), 0))
    full = lambda a: pl.BlockSpec(a.shape, lambda i: (0,) * a.ndim)
    return pl.pallas_call(
        functools.partial(_merge_kernel, lam_init=lam_init, n_lat_tiles=n_lat_tiles),
        grid=(t // tm,),
        in_specs=_stream_specs(xs, tm) + [
                  pl.BlockSpec((1, 6, D_MODEL), lambda i: (mod_row(i), 0, 0)),
                  full(n1), lat, ctx, lat, ctx, lat, ctx, tok(3), lat, ctx, lat, ctx, tok(3), lat, ctx,
                  full(hn), full(bm16), full(wg), full(wb), full(wo)],
        out_specs=pl.BlockSpec((tm, D_MODEL), lambda i: (i, 0)),
        out_shape=jax.ShapeDtypeStruct((t, D_MODEL), F32),
        compiler_params=_params(("parallel",)),
        name="merge_out",
    )(xs[0], xs[1], mod_l, n1, *na, *gla_f, *gla_b, z_gla, *gdn_f, *gdn_b, z_gdn, *df, hn, bm16, wg, wb, wo)


def _route(h2, wr_ref, br_ref):
    s_t = jax.nn.sigmoid(_dot3(h2, wr_ref[...])).T
    s = [s_t[e:e + 1] for e in range(N_EXPERTS)]
    sel = [s[e] + br_ref[e] for e in range(N_EXPERTS)]
    best = None
    for g in range(N_GROUPS):
        v = sel[g * GROUP_E:(g + 1) * GROUP_E]
        score = None
        for i in range(GROUP_E):
            for j in range(i + 1, GROUP_E):
                score = v[i] + v[j] if score is None else jnp.maximum(score, v[i] + v[j])
        if best is None:
            best, gi = score, jnp.zeros_like(score)
        else:
            better = score > best
            gi = jnp.where(better, float(g), gi)
            best = jnp.where(better, score, best)
    vals = [jnp.where(gi == float(e // GROUP_E), sel[e], NEG) for e in range(N_EXPERTS)]

    def first_max(v):
        mx, idx = v[0], jnp.zeros_like(v[0])
        for e in range(1, N_EXPERTS):
            better = v[e] > mx
            idx = jnp.where(better, float(e), idx)
            mx = jnp.where(better, v[e], mx)
        return idx

    i1 = first_max(vals)
    i2 = first_max([jnp.where(i1 == float(e), NEG, vals[e]) for e in range(N_EXPERTS)])
    s1 = sum(jnp.where(i1 == float(e), s[e], 0.0) for e in range(N_EXPERTS))
    s2 = sum(jnp.where(i2 == float(e), s[e], 0.0) for e in range(N_EXPERTS))
    inv = 1.0 / (s1 + s2)
    rows = [jnp.where(i1 == float(e), s1 * inv, 0.0) + jnp.where(i2 == float(e), s2 * inv, 0.0)
            for e in range(N_EXPERTS)]
    gates_t = jnp.concatenate(rows + [jnp.zeros((SMALL_W - N_EXPERTS, h2.shape[0]), F32)], axis=0)
    return gates_t.T


def _moe_kernel(x_ref, mod_ref, n2_ref, wr_ref, br_ref, w1_ref, w3_ref, w2_ref, *rest, n_lat_tiles):
    if n_lat_tiles is None:
        o_ref, h_scr, g_scr, acc_scr = rest
    else:
        fw_ref, ol_ref, oc_ref, h_scr, g_scr, acc_scr = rest
    g = pl.program_id(1)
    mod = mod_ref[0]

    @pl.when(g == 0)
    def _():
        h2 = _rms_rows(x_ref[...], n2_ref[...]) * (1.0 + mod[4:5]) + mod[3:4]
        h_scr[...] = h2.astype(BF16)
        g_scr[...] = _route(h2, wr_ref, br_ref)
        acc_scr[...] = jnp.zeros_like(acc_scr)

    hb = h_scr[...]
    gates = g_scr[...]
    lane = lax.broadcasted_iota(jnp.int32, (1, SMALL_W), 1)
    he = jnp.concatenate(
        [_silu(_dot(hb, w1_ref[e])) * _dot(hb, w3_ref[e])
         * jnp.sum(jnp.where(lane == g * GROUP_E + e, gates, 0.0), axis=-1, keepdims=True)
         for e in range(GROUP_E)], axis=1)
    acc_scr[...] += _dot(he.astype(BF16), w2_ref[...])

    if n_lat_tiles is None:
        @pl.when(g == N_GROUPS - 1)
        def _():
            o_ref[...] = x_ref[...] + mod[5:6] * acc_scr[...]
    else:
        is_lat = pl.program_id(0) < n_lat_tiles

        @pl.when((g == N_GROUPS - 1) & is_lat)
        def _():
            ol_ref[...] = _rms_rows(x_ref[...] + mod[5:6] * acc_scr[...], fw_ref[...])

        @pl.when((g == N_GROUPS - 1) & jnp.logical_not(is_lat))
        def _():
            oc_ref[...] = _rms_rows(x_ref[...] + mod[5:6] * acc_scr[...], fw_ref[...])


def _moe(x, mod_l, n2, wr, br, w1, w3, w2, tm, mod_row, final_w=None, n_lat_tiles=None):
    t = x.shape[0]
    gw = GROUP_E * D_FF
    in_specs = [pl.BlockSpec((tm, D_MODEL), lambda i, g: (i, 0)),
                pl.BlockSpec((1, 6, D_MODEL), lambda i, g: (mod_row(i), 0, 0)),
                pl.BlockSpec(n2.shape, lambda i, g: (0, 0)),
                pl.BlockSpec(wr.shape, lambda i, g: (0, 0)),
                pl.BlockSpec(memory_space=pltpu.SMEM),
                pl.BlockSpec((GROUP_E, D_MODEL, D_FF), lambda i, g: (g, 0, 0)),
                pl.BlockSpec((GROUP_E, D_MODEL, D_FF), lambda i, g: (g, 0, 0)),
                pl.BlockSpec((gw, D_MODEL), lambda i, g: (g, 0))]
    args = [x, mod_l, n2, wr, br, w1, w3, w2]
    if final_w is None:
        out_specs = pl.BlockSpec((tm, D_MODEL), lambda i, g: (i, 0))
        out_shape = jax.ShapeDtypeStruct((t, D_MODEL), F32)
    else:
        in_specs.append(pl.BlockSpec(final_w.shape, lambda i, g: (0, 0)))
        args.append(final_w)
        out_specs = [pl.BlockSpec((tm, D_MODEL), lambda i, g: (jnp.minimum(i, n_lat_tiles - 1), 0)),
                     pl.BlockSpec((tm, D_MODEL), lambda i, g: (jnp.maximum(i - n_lat_tiles, 0), 0))]
        out_shape = [jax.ShapeDtypeStruct((n_lat_tiles * tm, D_MODEL), F32),
                     jax.ShapeDtypeStruct((t - n_lat_tiles * tm, D_MODEL), F32)]
    return pl.pallas_call(
        functools.partial(_moe_kernel, n_lat_tiles=n_lat_tiles),
        grid=(t // tm, N_GROUPS),
        in_specs=in_specs,
        out_specs=out_specs,
        out_shape=out_shape,
        scratch_shapes=[pltpu.VMEM((tm, D_MODEL), BF16), pltpu.VMEM((tm, SMALL_W), F32),
                        pltpu.VMEM((tm, D_MODEL), F32)],
        compiler_params=_params(("arbitrary", "arbitrary")),
        name="moe",
    )(*args)


def _block_diag_state(s, transpose):
    if transpose:
        s = jnp.swapaxes(s, -1, -2)
    eye = jnp.eye(HEADS, dtype=s.dtype)
    out = jnp.einsum('...hij,hg->...higj', s, eye)
    return out.reshape(s.shape[:-3] + (MIX_W, MIX_W))


def _diag_blocks(s_bd, transpose):
    s = s_bd.reshape(s_bd.shape[:-2] + (HEADS, HEAD_DIM, HEADS, HEAD_DIM))
    s = jnp.stack([s[..., h, :, h, :] for h in range(HEADS)], axis=-3)
    return jnp.swapaxes(s, -1, -2) if transpose else s


def _pick_tile(*lengths):
    for tm in (512, 256, 128, 64):
        if all(n % tm == 0 for n in lengths):
            return tm
    raise ValueError("token counts must be multiples of 64")


def kernel(x_prompt, x_sample, cache_na_k, cache_na_v, cache_diff_k, cache_diff_v, state_gla, state_gdn, c, c_ctx, w_ada, b_ada, norm1_w, norm2_w, w_in, na_rpb, gla_gate_up, gla_gate_bias, gla_norm_w, gdn_conv_w, gdn_A_log, gdn_dt_bias, gdn_norm_w, diff_lambda, diff_subln_w, w_branch, w_out, w_router, b_router, w_e1, w_e3, w_e2, final_norm_w):
    n_ctx, l_ctx, _ = x_prompt.shape
    n_lat, n_tok, _ = x_sample.shape
    depth = w_ada.shape[0]
    p_len = cache_na_k.shape[2]
    lat_tokens = n_lat * n_tok
    ctx_tokens = n_ctx * l_ctx
    assert n_lat + 1 <= 8 and n_tok % l_ctx == 0 and n_tok % GRID_W == 0 and l_ctx % CHUNK == 0
    tm = _pick_tile(n_tok, ctx_tokens)

    def mod_row(i):
        return jnp.where(i * tm < lat_tokens, (i * tm) // n_tok, n_lat)

    tm_moe = MOE_TILE if (n_tok % MOE_TILE == 0 and ctx_tokens % MOE_TILE == 0) else tm

    def mod_row_moe(i):
        return jnp.where(i * tm_moe < lat_tokens, (i * tm_moe) // n_tok, n_lat)

    tokens = lat_tokens + ctx_tokens
    xs = (x_sample.reshape(lat_tokens, D_MODEL), x_prompt.reshape(ctx_tokens, D_MODEL), 0, lat_tokens // tm)
    cond = jnp.zeros((8, D_MODEL), F32).at[:n_lat].set(c).at[n_lat].set(c_ctx)
    mod = _ada(cond, w_ada, b_ada).reshape(depth, 8, 6, D_MODEL)

    gla_pm = _gla_pair_masks()
    lsm, lbt, pm = _gdn_consts()
    bm = jnp.asarray(_block_mask(), F32)
    bm16 = bm.astype(BF16)
    sm = _solve_masks()
    cos, sin, perm = _rope_consts(n_tok)
    edge = _seq_edges(lat_tokens, n_tok, ctx_tokens, l_ctx)
    lane = np.arange(MIX_W)
    small_rows = np.arange(SMALL_W)
    def expand_cols(first_col):
        return jnp.asarray((small_rows[:, None] == first_col + lane[None, :] // HEAD_DIM).astype(np.float32), BF16)
    ea = jnp.stack([expand_cols(32), expand_cols(36)])
    eb = jnp.stack([expand_cols(40), expand_cols(44)])
    wr = jnp.zeros((D_MODEL, SMALL_W), F32).at[:, :N_EXPERTS].set(w_router)
    br = b_router.astype(F32)

    def cols(*names):
        return [w_in[:, :, _OFF[n][0]:_OFF[n][1]] for n in names]

    wcat = jnp.concatenate(
        cols('na_q', 'na_k', 'na_v', 'gla_q', 'gla_k', 'gla_v', 'gla_og', 'gdn_qkv', 'gdn_og',
             'diff_q', 'diff_k', 'diff_v', 'gla_gf', 'gla_gb', 'gdn_af', 'gdn_ab', 'gdn_bf', 'gdn_bb')
        + [jnp.zeros((depth, D_MODEL, SMALL_W - 2 * GATE_RANK - 4 * HEADS), F32)], axis=2).astype(BF16)
    w_gate = w_in[:, :, _OFF['branch_gate'][0]:].astype(BF16)
    w_branch16 = w_branch.astype(BF16)
    w_out16 = w_out.astype(BF16)
    w1 = w_e1.astype(BF16)
    w3 = w_e3.astype(BF16)
    w2 = w_e2.reshape(depth, N_EXPERTS * D_FF, D_MODEL).astype(BF16)
    gup = jnp.zeros((depth, 2, SMALL_W, MIX_W), F32)
    gup = gup.at[:, 0, 0:GATE_RANK].set(gla_gate_up[:, 0]).at[:, 1, GATE_RANK:2 * GATE_RANK].set(gla_gate_up[:, 1])
    gbias = gla_gate_bias[:, :, None, :]
    alog = jnp.repeat(gdn_A_log, HEAD_DIM, axis=-1)[:, :, None, :]
    dtb = jnp.repeat(gdn_dt_bias, HEAD_DIM, axis=-1)[:, :, None, :]
    hn = jnp.stack([jnp.tile(gla_norm_w, (1, HEADS)), jnp.tile(gdn_norm_w, (1, HEADS)),
                    jnp.tile(diff_subln_w, (1, HEADS))], axis=1)[:, :, None, :]
    kh = min(NA_WIN_H, n_tok // GRID_W)
    na_bias = _na_bias_table(na_rpb.reshape(depth * HEADS, 2 * NA_WIN_H - 1, 2 * NA_WIN_W - 1), kh)
    na_bias = na_bias.reshape(depth, HEADS, kh, GRID_W, kh * GRID_W)

    def cache_rows(cache):
        return cache.transpose(1, 0, 2, 3, 4).reshape(depth, n_lat * p_len, MIX_W)

    ck_na, cv_na, cv_df = (cache_rows(t) for t in (cache_na_k, cache_na_v, cache_diff_v))
    ck_df_t = cache_diff_k.reshape(n_lat, depth, p_len, MIX_W).transpose(1, 0, 3, 2).astype(BF16)
    s0_gla = _block_diag_state(state_gla, True)
    s0_gdn = _block_diag_state(state_gdn, False)
    zeros_state = jnp.zeros((n_ctx, 2, MIX_W, MIX_W), F32)
    lkv = n_tok + p_len
    tq_diff = DIFF_Q_TILE if n_tok % DIFF_Q_TILE == 0 else CHUNK

    new_kv, new_gla, new_gdn = [], [], []
    for l in range(depth):
        lam_init = 0.8 - 0.6 * math.exp(-0.3 * l)
        z_na, z_gla, z_gdn, z_diff, z_small = _proj(xs, tokens, mod[l], norm1_w[l][None], wcat[l], tm, mod_row)

        dl = diff_lambda[l]
        na_lat = _na_latent(z_na, ck_na[l], cv_na[l], na_bias[l], n_lat, n_tok, p_len)
        na_ctx = _attention(z_na, 0, z_na, 1, z_na, 2, dl, n_sub=1, lam_init=0.0, n_seq=n_ctx, lq=l_ctx,
                            lk=l_ctx, q_base=lat_tokens, k_base=lat_tokens, tq=l_ctx, name="na_ctx")

        q_r, k_rt = _rope(z_diff, cos, sin, perm, n_lat, n_tok, tm)
        kk_t = jnp.concatenate([k_rt.reshape(n_lat, MIX_W, n_tok), ck_df_t[l]], 2).reshape(n_lat * MIX_W, lkv)
        vv = jnp.concatenate([z_diff[:lat_tokens, 2 * MIX_W:].reshape(n_lat, n_tok, MIX_W),
                              cv_df[l].reshape(n_lat, p_len, MIX_W)], 1)
        df_lat = _diff_latent(q_r, kk_t, vv.astype(BF16).reshape(n_lat * lkv, MIX_W), dl, lam_init,
                              n_lat, n_tok, lkv, tq_diff)
        df_ctx = _attention(z_diff, 0, z_diff, 1, z_diff, 2, dl, n_sub=2, lam_init=lam_init, n_seq=n_ctx,
                            lq=l_ctx, lk=l_ctx, q_base=lat_tokens, k_base=lat_tokens, tq=l_ctx, name="diff_ctx")

        gla_consts = (gup[l], gbias[l], lsm, gla_pm, bm, bm16)
        glf_lat, glb_lat, _ = _scan(_gla_step, z_gla, z_small, gla_consts, s0_gla[:, l], n_lat, n_tok, 0,
                                    GLA_SEQS_PER_STEP, "gla_scan")
        glf_ctx, glb_ctx, st_ctx = _scan(_gla_step, z_gla, z_small, gla_consts, zeros_state,
                                         n_ctx, l_ctx, lat_tokens, GLA_SEQS_PER_STEP, "gla_scan")

        qkv_n = _gdn_pre(z_gdn, edge, gdn_conv_w[l], bm16, tm)
        gdn_consts = (ea, eb, alog[l], dtb[l], lsm, lbt, pm, bm, bm16, sm)
        gdf_lat, gdb_lat, _ = _scan(_gdn_step, qkv_n, z_small, gdn_consts, s0_gdn[:, l], n_lat, n_tok, 0,
                                    GDN_SEQS_PER_STEP, "gdn_scan")
        gdf_ctx, gdb_ctx, s_ctx = _scan(_gdn_step, qkv_n, z_small, gdn_consts, zeros_state,
                                        n_ctx, l_ctx, lat_tokens, GDN_SEQS_PER_STEP, "gdn_scan")

        x = _merge(xs, tokens, mod[l], norm1_w[l][None], (na_lat, na_ctx), (glf_lat, glf_ctx), (glb_lat, glb_ctx), z_gla,
                   (gdf_lat, gdf_ctx), (gdb_lat, gdb_ctx), z_gdn, (df_lat, df_ctx), hn[l], bm16,
                   w_gate[l], w_branch16[l], w_out16[l], tm, mod_row, lam_init, lat_tokens // tm)
        if l < depth - 1:
            x = _moe(x, mod[l], norm2_w[l][None], wr, br, w1[l], w3[l], w2[l], tm_moe, mod_row_moe)
            xs = (x, x, lat_tokens // tm, lat_tokens // tm)
        else:
            y_lat, y_ctx = _moe(x, mod[l], norm2_w[l][None], wr, br, w1[l], w3[l], w2[l], tm_moe, mod_row_moe,
                                final_norm_w[None], lat_tokens // tm_moe)

        new_kv.append((z_na[lat_tokens:, MIX_W:], z_diff[lat_tokens:, MIX_W:]))
        new_gla.append(st_ctx)
        new_gdn.append(s_ctx)

    y_sample = y_lat.reshape(n_lat, n_tok, D_MODEL)
    y_prompt = y_ctx.reshape(n_ctx, l_ctx, D_MODEL)
    def ctx_kv(j, col):
        kv = jnp.stack([t[j] for t in new_kv])[:, :, col * MIX_W:(col + 1) * MIX_W]
        return kv.reshape(depth, n_ctx, l_ctx, HEADS, HEAD_DIM).transpose(1, 0, 2, 3, 4)
    new_state_gla = _diag_blocks(jnp.stack(new_gla, axis=1), True)
    new_state_gdn = _diag_blocks(jnp.stack(new_gdn, axis=1), False)
    return (y_prompt, y_sample, ctx_kv(0, 0), ctx_kv(0, 1), ctx_kv(1, 0), ctx_kv(1, 1), new_state_gla, new_state_gdn)
```

```python
import functools
import math

import numpy as np
import jax
import jax.numpy as jnp
from jax import lax
from jax.experimental import pallas as pl
from jax.experimental.pallas import tpu as pltpu

F32 = jnp.float32
BF16 = jnp.bfloat16

D_MODEL = 1024
HEADS = 4
HEAD_DIM = 64
MIX_W = HEADS * HEAD_DIM
GRID_W = 64
NA_WIN_H = 8
NA_WIN_W = 16
GATE_RANK = 16
GATE_NORM = 16.0
CHUNK = 64
N_EXPERTS = 16
N_GROUPS = 4
GROUP_E = N_EXPERTS // N_GROUPS
D_FF = D_MODEL // 4
ROPE_BASE = 10000.0
EPS = 1e-6
NEG = -1e30
LOG2E = 1.4426950408889634
SMALL_W = 128
V7X_VMEM_LIMIT = 56 * 1024 * 1024
LEAF = 8
ATTN_HEAD_GROUP = 2
NA_ROWS_PER_STEP = 4
GLA_SEQS_PER_STEP = 4
GDN_SEQS_PER_STEP = 4
MOE_TILE = 1024

_IN_SPLITS = (
    ('na_q', MIX_W), ('na_k', MIX_W), ('na_v', MIX_W),
    ('gla_q', MIX_W), ('gla_k', MIX_W), ('gla_v', MIX_W),
    ('gla_gf', GATE_RANK), ('gla_gb', GATE_RANK), ('gla_og', MIX_W),
    ('gdn_qkv', 3 * MIX_W), ('gdn_af', HEADS), ('gdn_ab', HEADS),
    ('gdn_bf', HEADS), ('gdn_bb', HEADS), ('gdn_og', MIX_W),
    ('diff_q', MIX_W), ('diff_k', MIX_W), ('diff_v', MIX_W),
    ('branch_gate', 4 * D_MODEL),
)
_OFF = {}
_o = 0
for _n, _s in _IN_SPLITS:
    _OFF[_n] = (_o, _o + _s)
    _o += _s

def _params(sem):
    return pltpu.CompilerParams(dimension_semantics=sem, vmem_limit_bytes=V7X_VMEM_LIMIT)


def _dot(a, b):
    return jnp.dot(a, b, preferred_element_type=F32)


def _bdot(a, b):
    return _dot(a.astype(BF16), b.astype(BF16))


def _bdot_nt(a, b):
    return lax.dot_general(a.astype(BF16), b.astype(BF16), (((1,), (1,)), ((), ())),
                           preferred_element_type=F32)


def _bdot_tn(a, b):
    return lax.dot_general(a.astype(BF16), b.astype(BF16), (((0,), (0,)), ((), ())),
                           preferred_element_type=F32)


def _split(x):
    hi = x.astype(BF16)
    lo = (x - hi.astype(F32)).astype(BF16)
    return hi, lo


def _dot_rx(x, m):
    hi, lo = _split(x)
    return _dot(hi, m) + _dot(lo, m)


def _dot_lx(m, x):
    hi, lo = _split(x)
    return _dot(m, hi) + _dot(m, lo)


def _dot3(a, b):
    ah, al = _split(a)
    bh, bl = _split(b)
    return _dot(ah, bh) + _dot(ah, bl) + _dot(al, bh)


def _silu(x):
    return x * jax.nn.sigmoid(x)


def _softplus(x):
    return jnp.maximum(x, 0.0) + jnp.log(1.0 + jnp.exp(-jnp.abs(x)))


def _rms_rows(x, w):
    return x * lax.rsqrt(jnp.mean(x * x, axis=-1, keepdims=True) + EPS) * w


def _head_sum(x, bm16):
    return _dot_rx(x, bm16)


def _bd_rows(x, bm):
    return jnp.concatenate([x] * HEADS, axis=0) * bm


def _lane_mask(lo, width, n=MIX_W):
    lane = lax.broadcasted_iota(jnp.int32, (1, n), 1)
    return (lane >= lo) & (lane < lo + width)


def _gdn_consts():
    c = CHUNK
    i = np.arange(c)[:, None]
    t = np.arange(c)[None, :]
    lb = (t <= i).astype(np.float32)
    lbl = (t > i).astype(np.float32)
    tot = np.ones((16, c), np.float32)
    strict = (t < i).astype(np.float32)
    incl = (t <= i).astype(np.float32)
    rev = lambda a: a[::-1, ::-1]
    lsm = np.stack([np.concatenate([lb, lbl, tot], 0), np.concatenate([rev(lb), rev(lbl), tot], 0)])
    lbt = np.stack([np.tile(lb.T, (1, HEADS)), np.tile(rev(lb).T, (1, HEADS))])
    pm = np.stack([np.stack([np.tile(strict, (1, HEADS)), np.tile(incl, (1, HEADS))]),
                   np.stack([np.tile(rev(strict), (1, HEADS)), np.tile(rev(incl), (1, HEADS))])])
    return jnp.asarray(lsm, BF16), jnp.asarray(lbt, F32), jnp.asarray(pm, F32)


def _gla_pair_masks():
    i = np.arange(CHUNK)[:, None]
    j = np.arange(CHUNK)[None, :]
    nblk = CHUNK // LEAF
    out = []
    for d in range(2):
        later = (i // LEAF > j // LEAF) if d == 0 else (i // LEAF < j // LEAF)
        order = (j <= i) if d == 0 else (j >= i)
        far = [(j // LEAF == blk) & later for blk in range(nblk)]
        near = [(i // LEAF == j // LEAF) & (j % LEAF == r) & order for r in range(LEAF)]
        out.append(np.stack([np.tile(m.astype(np.float32), (1, HEADS)) for m in far + near]))
    return jnp.asarray(np.stack(out), F32)


def _block_mask():
    r = np.arange(MIX_W)
    return (r[:, None] // HEAD_DIM == r[None, :] // HEAD_DIM).astype(np.float32)


def _solve_masks():
    r = np.arange(MIX_W)[:, None]
    c = np.arange(MIX_W)[None, :]
    out = [(r // LEAF == c // LEAF)]
    s = LEAF
    while s < CHUNK:
        out.append((r // (2 * s) == c // (2 * s)) & (r // s != c // s))
        s *= 2
    return jnp.asarray(np.stack(out).astype(np.float32))


def _rope_consts(n_tok):
    t = np.arange(n_tok)
    pos = np.stack([t // GRID_W, t % GRID_W], 0).astype(np.float32)
    lane = np.arange(MIX_W)
    u = lane % 32
    axis = u // 16
    w = u % 16
    first = w < 8
    inv = ROPE_BASE ** (-(w % 8).astype(np.float32) / 8.0)
    ang = pos[axis, :].T * inv[None, :]
    cos = np.cos(ang)
    sin = np.sin(ang) * np.where(first, -1.0, 1.0)[None, :]
    partner = np.where(first, lane + 8, lane - 8)
    perm = np.zeros((MIX_W, MIX_W), np.float32)
    perm[partner, lane] = 1.0
    return jnp.asarray(cos, F32), jnp.asarray(sin, F32), jnp.asarray(perm, BF16)


def _ada_kernel(c_ref, w_ref, b_ref, o_ref):
    c = c_ref[...]
    o_ref[0] = _bdot(_silu(c), w_ref[0]) + b_ref[0]


def _ada(cond, w_ada, b_ada):
    depth, _, n = w_ada.shape
    tn = 1536
    return pl.pallas_call(
        _ada_kernel,
        grid=(depth, n // tn),
        in_specs=[pl.BlockSpec((8, D_MODEL), lambda l, j: (0, 0)),
                  pl.BlockSpec((1, D_MODEL, tn), lambda l, j: (l, 0, j)),
                  pl.BlockSpec((1, 1, tn), lambda l, j: (l, 0, j))],
        out_specs=pl.BlockSpec((1, 8, tn), lambda l, j: (l, 0, j)),
        out_shape=jax.ShapeDtypeStruct((depth, 8, n), F32),
        compiler_params=_params(("parallel", "parallel")),
        name="ada_mod",
    )(cond, w_ada, b_ada.reshape(depth, 1, n))


def _proj_kernel(xl_ref, xc_ref, mod_ref, nw_ref, w_ref, o_na, o_gla, o_gdn, o_diff, o_small, *, n_lat_tiles):
    mod = mod_ref[0]
    x = jnp.where(pl.program_id(0) < n_lat_tiles, xl_ref[...], xc_ref[...])
    h = _rms_rows(x, nw_ref[...]) * (1.0 + mod[1:2]) + mod[0:1]
    hb = h.astype(BF16)
    o_na[...] = _dot(hb, w_ref[:, 0:768])
    o_gla[...] = _dot(hb, w_ref[:, 768:1792])
    o_gdn[...] = _dot(hb, w_ref[:, 1792:2816])
    o_diff[...] = _dot(hb, w_ref[:, 2816:3584])
    o_small[...] = _dot(hb, w_ref[:, 3584:3712])


def _stream_specs(xs, tm):
    _, _, ctx_first, n_lat_tiles = xs
    return [pl.BlockSpec((tm, D_MODEL), lambda i: (jnp.minimum(i, n_lat_tiles - 1), 0)),
            pl.BlockSpec((tm, D_MODEL), lambda i: (ctx_first + jnp.maximum(i - n_lat_tiles, 0), 0))]


def _proj(xs, t, mod_l, nw, wcat, tm, mod_row):
    widths = (768, 1024, 1024, 768, SMALL_W)
    return pl.pallas_call(
        functools.partial(_proj_kernel, n_lat_tiles=xs[3]),
        grid=(t // tm,),
        in_specs=_stream_specs(xs, tm) + [
                  pl.BlockSpec((1, 6, D_MODEL), lambda i: (mod_row(i), 0, 0)),
                  pl.BlockSpec((1, D_MODEL), lambda i: (0, 0)),
                  pl.BlockSpec(wcat.shape, lambda i: (0, 0))],
        out_specs=[pl.BlockSpec((tm, w), lambda i: (i, 0)) for w in widths],
        out_shape=[jax.ShapeDtypeStruct((t, w), F32) for w in widths],
        compiler_params=_params(("parallel",)),
        name="in_proj",
    )(xs[0], xs[1], mod_l, nw, wcat)


def _attn_core(q, k, v, n_sub, lam):
    kb = k.astype(BF16)
    vb = v.astype(BF16)
    tq = q.shape[0]
    sub_w = HEAD_DIM // n_sub
    scale = sub_w ** -0.5 * LOG2E
    out = jnp.zeros(q.shape, F32)
    for h0 in range(0, HEADS, ATTN_HEAD_GROUP):
        maps = [(h, m) for h in range(h0, h0 + ATTN_HEAD_GROUP) for m in range(n_sub)]
        qs = jnp.concatenate(
            [(q * jnp.where(_lane_mask(h * HEAD_DIM + m * sub_w, sub_w), scale, 0.0)).astype(BF16)
             for h, m in maps], axis=0)
        s = lax.dot_general(qs, kb, (((1,), (1,)), ((), ())), preferred_element_type=F32)
        e = jnp.exp2(s - jnp.max(s, axis=-1, keepdims=True))
        inv = 1.0 / jnp.sum(e, axis=-1, keepdims=True)
        if n_sub == 1:
            p = e * inv
        else:
            p = jnp.concatenate(
                [e[(2 * i) * tq:(2 * i + 1) * tq] * inv[(2 * i) * tq:(2 * i + 1) * tq]
                 - e[(2 * i + 1) * tq:(2 * i + 2) * tq] * (lam * inv[(2 * i + 1) * tq:(2 * i + 2) * tq])
                 for i in range(ATTN_HEAD_GROUP)], axis=0)
        o_all = _dot(p.astype(BF16), vb)
        for i in range(ATTN_HEAD_GROUP):
            out = jnp.where(_lane_mask((h0 + i) * HEAD_DIM, HEAD_DIM), o_all[i * tq:(i + 1) * tq], out)
    return out


def _diff_lambda(dl, lam_init):
    a = jnp.sum(dl[0:1] * dl[1:2], axis=-1, keepdims=True)
    b = jnp.sum(dl[2:3] * dl[3:4], axis=-1, keepdims=True)
    return jnp.exp(a) - jnp.exp(b) + lam_init


def _attn_kernel(q_ref, k_ref, v_ref, dl_ref, o_ref, *, n_sub, lam_init):
    lam = _diff_lambda(dl_ref[...], lam_init) if n_sub == 2 else None
    o_ref[...] = _attn_core(q_ref[...], k_ref[...], v_ref[...], n_sub, lam)


def _attention(q_arr, q_col, k_arr, k_col, v_arr, v_col, dl, *, n_sub, lam_init, n_seq, lq, lk,
               q_base, k_base, tq, name):
    nq = lq // tq
    return pl.pallas_call(
        functools.partial(_attn_kernel, n_sub=n_sub, lam_init=lam_init),
        grid=(n_seq, nq),
        in_specs=[pl.BlockSpec((tq, MIX_W), lambda s, j: (q_base // tq + s * nq + j, q_col)),
                  pl.BlockSpec((lk, MIX_W), lambda s, j: (k_base // lk + s, k_col)),
                  pl.BlockSpec((lk, MIX_W), lambda s, j: (k_base // lk + s, v_col)),
                  pl.BlockSpec(dl.shape, lambda s, j: (0, 0))],
        out_specs=pl.BlockSpec((tq, MIX_W), lambda s, j: (s * nq + j, 0)),
        out_shape=jax.ShapeDtypeStruct((n_seq * lq, MIX_W), F32),
        compiler_params=_params(("parallel", "parallel")),
        name=name,
    )(q_arr, k_arr, v_arr, dl)


DIFF_KEY_CHUNK = 256
DIFF_HEAD_GROUP = 1
DIFF_Q_TILE = 256


def _lane_fold(x, op):
    out = x[:, 0:128]
    for t in range(1, x.shape[1] // 128):
        out = op(out, x[:, t * 128:(t + 1) * 128])
    return out


def _diff_group(h0, q, kt_ref, v_ref, lam, tkc):
    tq = q.shape[0]
    nck = kt_ref.shape[1] // tkc
    sub_w = HEAD_DIM // 2
    scale = sub_w ** -0.5 * LOG2E
    maps = [(h, m) for h in range(h0, h0 + DIFF_HEAD_GROUP) for m in range(2)]
    qs = jnp.concatenate(
        [(q * jnp.where(_lane_mask(h * HEAD_DIM + m * sub_w, sub_w), scale, 0.0)).astype(BF16) for h, m in maps],
        axis=0)
    s, mvec = [], None
    for c in range(nck):
        sc = _dot(qs, kt_ref[:, c * tkc:(c + 1) * tkc])
        s.append(sc)
        mc = _lane_fold(sc, jnp.maximum)
        mvec = mc if mvec is None else jnp.maximum(mvec, mc)
        yield
    mrun = jnp.max(mvec, axis=-1, keepdims=True)
    e, lvec = [], None
    for c in range(nck):
        ec = jnp.exp2(s[c] - mrun)
        e.append(ec)
        lc = _lane_fold(ec, jnp.add)
        lvec = lc if lvec is None else lvec + lc
        yield
    inv = 1.0 / jnp.sum(lvec, axis=-1, keepdims=True)
    acc = None
    for c in range(nck):
        a = jnp.concatenate(
            [e[c][(2 * i) * tq:(2 * i + 1) * tq] * inv[(2 * i) * tq:(2 * i + 1) * tq]
             - e[c][(2 * i + 1) * tq:(2 * i + 2) * tq] * (lam * inv[(2 * i + 1) * tq:(2 * i + 2) * tq])
             for i in range(DIFF_HEAD_GROUP)], axis=0)
        o = _dot(a.astype(BF16), v_ref[c * tkc:(c + 1) * tkc, :])
        acc = o if acc is None else acc + o
        if c < nck - 1:
            yield
    return acc


_DIFF_PASSES = 3


def _diff_lat_kernel(q_ref, kt_ref, v_ref, dl_ref, o_ref, *, lam_init, tkc):
    lam = _diff_lambda(dl_ref[...], lam_init)
    q = q_ref[...]
    tq = q.shape[0]
    nck = kt_ref.shape[1] // tkc
    n_grp = HEADS // DIFF_HEAD_GROUP
    groups = [_diff_group(g * DIFF_HEAD_GROUP, q, kt_ref, v_ref, lam, tkc) for g in range(n_grp)]
    res = [None] * n_grp
    for t in range(n_grp + _DIFF_PASSES - 1):
        active = [g for g in range(n_grp) if 0 <= t - g < _DIFF_PASSES]
        for _ in range(nck):
            for g in active:
                try:
                    next(groups[g])
                except StopIteration as done:
                    res[g] = done.value
    out = jnp.zeros(q.shape, F32)
    for h in range(HEADS):
        g, i = divmod(h, DIFF_HEAD_GROUP)
        out = jnp.where(_lane_mask(h * HEAD_DIM, HEAD_DIM), res[g][i * tq:(i + 1) * tq], out)
    o_ref[...] = out


def _diff_latent(q_r, kk_t, vv16, dl, lam_init, n_b, n_tok, lk, tq):
    tkc = DIFF_KEY_CHUNK if lk % DIFF_KEY_CHUNK == 0 else 128
    assert lk % tkc == 0
    nq = n_tok // tq
    return pl.pallas_call(
        functools.partial(_diff_lat_kernel, lam_init=lam_init, tkc=tkc),
        grid=(n_b, nq),
        in_specs=[pl.BlockSpec((tq, MIX_W), lambda b, j: (b * nq + j, 0)),
                  pl.BlockSpec((MIX_W, lk), lambda b, j: (b, 0)),
                  pl.BlockSpec((lk, MIX_W), lambda b, j: (b, 0)),
                  pl.BlockSpec(dl.shape, lambda b, j: (0, 0))],
        out_specs=pl.BlockSpec((tq, MIX_W), lambda b, j: (b * nq + j, 0)),
        out_shape=jax.ShapeDtypeStruct((n_b * n_tok, MIX_W), F32),
        compiler_params=_params(("parallel", "arbitrary")),
        name="diff_latent",
    )(q_r, kk_t, vv16, dl)


def _na_lat_kernel(q_ref, k_ref, v_ref, ck_ref, cv_ref, bias_ref, o_ref, *, rows, kh, rps):
    g = pl.program_id(1)
    ck = ck_ref[...].astype(BF16)
    cv = cv_ref[...].astype(BF16)
    scale = HEAD_DIM ** -0.5 * LOG2E
    head_scale = [jnp.where(_lane_mask(h * HEAD_DIM, HEAD_DIM), scale, 0.0) for h in range(HEADS)]
    hw = HEADS * GRID_W
    qs = []
    for j in range(rps):
        qj = q_ref[j * GRID_W:(j + 1) * GRID_W, :]
        qs.append(jnp.concatenate([(qj * hs).astype(BF16) for hs in head_scale], axis=0))
    s_ctx_all = lax.dot_general(jnp.concatenate(qs, axis=0), ck, (((1,), (1,)), ((), ())),
                                preferred_element_type=F32)
    p_ctx, o_loc = [], []
    for j in range(rps):
        r = g * rps + j
        start = jnp.clip(r - kh // 2, 0, rows - kh)
        cls = r - start
        ws = pl.multiple_of(start * GRID_W, GRID_W)
        kw = k_ref[pl.ds(ws, kh * GRID_W), :].astype(BF16)
        vw = v_ref[pl.ds(ws, kh * GRID_W), :].astype(BF16)
        bias = jnp.concatenate([bias_ref[h, pl.ds(cls, 1)][0] for h in range(HEADS)], axis=0)
        s_loc = lax.dot_general(qs[j], kw, (((1,), (1,)), ((), ())), preferred_element_type=F32) + bias
        s_ctx = s_ctx_all[j * hw:(j + 1) * hw]
        mx = jnp.maximum(jnp.max(s_loc, axis=-1, keepdims=True), jnp.max(s_ctx, axis=-1, keepdims=True))
        e_loc = jnp.exp2(s_loc - mx)
        e_ctx = jnp.exp2(s_ctx - mx)
        inv = 1.0 / (jnp.sum(e_loc, axis=-1, keepdims=True) + jnp.sum(e_ctx, axis=-1, keepdims=True))
        p_ctx.append((e_ctx * inv).astype(BF16))
        o_loc.append(_dot((e_loc * inv).astype(BF16), vw))
    o_ctx_all = _dot(jnp.concatenate(p_ctx, axis=0), cv)
    for j in range(rps):
        o_all = o_loc[j] + o_ctx_all[j * hw:(j + 1) * hw]
        out = jnp.zeros((GRID_W, MIX_W), F32)
        for h in range(HEADS):
            out = jnp.where(_lane_mask(h * HEAD_DIM, HEAD_DIM), o_all[h * GRID_W:(h + 1) * GRID_W], out)
        o_ref[j * GRID_W:(j + 1) * GRID_W, :] = out


def _na_bias_table(rpb, kh):
    cidx = np.arange(GRID_W)
    cls = np.arange(kh)
    drow = np.arange(kh)[None, :] - cls[:, None] + NA_WIN_H - 1
    col_start = np.clip(cidx - NA_WIN_W // 2, 0, GRID_W - NA_WIN_W)
    col_ok = (cidx[None, :] >= col_start[:, None]) & (cidx[None, :] < col_start[:, None] + NA_WIN_W)
    dcol = np.clip(cidx[None, :] - cidx[:, None], 1 - NA_WIN_W, NA_WIN_W - 1) + NA_WIN_W - 1
    onehot = (dcol.reshape(1, -1) == np.arange(2 * NA_WIN_W - 1)[:, None]).astype(np.float32)
    toep = jnp.dot(rpb.astype(F32).reshape(-1, 2 * NA_WIN_W - 1), onehot, precision=lax.Precision.HIGHEST)
    toep = toep.reshape(-1, 2 * NA_WIN_H - 1, GRID_W, GRID_W)
    bias = jnp.stack([toep[:, drow[c, 0]:drow[c, 0] + kh] for c in range(kh)], axis=1)
    bias = bias.transpose(0, 1, 3, 2, 4)
    bias = jnp.where(col_ok[:, None, :], bias * LOG2E, NEG)
    return bias.reshape(-1, kh, GRID_W, kh * GRID_W)


def _na_latent(z_na, ck, cv, bias, n_b, n_tok, p_len):
    rows = n_tok // GRID_W
    kh = min(NA_WIN_H, rows)
    rps = NA_ROWS_PER_STEP
    assert rows % rps == 0
    steps = rows // rps
    return pl.pallas_call(
        functools.partial(_na_lat_kernel, rows=rows, kh=kh, rps=rps),
        grid=(n_b, steps),
        in_specs=[pl.BlockSpec((rps * GRID_W, MIX_W), lambda b, r: (b * steps + r, 0)),
                  pl.BlockSpec((n_tok, MIX_W), lambda b, r: (b, 1)),
                  pl.BlockSpec((n_tok, MIX_W), lambda b, r: (b, 2)),
                  pl.BlockSpec((p_len, MIX_W), lambda b, r: (b, 0)),
                  pl.BlockSpec((p_len, MIX_W), lambda b, r: (b, 0)),
                  pl.BlockSpec(bias.shape, lambda b, r: (0, 0, 0, 0))],
        out_specs=pl.BlockSpec((rps * GRID_W, MIX_W), lambda b, r: (b * steps + r, 0)),
        out_shape=jax.ShapeDtypeStruct((n_b * n_tok, MIX_W), F32),
        compiler_params=_params(("parallel", "arbitrary")),
        name="na_latent",
    )(z_na, z_na, z_na, ck, cv, bias)


def _rope_kernel(q_ref, k_ref, cos_ref, sin_ref, p_ref, qo_ref, ko_ref):
    cos = cos_ref[...]
    sin = sin_ref[...]
    p = p_ref[...]
    q = q_ref[...]
    k = k_ref[...]
    qo_ref[...] = q * cos + _dot_rx(q, p) * sin
    ko_ref[...] = (k * cos + _dot_rx(k, p) * sin).T.astype(BF16)


def _rope(z_diff, cos, sin, perm, n_b, n_tok, tm):
    nt = n_tok // tm
    spec_t = pl.BlockSpec((tm, MIX_W), lambda i: (i % nt, 0))
    return pl.pallas_call(
        _rope_kernel,
        grid=(n_b * nt,),
        in_specs=[pl.BlockSpec((tm, MIX_W), lambda i: (i, 0)),
                  pl.BlockSpec((tm, MIX_W), lambda i: (i, 1)),
                  spec_t, spec_t,
                  pl.BlockSpec(perm.shape, lambda i: (0, 0))],
        out_specs=[pl.BlockSpec((tm, MIX_W), lambda i: (i, 0)),
                   pl.BlockSpec((MIX_W, tm), lambda i: (i // nt, i % nt))],
        out_shape=[jax.ShapeDtypeStruct((n_b * n_tok, MIX_W), F32),
                   jax.ShapeDtypeStruct((n_b * MIX_W, n_tok), BF16)],
        compiler_params=_params(("parallel",)),
        name="diff_rope",
    )(z_diff, z_diff, cos, sin, perm)


def _gdn_pre_kernel(x_ref, prev_ref, next_ref, edge_ref, w_ref, bm_ref, o_ref, *, tm):
    x = x_ref[...]
    w = w_ref[...]
    edge = edge_ref[...]
    row = lax.broadcasted_iota(jnp.int32, (tm, 1), 0)
    x_prev = jnp.where(row == 0, prev_ref[7:8, :], pltpu.roll(x, 1, 0)) * edge[:, 0:1]
    x_next = jnp.where(row == tm - 1, next_ref[0:1, :], pltpu.roll(x, tm - 1, 0)) * edge[:, 1:2]
    y = _silu(x_prev * w[0:1] + x * w[1:2] + x_next * w[2:3])
    bm16 = bm_ref[...]
    q = y[:, 0:MIX_W]
    k = y[:, MIX_W:2 * MIX_W]
    o_ref[:, 0:MIX_W] = q * lax.rsqrt(_head_sum(q * q, bm16) + EPS) * (HEAD_DIM ** -0.5)
    o_ref[:, MIX_W:2 * MIX_W] = k * lax.rsqrt(_head_sum(k * k, bm16) + EPS)
    o_ref[:, 2 * MIX_W:3 * MIX_W] = y[:, 2 * MIX_W:3 * MIX_W]


def _seq_edges(lat_tokens, n_tok, ctx_tokens, l_ctx):
    pos = np.concatenate([np.arange(lat_tokens) % n_tok, np.arange(ctx_tokens) % l_ctx])
    seq = np.concatenate([np.full(lat_tokens, n_tok), np.full(ctx_tokens, l_ctx)])
    edge = np.zeros((lat_tokens + ctx_tokens, 8), np.float32)
    edge[:, 0] = pos != 0
    edge[:, 1] = pos != seq - 1
    return jnp.asarray(edge)


def _gdn_pre(z_gdn, edge, conv_w, bm16, tm):
    t = z_gdn.shape[0]
    w3 = 3 * MIX_W
    nb8 = t // 8
    return pl.pallas_call(
        functools.partial(_gdn_pre_kernel, tm=tm),
        grid=(t // tm,),
        in_specs=[pl.BlockSpec((tm, w3), lambda i: (i, 0)),
                  pl.BlockSpec((8, w3), lambda i: (jnp.maximum(i * (tm // 8) - 1, 0), 0)),
                  pl.BlockSpec((8, w3), lambda i: (jnp.minimum((i + 1) * (tm // 8), nb8 - 1), 0)),
                  pl.BlockSpec((tm, 8), lambda i: (i, 0)),
                  pl.BlockSpec(conv_w.shape, lambda i: (0, 0)),
                  pl.BlockSpec(bm16.shape, lambda i: (0, 0))],
        out_specs=pl.BlockSpec((tm, w3), lambda i: (i, 0)),
        out_shape=jax.ShapeDtypeStruct((t, w3), F32),
        compiler_params=_params(("parallel",)),
        name="gdn_pre",
    )(z_gdn, z_gdn, z_gdn, edge, conv_w, bm16)


def _bd_rows16(x, bm16):
    return jnp.concatenate([x.astype(BF16)] * HEADS, axis=0) * bm16


def _gla_chain(q, k, v, zs, gup, gb, lsm, pm_ref, d, bm, bm16, st):
    nblk = CHUNK // LEAF
    q = q * (HEAD_DIM ** -0.5)
    gp = _bdot(zs, gup) + gb
    g = (jnp.minimum(gp, 0.0) - jnp.log(1.0 + jnp.exp(-jnp.abs(gp)))) * (LOG2E / GATE_NORM)
    yield
    e = _dot_lx(lsm, g)
    b = e[0:CHUNK]
    fb = jnp.exp2(b)
    fbl = jnp.exp2(e[CHUNK:2 * CHUNK])
    ftot = jnp.exp2(e[2 * CHUNK:2 * CHUNK + 1])
    yield
    edge = LEAF - 1 if d == 0 else 0
    blocks = range(nblk - 1) if d == 0 else range(1, nblk)
    b3 = b.reshape(nblk, LEAF, MIX_W)
    b_edge = jnp.broadcast_to(b3[:, edge:edge + 1, :], b3.shape).reshape(CHUNK, MIX_W)
    kf = jnp.exp2(jnp.minimum(b_edge - b, 0.0))
    q_far = jnp.concatenate(
        [(q * jnp.exp2(jnp.minimum(b - b[blk * LEAF + edge:blk * LEAF + edge + 1, :], 0.0))).astype(BF16)
         for blk in blocks], axis=0)
    s_far = lax.dot_general(q_far, _bd_rows16(k * kf, bm16), (((1,), (1,)), ((), ())),
                            preferred_element_type=F32)
    yield
    q_near = jnp.concatenate(
        [(q * jnp.exp2(jnp.minimum(
            b - jnp.broadcast_to(b3[:, r:r + 1, :], b3.shape).reshape(CHUNK, MIX_W), 0.0))).astype(BF16)
         for r in range(LEAF)], axis=0)
    s_near = lax.dot_general(q_near, _bd_rows16(k, bm16), (((1,), (1,)), ((), ())),
                             preferred_element_type=F32)
    yield
    att = None
    for n, blk in enumerate(blocks):
        term = s_far[n * CHUNK:(n + 1) * CHUNK] * pm_ref[d, blk]
        att = term if att is None else att + term
    for r in range(LEAF):
        att = att + s_near[r * CHUNK:(r + 1) * CHUNK] * pm_ref[d, nblk + r]
    yield
    o = _bdot_nt(q * fb, st) + _dot(att.astype(BF16), _bd_rows16(v, bm16))
    yield
    return o, st * ftot + _bdot_tn(v, k * fbl) * bm


def _row_pieces(s, d):
    out = []
    for h in range(HEADS):
        for blk in range(CHUNK // s):
            lo = h * CHUNK + blk * s
            out.append((lo, lo + s, blk % 2 == 1 - d))
    return out


def _tri_inverse(mbd, sm_ref, d):
    r = lax.broadcasted_iota(jnp.int32, (MIX_W, MIX_W), 0)
    c = lax.broadcasted_iota(jnp.int32, (MIX_W, MIX_W), 1)
    eye = jnp.where(r == c, 1.0, 0.0)
    md = mbd * sm_ref[0]
    m2 = _bdot(md, md)
    yield
    m4 = _bdot(m2, m2)
    t = eye - md
    t = t + _bdot(t, m2)
    yield
    t = t + _bdot(t, m4)
    s = LEAF
    for lvl in range(1, sm_ref.shape[0]):
        pieces = _row_pieces(s, d)
        take = lambda x: jnp.concatenate([x[lo:hi] for lo, hi, upd in pieces if upd], axis=0)
        yield
        t16 = t.astype(BF16)
        a = _dot(take(mbd * sm_ref[lvl]).astype(BF16), t16).astype(BF16)
        rows, n = [], 0
        for lo, hi, upd in pieces:
            rows.append(a[n:n + s] if upd else jnp.zeros((s, MIX_W), BF16))
            n += s if upd else 0
        yield
        b = _dot(take(t16), jnp.concatenate(rows, axis=0))
        rows, n = [], 0
        for lo, hi, upd in pieces:
            rows.append(t[lo:hi] - b[n:n + s] if upd else t[lo:hi])
            n += s if upd else 0
        t = jnp.concatenate(rows, axis=0)
        s *= 2
    return t


def _gdn_chain(q, k, v, zs, ea, eb, alog, dtb, lsm, lbt, pm_ref, d, bm, bm16, sm_ref, s):
    g = -jnp.exp(alog) * _softplus(_dot_rx(zs, ea) + dtb)
    beta = jax.nn.sigmoid(_dot_rx(zs, eb))
    yield
    e = _dot_lx(lsm, g)
    dcol = e[0:CHUNK]
    drow = _dot_lx(jnp.ones((8, CHUNK), BF16), g * lbt)[0:1]
    kb = k * beta
    s2 = lax.dot_general(jnp.concatenate([kb, q], axis=0).astype(BF16), _bd_rows16(k, bm16),
                         (((1,), (1,)), ((), ())), preferred_element_type=F32)
    yield
    dec = jnp.exp(jnp.where(pm_ref[d, 1] > 0.0, dcol - drow, NEG))
    m_cat = s2[0:CHUNK] * dec * pm_ref[d, 0]
    att = s2[CHUNK:2 * CHUNK] * dec
    t_inv = yield from _tri_inverse(_bd_rows(m_cat, bm), sm_ref, d)
    t16 = (t_inv[0:CHUNK] + t_inv[CHUNK:2 * CHUNK] + t_inv[2 * CHUNK:3 * CHUNK] + t_inv[3 * CHUNK:]).astype(BF16)
    fb = jnp.exp(dcol)
    fbl = jnp.exp(e[CHUNK:2 * CHUNK])
    ftot = jnp.exp(e[2 * CHUNK:2 * CHUNK + 1])
    yield
    u = _dot(t16, _bd_rows16(v * beta, bm16))
    w = _dot(t16, _bd_rows16(kb * fb, bm16))
    yield
    ws = _dot(jnp.concatenate([w.astype(BF16), (q * fb).astype(BF16)], axis=0), s.astype(BF16))
    v_new = u - ws[0:CHUNK]
    yield
    o = ws[CHUNK:2 * CHUNK] + _dot(att.astype(BF16), _bd_rows16(v_new, bm16))
    return o, s * ftot + lax.dot_general((k * fbl).astype(BF16), v_new.astype(BF16), (((0,), (0,)), ((), ())),
                                         preferred_element_type=F32) * bm


def _scan_kernel(*refs, chain, n_const, nb, n_chunks):
    n_tok = nb * 2 * 4
    tok = refs[:n_tok]
    consts = refs[n_tok:n_tok + n_const]
    s0_ref, of_ref, ob_ref, sf_ref, s_scr = refs[n_tok + n_const:]
    n = pl.program_id(1)

    @pl.when(n == 0)
    def _():
        s_scr[...] = s0_ref[...]

    ids = [(j, d) for j in range(nb) for d in range(2)]
    gens = []
    for j, d in ids:
        q_ref, k_ref, v_ref, zs_ref = tok[(j * 2 + d) * 4:(j * 2 + d) * 4 + 4]
        gens.append(chain(q_ref[...], k_ref[...], v_ref[...], zs_ref[...], consts, d, s_scr[j, d]))
    live = list(range(len(gens)))
    while live:
        for c in list(live):
            try:
                next(gens[c])
            except StopIteration as done:
                j, d = ids[c]
                o, s_new = done.value
                (of_ref, ob_ref)[d][j] = o
                s_scr[j, d] = s_new
                live.remove(c)

    @pl.when(n == n_chunks - 1)
    def _():
        sf_ref[...] = s_scr[...]


def _gla_step(q, k, v, zs, consts, d, st):
    gup_ref, gb_ref, lsm_ref, pm_ref, bm_ref, bm16_ref = consts
    return _gla_chain(q, k, v, zs, gup_ref[d], gb_ref[d], lsm_ref[d], pm_ref, d, bm_ref[...], bm16_ref[...], st)


def _gdn_step(q, k, v, zs, consts, d, s):
    ea_ref, eb_ref, alog_ref, dtb_ref, lsm_ref, lbt_ref, pm_ref, bm_ref, bm16_ref, sm_ref = consts
    return _gdn_chain(q, k, v, zs, ea_ref[d], eb_ref[d], alog_ref[d], dtb_ref[d], lsm_ref[d], lbt_ref[d],
                      pm_ref, d, bm_ref[...], bm16_ref[...], sm_ref, s)


def _scan(chain, qkv, z_small, consts, s0, n_seq, l_seq, base, seqs_per_step, name):
    nc = l_seq // CHUNK
    nb = seqs_per_step if n_seq % seqs_per_step == 0 else 1

    def chunk(n, d):
        return n + d * (nc - 1 - 2 * n)

    in_specs, args = [], []
    for j in range(nb):
        for d in range(2):
            for arr, col, width in ((qkv, 0, MIX_W), (qkv, 1, MIX_W), (qkv, 2, MIX_W), (z_small, 0, SMALL_W)):
                in_specs.append(pl.BlockSpec(
                    (CHUNK, width),
                    functools.partial(lambda g, n, j, d, col: (base // CHUNK + (g * nb + j) * nc + chunk(n, d), col),
                                      j=j, d=d, col=col)))
                args.append(arr)
    for cst in consts:
        in_specs.append(pl.BlockSpec(cst.shape, functools.partial(lambda g, n, nd: (0,) * nd, nd=cst.ndim)))
        args.append(cst)
    state_spec = pl.BlockSpec((nb, 2, MIX_W, MIX_W), lambda g, n: (g, 0, 0, 0))
    in_specs.append(state_spec)
    args.append(s0)
    o_f, o_b, s_f = pl.pallas_call(
        functools.partial(_scan_kernel, chain=chain, n_const=len(consts), nb=nb, n_chunks=nc),
        grid=(n_seq // nb, nc),
        in_specs=in_specs,
        out_specs=[pl.BlockSpec((nb, CHUNK, MIX_W), lambda g, n: (g, chunk(n, 0), 0)),
                   pl.BlockSpec((nb, CHUNK, MIX_W), lambda g, n: (g, chunk(n, 1), 0)),
                   state_spec],
        out_shape=[jax.ShapeDtypeStruct((n_seq, l_seq, MIX_W), F32),
                   jax.ShapeDtypeStruct((n_seq, l_seq, MIX_W), F32),
                   jax.ShapeDtypeStruct((n_seq, 2, MIX_W, MIX_W), F32)],
        scratch_shapes=[pltpu.VMEM((nb, 2, MIX_W, MIX_W), F32)],
        compiler_params=_params(("parallel", "arbitrary")),
        name=name,
    )(*args)
    return o_f.reshape(n_seq * l_seq, MIX_W), o_b.reshape(n_seq * l_seq, MIX_W), s_f


def _merge_kernel(xl_ref, xc_ref, mod_ref, n1_ref, na_l, na_c, glaf_l, glaf_c, glab_l, glab_c, glag_ref, gdnf_l, gdnf_c,
                  gdnb_l, gdnb_c, gdng_ref, df_l, df_c, hn_ref, bm_ref, wg_ref, wb_ref, wo_ref, o_ref, *,
                  lam_init, n_lat_tiles):
    is_lat = pl.program_id(0) < n_lat_tiles

    def pick(lat_ref, ctx_ref):
        return jnp.where(is_lat, lat_ref[...], ctx_ref[...])

    mod = mod_ref[0]
    x = pick(xl_ref, xc_ref)
    h = _rms_rows(x, n1_ref[...]) * (1.0 + mod[1:2]) + mod[0:1]
    hb = h.astype(BF16)
    bm16 = bm_ref[...]

    def head_norm(o, w):
        return o * lax.rsqrt(_head_sum(o * o, bm16) * (1.0 / HEAD_DIM) + EPS) * w

    branches = (
        pick(na_l, na_c),
        head_norm(pick(glaf_l, glaf_c) + pick(glab_l, glab_c), hn_ref[0]) * _silu(glag_ref[...]),
        head_norm(pick(gdnf_l, gdnf_c) + pick(gdnb_l, gdnb_c), hn_ref[1]) * _silu(gdng_ref[...]),
        head_norm(pick(df_l, df_c), hn_ref[2]) * (1.0 - lam_init),
    )
    merged = None
    for n, br in enumerate(branches):
        gate = jax.nn.sigmoid(_dot(hb, wg_ref[:, n * D_MODEL:(n + 1) * D_MODEL]))
        term = gate * _bdot(br, wb_ref[n])
        merged = term if merged is None else merged + term
    o_ref[...] = x + mod[2:3] * _bdot(merged, wo_ref[...])


def _merge(xs, t, mod_l, n1, na, gla_f, gla_b, z_gla, gdn_f, gdn_b, z_gdn, df, hn, bm16, wg, wb, wo, tm, mod_row,
           lam_init, n_lat_tiles):
    tok = lambda col: pl.BlockSpec((tm, MIX_W), lambda i: (i, col))
    lat = pl.BlockSpec((tm, MIX_W), lambda i: (jnp.minimum(i, n_lat_tiles - 1), 0))
    ctx = pl.BlockSpec((tm, MIX_W), lambda i: (jnp.maximum(i - n_lat_tiles, 0), 0))
    full = lambda a: pl.BlockSpec(a.shape, lambda i: (0,) * a.ndim)
    return pl.pallas_call(
        functools.partial(_merge_kernel, lam_init=lam_init, n_lat_tiles=n_lat_tiles),
        grid=(t // tm,),
        in_specs=_stream_specs(xs, tm) + [
                  pl.BlockSpec((1, 6, D_MODEL), lambda i: (mod_row(i), 0, 0)),
                  full(n1), lat, ctx, lat, ctx, lat, ctx, tok(3), lat, ctx, lat, ctx, tok(3), lat, ctx,
                  full(hn), full(bm16), full(wg), full(wb), full(wo)],
        out_specs=pl.BlockSpec((tm, D_MODEL), lambda i: (i, 0)),
        out_shape=jax.ShapeDtypeStruct((t, D_MODEL), F32),
        compiler_params=_params(("parallel",)),
        name="merge_out",
    )(xs[0], xs[1], mod_l, n1, *na, *gla_f, *gla_b, z_gla, *gdn_f, *gdn_b, z_gdn, *df, hn, bm16, wg, wb, wo)


def _route(h2, wr_ref, br_ref):
    s_t = jax.nn.sigmoid(_dot3(h2, wr_ref[...])).T
    s = [s_t[e:e + 1] for e in range(N_EXPERTS)]
    sel = [s[e] + br_ref[e] for e in range(N_EXPERTS)]
    best = None
    for g in range(N_GROUPS):
        v = sel[g * GROUP_E:(g + 1) * GROUP_E]
        score = None
        for i in range(GROUP_E):
            for j in range(i + 1, GROUP_E):
                score = v[i] + v[j] if score is None else jnp.maximum(score, v[i] + v[j])
        if best is None:
            best, gi = score, jnp.zeros_like(score)
        else:
            better = score > best
            gi = jnp.where(better, float(g), gi)
            best = jnp.where(better, score, best)
    vals = [jnp.where(gi == float(e // GROUP_E), sel[e], NEG) for e in range(N_EXPERTS)]

    def first_max(v):
        mx, idx = v[0], jnp.zeros_like(v[0])
        for e in range(1, N_EXPERTS):
            better = v[e] > mx
            idx = jnp.where(better, float(e), idx)
            mx = jnp.where(better, v[e], mx)
        return idx

    i1 = first_max(vals)
    i2 = first_max([jnp.where(i1 == float(e), NEG, vals[e]) for e in range(N_EXPERTS)])
    s1 = sum(jnp.where(i1 == float(e), s[e], 0.0) for e in range(N_EXPERTS))
    s2 = sum(jnp.where(i2 == float(e), s[e], 0.0) for e in range(N_EXPERTS))
    inv = 1.0 / (s1 + s2)
    rows = [jnp.where(i1 == float(e), s1 * inv, 0.0) + jnp.where(i2 == float(e), s2 * inv, 0.0)
            for e in range(N_EXPERTS)]
    gates_t = jnp.concatenate(rows + [jnp.zeros((SMALL_W - N_EXPERTS, h2.shape[0]), F32)], axis=0)
    return gates_t.T


def _moe_kernel(x_ref, mod_ref, n2_ref, wr_ref, br_ref, w1_ref, w3_ref, w2_ref, *rest, n_lat_tiles):
    if n_lat_tiles is None:
        o_ref, h_scr, g_scr, acc_scr = rest
    else:
        fw_ref, ol_ref, oc_ref, h_scr, g_scr, acc_scr = rest
    g = pl.program_id(1)
    mod = mod_ref[0]

    @pl.when(g == 0)
    def _():
        h2 = _rms_rows(x_ref[...], n2_ref[...]) * (1.0 + mod[4:5]) + mod[3:4]
        h_scr[...] = h2.astype(BF16)
        g_scr[...] = _route(h2, wr_ref, br_ref)
        acc_scr[...] = jnp.zeros_like(acc_scr)

    hb = h_scr[...]
    gates = g_scr[...]
    lane = lax.broadcasted_iota(jnp.int32, (1, SMALL_W), 1)
    he = jnp.concatenate(
        [_silu(_dot(hb, w1_ref[e])) * _dot(hb, w3_ref[e])
         * jnp.sum(jnp.where(lane == g * GROUP_E + e, gates, 0.0), axis=-1, keepdims=True)
         for e in range(GROUP_E)], axis=1)
    acc_scr[...] += _dot(he.astype(BF16), w2_ref[...])

    if n_lat_tiles is None:
        @pl.when(g == N_GROUPS - 1)
        def _():
            o_ref[...] = x_ref[...] + mod[5:6] * acc_scr[...]
    else:
        is_lat = pl.program_id(0) < n_lat_tiles

        @pl.when((g == N_GROUPS - 1) & is_lat)
        def _():
            ol_ref[...] = _rms_rows(x_ref[...] + mod[5:6] * acc_scr[...], fw_ref[...])

        @pl.when((g == N_GROUPS - 1) & jnp.logical_not(is_lat))
        def _():
            oc_ref[...] = _rms_rows(x_ref[...] + mod[5:6] * acc_scr[...], fw_ref[...])


def _moe(x, mod_l, n2, wr, br, w1, w3, w2, tm, mod_row, final_w=None, n_lat_tiles=None):
    t = x.shape[0]
    gw = GROUP_E * D_FF
    in_specs = [pl.BlockSpec((tm, D_MODEL), lambda i, g: (i, 0)),
                pl.BlockSpec((1, 6, D_MODEL), lambda i, g: (mod_row(i), 0, 0)),
                pl.BlockSpec(n2.shape, lambda i, g: (0, 0)),
                pl.BlockSpec(wr.shape, lambda i, g: (0, 0)),
                pl.BlockSpec(memory_space=pltpu.SMEM),
                pl.BlockSpec((GROUP_E, D_MODEL, D_FF), lambda i, g: (g, 0, 0)),
                pl.BlockSpec((GROUP_E, D_MODEL, D_FF), lambda i, g: (g, 0, 0)),
                pl.BlockSpec((gw, D_MODEL), lambda i, g: (g, 0))]
    args = [x, mod_l, n2, wr, br, w1, w3, w2]
    if final_w is None:
        out_specs = pl.BlockSpec((tm, D_MODEL), lambda i, g: (i, 0))
        out_shape = jax.ShapeDtypeStruct((t, D_MODEL), F32)
    else:
        in_specs.append(pl.BlockSpec(final_w.shape, lambda i, g: (0, 0)))
        args.append(final_w)
        out_specs = [pl.BlockSpec((tm, D_MODEL), lambda i, g: (jnp.minimum(i, n_lat_tiles - 1), 0)),
                     pl.BlockSpec((tm, D_MODEL), lambda i, g: (jnp.maximum(i - n_lat_tiles, 0), 0))]
        out_shape = [jax.ShapeDtypeStruct((n_lat_tiles * tm, D_MODEL), F32),
                     jax.ShapeDtypeStruct((t - n_lat_tiles * tm, D_MODEL), F32)]
    return pl.pallas_call(
        functools.partial(_moe_kernel, n_lat_tiles=n_lat_tiles),
        grid=(t // tm, N_GROUPS),
        in_specs=in_specs,
        out_specs=out_specs,
        out_shape=out_shape,
        scratch_shapes=[pltpu.VMEM((tm, D_MODEL), BF16), pltpu.VMEM((tm, SMALL_W), F32),
                        pltpu.VMEM((tm, D_MODEL), F32)],
        compiler_params=_params(("arbitrary", "arbitrary")),
        name="moe",
    )(*args)


def _block_diag_state(s, transpose):
    if transpose:
        s = jnp.swapaxes(s, -1, -2)
    eye = jnp.eye(HEADS, dtype=s.dtype)
    out = jnp.einsum('...hij,hg->...higj', s, eye)
    return out.reshape(s.shape[:-3] + (MIX_W, MIX_W))


def _diag_blocks(s_bd, transpose):
    s = s_bd.reshape(s_bd.shape[:-2] + (HEADS, HEAD_DIM, HEADS, HEAD_DIM))
    s = jnp.stack([s[..., h, :, h, :] for h in range(HEADS)], axis=-3)
    return jnp.swapaxes(s, -1, -2) if transpose else s


def _pick_tile(*lengths):
    for tm in (512, 256, 128, 64):
        if all(n % tm == 0 for n in lengths):
            return tm
    raise ValueError("token counts must be multiples of 64")


def kernel(x_prompt, x_sample, cache_na_k, cache_na_v, cache_diff_k, cache_diff_v, state_gla, state_gdn, c, c_ctx, w_ada, b_ada, norm1_w, norm2_w, w_in, na_rpb, gla_gate_up, gla_gate_bias, gla_norm_w, gdn_conv_w, gdn_A_log, gdn_dt_bias, gdn_norm_w, diff_lambda, diff_subln_w, w_branch, w_out, w_router, b_router, w_e1, w_e3, w_e2, final_norm_w):
    n_ctx, l_ctx, _ = x_prompt.shape
    n_lat, n_tok, _ = x_sample.shape
    depth = w_ada.shape[0]
    p_len = cache_na_k.shape[2]
    lat_tokens = n_lat * n_tok
    ctx_tokens = n_ctx * l_ctx
    assert n_lat + 1 <= 8 and n_tok % l_ctx == 0 and n_tok % GRID_W == 0 and l_ctx % CHUNK == 0
    tm = _pick_tile(n_tok, ctx_tokens)

    def mod_row(i):
        return jnp.where(i * tm < lat_tokens, (i * tm) // n_tok, n_lat)

    tm_moe = MOE_TILE if (n_tok % MOE_TILE == 0 and ctx_tokens % MOE_TILE == 0) else tm

    def mod_row_moe(i):
        return jnp.where(i * tm_moe < lat_tokens, (i * tm_moe) // n_tok, n_lat)

    tokens = lat_tokens + ctx_tokens
    xs = (x_sample.reshape(lat_tokens, D_MODEL), x_prompt.reshape(ctx_tokens, D_MODEL), 0, lat_tokens // tm)
    cond = jnp.zeros((8, D_MODEL), F32).at[:n_lat].set(c).at[n_lat].set(c_ctx)
    mod = _ada(cond, w_ada, b_ada).reshape(depth, 8, 6, D_MODEL)

    gla_pm = _gla_pair_masks()
    lsm, lbt, pm = _gdn_consts()
    bm = jnp.asarray(_block_mask(), F32)
    bm16 = bm.astype(BF16)
    sm = _solve_masks()
    cos, sin, perm = _rope_consts(n_tok)
    edge = _seq_edges(lat_tokens, n_tok, ctx_tokens, l_ctx)
    lane = np.arange(MIX_W)
    small_rows = np.arange(SMALL_W)
    def expand_cols(first_col):
        return jnp.asarray((small_rows[:, None] == first_col + lane[None, :] // HEAD_DIM).astype(np.float32), BF16)
    ea = jnp.stack([expand_cols(32), expand_cols(36)])
    eb = jnp.stack([expand_cols(40), expand_cols(44)])
    wr = jnp.zeros((D_MODEL, SMALL_W), F32).at[:, :N_EXPERTS].set(w_router)
    br = b_router.astype(F32)

    def w_layer(l):
        wl = w_in[l]
        parts = [wl[:, _OFF[n][0]:_OFF[n][1]] for n in (
            'na_q', 'na_k', 'na_v', 'gla_q', 'gla_k', 'gla_v', 'gla_og', 'gdn_qkv', 'gdn_og', 'diff_q', 'diff_k',
            'diff_v', 'gla_gf', 'gla_gb', 'gdn_af', 'gdn_ab', 'gdn_bf', 'gdn_bb')]
        parts.append(jnp.zeros((D_MODEL, SMALL_W - 2 * GATE_RANK - 4 * HEADS), F32))
        return (jnp.concatenate(parts, axis=1).astype(BF16), wl[:, _OFF['branch_gate'][0]:].astype(BF16),
                w_branch[l].astype(BF16), w_out[l].astype(BF16), w_e1[l].astype(BF16), w_e3[l].astype(BF16),
                w_e2[l].reshape(N_EXPERTS * D_FF, D_MODEL).astype(BF16))

    gup = jnp.zeros((depth, 2, SMALL_W, MIX_W), F32)
    gup = gup.at[:, 0, 0:GATE_RANK].set(gla_gate_up[:, 0]).at[:, 1, GATE_RANK:2 * GATE_RANK].set(gla_gate_up[:, 1])
    gbias = gla_gate_bias[:, :, None, :]
    alog = jnp.repeat(gdn_A_log, HEAD_DIM, axis=-1)[:, :, None, :]
    dtb = jnp.repeat(gdn_dt_bias, HEAD_DIM, axis=-1)[:, :, None, :]
    hn = jnp.stack([jnp.tile(gla_norm_w, (1, HEADS)), jnp.tile(gdn_norm_w, (1, HEADS)),
                    jnp.tile(diff_subln_w, (1, HEADS))], axis=1)[:, :, None, :]
    kh = min(NA_WIN_H, n_tok // GRID_W)
    na_bias = _na_bias_table(na_rpb.reshape(depth * HEADS, 2 * NA_WIN_H - 1, 2 * NA_WIN_W - 1), kh)
    na_bias = na_bias.reshape(depth, HEADS, kh, GRID_W, kh * GRID_W)

    def cache_rows(cache):
        return cache.transpose(1, 0, 2, 3, 4).reshape(depth, n_lat * p_len, MIX_W)

    ck_na, cv_na, cv_df = (cache_rows(t) for t in (cache_na_k, cache_na_v, cache_diff_v))
    ck_df_t = cache_diff_k.reshape(n_lat, depth, p_len, MIX_W).transpose(1, 0, 3, 2).astype(BF16)
    s0_gla = _block_diag_state(state_gla, True)
    s0_gdn = _block_diag_state(state_gdn, False)
    zeros_state = jnp.zeros((n_ctx, 2, MIX_W, MIX_W), F32)
    lkv = n_tok + p_len
    tq_diff = DIFF_Q_TILE if n_tok % DIFF_Q_TILE == 0 else CHUNK

    new_kv, new_gla, new_gdn = [], [], []
    for l in range(depth):
        lam_init = 0.8 - 0.6 * math.exp(-0.3 * l)
        wcat, w_gate, w_branch16, w_out16, w1, w3, w2 = w_layer(l)
        z_na, z_gla, z_gdn, z_diff, z_small = _proj(xs, tokens, mod[l], norm1_w[l][None], wcat, tm, mod_row)

        dl = diff_lambda[l]
        na_lat = _na_latent(z_na, ck_na[l], cv_na[l], na_bias[l], n_lat, n_tok, p_len)
        na_ctx = _attention(z_na, 0, z_na, 1, z_na, 2, dl, n_sub=1, lam_init=0.0, n_seq=n_ctx, lq=l_ctx,
                            lk=l_ctx, q_base=lat_tokens, k_base=lat_tokens, tq=l_ctx, name="na_ctx")

        q_r, k_rt = _rope(z_diff, cos, sin, perm, n_lat, n_tok, tm)
        kk_t = jnp.concatenate([k_rt.reshape(n_lat, MIX_W, n_tok), ck_df_t[l]], 2).reshape(n_lat * MIX_W, lkv)
        vv = jnp.concatenate([z_diff[:lat_tokens, 2 * MIX_W:].reshape(n_lat, n_tok, MIX_W),
                              cv_df[l].reshape(n_lat, p_len, MIX_W)], 1)
        df_lat = _diff_latent(q_r, kk_t, vv.astype(BF16).reshape(n_lat * lkv, MIX_W), dl, lam_init,
                              n_lat, n_tok, lkv, tq_diff)
        df_ctx = _attention(z_diff, 0, z_diff, 1, z_diff, 2, dl, n_sub=2, lam_init=lam_init, n_seq=n_ctx,
                            lq=l_ctx, lk=l_ctx, q_base=lat_tokens, k_base=lat_tokens, tq=l_ctx, name="diff_ctx")

        gla_consts = (gup[l], gbias[l], lsm, gla_pm, bm, bm16)
        glf_lat, glb_lat, _ = _scan(_gla_step, z_gla, z_small, gla_consts, s0_gla[:, l], n_lat, n_tok, 0,
                                    GLA_SEQS_PER_STEP, "gla_scan")
        glf_ctx, glb_ctx, st_ctx = _scan(_gla_step, z_gla, z_small, gla_consts, zeros_state,
                                         n_ctx, l_ctx, lat_tokens, GLA_SEQS_PER_STEP, "gla_scan")

        qkv_n = _gdn_pre(z_gdn, edge, gdn_conv_w[l], bm16, tm)
        gdn_consts = (ea, eb, alog[l], dtb[l], lsm, lbt, pm, bm, bm16, sm)
        gdf_lat, gdb_lat, _ = _scan(_gdn_step, qkv_n, z_small, gdn_consts, s0_gdn[:, l], n_lat, n_tok, 0,
                                    GDN_SEQS_PER_STEP, "gdn_scan")
        gdf_ctx, gdb_ctx, s_ctx = _scan(_gdn_step, qkv_n, z_small, gdn_consts, zeros_state,
                                        n_ctx, l_ctx, lat_tokens, GDN_SEQS_PER_STEP, "gdn_scan")

        x = _merge(xs, tokens, mod[l], norm1_w[l][None], (na_lat, na_ctx), (glf_lat, glf_ctx), (glb_lat, glb_ctx), z_gla,
                   (gdf_lat, gdf_ctx), (gdb_lat, gdb_ctx), z_gdn, (df_lat, df_ctx), hn[l], bm16,
                   w_gate, w_branch16, w_out16, tm, mod_row, lam_init, lat_tokens // tm)
        if l < depth - 1:
            x = _moe(x, mod[l], norm2_w[l][None], wr, br, w1, w3, w2, tm_moe, mod_row_moe)
            xs = (x, x, lat_tokens // tm, lat_tokens // tm)
        else:
            y_lat, y_ctx = _moe(x, mod[l], norm2_w[l][None], wr, br, w1, w3, w2, tm_moe, mod_row_moe,
                                final_norm_w[None], lat_tokens // tm_moe)

        new_kv.append((z_na[lat_tokens:, MIX_W:], z_diff[lat_tokens:, MIX_W:]))
        new_gla.append(st_ctx)
        new_gdn.append(s_ctx)

    y_sample = y_lat.reshape(n_lat, n_tok, D_MODEL)
    y_prompt = y_ctx.reshape(n_ctx, l_ctx, D_MODEL)
    def ctx_kv(j, col):
        kv = jnp.stack([t[j] for t in new_kv])[:, :, col * MIX_W:(col + 1) * MIX_W]
        return kv.reshape(depth, n_ctx, l_ctx, HEADS, HEAD_DIM).transpose(1, 0, 2, 3, 4)
    new_state_gla = _diag_blocks(jnp.stack(new_gla, axis=1), True)
    new_state_gdn = _diag_blocks(jnp.stack(new_gdn, axis=1), False)
    return (y_prompt, y_sample, ctx_kv(0, 0), ctx_kv(0, 1), ctx_kv(1, 0), ctx_kv(1, 1), new_state_gla, new_state_gdn)
```

```python
import functools
import math

import numpy as np
import jax
import jax.numpy as jnp
from jax import lax
from jax.experimental import pallas as pl
from jax.experimental.pallas import tpu as pltpu

F32 = jnp.float32
BF16 = jnp.bfloat16

D_MODEL = 1024
HEADS = 4
HEAD_DIM = 64
MIX_W = HEADS * HEAD_DIM
GRID_W = 64
NA_WIN_H = 8
NA_WIN_W = 16
GATE_RANK = 16
GATE_NORM = 16.0
CHUNK = 64
N_EXPERTS = 16
N_GROUPS = 4
GROUP_E = N_EXPERTS // N_GROUPS
D_FF = D_MODEL // 4
ROPE_BASE = 10000.0
EPS = 1e-6
NEG = -1e30
LOG2E = 1.4426950408889634
SMALL_W = 128
V7X_VMEM_LIMIT = 56 * 1024 * 1024
LEAF = 8
ATTN_HEAD_GROUP = 2
NA_ROWS_PER_STEP = 8
SCAN_SEQS_PER_STEP = 4
MOE_TILE = 1024

_IN_SPLITS = (
    ('na_q', MIX_W), ('na_k', MIX_W), ('na_v', MIX_W),
    ('gla_q', MIX_W), ('gla_k', MIX_W), ('gla_v', MIX_W),
    ('gla_gf', GATE_RANK), ('gla_gb', GATE_RANK), ('gla_og', MIX_W),
    ('gdn_qkv', 3 * MIX_W), ('gdn_af', HEADS), ('gdn_ab', HEADS),
    ('gdn_bf', HEADS), ('gdn_bb', HEADS), ('gdn_og', MIX_W),
    ('diff_q', MIX_W), ('diff_k', MIX_W), ('diff_v', MIX_W),
    ('branch_gate', 4 * D_MODEL),
)
_OFF = {}
_o = 0
for _n, _s in _IN_SPLITS:
    _OFF[_n] = (_o, _o + _s)
    _o += _s

def _params(sem):
    return pltpu.CompilerParams(dimension_semantics=sem, vmem_limit_bytes=V7X_VMEM_LIMIT)


def _dot(a, b):
    return jnp.dot(a, b, preferred_element_type=F32)


def _bdot(a, b):
    return _dot(a.astype(BF16), b.astype(BF16))


def _bdot_nt(a, b):
    return lax.dot_general(a.astype(BF16), b.astype(BF16), (((1,), (1,)), ((), ())),
                           preferred_element_type=F32)


def _bdot_tn(a, b):
    return lax.dot_general(a.astype(BF16), b.astype(BF16), (((0,), (0,)), ((), ())),
                           preferred_element_type=F32)


def _split(x):
    hi = x.astype(BF16)
    lo = (x - hi.astype(F32)).astype(BF16)
    return hi, lo


def _dot_rx(x, m):
    hi, lo = _split(x)
    return _dot(hi, m) + _dot(lo, m)


def _dot_lx(m, x):
    hi, lo = _split(x)
    return _dot(m, hi) + _dot(m, lo)


def _dot3(a, b):
    ah, al = _split(a)
    bh, bl = _split(b)
    return _dot(ah, bh) + _dot(ah, bl) + _dot(al, bh)


def _silu(x):
    return x * jax.nn.sigmoid(x)


def _softplus(x):
    return jnp.maximum(x, 0.0) + jnp.log(1.0 + jnp.exp(-jnp.abs(x)))


def _rms_rows(x, w):
    return x * lax.rsqrt(jnp.mean(x * x, axis=-1, keepdims=True) + EPS) * w


def _head_sum(x, bm16):
    return _dot_rx(x, bm16)


def _bd_rows(x, bm):
    return jnp.concatenate([x] * HEADS, axis=0) * bm


def _lane_mask(lo, width, n=MIX_W):
    lane = lax.broadcasted_iota(jnp.int32, (1, n), 1)
    return (lane >= lo) & (lane < lo + width)


def _gdn_consts():
    c = CHUNK
    i = np.arange(c)[:, None]
    t = np.arange(c)[None, :]
    lb = (t <= i).astype(np.float32)
    lbl = (t > i).astype(np.float32)
    tot = np.ones((16, c), np.float32)
    strict = (t < i).astype(np.float32)
    incl = (t <= i).astype(np.float32)
    rev = lambda a: a[::-1, ::-1]
    lsm = np.stack([np.concatenate([lb, lbl, tot], 0), np.concatenate([rev(lb), rev(lbl), tot], 0)])
    lbt = np.stack([np.tile(lb.T, (1, HEADS)), np.tile(rev(lb).T, (1, HEADS))])
    pm = np.stack([np.stack([np.tile(strict, (1, HEADS)), np.tile(incl, (1, HEADS))]),
                   np.stack([np.tile(rev(strict), (1, HEADS)), np.tile(rev(incl), (1, HEADS))])])
    return jnp.asarray(lsm, BF16), jnp.asarray(lbt, F32), jnp.asarray(pm, F32)


def _gla_pair_masks():
    i = np.arange(CHUNK)[:, None]
    j = np.arange(CHUNK)[None, :]
    nblk = CHUNK // LEAF
    out = []
    for d in range(2):
        later = (i // LEAF > j // LEAF) if d == 0 else (i // LEAF < j // LEAF)
        order = (j <= i) if d == 0 else (j >= i)
        far = [(j // LEAF == blk) & later for blk in range(nblk)]
        near = [(i // LEAF == j // LEAF) & (j % LEAF == r) & order for r in range(LEAF)]
        out.append(np.stack([np.tile(m.astype(np.float32), (1, HEADS)) for m in far + near]))
    return jnp.asarray(np.stack(out), F32)


def _block_mask():
    r = np.arange(MIX_W)
    return (r[:, None] // HEAD_DIM == r[None, :] // HEAD_DIM).astype(np.float32)


def _solve_masks():
    r = np.arange(MIX_W)[:, None]
    c = np.arange(MIX_W)[None, :]
    out = [(r // LEAF == c // LEAF)]
    s = LEAF
    while s < CHUNK:
        out.append((r // (2 * s) == c // (2 * s)) & (r // s != c // s))
        s *= 2
    return jnp.asarray(np.stack(out).astype(np.float32))


def _rope_consts(n_tok):
    t = np.arange(n_tok)
    pos = np.stack([t // GRID_W, t % GRID_W], 0).astype(np.float32)
    lane = np.arange(MIX_W)
    u = lane % 32
    axis = u // 16
    w = u % 16
    first = w < 8
    inv = ROPE_BASE ** (-(w % 8).astype(np.float32) / 8.0)
    ang = pos[axis, :].T * inv[None, :]
    cos = np.cos(ang)
    sin = np.sin(ang) * np.where(first, -1.0, 1.0)[None, :]
    partner = np.where(first, lane + 8, lane - 8)
    perm = np.zeros((MIX_W, MIX_W), np.float32)
    perm[partner, lane] = 1.0
    return jnp.asarray(cos, F32), jnp.asarray(sin, F32), jnp.asarray(perm, BF16)


def _ada_kernel(c_ref, w_ref, b_ref, o_ref):
    c = c_ref[...]
    o_ref[0] = _bdot(_silu(c), w_ref[0]) + b_ref[0]


def _ada(cond, w_ada, b_ada):
    depth, _, n = w_ada.shape
    tn = 1536
    return pl.pallas_call(
        _ada_kernel,
        grid=(depth, n // tn),
        in_specs=[pl.BlockSpec((8, D_MODEL), lambda l, j: (0, 0)),
                  pl.BlockSpec((1, D_MODEL, tn), lambda l, j: (l, 0, j)),
                  pl.BlockSpec((1, 1, tn), lambda l, j: (l, 0, j))],
        out_specs=pl.BlockSpec((1, 8, tn), lambda l, j: (l, 0, j)),
        out_shape=jax.ShapeDtypeStruct((depth, 8, n), F32),
        compiler_params=_params(("parallel", "parallel")),
        name="ada_mod",
    )(cond, w_ada, b_ada.reshape(depth, 1, n))


def _proj_kernel(xl_ref, xc_ref, mod_ref, nw_ref, w_ref, o_na, o_gla, o_gdn, o_diff, o_small, *, n_lat_tiles):
    mod = mod_ref[0]
    x = jnp.where(pl.program_id(0) < n_lat_tiles, xl_ref[...], xc_ref[...])
    h = _rms_rows(x, nw_ref[...]) * (1.0 + mod[1:2]) + mod[0:1]
    hb = h.astype(BF16)
    o_na[...] = _dot(hb, w_ref[:, 0:768])
    o_gla[...] = _dot(hb, w_ref[:, 768:1792])
    o_gdn[...] = _dot(hb, w_ref[:, 1792:2816])
    o_diff[...] = _dot(hb, w_ref[:, 2816:3584])
    o_small[...] = _dot(hb, w_ref[:, 3584:3712])


def _stream_specs(xs, tm):
    _, _, ctx_first, n_lat_tiles = xs
    return [pl.BlockSpec((tm, D_MODEL), lambda i: (jnp.minimum(i, n_lat_tiles - 1), 0)),
            pl.BlockSpec((tm, D_MODEL), lambda i: (ctx_first + jnp.maximum(i - n_lat_tiles, 0), 0))]


def _proj(xs, t, mod_l, nw, wcat, tm, mod_row):
    widths = (768, 1024, 1024, 768, SMALL_W)
    return pl.pallas_call(
        functools.partial(_proj_kernel, n_lat_tiles=xs[3]),
        grid=(t // tm,),
        in_specs=_stream_specs(xs, tm) + [
                  pl.BlockSpec((1, 6, D_MODEL), lambda i: (mod_row(i), 0, 0)),
                  pl.BlockSpec((1, D_MODEL), lambda i: (0, 0)),
                  pl.BlockSpec(wcat.shape, lambda i: (0, 0))],
        out_specs=[pl.BlockSpec((tm, w), lambda i: (i, 0)) for w in widths],
        out_shape=[jax.ShapeDtypeStruct((t, w), F32) for w in widths],
        compiler_params=_params(("parallel",)),
        name="in_proj",
    )(xs[0], xs[1], mod_l, nw, wcat)


def _attn_core(q, k, v, n_sub, lam):
    kb = k.astype(BF16)
    vb = v.astype(BF16)
    tq = q.shape[0]
    sub_w = HEAD_DIM // n_sub
    scale = sub_w ** -0.5 * LOG2E
    out = jnp.zeros(q.shape, F32)
    for h0 in range(0, HEADS, ATTN_HEAD_GROUP):
        maps = [(h, m) for h in range(h0, h0 + ATTN_HEAD_GROUP) for m in range(n_sub)]
        qs = jnp.concatenate(
            [(q * jnp.where(_lane_mask(h * HEAD_DIM + m * sub_w, sub_w), scale, 0.0)).astype(BF16)
             for h, m in maps], axis=0)
        s = lax.dot_general(qs, kb, (((1,), (1,)), ((), ())), preferred_element_type=F32)
        e = jnp.exp2(s - jnp.max(s, axis=-1, keepdims=True))
        inv = 1.0 / jnp.sum(e, axis=-1, keepdims=True)
        if n_sub == 1:
            p = e * inv
        else:
            p = jnp.concatenate(
                [e[(2 * i) * tq:(2 * i + 1) * tq] * inv[(2 * i) * tq:(2 * i + 1) * tq]
                 - e[(2 * i + 1) * tq:(2 * i + 2) * tq] * (lam * inv[(2 * i + 1) * tq:(2 * i + 2) * tq])
                 for i in range(ATTN_HEAD_GROUP)], axis=0)
        o_all = _dot(p.astype(BF16), vb)
        for i in range(ATTN_HEAD_GROUP):
            out = jnp.where(_lane_mask((h0 + i) * HEAD_DIM, HEAD_DIM), o_all[i * tq:(i + 1) * tq], out)
    return out


def _diff_lambda(dl, lam_init):
    a = jnp.sum(dl[0:1] * dl[1:2], axis=-1, keepdims=True)
    b = jnp.sum(dl[2:3] * dl[3:4], axis=-1, keepdims=True)
    return jnp.exp(a) - jnp.exp(b) + lam_init


def _attn_kernel(q_ref, k_ref, v_ref, dl_ref, o_ref, *, n_sub, lam_init):
    lam = _diff_lambda(dl_ref[...], lam_init) if n_sub == 2 else None
    o_ref[...] = _attn_core(q_ref[...], k_ref[...], v_ref[...], n_sub, lam)


def _attention(q_arr, q_col, k_arr, k_col, v_arr, v_col, dl, *, n_sub, lam_init, n_seq, lq, lk,
               q_base, k_base, tq, name):
    nq = lq // tq
    return pl.pallas_call(
        functools.partial(_attn_kernel, n_sub=n_sub, lam_init=lam_init),
        grid=(n_seq, nq),
        in_specs=[pl.BlockSpec((tq, MIX_W), lambda s, j: (q_base // tq + s * nq + j, q_col)),
                  pl.BlockSpec((lk, MIX_W), lambda s, j: (k_base // lk + s, k_col)),
                  pl.BlockSpec((lk, MIX_W), lambda s, j: (k_base // lk + s, v_col)),
                  pl.BlockSpec(dl.shape, lambda s, j: (0, 0))],
        out_specs=pl.BlockSpec((tq, MIX_W), lambda s, j: (s * nq + j, 0)),
        out_shape=jax.ShapeDtypeStruct((n_seq * lq, MIX_W), F32),
        compiler_params=_params(("parallel", "parallel")),
        name=name,
    )(q_arr, k_arr, v_arr, dl)


DIFF_KEY_CHUNK = 256
DIFF_HEAD_GROUP = 1
DIFF_Q_TILE = 256


def _lane_fold(x, op):
    out = x[:, 0:128]
    for t in range(1, x.shape[1] // 128):
        out = op(out, x[:, t * 128:(t + 1) * 128])
    return out


def _diff_group(h0, q, kt_ref, v_ref, lam, tkc):
    tq = q.shape[0]
    nck = kt_ref.shape[1] // tkc
    sub_w = HEAD_DIM // 2
    scale = sub_w ** -0.5 * LOG2E
    maps = [(h, m) for h in range(h0, h0 + DIFF_HEAD_GROUP) for m in range(2)]
    qs = jnp.concatenate(
        [(q * jnp.where(_lane_mask(h * HEAD_DIM + m * sub_w, sub_w), scale, 0.0)).astype(BF16) for h, m in maps],
        axis=0)
    s, mvec = [], None
    for c in range(nck):
        sc = _dot(qs, kt_ref[:, c * tkc:(c + 1) * tkc])
        s.append(sc)
        mc = _lane_fold(sc, jnp.maximum)
        mvec = mc if mvec is None else jnp.maximum(mvec, mc)
        yield
    mrun = jnp.max(mvec, axis=-1, keepdims=True)
    e, lvec = [], None
    for c in range(nck):
        ec = jnp.exp2(s[c] - mrun)
        e.append(ec)
        lc = _lane_fold(ec, jnp.add)
        lvec = lc if lvec is None else lvec + lc
        yield
    inv = 1.0 / jnp.sum(lvec, axis=-1, keepdims=True)
    acc = None
    for c in range(nck):
        a = jnp.concatenate(
            [e[c][(2 * i) * tq:(2 * i + 1) * tq] * inv[(2 * i) * tq:(2 * i + 1) * tq]
             - e[c][(2 * i + 1) * tq:(2 * i + 2) * tq] * (lam * inv[(2 * i + 1) * tq:(2 * i + 2) * tq])
             for i in range(DIFF_HEAD_GROUP)], axis=0)
        o = _dot(a.astype(BF16), v_ref[c * tkc:(c + 1) * tkc, :])
        acc = o if acc is None else acc + o
        if c < nck - 1:
            yield
    return acc


_DIFF_PASSES = 3


def _diff_lat_kernel(q_ref, kt_ref, v_ref, dl_ref, o_ref, *, lam_init, tkc):
    lam = _diff_lambda(dl_ref[...], lam_init)
    q = q_ref[...]
    tq = q.shape[0]
    nck = kt_ref.shape[1] // tkc
    n_grp = HEADS // DIFF_HEAD_GROUP
    groups = [_diff_group(g * DIFF_HEAD_GROUP, q, kt_ref, v_ref, lam, tkc) for g in range(n_grp)]
    res = [None] * n_grp
    for t in range(n_grp + _DIFF_PASSES - 1):
        active = [g for g in range(n_grp) if 0 <= t - g < _DIFF_PASSES]
        for _ in range(nck):
            for g in active:
                try:
                    next(groups[g])
                except StopIteration as done:
                    res[g] = done.value
    out = jnp.zeros(q.shape, F32)
    for h in range(HEADS):
        g, i = divmod(h, DIFF_HEAD_GROUP)
        out = jnp.where(_lane_mask(h * HEAD_DIM, HEAD_DIM), res[g][i * tq:(i + 1) * tq], out)
    o_ref[...] = out


def _diff_latent(q_r, kk_t, vv16, dl, lam_init, n_b, n_tok, lk, tq):
    tkc = DIFF_KEY_CHUNK if lk % DIFF_KEY_CHUNK == 0 else 128
    assert lk % tkc == 0
    nq = n_tok // tq
    return pl.pallas_call(
        functools.partial(_diff_lat_kernel, lam_init=lam_init, tkc=tkc),
        grid=(n_b, nq),
        in_specs=[pl.BlockSpec((tq, MIX_W), lambda b, j: (b * nq + j, 0)),
                  pl.BlockSpec((MIX_W, lk), lambda b, j: (b, 0)),
                  pl.BlockSpec((lk, MIX_W), lambda b, j: (b, 0)),
                  pl.BlockSpec(dl.shape, lambda b, j: (0, 0))],
        out_specs=pl.BlockSpec((tq, MIX_W), lambda b, j: (b * nq + j, 0)),
        out_shape=jax.ShapeDtypeStruct((n_b * n_tok, MIX_W), F32),
        compiler_params=_params(("parallel", "arbitrary")),
        name="diff_latent",
    )(q_r, kk_t, vv16, dl)


def _na_lat_kernel(q_ref, k_ref, v_ref, ck_ref, cv_ref, bias_ref, o_ref, *, rows, kh, rps):
    g = pl.program_id(1)
    ck = ck_ref[...].astype(BF16)
    cv = cv_ref[...].astype(BF16)
    scale = HEAD_DIM ** -0.5 * LOG2E
    head_scale = [jnp.where(_lane_mask(h * HEAD_DIM, HEAD_DIM), scale, 0.0) for h in range(HEADS)]
    hw = HEADS * GRID_W
    qs = []
    for j in range(rps):
        qj = q_ref[j * GRID_W:(j + 1) * GRID_W, :]
        qs.append(jnp.concatenate([(qj * hs).astype(BF16) for hs in head_scale], axis=0))
    s_ctx_all = lax.dot_general(jnp.concatenate(qs, axis=0), ck, (((1,), (1,)), ((), ())),
                                preferred_element_type=F32)
    p_ctx, o_loc = [], []
    for j in range(rps):
        r = g * rps + j
        start = jnp.clip(r - kh // 2, 0, rows - kh)
        cls = r - start
        ws = pl.multiple_of(start * GRID_W, GRID_W)
        kw = k_ref[pl.ds(ws, kh * GRID_W), :].astype(BF16)
        vw = v_ref[pl.ds(ws, kh * GRID_W), :].astype(BF16)
        bias = jnp.concatenate([bias_ref[h, pl.ds(cls, 1)][0] for h in range(HEADS)], axis=0)
        s_loc = lax.dot_general(qs[j], kw, (((1,), (1,)), ((), ())), preferred_element_type=F32) + bias
        s_ctx = s_ctx_all[j * hw:(j + 1) * hw]
        mx = jnp.maximum(jnp.max(s_loc, axis=-1, keepdims=True), jnp.max(s_ctx, axis=-1, keepdims=True))
        e_loc = jnp.exp2(s_loc - mx)
        e_ctx = jnp.exp2(s_ctx - mx)
        inv = 1.0 / (jnp.sum(e_loc, axis=-1, keepdims=True) + jnp.sum(e_ctx, axis=-1, keepdims=True))
        p_ctx.append((e_ctx * inv).astype(BF16))
        o_loc.append(_dot((e_loc * inv).astype(BF16), vw))
    o_ctx_all = _dot(jnp.concatenate(p_ctx, axis=0), cv)
    for j in range(rps):
        o_all = o_loc[j] + o_ctx_all[j * hw:(j + 1) * hw]
        out = jnp.zeros((GRID_W, MIX_W), F32)
        for h in range(HEADS):
            out = jnp.where(_lane_mask(h * HEAD_DIM, HEAD_DIM), o_all[h * GRID_W:(h + 1) * GRID_W], out)
        o_ref[j * GRID_W:(j + 1) * GRID_W, :] = out


def _na_bias_table(rpb, kh):
    cidx = np.arange(GRID_W)
    cls = np.arange(kh)
    drow = np.arange(kh)[None, :] - cls[:, None] + NA_WIN_H - 1
    col_start = np.clip(cidx - NA_WIN_W // 2, 0, GRID_W - NA_WIN_W)
    col_ok = (cidx[None, :] >= col_start[:, None]) & (cidx[None, :] < col_start[:, None] + NA_WIN_W)
    dcol = np.clip(cidx[None, :] - cidx[:, None], 1 - NA_WIN_W, NA_WIN_W - 1) + NA_WIN_W - 1
    onehot = (dcol.reshape(1, -1) == np.arange(2 * NA_WIN_W - 1)[:, None]).astype(np.float32)
    toep = jnp.dot(rpb.astype(F32).reshape(-1, 2 * NA_WIN_W - 1), onehot, precision=lax.Precision.HIGHEST)
    toep = toep.reshape(-1, 2 * NA_WIN_H - 1, GRID_W, GRID_W)
    bias = jnp.stack([toep[:, drow[c, 0]:drow[c, 0] + kh] for c in range(kh)], axis=1)
    bias = bias.transpose(0, 1, 3, 2, 4)
    bias = jnp.where(col_ok[:, None, :], bias * LOG2E, NEG)
    return bias.reshape(-1, kh, GRID_W, kh * GRID_W)


def _na_latent(z_na, ck, cv, bias, n_b, n_tok, p_len):
    rows = n_tok // GRID_W
    kh = min(NA_WIN_H, rows)
    rps = NA_ROWS_PER_STEP
    assert rows % rps == 0
    steps = rows // rps
    return pl.pallas_call(
        functools.partial(_na_lat_kernel, rows=rows, kh=kh, rps=rps),
        grid=(n_b, steps),
        in_specs=[pl.BlockSpec((rps * GRID_W, MIX_W), lambda b, r: (b * steps + r, 0)),
                  pl.BlockSpec((n_tok, MIX_W), lambda b, r: (b, 1)),
                  pl.BlockSpec((n_tok, MIX_W), lambda b, r: (b, 2)),
                  pl.BlockSpec((p_len, MIX_W), lambda b, r: (b, 0)),
                  pl.BlockSpec((p_len, MIX_W), lambda b, r: (b, 0)),
                  pl.BlockSpec(bias.shape, lambda b, r: (0, 0, 0, 0))],
        out_specs=pl.BlockSpec((rps * GRID_W, MIX_W), lambda b, r: (b * steps + r, 0)),
        out_shape=jax.ShapeDtypeStruct((n_b * n_tok, MIX_W), F32),
        compiler_params=_params(("parallel", "arbitrary")),
        name="na_latent",
    )(z_na, z_na, z_na, ck, cv, bias)


def _rope_kernel(q_ref, k_ref, cos_ref, sin_ref, p_ref, qo_ref, ko_ref):
    cos = cos_ref[...]
    sin = sin_ref[...]
    p = p_ref[...]
    q = q_ref[...]
    k = k_ref[...]
    qo_ref[...] = q * cos + _dot_rx(q, p) * sin
    ko_ref[...] = (k * cos + _dot_rx(k, p) * sin).T.astype(BF16)


def _rope(z_diff, cos, sin, perm, n_b, n_tok, tm):
    nt = n_tok // tm
    spec_t = pl.BlockSpec((tm, MIX_W), lambda i: (i % nt, 0))
    return pl.pallas_call(
        _rope_kernel,
        grid=(n_b * nt,),
        in_specs=[pl.BlockSpec((tm, MIX_W), lambda i: (i, 0)),
                  pl.BlockSpec((tm, MIX_W), lambda i: (i, 1)),
                  spec_t, spec_t,
                  pl.BlockSpec(perm.shape, lambda i: (0, 0))],
        out_specs=[pl.BlockSpec((tm, MIX_W), lambda i: (i, 0)),
                   pl.BlockSpec((MIX_W, tm), lambda i: (i // nt, i % nt))],
        out_shape=[jax.ShapeDtypeStruct((n_b * n_tok, MIX_W), F32),
                   jax.ShapeDtypeStruct((n_b * MIX_W, n_tok), BF16)],
        compiler_params=_params(("parallel",)),
        name="diff_rope",
    )(z_diff, z_diff, cos, sin, perm)


def _gdn_pre_kernel(x_ref, prev_ref, next_ref, edge_ref, w_ref, bm_ref, o_ref, *, tm):
    x = x_ref[...]
    w = w_ref[...]
    edge = edge_ref[...]
    row = lax.broadcasted_iota(jnp.int32, (tm, 1), 0)
    x_prev = jnp.where(row == 0, prev_ref[7:8, :], pltpu.roll(x, 1, 0)) * edge[:, 0:1]
    x_next = jnp.where(row == tm - 1, next_ref[0:1, :], pltpu.roll(x, tm - 1, 0)) * edge[:, 1:2]
    y = _silu(x_prev * w[0:1] + x * w[1:2] + x_next * w[2:3])
    bm16 = bm_ref[...]
    q = y[:, 0:MIX_W]
    k = y[:, MIX_W:2 * MIX_W]
    o_ref[:, 0:MIX_W] = q * lax.rsqrt(_head_sum(q * q, bm16) + EPS) * (HEAD_DIM ** -0.5)
    o_ref[:, MIX_W:2 * MIX_W] = k * lax.rsqrt(_head_sum(k * k, bm16) + EPS)
    o_ref[:, 2 * MIX_W:3 * MIX_W] = y[:, 2 * MIX_W:3 * MIX_W]


def _seq_edges(lat_tokens, n_tok, ctx_tokens, l_ctx):
    pos = np.concatenate([np.arange(lat_tokens) % n_tok, np.arange(ctx_tokens) % l_ctx])
    seq = np.concatenate([np.full(lat_tokens, n_tok), np.full(ctx_tokens, l_ctx)])
    edge = np.zeros((lat_tokens + ctx_tokens, 8), np.float32)
    edge[:, 0] = pos != 0
    edge[:, 1] = pos != seq - 1
    return jnp.asarray(edge)


def _gdn_pre(z_gdn, edge, conv_w, bm16, tm):
    t = z_gdn.shape[0]
    w3 = 3 * MIX_W
    nb8 = t // 8
    return pl.pallas_call(
        functools.partial(_gdn_pre_kernel, tm=tm),
        grid=(t // tm,),
        in_specs=[pl.BlockSpec((tm, w3), lambda i: (i, 0)),
                  pl.BlockSpec((8, w3), lambda i: (jnp.maximum(i * (tm // 8) - 1, 0), 0)),
                  pl.BlockSpec((8, w3), lambda i: (jnp.minimum((i + 1) * (tm // 8), nb8 - 1), 0)),
                  pl.BlockSpec((tm, 8), lambda i: (i, 0)),
                  pl.BlockSpec(conv_w.shape, lambda i: (0, 0)),
                  pl.BlockSpec(bm16.shape, lambda i: (0, 0))],
        out_specs=pl.BlockSpec((tm, w3), lambda i: (i, 0)),
        out_shape=jax.ShapeDtypeStruct((t, w3), F32),
        compiler_params=_params(("parallel",)),
        name="gdn_pre",
    )(z_gdn, z_gdn, z_gdn, edge, conv_w, bm16)


def _bd_rows16(x, bm16):
    return jnp.concatenate([x.astype(BF16)] * HEADS, axis=0) * bm16


def _gla_chain(q, k, v, zs, gup, gb, lsm, pm_ref, d, bm, bm16, st):
    nblk = CHUNK // LEAF
    q = q * (HEAD_DIM ** -0.5)
    gp = _bdot(zs, gup) + gb
    g = (jnp.minimum(gp, 0.0) - jnp.log(1.0 + jnp.exp(-jnp.abs(gp)))) * (LOG2E / GATE_NORM)
    yield
    e = _dot_lx(lsm, g)
    b = e[0:CHUNK]
    fb = jnp.exp2(b)
    fbl = jnp.exp2(e[CHUNK:2 * CHUNK])
    ftot = jnp.exp2(e[2 * CHUNK:2 * CHUNK + 1])
    yield
    edge = LEAF - 1 if d == 0 else 0
    blocks = range(nblk - 1) if d == 0 else range(1, nblk)
    b3 = b.reshape(nblk, LEAF, MIX_W)
    b_edge = jnp.broadcast_to(b3[:, edge:edge + 1, :], b3.shape).reshape(CHUNK, MIX_W)
    kf = jnp.exp2(jnp.minimum(b_edge - b, 0.0))
    q_far = jnp.concatenate(
        [(q * jnp.exp2(jnp.minimum(b - b[blk * LEAF + edge:blk * LEAF + edge + 1, :], 0.0))).astype(BF16)
         for blk in blocks], axis=0)
    s_far = lax.dot_general(q_far, _bd_rows16(k * kf, bm16), (((1,), (1,)), ((), ())),
                            preferred_element_type=F32)
    yield
    q_near = jnp.concatenate(
        [(q * jnp.exp2(jnp.minimum(
            b - jnp.broadcast_to(b3[:, r:r + 1, :], b3.shape).reshape(CHUNK, MIX_W), 0.0))).astype(BF16)
         for r in range(LEAF)], axis=0)
    s_near = lax.dot_general(q_near, _bd_rows16(k, bm16), (((1,), (1,)), ((), ())),
                             preferred_element_type=F32)
    yield
    att = None
    for n, blk in enumerate(blocks):
        term = s_far[n * CHUNK:(n + 1) * CHUNK] * pm_ref[d, blk]
        att = term if att is None else att + term
    for r in range(LEAF):
        att = att + s_near[r * CHUNK:(r + 1) * CHUNK] * pm_ref[d, nblk + r]
    yield
    o = _bdot_nt(q * fb, st) + _dot(att.astype(BF16), _bd_rows16(v, bm16))
    yield
    return o, st * ftot + _bdot_tn(v, k * fbl) * bm


def _row_pieces(s, d):
    out = []
    for h in range(HEADS):
        for blk in range(CHUNK // s):
            lo = h * CHUNK + blk * s
            out.append((lo, lo + s, blk % 2 == 1 - d))
    return out


def _tri_inverse(mbd, sm_ref, d):
    r = lax.broadcasted_iota(jnp.int32, (MIX_W, MIX_W), 0)
    c = lax.broadcasted_iota(jnp.int32, (MIX_W, MIX_W), 1)
    eye = jnp.where(r == c, 1.0, 0.0)
    md = mbd * sm_ref[0]
    m2 = _bdot(md, md)
    yield
    m4 = _bdot(m2, m2)
    t = eye - md
    t = t + _bdot(t, m2)
    yield
    t = t + _bdot(t, m4)
    s = LEAF
    for lvl in range(1, sm_ref.shape[0]):
        pieces = _row_pieces(s, d)
        take = lambda x: jnp.concatenate([x[lo:hi] for lo, hi, upd in pieces if upd], axis=0)
        yield
        t16 = t.astype(BF16)
        a = _dot(take(mbd * sm_ref[lvl]).astype(BF16), t16).astype(BF16)
        rows, n = [], 0
        for lo, hi, upd in pieces:
            rows.append(a[n:n + s] if upd else jnp.zeros((s, MIX_W), BF16))
            n += s if upd else 0
        yield
        b = _dot(take(t16), jnp.concatenate(rows, axis=0))
        rows, n = [], 0
        for lo, hi, upd in pieces:
            rows.append(t[lo:hi] - b[n:n + s] if upd else t[lo:hi])
            n += s if upd else 0
        t = jnp.concatenate(rows, axis=0)
        s *= 2
    return t


def _gdn_chain(q, k, v, zs, ea, eb, alog, dtb, lsm, lbt, pm_ref, d, bm, bm16, sm_ref, s):
    g = -jnp.exp(alog) * _softplus(_dot_rx(zs, ea) + dtb)
    beta = jax.nn.sigmoid(_dot_rx(zs, eb))
    yield
    e = _dot_lx(lsm, g)
    dcol = e[0:CHUNK]
    drow = _dot_lx(jnp.ones((8, CHUNK), BF16), g * lbt)[0:1]
    kb = k * beta
    s2 = lax.dot_general(jnp.concatenate([kb, q], axis=0).astype(BF16), _bd_rows16(k, bm16),
                         (((1,), (1,)), ((), ())), preferred_element_type=F32)
    yield
    dec = jnp.exp(jnp.where(pm_ref[d, 1] > 0.0, dcol - drow, NEG))
    m_cat = s2[0:CHUNK] * dec * pm_ref[d, 0]
    att = s2[CHUNK:2 * CHUNK] * dec
    t_inv = yield from _tri_inverse(_bd_rows(m_cat, bm), sm_ref, d)
    t16 = (t_inv[0:CHUNK] + t_inv[CHUNK:2 * CHUNK] + t_inv[2 * CHUNK:3 * CHUNK] + t_inv[3 * CHUNK:]).astype(BF16)
    fb = jnp.exp(dcol)
    fbl = jnp.exp(e[CHUNK:2 * CHUNK])
    ftot = jnp.exp(e[2 * CHUNK:2 * CHUNK + 1])
    yield
    u = _dot(t16, _bd_rows16(v * beta, bm16))
    w = _dot(t16, _bd_rows16(kb * fb, bm16))
    yield
    ws = _dot(jnp.concatenate([w.astype(BF16), (q * fb).astype(BF16)], axis=0), s.astype(BF16))
    v_new = u - ws[0:CHUNK]
    yield
    o = ws[CHUNK:2 * CHUNK] + _dot(att.astype(BF16), _bd_rows16(v_new, bm16))
    return o, s * ftot + lax.dot_general((k * fbl).astype(BF16), v_new.astype(BF16), (((0,), (0,)), ((), ())),
                                         preferred_element_type=F32) * bm


def _scan_kernel(*refs, chains, n_consts, nb, n_chunks):
    ns = len(chains)
    zs_refs, pos = refs[:nb * 2], nb * 2
    streams = []
    for i in range(ns):
        tok = refs[pos:pos + nb * 2 * 3]
        consts = refs[pos + nb * 2 * 3:pos + nb * 2 * 3 + n_consts[i]]
        streams.append((tok, consts, refs[pos + nb * 2 * 3 + n_consts[i]]))
        pos += nb * 2 * 3 + n_consts[i] + 1
    outs, scrs = refs[pos:pos + 3 * ns], refs[pos + 3 * ns:]
    n = pl.program_id(1)

    @pl.when(n == 0)
    def _():
        for i in range(ns):
            scrs[i][...] = streams[i][2][...]

    ids = [(i, j, d) for j in range(nb) for d in range(2) for i in range(ns)]
    gens = []
    for i, j, d in ids:
        tok, consts, _ = streams[i]
        q_ref, k_ref, v_ref = tok[(j * 2 + d) * 3:(j * 2 + d) * 3 + 3]
        gens.append(chains[i](q_ref[...], k_ref[...], v_ref[...], zs_refs[j * 2 + d][...], consts, d, scrs[i][j, d]))
    live = list(range(len(gens)))
    while live:
        for c in list(live):
            try:
                next(gens[c])
            except StopIteration as done:
                i, j, d = ids[c]
                o, s_new = done.value
                outs[3 * i + d][j] = o
                scrs[i][j, d] = s_new
                live.remove(c)

    @pl.when(n == n_chunks - 1)
    def _():
        for i in range(ns):
            outs[3 * i + 2][...] = scrs[i][...]


def _gla_step(q, k, v, zs, consts, d, st):
    gup_ref, gb_ref, lsm_ref, pm_ref, bm_ref, bm16_ref = consts
    return _gla_chain(q, k, v, zs, gup_ref[d], gb_ref[d], lsm_ref[d], pm_ref, d, bm_ref[...], bm16_ref[...], st)


def _gdn_step(q, k, v, zs, consts, d, s):
    ea_ref, eb_ref, alog_ref, dtb_ref, lsm_ref, lbt_ref, pm_ref, bm_ref, bm16_ref, sm_ref = consts
    return _gdn_chain(q, k, v, zs, ea_ref[d], eb_ref[d], alog_ref[d], dtb_ref[d], lsm_ref[d], lbt_ref[d],
                      pm_ref, d, bm_ref[...], bm16_ref[...], sm_ref, s)


def _scan(streams, z_small, n_seq, l_seq, base, seqs_per_step, name):
    nc = l_seq // CHUNK
    nb = seqs_per_step if n_seq % seqs_per_step == 0 else 1

    def chunk(n, d):
        return n + d * (nc - 1 - 2 * n)

    def tok_spec(j, d, col, width):
        return pl.BlockSpec((CHUNK, width),
                            lambda g, n: (base // CHUNK + (g * nb + j) * nc + chunk(n, d), col))

    state_spec = pl.BlockSpec((nb, 2, MIX_W, MIX_W), lambda g, n: (g, 0, 0, 0))
    in_specs = [tok_spec(j, d, 0, SMALL_W) for j in range(nb) for d in range(2)]
    args = [z_small] * (nb * 2)
    out_specs, out_shape, scratch = [], [], []
    for chain, qkv, consts, s0 in streams:
        for j in range(nb):
            for d in range(2):
                in_specs += [tok_spec(j, d, col, MIX_W) for col in range(3)]
                args += [qkv] * 3
        for cst in consts:
            in_specs.append(pl.BlockSpec(cst.shape, functools.partial(lambda g, n, nd: (0,) * nd, nd=cst.ndim)))
            args.append(cst)
        in_specs.append(state_spec)
        args.append(s0)
        out_specs += [pl.BlockSpec((nb, CHUNK, MIX_W), lambda g, n: (g, chunk(n, 0), 0)),
                      pl.BlockSpec((nb, CHUNK, MIX_W), lambda g, n: (g, chunk(n, 1), 0)),
                      state_spec]
        out_shape += [jax.ShapeDtypeStruct((n_seq, l_seq, MIX_W), F32),
                      jax.ShapeDtypeStruct((n_seq, l_seq, MIX_W), F32),
                      jax.ShapeDtypeStruct((n_seq, 2, MIX_W, MIX_W), F32)]
        scratch.append(pltpu.VMEM((nb, 2, MIX_W, MIX_W), F32))
    res = pl.pallas_call(
        functools.partial(_scan_kernel, chains=tuple(st[0] for st in streams),
                          n_consts=tuple(len(st[2]) for st in streams), nb=nb, n_chunks=nc),
        grid=(n_seq // nb, nc),
        in_specs=in_specs,
        out_specs=out_specs,
        out_shape=out_shape,
        scratch_shapes=scratch,
        compiler_params=_params(("parallel", "arbitrary")),
        name=name,
    )(*args)
    return [(res[3 * i].reshape(n_seq * l_seq, MIX_W), res[3 * i + 1].reshape(n_seq * l_seq, MIX_W), res[3 * i + 2])
            for i in range(len(streams))]


def _merge_kernel(xl_ref, xc_ref, mod_ref, n1_ref, na_l, na_c, glaf_l, glaf_c, glab_l, glab_c, glag_ref, gdnf_l, gdnf_c,
                  gdnb_l, gdnb_c, gdng_ref, df_l, df_c, hn_ref, bm_ref, wg_ref, wb_ref, wo_ref, o_ref, *,
                  lam_init, n_lat_tiles):
    is_lat = pl.program_id(0) < n_lat_tiles

    def pick(lat_ref, ctx_ref):
        return jnp.where(is_lat, lat_ref[...], ctx_ref[...])

    mod = mod_ref[0]
    x = pick(xl_ref, xc_ref)
    h = _rms_rows(x, n1_ref[...]) * (1.0 + mod[1:2]) + mod[0:1]
    hb = h.astype(BF16)
    bm16 = bm_ref[...]

    def head_norm(o, w):
        return o * lax.rsqrt(_head_sum(o * o, bm16) * (1.0 / HEAD_DIM) + EPS) * w

    branches = (
        pick(na_l, na_c),
        head_norm(pick(glaf_l, glaf_c) + pick(glab_l, glab_c), hn_ref[0]) * _silu(glag_ref[...]),
        head_norm(pick(gdnf_l, gdnf_c) + pick(gdnb_l, gdnb_c), hn_ref[1]) * _silu(gdng_ref[...]),
        head_norm(pick(df_l, df_c), hn_ref[2]) * (1.0 - lam_init),
    )
    merged = None
    for n, br in enumerate(branches):
        gate = jax.nn.sigmoid(_dot(hb, wg_ref[:, n * D_MODEL:(n + 1) * D_MODEL]))
        term = gate * _bdot(br, wb_ref[n])
        merged = term if merged is None else merged + term
    o_ref[...] = x + mod[2:3] * _bdot(merged, wo_ref[...])


def _merge(xs, t, mod_l, n1, na, gla_f, gla_b, z_gla, gdn_f, gdn_b, z_gdn, df, hn, bm16, wg, wb, wo, tm, mod_row,
           lam_init, n_lat_tiles):
    tok = lambda col: pl.BlockSpec((tm, MIX_W), lambda i: (i, col))
    lat = pl.BlockSpec((tm, MIX_W), lambda i: (jnp.minimum(i, n_lat_tiles - 1), 0))
    ctx = pl.BlockSpec((tm, MIX_W), lambda i: (jnp.maximum(i - n_lat_tiles, 0), 0))
    full = lambda a: pl.BlockSpec(a.shape, lambda i: (0,) * a.ndim)
    return pl.pallas_call(
        functools.partial(_merge_kernel, lam_init=lam_init, n_lat_tiles=n_lat_tiles),
        grid=(t // tm,),
        in_specs=_stream_specs(xs, tm) + [
                  pl.BlockSpec((1, 6, D_MODEL), lambda i: (mod_row(i), 0, 0)),
                  full(n1), lat, ctx, lat, ctx, lat, ctx, tok(3), lat, ctx, lat, ctx, tok(3), lat, ctx,
                  full(hn), full(bm16), full(wg), full(wb), full(wo)],
        out_specs=pl.BlockSpec((tm, D_MODEL), lambda i: (i, 0)),
        out_shape=jax.ShapeDtypeStruct((t, D_MODEL), F32),
        compiler_params=_params(("parallel",)),
        name="merge_out",
    )(xs[0], xs[1], mod_l, n1, *na, *gla_f, *gla_b, z_gla, *gdn_f, *gdn_b, z_gdn, *df, hn, bm16, wg, wb, wo)


def _route(h2, wr_ref, br_ref):
    s_t = jax.nn.sigmoid(_dot3(h2, wr_ref[...])).T
    s = [s_t[e:e + 1] for e in range(N_EXPERTS)]
    sel = [s[e] + br_ref[e] for e in range(N_EXPERTS)]
    best = None
    for g in range(N_GROUPS):
        v = sel[g * GROUP_E:(g + 1) * GROUP_E]
        score = None
        for i in range(GROUP_E):
            for j in range(i + 1, GROUP_E):
                score = v[i] + v[j] if score is None else jnp.maximum(score, v[i] + v[j])
        if best is None:
            best, gi = score, jnp.zeros_like(score)
        else:
            better = score > best
            gi = jnp.where(better, float(g), gi)
            best = jnp.where(better, score, best)
    vals = [jnp.where(gi == float(e // GROUP_E), sel[e], NEG) for e in range(N_EXPERTS)]

    def first_max(v):
        mx, idx = v[0], jnp.zeros_like(v[0])
        for e in range(1, N_EXPERTS):
            better = v[e] > mx
            idx = jnp.where(better, float(e), idx)
            mx = jnp.where(better, v[e], mx)
        return idx

    i1 = first_max(vals)
    i2 = first_max([jnp.where(i1 == float(e), NEG, vals[e]) for e in range(N_EXPERTS)])
    s1 = sum(jnp.where(i1 == float(e), s[e], 0.0) for e in range(N_EXPERTS))
    s2 = sum(jnp.where(i2 == float(e), s[e], 0.0) for e in range(N_EXPERTS))
    inv = 1.0 / (s1 + s2)
    rows = [jnp.where(i1 == float(e), s1 * inv, 0.0) + jnp.where(i2 == float(e), s2 * inv, 0.0)
            for e in range(N_EXPERTS)]
    gates_t = jnp.concatenate(rows + [jnp.zeros((SMALL_W - N_EXPERTS, h2.shape[0]), F32)], axis=0)
    return gates_t.T


def _moe_kernel(x_ref, mod_ref, n2_ref, wr_ref, br_ref, w1_ref, w3_ref, w2_ref, *rest, n_lat_tiles):
    if n_lat_tiles is None:
        o_ref, h_scr, g_scr, acc_scr = rest
    else:
        fw_ref, ol_ref, oc_ref, h_scr, g_scr, acc_scr = rest
    g = pl.program_id(1)
    mod = mod_ref[0]

    @pl.when(g == 0)
    def _():
        h2 = _rms_rows(x_ref[...], n2_ref[...]) * (1.0 + mod[4:5]) + mod[3:4]
        h_scr[...] = h2.astype(BF16)
        g_scr[...] = _route(h2, wr_ref, br_ref)
        acc_scr[...] = jnp.zeros_like(acc_scr)

    hb = h_scr[...]
    gates = g_scr[...]
    lane = lax.broadcasted_iota(jnp.int32, (1, SMALL_W), 1)
    he = jnp.concatenate(
        [_silu(_dot(hb, w1_ref[e])) * _dot(hb, w3_ref[e])
         * jnp.sum(jnp.where(lane == g * GROUP_E + e, gates, 0.0), axis=-1, keepdims=True)
         for e in range(GROUP_E)], axis=1)
    acc_scr[...] += _dot(he.astype(BF16), w2_ref[...])

    if n_lat_tiles is None:
        @pl.when(g == N_GROUPS - 1)
        def _():
            o_ref[...] = x_ref[...] + mod[5:6] * acc_scr[...]
    else:
        is_lat = pl.program_id(0) < n_lat_tiles

        @pl.when((g == N_GROUPS - 1) & is_lat)
        def _():
            ol_ref[...] = _rms_rows(x_ref[...] + mod[5:6] * acc_scr[...], fw_ref[...])

        @pl.when((g == N_GROUPS - 1) & jnp.logical_not(is_lat))
        def _():
            oc_ref[...] = _rms_rows(x_ref[...] + mod[5:6] * acc_scr[...], fw_ref[...])


def _moe(x, mod_l, n2, wr, br, w1, w3, w2, tm, mod_row, final_w=None, n_lat_tiles=None):
    t = x.shape[0]
    gw = GROUP_E * D_FF
    in_specs = [pl.BlockSpec((tm, D_MODEL), lambda i, g: (i, 0)),
                pl.BlockSpec((1, 6, D_MODEL), lambda i, g: (mod_row(i), 0, 0)),
                pl.BlockSpec(n2.shape, lambda i, g: (0, 0)),
                pl.BlockSpec(wr.shape, lambda i, g: (0, 0)),
                pl.BlockSpec(memory_space=pltpu.SMEM),
                pl.BlockSpec((GROUP_E, D_MODEL, D_FF), lambda i, g: (g, 0, 0)),
                pl.BlockSpec((GROUP_E, D_MODEL, D_FF), lambda i, g: (g, 0, 0)),
                pl.BlockSpec((gw, D_MODEL), lambda i, g: (g, 0))]
    args = [x, mod_l, n2, wr, br, w1, w3, w2]
    if final_w is None:
        out_specs = pl.BlockSpec((tm, D_MODEL), lambda i, g: (i, 0))
        out_shape = jax.ShapeDtypeStruct((t, D_MODEL), F32)
    else:
        in_specs.append(pl.BlockSpec(final_w.shape, lambda i, g: (0, 0)))
        args.append(final_w)
        out_specs = [pl.BlockSpec((tm, D_MODEL), lambda i, g: (jnp.minimum(i, n_lat_tiles - 1), 0)),
                     pl.BlockSpec((tm, D_MODEL), lambda i, g: (jnp.maximum(i - n_lat_tiles, 0), 0))]
        out_shape = [jax.ShapeDtypeStruct((n_lat_tiles * tm, D_MODEL), F32),
                     jax.ShapeDtypeStruct((t - n_lat_tiles * tm, D_MODEL), F32)]
    return pl.pallas_call(
        functools.partial(_moe_kernel, n_lat_tiles=n_lat_tiles),
        grid=(t // tm, N_GROUPS),
        in_specs=in_specs,
        out_specs=out_specs,
        out_shape=out_shape,
        scratch_shapes=[pltpu.VMEM((tm, D_MODEL), BF16), pltpu.VMEM((tm, SMALL_W), F32),
                        pltpu.VMEM((tm, D_MODEL), F32)],
        compiler_params=_params(("arbitrary", "arbitrary")),
        name="moe",
    )(*args)


def _block_diag_state(s, transpose):
    if transpose:
        s = jnp.swapaxes(s, -1, -2)
    eye = jnp.eye(HEADS, dtype=s.dtype)
    out = jnp.einsum('...hij,hg->...higj', s, eye)
    return out.reshape(s.shape[:-3] + (MIX_W, MIX_W))


def _diag_blocks(s_bd, transpose):
    s = s_bd.reshape(s_bd.shape[:-2] + (HEADS, HEAD_DIM, HEADS, HEAD_DIM))
    s = jnp.stack([s[..., h, :, h, :] for h in range(HEADS)], axis=-3)
    return jnp.swapaxes(s, -1, -2) if transpose else s


def _pick_tile(*lengths):
    for tm in (512, 256, 128, 64):
        if all(n % tm == 0 for n in lengths):
            return tm
    raise ValueError("token counts must be multiples of 64")


def kernel(x_prompt, x_sample, cache_na_k, cache_na_v, cache_diff_k, cache_diff_v, state_gla, state_gdn, c, c_ctx, w_ada, b_ada, norm1_w, norm2_w, w_in, na_rpb, gla_gate_up, gla_gate_bias, gla_norm_w, gdn_conv_w, gdn_A_log, gdn_dt_bias, gdn_norm_w, diff_lambda, diff_subln_w, w_branch, w_out, w_router, b_router, w_e1, w_e3, w_e2, final_norm_w):
    n_ctx, l_ctx, _ = x_prompt.shape
    n_lat, n_tok, _ = x_sample.shape
    depth = w_ada.shape[0]
    p_len = cache_na_k.shape[2]
    lat_tokens = n_lat * n_tok
    ctx_tokens = n_ctx * l_ctx
    assert n_lat + 1 <= 8 and n_tok % l_ctx == 0 and n_tok % GRID_W == 0 and l_ctx % CHUNK == 0
    tm = _pick_tile(n_tok, ctx_tokens)

    def mod_row(i):
        return jnp.where(i * tm < lat_tokens, (i * tm) // n_tok, n_lat)

    tm_moe = MOE_TILE if (n_tok % MOE_TILE == 0 and ctx_tokens % MOE_TILE == 0) else tm

    def mod_row_moe(i):
        return jnp.where(i * tm_moe < lat_tokens, (i * tm_moe) // n_tok, n_lat)

    tokens = lat_tokens + ctx_tokens
    xs = (x_sample.reshape(lat_tokens, D_MODEL), x_prompt.reshape(ctx_tokens, D_MODEL), 0, lat_tokens // tm)
    cond = jnp.zeros((8, D_MODEL), F32).at[:n_lat].set(c).at[n_lat].set(c_ctx)
    mod = _ada(cond, w_ada, b_ada).reshape(depth, 8, 6, D_MODEL)

    gla_pm = _gla_pair_masks()
    lsm, lbt, pm = _gdn_consts()
    bm = jnp.asarray(_block_mask(), F32)
    bm16 = bm.astype(BF16)
    sm = _solve_masks()
    cos, sin, perm = _rope_consts(n_tok)
    edge = _seq_edges(lat_tokens, n_tok, ctx_tokens, l_ctx)
    lane = np.arange(MIX_W)
    small_rows = np.arange(SMALL_W)
    def expand_cols(first_col):
        return jnp.asarray((small_rows[:, None] == first_col + lane[None, :] // HEAD_DIM).astype(np.float32), BF16)
    ea = jnp.stack([expand_cols(32), expand_cols(36)])
    eb = jnp.stack([expand_cols(40), expand_cols(44)])
    wr = jnp.zeros((D_MODEL, SMALL_W), F32).at[:, :N_EXPERTS].set(w_router)
    br = b_router.astype(F32)

    def w_layer(l):
        wl = w_in[l]
        parts = [wl[:, _OFF[n][0]:_OFF[n][1]] for n in (
            'na_q', 'na_k', 'na_v', 'gla_q', 'gla_k', 'gla_v', 'gla_og', 'gdn_qkv', 'gdn_og', 'diff_q', 'diff_k',
            'diff_v', 'gla_gf', 'gla_gb', 'gdn_af', 'gdn_ab', 'gdn_bf', 'gdn_bb')]
        parts.append(jnp.zeros((D_MODEL, SMALL_W - 2 * GATE_RANK - 4 * HEADS), F32))
        return (jnp.concatenate(parts, axis=1).astype(BF16), wl[:, _OFF['branch_gate'][0]:].astype(BF16),
                w_branch[l].astype(BF16), w_out[l].astype(BF16), w_e1[l].astype(BF16), w_e3[l].astype(BF16),
                w_e2[l].reshape(N_EXPERTS * D_FF, D_MODEL).astype(BF16))

    gup = jnp.zeros((depth, 2, SMALL_W, MIX_W), F32)
    gup = gup.at[:, 0, 0:GATE_RANK].set(gla_gate_up[:, 0]).at[:, 1, GATE_RANK:2 * GATE_RANK].set(gla_gate_up[:, 1])
    gbias = gla_gate_bias[:, :, None, :]
    alog = jnp.repeat(gdn_A_log, HEAD_DIM, axis=-1)[:, :, None, :]
    dtb = jnp.repeat(gdn_dt_bias, HEAD_DIM, axis=-1)[:, :, None, :]
    hn = jnp.stack([jnp.tile(gla_norm_w, (1, HEADS)), jnp.tile(gdn_norm_w, (1, HEADS)),
                    jnp.tile(diff_subln_w, (1, HEADS))], axis=1)[:, :, None, :]
    kh = min(NA_WIN_H, n_tok // GRID_W)
    na_bias = _na_bias_table(na_rpb.reshape(depth * HEADS, 2 * NA_WIN_H - 1, 2 * NA_WIN_W - 1), kh)
    na_bias = na_bias.reshape(depth, HEADS, kh, GRID_W, kh * GRID_W)

    def cache_rows(cache):
        return cache.transpose(1, 0, 2, 3, 4).reshape(depth, n_lat * p_len, MIX_W)

    ck_na, cv_na, cv_df = (cache_rows(t) for t in (cache_na_k, cache_na_v, cache_diff_v))
    ck_df_t = cache_diff_k.reshape(n_lat, depth, p_len, MIX_W).transpose(1, 0, 3, 2).astype(BF16)
    s0_gla = _block_diag_state(state_gla, True)
    s0_gdn = _block_diag_state(state_gdn, False)
    zeros_state = jnp.zeros((n_ctx, 2, MIX_W, MIX_W), F32)
    lkv = n_tok + p_len
    tq_diff = DIFF_Q_TILE if n_tok % DIFF_Q_TILE == 0 else CHUNK

    new_kv, new_gla, new_gdn = [], [], []
    for l in range(depth):
        lam_init = 0.8 - 0.6 * math.exp(-0.3 * l)
        wcat, w_gate, w_branch16, w_out16, w1, w3, w2 = w_layer(l)
        z_na, z_gla, z_gdn, z_diff, z_small = _proj(xs, tokens, mod[l], norm1_w[l][None], wcat, tm, mod_row)

        dl = diff_lambda[l]
        na_lat = _na_latent(z_na, ck_na[l], cv_na[l], na_bias[l], n_lat, n_tok, p_len)
        na_ctx = _attention(z_na, 0, z_na, 1, z_na, 2, dl, n_sub=1, lam_init=0.0, n_seq=n_ctx, lq=l_ctx,
                            lk=l_ctx, q_base=lat_tokens, k_base=lat_tokens, tq=l_ctx, name="na_ctx")

        q_r, k_rt = _rope(z_diff, cos, sin, perm, n_lat, n_tok, tm)
        kk_t = jnp.concatenate([k_rt.reshape(n_lat, MIX_W, n_tok), ck_df_t[l]], 2).reshape(n_lat * MIX_W, lkv)
        vv = jnp.concatenate([z_diff[:lat_tokens, 2 * MIX_W:].reshape(n_lat, n_tok, MIX_W),
                              cv_df[l].reshape(n_lat, p_len, MIX_W)], 1)
        df_lat = _diff_latent(q_r, kk_t, vv.astype(BF16).reshape(n_lat * lkv, MIX_W), dl, lam_init,
                              n_lat, n_tok, lkv, tq_diff)
        df_ctx = _attention(z_diff, 0, z_diff, 1, z_diff, 2, dl, n_sub=2, lam_init=lam_init, n_seq=n_ctx,
                            lq=l_ctx, lk=l_ctx, q_base=lat_tokens, k_base=lat_tokens, tq=l_ctx, name="diff_ctx")

        qkv_n = _gdn_pre(z_gdn, edge, gdn_conv_w[l], bm16, tm)
        gla_consts = (gup[l], gbias[l], lsm, gla_pm, bm, bm16)
        gdn_consts = (ea, eb, alog[l], dtb[l], lsm, lbt, pm, bm, bm16, sm)
        (glf_lat, glb_lat, _), (gdf_lat, gdb_lat, _) = _scan(
            [(_gla_step, z_gla, gla_consts, s0_gla[:, l]), (_gdn_step, qkv_n, gdn_consts, s0_gdn[:, l])],
            z_small, n_lat, n_tok, 0, SCAN_SEQS_PER_STEP, "gla_gdn_scan")
        (glf_ctx, glb_ctx, st_ctx), (gdf_ctx, gdb_ctx, s_ctx) = _scan(
            [(_gla_step, z_gla, gla_consts, zeros_state), (_gdn_step, qkv_n, gdn_consts, zeros_state)],
            z_small, n_ctx, l_ctx, lat_tokens, SCAN_SEQS_PER_STEP, "gla_gdn_scan")

        x = _merge(xs, tokens, mod[l], norm1_w[l][None], (na_lat, na_ctx), (glf_lat, glf_ctx), (glb_lat, glb_ctx), z_gla,
                   (gdf_lat, gdf_ctx), (gdb_lat, gdb_ctx), z_gdn, (df_lat, df_ctx), hn[l], bm16,
                   w_gate, w_branch16, w_out16, tm, mod_row, lam_init, lat_tokens // tm)
        if l < depth - 1:
            x = _moe(x, mod[l], norm2_w[l][None], wr, br, w1, w3, w2, tm_moe, mod_row_moe)
            xs = (x, x, lat_tokens // tm, lat_tokens // tm)
        else:
            y_lat, y_ctx = _moe(x, mod[l], norm2_w[l][None], wr, br, w1, w3, w2, tm_moe, mod_row_moe,
                                final_norm_w[None], lat_tokens // tm_moe)

        new_kv.append((z_na[lat_tokens:, MIX_W:], z_diff[lat_tokens:, MIX_W:]))
        new_gla.append(st_ctx)
        new_gdn.append(s_ctx)

    y_sample = y_lat.reshape(n_lat, n_tok, D_MODEL)
    y_prompt = y_ctx.reshape(n_ctx, l_ctx, D_MODEL)
    def ctx_kv(j, col):
        kv = jnp.stack([t[j] for t in new_kv])[:, :, col * MIX_W:(col + 1) * MIX_W]
        return kv.reshape(depth, n_ctx, l_ctx, HEADS, HEAD_DIM).transpose(1, 0, 2, 3, 4)
    new_state_gla = _diag_blocks(jnp.stack(new_gla, axis=1), True)
    new_state_gdn = _diag_blocks(jnp.stack(new_gdn, axis=1), False)
    return (y_prompt, y_sample, ctx_kv(0, 0), ctx_kv(0, 1), ctx_kv(1, 0), ctx_kv(1, 1), new_state_gla, new_state_gdn)
```

```python
import functools
import math

import numpy as np
import jax
import jax.numpy as jnp
from jax import lax
from jax.experimental import pallas as pl
from jax.experimental.pallas import tpu as pltpu

F32 = jnp.float32
BF16 = jnp.bfloat16

D_MODEL = 1024
HEADS = 4
HEAD_DIM = 64
MIX_W = HEADS * HEAD_DIM
GRID_W = 64
NA_WIN_H = 8
NA_WIN_W = 16
GATE_RANK = 16
GATE_NORM = 16.0
CHUNK = 64
N_EXPERTS = 16
N_GROUPS = 4
GROUP_E = N_EXPERTS // N_GROUPS
D_FF = D_MODEL // 4
ROPE_BASE = 10000.0
EPS = 1e-6
NEG = -1e30
LOG2E = 1.4426950408889634
SMALL_W = 128
V7X_VMEM_LIMIT = 56 * 1024 * 1024
LEAF = 8
ATTN_HEAD_GROUP = 2
NA_ROWS_PER_STEP = 8
SCAN_SEQS_PER_STEP = 4
MOE_TILE = 1024

_IN_SPLITS = (
    ('na_q', MIX_W), ('na_k', MIX_W), ('na_v', MIX_W),
    ('gla_q', MIX_W), ('gla_k', MIX_W), ('gla_v', MIX_W),
    ('gla_gf', GATE_RANK), ('gla_gb', GATE_RANK), ('gla_og', MIX_W),
    ('gdn_qkv', 3 * MIX_W), ('gdn_af', HEADS), ('gdn_ab', HEADS),
    ('gdn_bf', HEADS), ('gdn_bb', HEADS), ('gdn_og', MIX_W),
    ('diff_q', MIX_W), ('diff_k', MIX_W), ('diff_v', MIX_W),
    ('branch_gate', 4 * D_MODEL),
)
_OFF = {}
_o = 0
for _n, _s in _IN_SPLITS:
    _OFF[_n] = (_o, _o + _s)
    _o += _s

def _params(sem):
    return pltpu.CompilerParams(dimension_semantics=sem, vmem_limit_bytes=V7X_VMEM_LIMIT)


def _dot(a, b):
    return jnp.dot(a, b, preferred_element_type=F32)


def _bdot(a, b):
    return _dot(a.astype(BF16), b.astype(BF16))


def _bdot_nt(a, b):
    return lax.dot_general(a.astype(BF16), b.astype(BF16), (((1,), (1,)), ((), ())),
                           preferred_element_type=F32)


def _bdot_tn(a, b):
    return lax.dot_general(a.astype(BF16), b.astype(BF16), (((0,), (0,)), ((), ())),
                           preferred_element_type=F32)


def _split(x):
    hi = x.astype(BF16)
    lo = (x - hi.astype(F32)).astype(BF16)
    return hi, lo


def _dot_rx(x, m):
    hi, lo = _split(x)
    return _dot(hi, m) + _dot(lo, m)


def _dot_lx(m, x):
    hi, lo = _split(x)
    return _dot(m, hi) + _dot(m, lo)


def _dot3(a, b):
    ah, al = _split(a)
    bh, bl = _split(b)
    return _dot(ah, bh) + _dot(ah, bl) + _dot(al, bh)


def _silu(x):
    return x * jax.nn.sigmoid(x)


def _softplus(x):
    return jnp.maximum(x, 0.0) + jnp.log(1.0 + jnp.exp(-jnp.abs(x)))


def _rms_rows(x, w):
    return x * lax.rsqrt(jnp.mean(x * x, axis=-1, keepdims=True) + EPS) * w


def _head_sum(x, bm16):
    return _dot_rx(x, bm16)


def _bd_rows(x, bm):
    return jnp.concatenate([x] * HEADS, axis=0) * bm


def _lane_mask(lo, width, n=MIX_W):
    lane = lax.broadcasted_iota(jnp.int32, (1, n), 1)
    return (lane >= lo) & (lane < lo + width)


def _gdn_consts():
    c = CHUNK
    i = np.arange(c)[:, None]
    t = np.arange(c)[None, :]
    lb = (t <= i).astype(np.float32)
    lbl = (t > i).astype(np.float32)
    tot = np.ones((16, c), np.float32)
    strict = (t < i).astype(np.float32)
    incl = (t <= i).astype(np.float32)
    rev = lambda a: a[::-1, ::-1]
    lsm = np.stack([np.concatenate([lb, lbl, tot], 0), np.concatenate([rev(lb), rev(lbl), tot], 0)])
    lbt = np.stack([np.tile(lb.T, (1, HEADS)), np.tile(rev(lb).T, (1, HEADS))])
    pm = np.stack([np.stack([np.tile(strict, (1, HEADS)), np.tile(incl, (1, HEADS))]),
                   np.stack([np.tile(rev(strict), (1, HEADS)), np.tile(rev(incl), (1, HEADS))])])
    return jnp.asarray(lsm, BF16), jnp.asarray(lbt, F32), jnp.asarray(pm, F32)


def _gla_pair_masks():
    i = np.arange(CHUNK)[:, None]
    j = np.arange(CHUNK)[None, :]
    nblk = CHUNK // LEAF
    out = []
    for d in range(2):
        later = (i // LEAF > j // LEAF) if d == 0 else (i // LEAF < j // LEAF)
        order = (j <= i) if d == 0 else (j >= i)
        far = [(j // LEAF == blk) & later for blk in range(nblk)]
        near = [(i // LEAF == j // LEAF) & (j % LEAF == r) & order for r in range(LEAF)]
        out.append(np.stack([np.tile(m.astype(np.float32), (1, HEADS)) for m in far + near]))
    return jnp.asarray(np.stack(out), F32)


def _block_mask():
    r = np.arange(MIX_W)
    return (r[:, None] // HEAD_DIM == r[None, :] // HEAD_DIM).astype(np.float32)


def _solve_masks():
    r = np.arange(MIX_W)[:, None]
    c = np.arange(MIX_W)[None, :]
    out = [(r // LEAF == c // LEAF)]
    s = LEAF
    while s < CHUNK:
        out.append((r // (2 * s) == c // (2 * s)) & (r // s != c // s))
        s *= 2
    return jnp.asarray(np.stack(out).astype(np.float32))


def _rope_consts(n_tok):
    t = np.arange(n_tok)
    pos = np.stack([t // GRID_W, t % GRID_W], 0).astype(np.float32)
    lane = np.arange(MIX_W)
    u = lane % 32
    axis = u // 16
    w = u % 16
    first = w < 8
    inv = ROPE_BASE ** (-(w % 8).astype(np.float32) / 8.0)
    ang = pos[axis, :].T * inv[None, :]
    cos = np.cos(ang)
    sin = np.sin(ang) * np.where(first, -1.0, 1.0)[None, :]
    partner = np.where(first, lane + 8, lane - 8)
    perm = np.zeros((MIX_W, MIX_W), np.float32)
    perm[partner, lane] = 1.0
    return jnp.asarray(cos, F32), jnp.asarray(sin, F32), jnp.asarray(perm, BF16)


def _ada_kernel(c_ref, w_ref, b_ref, o_ref):
    c = c_ref[...]
    o_ref[0] = _bdot(_silu(c), w_ref[0]) + b_ref[0]


def _ada(cond, w_ada, b_ada):
    depth, _, n = w_ada.shape
    tn = 1536
    return pl.pallas_call(
        _ada_kernel,
        grid=(depth, n // tn),
        in_specs=[pl.BlockSpec((8, D_MODEL), lambda l, j: (0, 0)),
                  pl.BlockSpec((1, D_MODEL, tn), lambda l, j: (l, 0, j)),
                  pl.BlockSpec((1, 1, tn), lambda l, j: (l, 0, j))],
        out_specs=pl.BlockSpec((1, 8, tn), lambda l, j: (l, 0, j)),
        out_shape=jax.ShapeDtypeStruct((depth, 8, n), F32),
        compiler_params=_params(("parallel", "parallel")),
        name="ada_mod",
    )(cond, w_ada, b_ada.reshape(depth, 1, n))


def _proj_kernel(xl_ref, xc_ref, mod_ref, nw_ref, w_ref, o_na, o_gla, o_gdn, o_diff, o_small, *, n_lat_tiles):
    mod = mod_ref[0]
    x = jnp.where(pl.program_id(0) < n_lat_tiles, xl_ref[...], xc_ref[...])
    h = _rms_rows(x, nw_ref[...]) * (1.0 + mod[1:2]) + mod[0:1]
    hb = h.astype(BF16)
    o_na[...] = _dot(hb, w_ref[:, 0:768])
    o_gla[...] = _dot(hb, w_ref[:, 768:1792])
    o_gdn[...] = _dot(hb, w_ref[:, 1792:2816])
    o_diff[...] = _dot(hb, w_ref[:, 2816:3584])
    o_small[...] = _dot(hb, w_ref[:, 3584:3712])


def _stream_specs(xs, tm):
    _, _, ctx_first, n_lat_tiles = xs
    return [pl.BlockSpec((tm, D_MODEL), lambda i: (jnp.minimum(i, n_lat_tiles - 1), 0)),
            pl.BlockSpec((tm, D_MODEL), lambda i: (ctx_first + jnp.maximum(i - n_lat_tiles, 0), 0))]


def _proj(xs, t, mod_l, nw, wcat, tm, mod_row):
    widths = (768, 1024, 1024, 768, SMALL_W)
    return pl.pallas_call(
        functools.partial(_proj_kernel, n_lat_tiles=xs[3]),
        grid=(t // tm,),
        in_specs=_stream_specs(xs, tm) + [
                  pl.BlockSpec((1, 6, D_MODEL), lambda i: (mod_row(i), 0, 0)),
                  pl.BlockSpec((1, D_MODEL), lambda i: (0, 0)),
                  pl.BlockSpec(wcat.shape, lambda i: (0, 0))],
        out_specs=[pl.BlockSpec((tm, w), lambda i: (i, 0)) for w in widths],
        out_shape=[jax.ShapeDtypeStruct((t, w), F32) for w in widths],
        compiler_params=_params(("parallel",)),
        name="in_proj",
    )(xs[0], xs[1], mod_l, nw, wcat)


def _attn_core(q, k, v, n_sub, lam):
    kb = k.astype(BF16)
    vb = v.astype(BF16)
    tq = q.shape[0]
    sub_w = HEAD_DIM // n_sub
    scale = sub_w ** -0.5 * LOG2E
    out = jnp.zeros(q.shape, F32)
    for h0 in range(0, HEADS, ATTN_HEAD_GROUP):
        maps = [(h, m) for h in range(h0, h0 + ATTN_HEAD_GROUP) for m in range(n_sub)]
        qs = jnp.concatenate(
            [(q * jnp.where(_lane_mask(h * HEAD_DIM + m * sub_w, sub_w), scale, 0.0)).astype(BF16)
             for h, m in maps], axis=0)
        s = lax.dot_general(qs, kb, (((1,), (1,)), ((), ())), preferred_element_type=F32)
        e = jnp.exp2(s - jnp.max(s, axis=-1, keepdims=True))
        inv = 1.0 / jnp.sum(e, axis=-1, keepdims=True)
        if n_sub == 1:
            p = e * inv
        else:
            p = jnp.concatenate(
                [e[(2 * i) * tq:(2 * i + 1) * tq] * inv[(2 * i) * tq:(2 * i + 1) * tq]
                 - e[(2 * i + 1) * tq:(2 * i + 2) * tq] * (lam * inv[(2 * i + 1) * tq:(2 * i + 2) * tq])
                 for i in range(ATTN_HEAD_GROUP)], axis=0)
        o_all = _dot(p.astype(BF16), vb)
        for i in range(ATTN_HEAD_GROUP):
            out = jnp.where(_lane_mask((h0 + i) * HEAD_DIM, HEAD_DIM), o_all[i * tq:(i + 1) * tq], out)
    return out


def _diff_lambda(dl, lam_init):
    a = jnp.sum(dl[0:1] * dl[1:2], axis=-1, keepdims=True)
    b = jnp.sum(dl[2:3] * dl[3:4], axis=-1, keepdims=True)
    return jnp.exp(a) - jnp.exp(b) + lam_init


def _attn_kernel(q_ref, k_ref, v_ref, dl_ref, o_ref, *, n_sub, lam_init):
    lam = _diff_lambda(dl_ref[...], lam_init) if n_sub == 2 else None
    o_ref[...] = _attn_core(q_ref[...], k_ref[...], v_ref[...], n_sub, lam)


def _attention(q_arr, q_col, k_arr, k_col, v_arr, v_col, dl, *, n_sub, lam_init, n_seq, lq, lk,
               q_base, k_base, tq, name):
    nq = lq // tq
    return pl.pallas_call(
        functools.partial(_attn_kernel, n_sub=n_sub, lam_init=lam_init),
        grid=(n_seq, nq),
        in_specs=[pl.BlockSpec((tq, MIX_W), lambda s, j: (q_base // tq + s * nq + j, q_col)),
                  pl.BlockSpec((lk, MIX_W), lambda s, j: (k_base // lk + s, k_col)),
                  pl.BlockSpec((lk, MIX_W), lambda s, j: (k_base // lk + s, v_col)),
                  pl.BlockSpec(dl.shape, lambda s, j: (0, 0))],
        out_specs=pl.BlockSpec((tq, MIX_W), lambda s, j: (s * nq + j, 0)),
        out_shape=jax.ShapeDtypeStruct((n_seq * lq, MIX_W), F32),
        compiler_params=_params(("parallel", "parallel")),
        name=name,
    )(q_arr, k_arr, v_arr, dl)


DIFF_KEY_CHUNK = 256
DIFF_HEAD_GROUP = 1
DIFF_Q_TILE = 256


def _lane_fold(x, op):
    out = x[:, 0:128]
    for t in range(1, x.shape[1] // 128):
        out = op(out, x[:, t * 128:(t + 1) * 128])
    return out


def _diff_group(h0, q, kt_chunk, v_chunk, nck, lam):
    tq = q.shape[0]
    sub_w = HEAD_DIM // 2
    scale = sub_w ** -0.5 * LOG2E
    maps = [(h, m) for h in range(h0, h0 + DIFF_HEAD_GROUP) for m in range(2)]
    qs = jnp.concatenate(
        [(q * jnp.where(_lane_mask(h * HEAD_DIM + m * sub_w, sub_w), scale, 0.0)).astype(BF16) for h, m in maps],
        axis=0)
    s, mvec = [], None
    for c in range(nck):
        sc = _dot(qs, kt_chunk(c))
        s.append(sc)
        mc = _lane_fold(sc, jnp.maximum)
        mvec = mc if mvec is None else jnp.maximum(mvec, mc)
        yield
    mrun = jnp.max(mvec, axis=-1, keepdims=True)
    e, lvec = [], None
    for c in range(nck):
        ec = jnp.exp2(s[c] - mrun)
        e.append(ec)
        lc = _lane_fold(ec, jnp.add)
        lvec = lc if lvec is None else lvec + lc
        yield
    inv = 1.0 / jnp.sum(lvec, axis=-1, keepdims=True)
    acc = None
    for c in range(nck):
        a = jnp.concatenate(
            [e[c][(2 * i) * tq:(2 * i + 1) * tq] * inv[(2 * i) * tq:(2 * i + 1) * tq]
             - e[c][(2 * i + 1) * tq:(2 * i + 2) * tq] * (lam * inv[(2 * i + 1) * tq:(2 * i + 2) * tq])
             for i in range(DIFF_HEAD_GROUP)], axis=0)
        o = _dot(a.astype(BF16), v_chunk(c))
        acc = o if acc is None else acc + o
        if c < nck - 1:
            yield
    return acc


_DIFF_PASSES = 3


def _diff_lat_kernel(q_ref, kt_ref, ckt_ref, v_ref, cv_ref, dl_ref, o_ref, *, lam_init, tkc):
    lam = _diff_lambda(dl_ref[...], lam_init)
    q = q_ref[...]
    tq = q.shape[0]
    n_new = kt_ref.shape[1] // tkc
    nck = n_new + ckt_ref.shape[1] // tkc

    def kt_chunk(c):
        ref, j = (kt_ref, c) if c < n_new else (ckt_ref, c - n_new)
        return ref[:, j * tkc:(j + 1) * tkc]

    def v_chunk(c):
        ref, j = (v_ref, c) if c < n_new else (cv_ref, c - n_new)
        return ref[j * tkc:(j + 1) * tkc, :].astype(BF16)

    n_grp = HEADS // DIFF_HEAD_GROUP
    groups = [_diff_group(g * DIFF_HEAD_GROUP, q, kt_chunk, v_chunk, nck, lam) for g in range(n_grp)]
    res = [None] * n_grp
    for t in range(n_grp + _DIFF_PASSES - 1):
        active = [g for g in range(n_grp) if 0 <= t - g < _DIFF_PASSES]
        for _ in range(nck):
            for g in active:
                try:
                    next(groups[g])
                except StopIteration as done:
                    res[g] = done.value
    out = jnp.zeros(q.shape, F32)
    for h in range(HEADS):
        g, i = divmod(h, DIFF_HEAD_GROUP)
        out = jnp.where(_lane_mask(h * HEAD_DIM, HEAD_DIM), res[g][i * tq:(i + 1) * tq], out)
    o_ref[...] = out


def _diff_latent(q_r, k_t, ck_t, z_diff, cv, dl, lam_init, n_b, n_tok, p_len, tq):
    tkc = DIFF_KEY_CHUNK if (n_tok % DIFF_KEY_CHUNK == 0 and p_len % DIFF_KEY_CHUNK == 0) else 128
    assert n_tok % tkc == 0 and p_len % tkc == 0
    nq = n_tok // tq
    return pl.pallas_call(
        functools.partial(_diff_lat_kernel, lam_init=lam_init, tkc=tkc),
        grid=(n_b, nq),
        in_specs=[pl.BlockSpec((tq, MIX_W), lambda b, j: (b * nq + j, 0)),
                  pl.BlockSpec((MIX_W, n_tok), lambda b, j: (b, 0)),
                  pl.BlockSpec((MIX_W, p_len), lambda b, j: (b, 0)),
                  pl.BlockSpec((n_tok, MIX_W), lambda b, j: (b, 2)),
                  pl.BlockSpec((p_len, MIX_W), lambda b, j: (b, 0)),
                  pl.BlockSpec(dl.shape, lambda b, j: (0, 0))],
        out_specs=pl.BlockSpec((tq, MIX_W), lambda b, j: (b * nq + j, 0)),
        out_shape=jax.ShapeDtypeStruct((n_b * n_tok, MIX_W), F32),
        compiler_params=_params(("parallel", "arbitrary")),
        name="diff_latent",
    )(q_r, k_t, ck_t, z_diff, cv, dl)


def _na_lat_kernel(q_ref, k_ref, v_ref, ck_ref, cv_ref, bias_ref, o_ref, *, rows, kh, rps):
    g = pl.program_id(1)
    ck = ck_ref[...].astype(BF16)
    cv = cv_ref[...].astype(BF16)
    scale = HEAD_DIM ** -0.5 * LOG2E
    head_scale = [jnp.where(_lane_mask(h * HEAD_DIM, HEAD_DIM), scale, 0.0) for h in range(HEADS)]
    hw = HEADS * GRID_W
    qs = []
    for j in range(rps):
        qj = q_ref[j * GRID_W:(j + 1) * GRID_W, :]
        qs.append(jnp.concatenate([(qj * hs).astype(BF16) for hs in head_scale], axis=0))
    s_ctx_all = lax.dot_general(jnp.concatenate(qs, axis=0), ck, (((1,), (1,)), ((), ())),
                                preferred_element_type=F32)
    p_ctx, o_loc = [], []
    for j in range(rps):
        r = g * rps + j
        start = jnp.clip(r - kh // 2, 0, rows - kh)
        cls = r - start
        ws = pl.multiple_of(start * GRID_W, GRID_W)
        kw = k_ref[pl.ds(ws, kh * GRID_W), :].astype(BF16)
        vw = v_ref[pl.ds(ws, kh * GRID_W), :].astype(BF16)
        bias = jnp.concatenate([bias_ref[h, pl.ds(cls, 1)][0] for h in range(HEADS)], axis=0)
        s_loc = lax.dot_general(qs[j], kw, (((1,), (1,)), ((), ())), preferred_element_type=F32) + bias
        s_ctx = s_ctx_all[j * hw:(j + 1) * hw]
        mx = jnp.maximum(jnp.max(s_loc, axis=-1, keepdims=True), jnp.max(s_ctx, axis=-1, keepdims=True))
        e_loc = jnp.exp2(s_loc - mx)
        e_ctx = jnp.exp2(s_ctx - mx)
        inv = 1.0 / (jnp.sum(e_loc, axis=-1, keepdims=True) + jnp.sum(e_ctx, axis=-1, keepdims=True))
        p_ctx.append((e_ctx * inv).astype(BF16))
        o_loc.append(_dot((e_loc * inv).astype(BF16), vw))
    o_ctx_all = _dot(jnp.concatenate(p_ctx, axis=0), cv)
    for j in range(rps):
        o_all = o_loc[j] + o_ctx_all[j * hw:(j + 1) * hw]
        out = jnp.zeros((GRID_W, MIX_W), F32)
        for h in range(HEADS):
            out = jnp.where(_lane_mask(h * HEAD_DIM, HEAD_DIM), o_all[h * GRID_W:(h + 1) * GRID_W], out)
        o_ref[j * GRID_W:(j + 1) * GRID_W, :] = out


def _na_bias_table(rpb, kh):
    cidx = np.arange(GRID_W)
    cls = np.arange(kh)
    drow = np.arange(kh)[None, :] - cls[:, None] + NA_WIN_H - 1
    col_start = np.clip(cidx - NA_WIN_W // 2, 0, GRID_W - NA_WIN_W)
    col_ok = (cidx[None, :] >= col_start[:, None]) & (cidx[None, :] < col_start[:, None] + NA_WIN_W)
    dcol = np.clip(cidx[None, :] - cidx[:, None], 1 - NA_WIN_W, NA_WIN_W - 1) + NA_WIN_W - 1
    onehot = (dcol.reshape(1, -1) == np.arange(2 * NA_WIN_W - 1)[:, None]).astype(np.float32)
    toep = jnp.dot(rpb.astype(F32).reshape(-1, 2 * NA_WIN_W - 1), onehot, precision=lax.Precision.HIGHEST)
    toep = toep.reshape(-1, 2 * NA_WIN_H - 1, GRID_W, GRID_W)
    bias = jnp.stack([toep[:, drow[c, 0]:drow[c, 0] + kh] for c in range(kh)], axis=1)
    bias = bias.transpose(0, 1, 3, 2, 4)
    bias = jnp.where(col_ok[:, None, :], bias * LOG2E, NEG)
    return bias.reshape(-1, kh, GRID_W, kh * GRID_W)


def _na_latent(z_na, ck, cv, bias, n_b, n_tok, p_len):
    rows = n_tok // GRID_W
    kh = min(NA_WIN_H, rows)
    rps = NA_ROWS_PER_STEP
    assert rows % rps == 0
    steps = rows // rps
    return pl.pallas_call(
        functools.partial(_na_lat_kernel, rows=rows, kh=kh, rps=rps),
        grid=(n_b, steps),
        in_specs=[pl.BlockSpec((rps * GRID_W, MIX_W), lambda b, r: (b * steps + r, 0)),
                  pl.BlockSpec((n_tok, MIX_W), lambda b, r: (b, 1)),
                  pl.BlockSpec((n_tok, MIX_W), lambda b, r: (b, 2)),
                  pl.BlockSpec((p_len, MIX_W), lambda b, r: (b, 0)),
                  pl.BlockSpec((p_len, MIX_W), lambda b, r: (b, 0)),
                  pl.BlockSpec(bias.shape, lambda b, r: (0, 0, 0, 0))],
        out_specs=pl.BlockSpec((rps * GRID_W, MIX_W), lambda b, r: (b * steps + r, 0)),
        out_shape=jax.ShapeDtypeStruct((n_b * n_tok, MIX_W), F32),
        compiler_params=_params(("parallel", "arbitrary")),
        name="na_latent",
    )(z_na, z_na, z_na, ck, cv, bias)


def _rope_kernel(q_ref, k_ref, cos_ref, sin_ref, p_ref, qo_ref, ko_ref):
    cos = cos_ref[...]
    sin = sin_ref[...]
    p = p_ref[...]
    q = q_ref[...]
    k = k_ref[...]
    qo_ref[...] = q * cos + _dot_rx(q, p) * sin
    ko_ref[...] = (k * cos + _dot_rx(k, p) * sin).T.astype(BF16)


def _rope(z_diff, cos, sin, perm, n_b, n_tok, tm):
    nt = n_tok // tm
    spec_t = pl.BlockSpec((tm, MIX_W), lambda i: (i % nt, 0))
    return pl.pallas_call(
        _rope_kernel,
        grid=(n_b * nt,),
        in_specs=[pl.BlockSpec((tm, MIX_W), lambda i: (i, 0)),
                  pl.BlockSpec((tm, MIX_W), lambda i: (i, 1)),
                  spec_t, spec_t,
                  pl.BlockSpec(perm.shape, lambda i: (0, 0))],
        out_specs=[pl.BlockSpec((tm, MIX_W), lambda i: (i, 0)),
                   pl.BlockSpec((MIX_W, tm), lambda i: (i // nt, i % nt))],
        out_shape=[jax.ShapeDtypeStruct((n_b * n_tok, MIX_W), F32),
                   jax.ShapeDtypeStruct((n_b * MIX_W, n_tok), BF16)],
        compiler_params=_params(("parallel",)),
        name="diff_rope",
    )(z_diff, z_diff, cos, sin, perm)


def _gdn_pre_kernel(x_ref, prev_ref, next_ref, edge_ref, w_ref, bm_ref, o_ref, *, tm):
    x = x_ref[...]
    w = w_ref[...]
    edge = edge_ref[...]
    row = lax.broadcasted_iota(jnp.int32, (tm, 1), 0)
    x_prev = jnp.where(row == 0, prev_ref[7:8, :], pltpu.roll(x, 1, 0)) * edge[:, 0:1]
    x_next = jnp.where(row == tm - 1, next_ref[0:1, :], pltpu.roll(x, tm - 1, 0)) * edge[:, 1:2]
    y = _silu(x_prev * w[0:1] + x * w[1:2] + x_next * w[2:3])
    bm16 = bm_ref[...]
    q = y[:, 0:MIX_W]
    k = y[:, MIX_W:2 * MIX_W]
    o_ref[:, 0:MIX_W] = q * lax.rsqrt(_head_sum(q * q, bm16) + EPS) * (HEAD_DIM ** -0.5)
    o_ref[:, MIX_W:2 * MIX_W] = k * lax.rsqrt(_head_sum(k * k, bm16) + EPS)
    o_ref[:, 2 * MIX_W:3 * MIX_W] = y[:, 2 * MIX_W:3 * MIX_W]


def _seq_edges(lat_tokens, n_tok, ctx_tokens, l_ctx):
    pos = np.concatenate([np.arange(lat_tokens) % n_tok, np.arange(ctx_tokens) % l_ctx])
    seq = np.concatenate([np.full(lat_tokens, n_tok), np.full(ctx_tokens, l_ctx)])
    edge = np.zeros((lat_tokens + ctx_tokens, 8), np.float32)
    edge[:, 0] = pos != 0
    edge[:, 1] = pos != seq - 1
    return jnp.asarray(edge)


def _gdn_pre(z_gdn, edge, conv_w, bm16, tm):
    t = z_gdn.shape[0]
    w3 = 3 * MIX_W
    nb8 = t // 8
    return pl.pallas_call(
        functools.partial(_gdn_pre_kernel, tm=tm),
        grid=(t // tm,),
        in_specs=[pl.BlockSpec((tm, w3), lambda i: (i, 0)),
                  pl.BlockSpec((8, w3), lambda i: (jnp.maximum(i * (tm // 8) - 1, 0), 0)),
                  pl.BlockSpec((8, w3), lambda i: (jnp.minimum((i + 1) * (tm // 8), nb8 - 1), 0)),
                  pl.BlockSpec((tm, 8), lambda i: (i, 0)),
                  pl.BlockSpec(conv_w.shape, lambda i: (0, 0)),
                  pl.BlockSpec(bm16.shape, lambda i: (0, 0))],
        out_specs=pl.BlockSpec((tm, w3), lambda i: (i, 0)),
        out_shape=jax.ShapeDtypeStruct((t, w3), F32),
        compiler_params=_params(("parallel",)),
        name="gdn_pre",
    )(z_gdn, z_gdn, z_gdn, edge, conv_w, bm16)


def _bd_rows16(x, bm16):
    return jnp.concatenate([x.astype(BF16)] * HEADS, axis=0) * bm16


def _gla_chain(q, k, v, zs, gup, gb, lsm, pm_ref, d, bm, bm16, st):
    nblk = CHUNK // LEAF
    q = q * (HEAD_DIM ** -0.5)
    gp = _bdot(zs, gup) + gb
    g = (jnp.minimum(gp, 0.0) - jnp.log(1.0 + jnp.exp(-jnp.abs(gp)))) * (LOG2E / GATE_NORM)
    yield
    e = _dot_lx(lsm, g)
    b = e[0:CHUNK]
    fb = jnp.exp2(b)
    fbl = jnp.exp2(e[CHUNK:2 * CHUNK])
    ftot = jnp.exp2(e[2 * CHUNK:2 * CHUNK + 1])
    yield
    edge = LEAF - 1 if d == 0 else 0
    blocks = range(nblk - 1) if d == 0 else range(1, nblk)
    b3 = b.reshape(nblk, LEAF, MIX_W)
    b_edge = jnp.broadcast_to(b3[:, edge:edge + 1, :], b3.shape).reshape(CHUNK, MIX_W)
    kf = jnp.exp2(jnp.minimum(b_edge - b, 0.0))
    q_far = jnp.concatenate(
        [(q * jnp.exp2(jnp.minimum(b - b[blk * LEAF + edge:blk * LEAF + edge + 1, :], 0.0))).astype(BF16)
         for blk in blocks], axis=0)
    s_far = lax.dot_general(q_far, _bd_rows16(k * kf, bm16), (((1,), (1,)), ((), ())),
                            preferred_element_type=F32)
    yield
    q_near = jnp.concatenate(
        [(q * jnp.exp2(jnp.minimum(
            b - jnp.broadcast_to(b3[:, r:r + 1, :], b3.shape).reshape(CHUNK, MIX_W), 0.0))).astype(BF16)
         for r in range(LEAF)], axis=0)
    s_near = lax.dot_general(q_near, _bd_rows16(k, bm16), (((1,), (1,)), ((), ())),
                             preferred_element_type=F32)
    yield
    att = None
    for n, blk in enumerate(blocks):
        term = s_far[n * CHUNK:(n + 1) * CHUNK] * pm_ref[d, blk]
        att = term if att is None else att + term
    for r in range(LEAF):
        att = att + s_near[r * CHUNK:(r + 1) * CHUNK] * pm_ref[d, nblk + r]
    yield
    o = _bdot_nt(q * fb, st) + _dot(att.astype(BF16), _bd_rows16(v, bm16))
    yield
    return o, st * ftot + _bdot_tn(v, k * fbl) * bm


def _row_pieces(s, d):
    out = []
    for h in range(HEADS):
        for blk in range(CHUNK // s):
            lo = h * CHUNK + blk * s
            out.append((lo, lo + s, blk % 2 == 1 - d))
    return out


def _tri_inverse(mbd, sm_ref, d):
    r = lax.broadcasted_iota(jnp.int32, (MIX_W, MIX_W), 0)
    c = lax.broadcasted_iota(jnp.int32, (MIX_W, MIX_W), 1)
    eye = jnp.where(r == c, 1.0, 0.0)
    md = mbd * sm_ref[0]
    m2 = _bdot(md, md)
    yield
    m4 = _bdot(m2, m2)
    t = eye - md
    t = t + _bdot(t, m2)
    yield
    t = t + _bdot(t, m4)
    s = LEAF
    for lvl in range(1, sm_ref.shape[0]):
        pieces = _row_pieces(s, d)
        take = lambda x: jnp.concatenate([x[lo:hi] for lo, hi, upd in pieces if upd], axis=0)
        yield
        t16 = t.astype(BF16)
        a = _dot(take(mbd * sm_ref[lvl]).astype(BF16), t16).astype(BF16)
        rows, n = [], 0
        for lo, hi, upd in pieces:
            rows.append(a[n:n + s] if upd else jnp.zeros((s, MIX_W), BF16))
            n += s if upd else 0
        yield
        b = _dot(take(t16), jnp.concatenate(rows, axis=0))
        rows, n = [], 0
        for lo, hi, upd in pieces:
            rows.append(t[lo:hi] - b[n:n + s] if upd else t[lo:hi])
            n += s if upd else 0
        t = jnp.concatenate(rows, axis=0)
        s *= 2
    return t


def _gdn_chain(q, k, v, zs, ea, eb, alog, dtb, lsm, lbt, pm_ref, d, bm, bm16, sm_ref, s):
    g = -jnp.exp(alog) * _softplus(_dot_rx(zs, ea) + dtb)
    beta = jax.nn.sigmoid(_dot_rx(zs, eb))
    yield
    e = _dot_lx(lsm, g)
    dcol = e[0:CHUNK]
    drow = _dot_lx(jnp.ones((8, CHUNK), BF16), g * lbt)[0:1]
    kb = k * beta
    s2 = lax.dot_general(jnp.concatenate([kb, q], axis=0).astype(BF16), _bd_rows16(k, bm16),
                         (((1,), (1,)), ((), ())), preferred_element_type=F32)
    yield
    dec = jnp.exp(jnp.where(pm_ref[d, 1] > 0.0, dcol - drow, NEG))
    m_cat = s2[0:CHUNK] * dec * pm_ref[d, 0]
    att = s2[CHUNK:2 * CHUNK] * dec
    t_inv = yield from _tri_inverse(_bd_rows(m_cat, bm), sm_ref, d)
    t16 = (t_inv[0:CHUNK] + t_inv[CHUNK:2 * CHUNK] + t_inv[2 * CHUNK:3 * CHUNK] + t_inv[3 * CHUNK:]).astype(BF16)
    fb = jnp.exp(dcol)
    fbl = jnp.exp(e[CHUNK:2 * CHUNK])
    ftot = jnp.exp(e[2 * CHUNK:2 * CHUNK + 1])
    yield
    u = _dot(t16, _bd_rows16(v * beta, bm16))
    w = _dot(t16, _bd_rows16(kb * fb, bm16))
    yield
    ws = _dot(jnp.concatenate([w.astype(BF16), (q * fb).astype(BF16)], axis=0), s.astype(BF16))
    v_new = u - ws[0:CHUNK]
    yield
    o = ws[CHUNK:2 * CHUNK] + _dot(att.astype(BF16), _bd_rows16(v_new, bm16))
    return o, s * ftot + lax.dot_general((k * fbl).astype(BF16), v_new.astype(BF16), (((0,), (0,)), ((), ())),
                                         preferred_element_type=F32) * bm


def _scan_kernel(*refs, chains, n_consts, nb, n_chunks):
    ns = len(chains)
    zs_refs, pos = refs[:nb * 2], nb * 2
    streams = []
    for i in range(ns):
        tok = refs[pos:pos + nb * 2 * 3]
        consts = refs[pos + nb * 2 * 3:pos + nb * 2 * 3 + n_consts[i]]
        streams.append((tok, consts, refs[pos + nb * 2 * 3 + n_consts[i]]))
        pos += nb * 2 * 3 + n_consts[i] + 1
    outs, scrs = refs[pos:pos + 3 * ns], refs[pos + 3 * ns:]
    n = pl.program_id(1)

    @pl.when(n == 0)
    def _():
        for i in range(ns):
            scrs[i][...] = streams[i][2][...]

    ids = [(i, j, d) for j in range(nb) for d in range(2) for i in range(ns)]
    gens = []
    for i, j, d in ids:
        tok, consts, _ = streams[i]
        q_ref, k_ref, v_ref = tok[(j * 2 + d) * 3:(j * 2 + d) * 3 + 3]
        gens.append(chains[i](q_ref[...], k_ref[...], v_ref[...], zs_refs[j * 2 + d][...], consts, d, scrs[i][j, d]))
    live = list(range(len(gens)))
    while live:
        for c in list(live):
            try:
                next(gens[c])
            except StopIteration as done:
                i, j, d = ids[c]
                o, s_new = done.value
                outs[3 * i + d][j] = o
                scrs[i][j, d] = s_new
                live.remove(c)

    @pl.when(n == n_chunks - 1)
    def _():
        for i in range(ns):
            outs[3 * i + 2][...] = scrs[i][...]


def _gla_step(q, k, v, zs, consts, d, st):
    gup_ref, gb_ref, lsm_ref, pm_ref, bm_ref, bm16_ref = consts
    return _gla_chain(q, k, v, zs, gup_ref[d], gb_ref[d], lsm_ref[d], pm_ref, d, bm_ref[...], bm16_ref[...], st)


def _gdn_step(q, k, v, zs, consts, d, s):
    ea_ref, eb_ref, alog_ref, dtb_ref, lsm_ref, lbt_ref, pm_ref, bm_ref, bm16_ref, sm_ref = consts
    return _gdn_chain(q, k, v, zs, ea_ref[d], eb_ref[d], alog_ref[d], dtb_ref[d], lsm_ref[d], lbt_ref[d],
                      pm_ref, d, bm_ref[...], bm16_ref[...], sm_ref, s)


def _scan(streams, z_small, n_seq, l_seq, base, seqs_per_step, name):
    nc = l_seq // CHUNK
    nb = seqs_per_step if n_seq % seqs_per_step == 0 else 1

    def chunk(n, d):
        return n + d * (nc - 1 - 2 * n)

    def tok_spec(j, d, col, width):
        return pl.BlockSpec((CHUNK, width),
                            lambda g, n: (base // CHUNK + (g * nb + j) * nc + chunk(n, d), col))

    state_spec = pl.BlockSpec((nb, 2, MIX_W, MIX_W), lambda g, n: (g, 0, 0, 0))
    in_specs = [tok_spec(j, d, 0, SMALL_W) for j in range(nb) for d in range(2)]
    args = [z_small] * (nb * 2)
    out_specs, out_shape, scratch = [], [], []
    for chain, qkv, consts, s0 in streams:
        for j in range(nb):
            for d in range(2):
                in_specs += [tok_spec(j, d, col, MIX_W) for col in range(3)]
                args += [qkv] * 3
        for cst in consts:
            in_specs.append(pl.BlockSpec(cst.shape, functools.partial(lambda g, n, nd: (0,) * nd, nd=cst.ndim)))
            args.append(cst)
        in_specs.append(state_spec)
        args.append(s0)
        out_specs += [pl.BlockSpec((nb, CHUNK, MIX_W), lambda g, n: (g, chunk(n, 0), 0)),
                      pl.BlockSpec((nb, CHUNK, MIX_W), lambda g, n: (g, chunk(n, 1), 0)),
                      state_spec]
        out_shape += [jax.ShapeDtypeStruct((n_seq, l_seq, MIX_W), F32),
                      jax.ShapeDtypeStruct((n_seq, l_seq, MIX_W), F32),
                      jax.ShapeDtypeStruct((n_seq, 2, MIX_W, MIX_W), F32)]
        scratch.append(pltpu.VMEM((nb, 2, MIX_W, MIX_W), F32))
    res = pl.pallas_call(
        functools.partial(_scan_kernel, chains=tuple(st[0] for st in streams),
                          n_consts=tuple(len(st[2]) for st in streams), nb=nb, n_chunks=nc),
        grid=(n_seq // nb, nc),
        in_specs=in_specs,
        out_specs=out_specs,
        out_shape=out_shape,
        scratch_shapes=scratch,
        compiler_params=_params(("parallel", "arbitrary")),
        name=name,
    )(*args)
    return [(res[3 * i].reshape(n_seq * l_seq, MIX_W), res[3 * i + 1].reshape(n_seq * l_seq, MIX_W), res[3 * i + 2])
            for i in range(len(streams))]


def _merge_kernel(xl_ref, xc_ref, mod_ref, n1_ref, na_l, na_c, glaf_l, glaf_c, glab_l, glab_c, glag_ref, gdnf_l, gdnf_c,
                  gdnb_l, gdnb_c, gdng_ref, df_l, df_c, hn_ref, bm_ref, wg_ref, wb_ref, wo_ref, o_ref, *,
                  lam_init, n_lat_tiles):
    is_lat = pl.program_id(0) < n_lat_tiles

    def pick(lat_ref, ctx_ref):
        return jnp.where(is_lat, lat_ref[...], ctx_ref[...])

    mod = mod_ref[0]
    x = pick(xl_ref, xc_ref)
    h = _rms_rows(x, n1_ref[...]) * (1.0 + mod[1:2]) + mod[0:1]
    hb = h.astype(BF16)
    bm16 = bm_ref[...]

    def head_norm(o, w):
        return o * lax.rsqrt(_head_sum(o * o, bm16) * (1.0 / HEAD_DIM) + EPS) * w

    branches = (
        pick(na_l, na_c),
        head_norm(pick(glaf_l, glaf_c) + pick(glab_l, glab_c), hn_ref[0]) * _silu(glag_ref[...]),
        head_norm(pick(gdnf_l, gdnf_c) + pick(gdnb_l, gdnb_c), hn_ref[1]) * _silu(gdng_ref[...]),
        head_norm(pick(df_l, df_c), hn_ref[2]) * (1.0 - lam_init),
    )
    merged = None
    for n, br in enumerate(branches):
        gate = jax.nn.sigmoid(_dot(hb, wg_ref[:, n * D_MODEL:(n + 1) * D_MODEL]))
        term = gate * _bdot(br, wb_ref[n])
        merged = term if merged is None else merged + term
    o_ref[...] = x + mod[2:3] * _bdot(merged, wo_ref[...])


def _merge(xs, t, mod_l, n1, na, gla_f, gla_b, z_gla, gdn_f, gdn_b, z_gdn, df, hn, bm16, wg, wb, wo, tm, mod_row,
           lam_init, n_lat_tiles):
    tok = lambda col: pl.BlockSpec((tm, MIX_W), lambda i: (i, col))
    lat = pl.BlockSpec((tm, MIX_W), lambda i: (jnp.minimum(i, n_lat_tiles - 1), 0))
    ctx = pl.BlockSpec((tm, MIX_W), lambda i: (jnp.maximum(i - n_lat_tiles, 0), 0))
    full = lambda a: pl.BlockSpec(a.shape, lambda i: (0,) * a.ndim)
    return pl.pallas_call(
        functools.partial(_merge_kernel, lam_init=lam_init, n_lat_tiles=n_lat_tiles),
        grid=(t // tm,),
        in_specs=_stream_specs(xs, tm) + [
                  pl.BlockSpec((1, 6, D_MODEL), lambda i: (mod_row(i), 0, 0)),
                  full(n1), lat, ctx, lat, ctx, lat, ctx, tok(3), lat, ctx, lat, ctx, tok(3), lat, ctx,
                  full(hn), full(bm16), full(wg), full(wb), full(wo)],
        out_specs=pl.BlockSpec((tm, D_MODEL), lambda i: (i, 0)),
        out_shape=jax.ShapeDtypeStruct((t, D_MODEL), F32),
        compiler_params=_params(("parallel",)),
        name="merge_out",
    )(xs[0], xs[1], mod_l, n1, *na, *gla_f, *gla_b, z_gla, *gdn_f, *gdn_b, z_gdn, *df, hn, bm16, wg, wb, wo)


def _route(h2, wr_ref, br_ref):
    s_t = jax.nn.sigmoid(_dot3(h2, wr_ref[...])).T
    s = [s_t[e:e + 1] for e in range(N_EXPERTS)]
    sel = [s[e] + br_ref[e] for e in range(N_EXPERTS)]
    best = None
    for g in range(N_GROUPS):
        v = sel[g * GROUP_E:(g + 1) * GROUP_E]
        score = None
        for i in range(GROUP_E):
            for j in range(i + 1, GROUP_E):
                score = v[i] + v[j] if score is None else jnp.maximum(score, v[i] + v[j])
        if best is None:
            best, gi = score, jnp.zeros_like(score)
        else:
            better = score > best
            gi = jnp.where(better, float(g), gi)
            best = jnp.where(better, score, best)
    vals = [jnp.where(gi == float(e // GROUP_E), sel[e], NEG) for e in range(N_EXPERTS)]

    def first_max(v):
        mx, idx = v[0], jnp.zeros_like(v[0])
        for e in range(1, N_EXPERTS):
            better = v[e] > mx
            idx = jnp.where(better, float(e), idx)
            mx = jnp.where(better, v[e], mx)
        return idx

    i1 = first_max(vals)
    i2 = first_max([jnp.where(i1 == float(e), NEG, vals[e]) for e in range(N_EXPERTS)])
    s1 = sum(jnp.where(i1 == float(e), s[e], 0.0) for e in range(N_EXPERTS))
    s2 = sum(jnp.where(i2 == float(e), s[e], 0.0) for e in range(N_EXPERTS))
    inv = 1.0 / (s1 + s2)
    rows = [jnp.where(i1 == float(e), s1 * inv, 0.0) + jnp.where(i2 == float(e), s2 * inv, 0.0)
            for e in range(N_EXPERTS)]
    gates_t = jnp.concatenate(rows + [jnp.zeros((SMALL_W - N_EXPERTS, h2.shape[0]), F32)], axis=0)
    return gates_t.T


def _moe_kernel(x_ref, mod_ref, n2_ref, wr_ref, br_ref, w1_ref, w3_ref, w2_ref, *rest, n_lat_tiles):
    if n_lat_tiles is None:
        o_ref, h_scr, g_scr, acc_scr = rest
    else:
        fw_ref, ol_ref, oc_ref, h_scr, g_scr, acc_scr = rest
    g = pl.program_id(1)
    mod = mod_ref[0]

    @pl.when(g == 0)
    def _():
        h2 = _rms_rows(x_ref[...], n2_ref[...]) * (1.0 + mod[4:5]) + mod[3:4]
        h_scr[...] = h2.astype(BF16)
        g_scr[...] = _route(h2, wr_ref, br_ref)
        acc_scr[...] = jnp.zeros_like(acc_scr)

    hb = h_scr[...]
    gates = g_scr[...]
    lane = lax.broadcasted_iota(jnp.int32, (1, SMALL_W), 1)
    he = jnp.concatenate(
        [_silu(_dot(hb, w1_ref[e])) * _dot(hb, w3_ref[e])
         * jnp.sum(jnp.where(lane == g * GROUP_E + e, gates, 0.0), axis=-1, keepdims=True)
         for e in range(GROUP_E)], axis=1)
    acc_scr[...] += _dot(he.astype(BF16), w2_ref[...])

    if n_lat_tiles is None:
        @pl.when(g == N_GROUPS - 1)
        def _():
            o_ref[...] = x_ref[...] + mod[5:6] * acc_scr[...]
    else:
        is_lat = pl.program_id(0) < n_lat_tiles

        @pl.when((g == N_GROUPS - 1) & is_lat)
        def _():
            ol_ref[...] = _rms_rows(x_ref[...] + mod[5:6] * acc_scr[...], fw_ref[...])

        @pl.when((g == N_GROUPS - 1) & jnp.logical_not(is_lat))
        def _():
            oc_ref[...] = _rms_rows(x_ref[...] + mod[5:6] * acc_scr[...], fw_ref[...])


def _moe(x, mod_l, n2, wr, br, w1, w3, w2, tm, mod_row, final_w=None, n_lat_tiles=None):
    t = x.shape[0]
    gw = GROUP_E * D_FF
    in_specs = [pl.BlockSpec((tm, D_MODEL), lambda i, g: (i, 0)),
                pl.BlockSpec((1, 6, D_MODEL), lambda i, g: (mod_row(i), 0, 0)),
                pl.BlockSpec(n2.shape, lambda i, g: (0, 0)),
                pl.BlockSpec(wr.shape, lambda i, g: (0, 0)),
                pl.BlockSpec(memory_space=pltpu.SMEM),
                pl.BlockSpec((GROUP_E, D_MODEL, D_FF), lambda i, g: (g, 0, 0)),
                pl.BlockSpec((GROUP_E, D_MODEL, D_FF), lambda i, g: (g, 0, 0)),
                pl.BlockSpec((gw, D_MODEL), lambda i, g: (g, 0))]
    args = [x, mod_l, n2, wr, br, w1, w3, w2]
    if final_w is None:
        out_specs = pl.BlockSpec((tm, D_MODEL), lambda i, g: (i, 0))
        out_shape = jax.ShapeDtypeStruct((t, D_MODEL), F32)
    else:
        in_specs.append(pl.BlockSpec(final_w.shape, lambda i, g: (0, 0)))
        args.append(final_w)
        out_specs = [pl.BlockSpec((tm, D_MODEL), lambda i, g: (jnp.minimum(i, n_lat_tiles - 1), 0)),
                     pl.BlockSpec((tm, D_MODEL), lambda i, g: (jnp.maximum(i - n_lat_tiles, 0), 0))]
        out_shape = [jax.ShapeDtypeStruct((n_lat_tiles * tm, D_MODEL), F32),
                     jax.ShapeDtypeStruct((t - n_lat_tiles * tm, D_MODEL), F32)]
    return pl.pallas_call(
        functools.partial(_moe_kernel, n_lat_tiles=n_lat_tiles),
        grid=(t // tm, N_GROUPS),
        in_specs=in_specs,
        out_specs=out_specs,
        out_shape=out_shape,
        scratch_shapes=[pltpu.VMEM((tm, D_MODEL), BF16), pltpu.VMEM((tm, SMALL_W), F32),
                        pltpu.VMEM((tm, D_MODEL), F32)],
        compiler_params=_params(("arbitrary", "arbitrary")),
        name="moe",
    )(*args)


def _block_diag_state(s, transpose):
    if transpose:
        s = jnp.swapaxes(s, -1, -2)
    eye = jnp.eye(HEADS, dtype=s.dtype)
    out = jnp.einsum('...hij,hg->...higj', s, eye)
    return out.reshape(s.shape[:-3] + (MIX_W, MIX_W))


def _diag_blocks(s_bd, transpose):
    s = s_bd.reshape(s_bd.shape[:-2] + (HEADS, HEAD_DIM, HEADS, HEAD_DIM))
    s = jnp.stack([s[..., h, :, h, :] for h in range(HEADS)], axis=-3)
    return jnp.swapaxes(s, -1, -2) if transpose else s


def _pick_tile(*lengths):
    for tm in (512, 256, 128, 64):
        if all(n % tm == 0 for n in lengths):
            return tm
    raise ValueError("token counts must be multiples of 64")


def kernel(x_prompt, x_sample, cache_na_k, cache_na_v, cache_diff_k, cache_diff_v, state_gla, state_gdn, c, c_ctx, w_ada, b_ada, norm1_w, norm2_w, w_in, na_rpb, gla_gate_up, gla_gate_bias, gla_norm_w, gdn_conv_w, gdn_A_log, gdn_dt_bias, gdn_norm_w, diff_lambda, diff_subln_w, w_branch, w_out, w_router, b_router, w_e1, w_e3, w_e2, final_norm_w):
    n_ctx, l_ctx, _ = x_prompt.shape
    n_lat, n_tok, _ = x_sample.shape
    depth = w_ada.shape[0]
    p_len = cache_na_k.shape[2]
    lat_tokens = n_lat * n_tok
    ctx_tokens = n_ctx * l_ctx
    assert n_lat + 1 <= 8 and n_tok % l_ctx == 0 and n_tok % GRID_W == 0 and l_ctx % CHUNK == 0
    tm = _pick_tile(n_tok, ctx_tokens)

    def mod_row(i):
        return jnp.where(i * tm < lat_tokens, (i * tm) // n_tok, n_lat)

    tm_moe = MOE_TILE if (n_tok % MOE_TILE == 0 and ctx_tokens % MOE_TILE == 0) else tm

    def mod_row_moe(i):
        return jnp.where(i * tm_moe < lat_tokens, (i * tm_moe) // n_tok, n_lat)

    tokens = lat_tokens + ctx_tokens
    xs = (x_sample.reshape(lat_tokens, D_MODEL), x_prompt.reshape(ctx_tokens, D_MODEL), 0, lat_tokens // tm)
    cond = jnp.zeros((8, D_MODEL), F32).at[:n_lat].set(c).at[n_lat].set(c_ctx)
    mod = _ada(cond, w_ada, b_ada).reshape(depth, 8, 6, D_MODEL)

    gla_pm = _gla_pair_masks()
    lsm, lbt, pm = _gdn_consts()
    bm = jnp.asarray(_block_mask(), F32)
    bm16 = bm.astype(BF16)
    sm = _solve_masks()
    cos, sin, perm = _rope_consts(n_tok)
    edge = _seq_edges(lat_tokens, n_tok, ctx_tokens, l_ctx)
    lane = np.arange(MIX_W)
    small_rows = np.arange(SMALL_W)
    def expand_cols(first_col):
        return jnp.asarray((small_rows[:, None] == first_col + lane[None, :] // HEAD_DIM).astype(np.float32), BF16)
    ea = jnp.stack([expand_cols(32), expand_cols(36)])
    eb = jnp.stack([expand_cols(40), expand_cols(44)])
    wr = jnp.zeros((D_MODEL, SMALL_W), F32).at[:, :N_EXPERTS].set(w_router)
    br = b_router.astype(F32)

    def w_layer(l):
        wl = w_in[l]
        parts = [wl[:, _OFF[n][0]:_OFF[n][1]] for n in (
            'na_q', 'na_k', 'na_v', 'gla_q', 'gla_k', 'gla_v', 'gla_og', 'gdn_qkv', 'gdn_og', 'diff_q', 'diff_k',
            'diff_v', 'gla_gf', 'gla_gb', 'gdn_af', 'gdn_ab', 'gdn_bf', 'gdn_bb')]
        parts.append(jnp.zeros((D_MODEL, SMALL_W - 2 * GATE_RANK - 4 * HEADS), F32))
        return (jnp.concatenate(parts, axis=1).astype(BF16), wl[:, _OFF['branch_gate'][0]:].astype(BF16),
                w_branch[l].astype(BF16), w_out[l].astype(BF16), w_e1[l].astype(BF16), w_e3[l].astype(BF16),
                w_e2[l].reshape(N_EXPERTS * D_FF, D_MODEL).astype(BF16))

    gup = jnp.zeros((depth, 2, SMALL_W, MIX_W), F32)
    gup = gup.at[:, 0, 0:GATE_RANK].set(gla_gate_up[:, 0]).at[:, 1, GATE_RANK:2 * GATE_RANK].set(gla_gate_up[:, 1])
    gbias = gla_gate_bias[:, :, None, :]
    alog = jnp.repeat(gdn_A_log, HEAD_DIM, axis=-1)[:, :, None, :]
    dtb = jnp.repeat(gdn_dt_bias, HEAD_DIM, axis=-1)[:, :, None, :]
    hn = jnp.stack([jnp.tile(gla_norm_w, (1, HEADS)), jnp.tile(gdn_norm_w, (1, HEADS)),
                    jnp.tile(diff_subln_w, (1, HEADS))], axis=1)[:, :, None, :]
    kh = min(NA_WIN_H, n_tok // GRID_W)
    na_bias = _na_bias_table(na_rpb.reshape(depth * HEADS, 2 * NA_WIN_H - 1, 2 * NA_WIN_W - 1), kh)
    na_bias = na_bias.reshape(depth, HEADS, kh, GRID_W, kh * GRID_W)

    def cache_rows(cache):
        return cache.transpose(1, 0, 2, 3, 4).reshape(depth, n_lat * p_len, MIX_W)

    ck_na, cv_na, cv_df = (cache_rows(t) for t in (cache_na_k, cache_na_v, cache_diff_v))
    ck_df_t = cache_diff_k.reshape(n_lat, depth, p_len, MIX_W).transpose(1, 0, 3, 2).astype(BF16)
    ck_df_t = ck_df_t.reshape(depth, n_lat * MIX_W, p_len)
    s0_gla = _block_diag_state(state_gla, True)
    s0_gdn = _block_diag_state(state_gdn, False)
    zeros_state = jnp.zeros((n_ctx, 2, MIX_W, MIX_W), F32)
    tq_diff = DIFF_Q_TILE if n_tok % DIFF_Q_TILE == 0 else CHUNK

    new_kv, new_gla, new_gdn = [], [], []
    for l in range(depth):
        lam_init = 0.8 - 0.6 * math.exp(-0.3 * l)
        wcat, w_gate, w_branch16, w_out16, w1, w3, w2 = w_layer(l)
        z_na, z_gla, z_gdn, z_diff, z_small = _proj(xs, tokens, mod[l], norm1_w[l][None], wcat, tm, mod_row)

        dl = diff_lambda[l]
        na_lat = _na_latent(z_na, ck_na[l], cv_na[l], na_bias[l], n_lat, n_tok, p_len)
        na_ctx = _attention(z_na, 0, z_na, 1, z_na, 2, dl, n_sub=1, lam_init=0.0, n_seq=n_ctx, lq=l_ctx,
                            lk=l_ctx, q_base=lat_tokens, k_base=lat_tokens, tq=l_ctx, name="na_ctx")

        q_r, k_rt = _rope(z_diff, cos, sin, perm, n_lat, n_tok, tm)
        df_lat = _diff_latent(q_r, k_rt, ck_df_t[l], z_diff, cv_df[l], dl, lam_init, n_lat, n_tok, p_len, tq_diff)
        df_ctx = _attention(z_diff, 0, z_diff, 1, z_diff, 2, dl, n_sub=2, lam_init=lam_init, n_seq=n_ctx,
                            lq=l_ctx, lk=l_ctx, q_base=lat_tokens, k_base=lat_tokens, tq=l_ctx, name="diff_ctx")

        qkv_n = _gdn_pre(z_gdn, edge, gdn_conv_w[l], bm16, tm)
        gla_consts = (gup[l], gbias[l], lsm, gla_pm, bm, bm16)
        gdn_consts = (ea, eb, alog[l], dtb[l], lsm, lbt, pm, bm, bm16, sm)
        (glf_lat, glb_lat, _), (gdf_lat, gdb_lat, _) = _scan(
            [(_gla_step, z_gla, gla_consts, s0_gla[:, l]), (_gdn_step, qkv_n, gdn_consts, s0_gdn[:, l])],
            z_small, n_lat, n_tok, 0, SCAN_SEQS_PER_STEP, "gla_gdn_scan")
        (glf_ctx, glb_ctx, st_ctx), (gdf_ctx, gdb_ctx, s_ctx) = _scan(
            [(_gla_step, z_gla, gla_consts, zeros_state), (_gdn_step, qkv_n, gdn_consts, zeros_state)],
            z_small, n_ctx, l_ctx, lat_tokens, SCAN_SEQS_PER_STEP, "gla_gdn_scan")

        x = _merge(xs, tokens, mod[l], norm1_w[l][None], (na_lat, na_ctx), (glf_lat, glf_ctx), (glb_lat, glb_ctx), z_gla,
                   (gdf_lat, gdf_ctx), (gdb_lat, gdb_ctx), z_gdn, (df_lat, df_ctx), hn[l], bm16,
                   w_gate, w_branch16, w_out16, tm, mod_row, lam_init, lat_tokens // tm)
        if l < depth - 1:
            x = _moe(x, mod[l], norm2_w[l][None], wr, br, w1, w3, w2, tm_moe, mod_row_moe)
            xs = (x, x, lat_tokens // tm, lat_tokens // tm)
        else:
            y_lat, y_ctx = _moe(x, mod[l], norm2_w[l][None], wr, br, w1, w3, w2, tm_moe, mod_row_moe,
                                final_norm_w[None], lat_tokens // tm_moe)

        new_kv.append((z_na[lat_tokens:, MIX_W:], z_diff[lat_tokens:, MIX_W:]))
        new_gla.append(st_ctx)
        new_gdn.append(s_ctx)

    y_sample = y_lat.reshape(n_lat, n_tok, D_MODEL)
    y_prompt = y_ctx.reshape(n_ctx, l_ctx, D_MODEL)
    def ctx_kv(j, col):
        kv = jnp.stack([t[j] for t in new_kv])[:, :, col * MIX_W:(col + 1) * MIX_W]
        return kv.reshape(depth, n_ctx, l_ctx, HEADS, HEAD_DIM).transpose(1, 0, 2, 3, 4)
    new_state_gla = _diag_blocks(jnp.stack(new_gla, axis=1), True)
    new_state_gdn = _diag_blocks(jnp.stack(new_gdn, axis=1), False)
    return (y_prompt, y_sample, ctx_kv(0, 0), ctx_kv(0, 1), ctx_kv(1, 0), ctx_kv(1, 1), new_state_gla, new_state_gdn)
```

```python
import functools
import math

import numpy as np
import jax
import jax.numpy as jnp
from jax import lax
from jax.experimental import pallas as pl
from jax.experimental.pallas import tpu as pltpu

F32 = jnp.float32
BF16 = jnp.bfloat16

D_MODEL = 1024
HEADS = 4
HEAD_DIM = 64
MIX_W = HEADS * HEAD_DIM
GRID_W = 64
NA_WIN_H = 8
NA_WIN_W = 16
GATE_RANK = 16
GATE_NORM = 16.0
CHUNK = 64
N_EXPERTS = 16
N_GROUPS = 4
GROUP_E = N_EXPERTS // N_GROUPS
D_FF = D_MODEL // 4
ROPE_BASE = 10000.0
EPS = 1e-6
NEG = -1e30
LOG2E = 1.4426950408889634
SMALL_W = 128
V7X_VMEM_LIMIT = 56 * 1024 * 1024
LEAF = 8
ATTN_HEAD_GROUP = 2
NA_ROWS_PER_STEP = 8
SCAN_SEQS_PER_STEP = 4
MOE_TILE = 2048
MOE_EXPERTS_PER_STEP = 1
MOE_LAST_TILE = 1024
MOE_LAST_EXPERTS_PER_STEP = 4

_IN_SPLITS = (
    ('na_q', MIX_W), ('na_k', MIX_W), ('na_v', MIX_W),
    ('gla_q', MIX_W), ('gla_k', MIX_W), ('gla_v', MIX_W),
    ('gla_gf', GATE_RANK), ('gla_gb', GATE_RANK), ('gla_og', MIX_W),
    ('gdn_qkv', 3 * MIX_W), ('gdn_af', HEADS), ('gdn_ab', HEADS),
    ('gdn_bf', HEADS), ('gdn_bb', HEADS), ('gdn_og', MIX_W),
    ('diff_q', MIX_W), ('diff_k', MIX_W), ('diff_v', MIX_W),
    ('branch_gate', 4 * D_MODEL),
)
_OFF = {}
_o = 0
for _n, _s in _IN_SPLITS:
    _OFF[_n] = (_o, _o + _s)
    _o += _s

def _params(sem):
    return pltpu.CompilerParams(dimension_semantics=sem, vmem_limit_bytes=V7X_VMEM_LIMIT)


def _dot(a, b):
    return jnp.dot(a, b, preferred_element_type=F32)


def _bdot(a, b):
    return _dot(a.astype(BF16), b.astype(BF16))


def _bdot_nt(a, b):
    return lax.dot_general(a.astype(BF16), b.astype(BF16), (((1,), (1,)), ((), ())),
                           preferred_element_type=F32)


def _bdot_tn(a, b):
    return lax.dot_general(a.astype(BF16), b.astype(BF16), (((0,), (0,)), ((), ())),
                           preferred_element_type=F32)


def _split(x):
    hi = x.astype(BF16)
    lo = (x - hi.astype(F32)).astype(BF16)
    return hi, lo


def _dot_rx(x, m):
    hi, lo = _split(x)
    return _dot(hi, m) + _dot(lo, m)


def _dot_lx(m, x):
    hi, lo = _split(x)
    return _dot(m, hi) + _dot(m, lo)


def _dot3(a, b):
    ah, al = _split(a)
    bh, bl = _split(b)
    return _dot(ah, bh) + _dot(ah, bl) + _dot(al, bh)


def _silu(x):
    return x * jax.nn.sigmoid(x)


def _softplus(x):
    return jnp.maximum(x, 0.0) + jnp.log(1.0 + jnp.exp(-jnp.abs(x)))


def _rms_rows(x, w):
    return x * lax.rsqrt(jnp.mean(x * x, axis=-1, keepdims=True) + EPS) * w


def _head_sum(x, bm16):
    return _dot_rx(x, bm16)


def _bd_rows(x, bm):
    return jnp.concatenate([x] * HEADS, axis=0) * bm


def _lane_mask(lo, width, n=MIX_W):
    lane = lax.broadcasted_iota(jnp.int32, (1, n), 1)
    return (lane >= lo) & (lane < lo + width)


def _gdn_consts():
    c = CHUNK
    i = np.arange(c)[:, None]
    t = np.arange(c)[None, :]
    lb = (t <= i).astype(np.float32)
    lbl = (t > i).astype(np.float32)
    tot = np.ones((16, c), np.float32)
    strict = (t < i).astype(np.float32)
    incl = (t <= i).astype(np.float32)
    rev = lambda a: a[::-1, ::-1]
    lsm = np.stack([np.concatenate([lb, lbl, tot], 0), np.concatenate([rev(lb), rev(lbl), tot], 0)])
    lbt = np.stack([np.tile(lb.T, (1, HEADS)), np.tile(rev(lb).T, (1, HEADS))])
    pm = np.stack([np.stack([np.tile(strict, (1, HEADS)), np.tile(incl, (1, HEADS))]),
                   np.stack([np.tile(rev(strict), (1, HEADS)), np.tile(rev(incl), (1, HEADS))])])
    return jnp.asarray(lsm, BF16), jnp.asarray(lbt, F32), jnp.asarray(pm, F32)


def _gla_pair_masks():
    i = np.arange(CHUNK)[:, None]
    j = np.arange(CHUNK)[None, :]
    nblk = CHUNK // LEAF
    out = []
    for d in range(2):
        later = (i // LEAF > j // LEAF) if d == 0 else (i // LEAF < j // LEAF)
        order = (j <= i) if d == 0 else (j >= i)
        far = [(j // LEAF == blk) & later for blk in range(nblk)]
        near = [(i // LEAF == j // LEAF) & (j % LEAF == r) & order for r in range(LEAF)]
        out.append(np.stack([np.tile(m.astype(np.float32), (1, HEADS)) for m in far + near]))
    return jnp.asarray(np.stack(out), F32)


def _block_mask():
    r = np.arange(MIX_W)
    return (r[:, None] // HEAD_DIM == r[None, :] // HEAD_DIM).astype(np.float32)


def _solve_masks():
    r = np.arange(MIX_W)[:, None]
    c = np.arange(MIX_W)[None, :]
    out = [(r // LEAF == c // LEAF)]
    s = LEAF
    while s < CHUNK:
        out.append((r // (2 * s) == c // (2 * s)) & (r // s != c // s))
        s *= 2
    return jnp.asarray(np.stack(out).astype(np.float32))


def _rope_consts(n_tok):
    t = np.arange(n_tok)
    pos = np.stack([t // GRID_W, t % GRID_W], 0).astype(np.float32)
    lane = np.arange(MIX_W)
    u = lane % 32
    axis = u // 16
    w = u % 16
    first = w < 8
    inv = ROPE_BASE ** (-(w % 8).astype(np.float32) / 8.0)
    ang = pos[axis, :].T * inv[None, :]
    cos = np.cos(ang)
    sin = np.sin(ang) * np.where(first, -1.0, 1.0)[None, :]
    partner = np.where(first, lane + 8, lane - 8)
    perm = np.zeros((MIX_W, MIX_W), np.float32)
    perm[partner, lane] = 1.0
    return jnp.asarray(cos, F32), jnp.asarray(sin, F32), jnp.asarray(perm, BF16)


def _ada_kernel(c_ref, w_ref, b_ref, o_ref):
    c = c_ref[...]
    o_ref[0] = _bdot(_silu(c), w_ref[0]) + b_ref[0]


def _ada(cond, w_ada, b_ada):
    depth, _, n = w_ada.shape
    tn = 1536
    return pl.pallas_call(
        _ada_kernel,
        grid=(depth, n // tn),
        in_specs=[pl.BlockSpec((8, D_MODEL), lambda l, j: (0, 0)),
                  pl.BlockSpec((1, D_MODEL, tn), lambda l, j: (l, 0, j)),
                  pl.BlockSpec((1, 1, tn), lambda l, j: (l, 0, j))],
        out_specs=pl.BlockSpec((1, 8, tn), lambda l, j: (l, 0, j)),
        out_shape=jax.ShapeDtypeStruct((depth, 8, n), F32),
        compiler_params=_params(("parallel", "parallel")),
        name="ada_mod",
    )(cond, w_ada, b_ada.reshape(depth, 1, n))


def _proj_kernel(xl_ref, xc_ref, mod_ref, nw_ref, w_ref, o_na, o_gla, o_gdn, o_diff, o_small, *, n_lat_tiles):
    mod = mod_ref[0]
    x = jnp.where(pl.program_id(0) < n_lat_tiles, xl_ref[...], xc_ref[...])
    h = _rms_rows(x, nw_ref[...]) * (1.0 + mod[1:2]) + mod[0:1]
    hb = h.astype(BF16)
    o_na[...] = _dot(hb, w_ref[:, 0:768])
    o_gla[...] = _dot(hb, w_ref[:, 768:1792])
    o_gdn[...] = _dot(hb, w_ref[:, 1792:2816])
    o_diff[...] = _dot(hb, w_ref[:, 2816:3584])
    o_small[...] = _dot(hb, w_ref[:, 3584:3712])


def _stream_specs(xs, tm):
    _, _, ctx_first, n_lat_tiles = xs
    return [pl.BlockSpec((tm, D_MODEL), lambda i: (jnp.minimum(i, n_lat_tiles - 1), 0)),
            pl.BlockSpec((tm, D_MODEL), lambda i: (ctx_first + jnp.maximum(i - n_lat_tiles, 0), 0))]


def _proj(xs, t, mod_l, nw, wcat, tm, mod_row):
    widths = (768, 1024, 1024, 768, SMALL_W)
    return pl.pallas_call(
        functools.partial(_proj_kernel, n_lat_tiles=xs[3]),
        grid=(t // tm,),
        in_specs=_stream_specs(xs, tm) + [
                  pl.BlockSpec((1, 6, D_MODEL), lambda i: (mod_row(i), 0, 0)),
                  pl.BlockSpec((1, D_MODEL), lambda i: (0, 0)),
                  pl.BlockSpec(wcat.shape, lambda i: (0, 0))],
        out_specs=[pl.BlockSpec((tm, w), lambda i: (i, 0)) for w in widths],
        out_shape=[jax.ShapeDtypeStruct((t, w), F32) for w in widths],
        compiler_params=_params(("parallel",)),
        name="in_proj",
    )(xs[0], xs[1], mod_l, nw, wcat)


def _attn_core(q, k, v, n_sub, lam):
    kb = k.astype(BF16)
    vb = v.astype(BF16)
    tq = q.shape[0]
    sub_w = HEAD_DIM // n_sub
    scale = sub_w ** -0.5 * LOG2E
    out = jnp.zeros(q.shape, F32)
    for h0 in range(0, HEADS, ATTN_HEAD_GROUP):
        maps = [(h, m) for h in range(h0, h0 + ATTN_HEAD_GROUP) for m in range(n_sub)]
        qs = jnp.concatenate(
            [(q * jnp.where(_lane_mask(h * HEAD_DIM + m * sub_w, sub_w), scale, 0.0)).astype(BF16)
             for h, m in maps], axis=0)
        s = lax.dot_general(qs, kb, (((1,), (1,)), ((), ())), preferred_element_type=F32)
        e = jnp.exp2(s - jnp.max(s, axis=-1, keepdims=True))
        inv = 1.0 / jnp.sum(e, axis=-1, keepdims=True)
        if n_sub == 1:
            p = e * inv
        else:
            p = jnp.concatenate(
                [e[(2 * i) * tq:(2 * i + 1) * tq] * inv[(2 * i) * tq:(2 * i + 1) * tq]
                 - e[(2 * i + 1) * tq:(2 * i + 2) * tq] * (lam * inv[(2 * i + 1) * tq:(2 * i + 2) * tq])
                 for i in range(ATTN_HEAD_GROUP)], axis=0)
        o_all = _dot(p.astype(BF16), vb)
        for i in range(ATTN_HEAD_GROUP):
            out = jnp.where(_lane_mask((h0 + i) * HEAD_DIM, HEAD_DIM), o_all[i * tq:(i + 1) * tq], out)
    return out


def _diff_lambda(dl, lam_init):
    a = jnp.sum(dl[0:1] * dl[1:2], axis=-1, keepdims=True)
    b = jnp.sum(dl[2:3] * dl[3:4], axis=-1, keepdims=True)
    return jnp.exp(a) - jnp.exp(b) + lam_init


def _attn_kernel(q_ref, k_ref, v_ref, dl_ref, o_ref, *, n_sub, lam_init):
    lam = _diff_lambda(dl_ref[...], lam_init) if n_sub == 2 else None
    o_ref[...] = _attn_core(q_ref[...], k_ref[...], v_ref[...], n_sub, lam)


def _attention(q_arr, q_col, k_arr, k_col, v_arr, v_col, dl, *, n_sub, lam_init, n_seq, lq, lk,
               q_base, k_base, tq, name):
    nq = lq // tq
    return pl.pallas_call(
        functools.partial(_attn_kernel, n_sub=n_sub, lam_init=lam_init),
        grid=(n_seq, nq),
        in_specs=[pl.BlockSpec((tq, MIX_W), lambda s, j: (q_base // tq + s * nq + j, q_col)),
                  pl.BlockSpec((lk, MIX_W), lambda s, j: (k_base // lk + s, k_col)),
                  pl.BlockSpec((lk, MIX_W), lambda s, j: (k_base // lk + s, v_col)),
                  pl.BlockSpec(dl.shape, lambda s, j: (0, 0))],
        out_specs=pl.BlockSpec((tq, MIX_W), lambda s, j: (s * nq + j, 0)),
        out_shape=jax.ShapeDtypeStruct((n_seq * lq, MIX_W), F32),
        compiler_params=_params(("parallel", "parallel")),
        name=name,
    )(q_arr, k_arr, v_arr, dl)


DIFF_KEY_CHUNK = 256
DIFF_HEAD_GROUP = 1
DIFF_Q_TILE = 256


def _lane_fold(x, op):
    out = x[:, 0:128]
    for t in range(1, x.shape[1] // 128):
        out = op(out, x[:, t * 128:(t + 1) * 128])
    return out


def _diff_group(h0, q, kt_chunk, v_chunk, nck, lam):
    tq = q.shape[0]
    sub_w = HEAD_DIM // 2
    scale = sub_w ** -0.5 * LOG2E
    maps = [(h, m) for h in range(h0, h0 + DIFF_HEAD_GROUP) for m in range(2)]
    qs = jnp.concatenate(
        [(q * jnp.where(_lane_mask(h * HEAD_DIM + m * sub_w, sub_w), scale, 0.0)).astype(BF16) for h, m in maps],
        axis=0)
    s, mvec = [], None
    for c in range(nck):
        sc = _dot(qs, kt_chunk(c))
        s.append(sc)
        mc = _lane_fold(sc, jnp.maximum)
        mvec = mc if mvec is None else jnp.maximum(mvec, mc)
        yield
    mrun = jnp.max(mvec, axis=-1, keepdims=True)
    e, lvec = [], None
    for c in range(nck):
        ec = jnp.exp2(s[c] - mrun)
        e.append(ec)
        lc = _lane_fold(ec, jnp.add)
        lvec = lc if lvec is None else lvec + lc
        yield
    inv = 1.0 / jnp.sum(lvec, axis=-1, keepdims=True)
    acc = None
    for c in range(nck):
        a = jnp.concatenate(
            [e[c][(2 * i) * tq:(2 * i + 1) * tq] * inv[(2 * i) * tq:(2 * i + 1) * tq]
             - e[c][(2 * i + 1) * tq:(2 * i + 2) * tq] * (lam * inv[(2 * i + 1) * tq:(2 * i + 2) * tq])
             for i in range(DIFF_HEAD_GROUP)], axis=0)
        o = _dot(a.astype(BF16), v_chunk(c))
        acc = o if acc is None else acc + o
        if c < nck - 1:
            yield
    return acc


_DIFF_PASSES = 3


def _diff_lat_kernel(q_ref, kt_ref, ckt_ref, v_ref, cv_ref, dl_ref, o_ref, *, lam_init, tkc):
    lam = _diff_lambda(dl_ref[...], lam_init)
    q = q_ref[...]
    tq = q.shape[0]
    n_new = kt_ref.shape[1] // tkc
    nck = n_new + ckt_ref.shape[1] // tkc

    def kt_chunk(c):
        ref, j = (kt_ref, c) if c < n_new else (ckt_ref, c - n_new)
        return ref[:, j * tkc:(j + 1) * tkc]

    def v_chunk(c):
        ref, j = (v_ref, c) if c < n_new else (cv_ref, c - n_new)
        return ref[j * tkc:(j + 1) * tkc, :].astype(BF16)

    n_grp = HEADS // DIFF_HEAD_GROUP
    groups = [_diff_group(g * DIFF_HEAD_GROUP, q, kt_chunk, v_chunk, nck, lam) for g in range(n_grp)]
    res = [None] * n_grp
    for t in range(n_grp + _DIFF_PASSES - 1):
        active = [g for g in range(n_grp) if 0 <= t - g < _DIFF_PASSES]
        for _ in range(nck):
            for g in active:
                try:
                    next(groups[g])
                except StopIteration as done:
                    res[g] = done.value
    out = jnp.zeros(q.shape, F32)
    for h in range(HEADS):
        g, i = divmod(h, DIFF_HEAD_GROUP)
        out = jnp.where(_lane_mask(h * HEAD_DIM, HEAD_DIM), res[g][i * tq:(i + 1) * tq], out)
    o_ref[...] = out


def _diff_latent(q_r, k_t, ck_t, z_diff, cv, dl, lam_init, n_b, n_tok, p_len, tq):
    tkc = DIFF_KEY_CHUNK if (n_tok % DIFF_KEY_CHUNK == 0 and p_len % DIFF_KEY_CHUNK == 0) else 128
    assert n_tok % tkc == 0 and p_len % tkc == 0
    nq = n_tok // tq
    return pl.pallas_call(
        functools.partial(_diff_lat_kernel, lam_init=lam_init, tkc=tkc),
        grid=(n_b, nq),
        in_specs=[pl.BlockSpec((tq, MIX_W), lambda b, j: (b * nq + j, 0)),
                  pl.BlockSpec((MIX_W, n_tok), lambda b, j: (b, 0)),
                  pl.BlockSpec((MIX_W, p_len), lambda b, j: (b, 0)),
                  pl.BlockSpec((n_tok, MIX_W), lambda b, j: (b, 2)),
                  pl.BlockSpec((p_len, MIX_W), lambda b, j: (b, 0)),
                  pl.BlockSpec(dl.shape, lambda b, j: (0, 0))],
        out_specs=pl.BlockSpec((tq, MIX_W), lambda b, j: (b * nq + j, 0)),
        out_shape=jax.ShapeDtypeStruct((n_b * n_tok, MIX_W), F32),
        compiler_params=_params(("parallel", "arbitrary")),
        name="diff_latent",
    )(q_r, k_t, ck_t, z_diff, cv, dl)


def _na_lat_kernel(q_ref, k_ref, v_ref, ck_ref, cv_ref, bias_ref, o_ref, *, rows, kh, rps):
    g = pl.program_id(1)
    ck = ck_ref[...].astype(BF16)
    cv = cv_ref[...].astype(BF16)
    scale = HEAD_DIM ** -0.5 * LOG2E
    head_scale = [jnp.where(_lane_mask(h * HEAD_DIM, HEAD_DIM), scale, 0.0) for h in range(HEADS)]
    hw = HEADS * GRID_W
    qs = []
    for j in range(rps):
        qj = q_ref[j * GRID_W:(j + 1) * GRID_W, :]
        qs.append(jnp.concatenate([(qj * hs).astype(BF16) for hs in head_scale], axis=0))
    s_ctx_all = lax.dot_general(jnp.concatenate(qs, axis=0), ck, (((1,), (1,)), ((), ())),
                                preferred_element_type=F32)
    p_ctx, o_loc = [], []
    for j in range(rps):
        r = g * rps + j
        start = jnp.clip(r - kh // 2, 0, rows - kh)
        cls = r - start
        ws = pl.multiple_of(start * GRID_W, GRID_W)
        kw = k_ref[pl.ds(ws, kh * GRID_W), :].astype(BF16)
        vw = v_ref[pl.ds(ws, kh * GRID_W), :].astype(BF16)
        bias = jnp.concatenate([bias_ref[h, pl.ds(cls, 1)][0] for h in range(HEADS)], axis=0)
        s_loc = lax.dot_general(qs[j], kw, (((1,), (1,)), ((), ())), preferred_element_type=F32) + bias
        s_ctx = s_ctx_all[j * hw:(j + 1) * hw]
        mx = jnp.maximum(jnp.max(s_loc, axis=-1, keepdims=True), jnp.max(s_ctx, axis=-1, keepdims=True))
        e_loc = jnp.exp2(s_loc - mx)
        e_ctx = jnp.exp2(s_ctx - mx)
        inv = 1.0 / (jnp.sum(e_loc, axis=-1, keepdims=True) + jnp.sum(e_ctx, axis=-1, keepdims=True))
        p_ctx.append((e_ctx * inv).astype(BF16))
        o_loc.append(_dot((e_loc * inv).astype(BF16), vw))
    o_ctx_all = _dot(jnp.concatenate(p_ctx, axis=0), cv)
    for j in range(rps):
        o_all = o_loc[j] + o_ctx_all[j * hw:(j + 1) * hw]
        out = jnp.zeros((GRID_W, MIX_W), F32)
        for h in range(HEADS):
            out = jnp.where(_lane_mask(h * HEAD_DIM, HEAD_DIM), o_all[h * GRID_W:(h + 1) * GRID_W], out)
        o_ref[j * GRID_W:(j + 1) * GRID_W, :] = out


def _na_bias_table(rpb, kh):
    cidx = np.arange(GRID_W)
    cls = np.arange(kh)
    drow = np.arange(kh)[None, :] - cls[:, None] + NA_WIN_H - 1
    col_start = np.clip(cidx - NA_WIN_W // 2, 0, GRID_W - NA_WIN_W)
    col_ok = (cidx[None, :] >= col_start[:, None]) & (cidx[None, :] < col_start[:, None] + NA_WIN_W)
    dcol = np.clip(cidx[None, :] - cidx[:, None], 1 - NA_WIN_W, NA_WIN_W - 1) + NA_WIN_W - 1
    onehot = (dcol.reshape(1, -1) == np.arange(2 * NA_WIN_W - 1)[:, None]).astype(np.float32)
    toep = jnp.dot(rpb.astype(F32).reshape(-1, 2 * NA_WIN_W - 1), onehot, precision=lax.Precision.HIGHEST)
    toep = toep.reshape(-1, 2 * NA_WIN_H - 1, GRID_W, GRID_W)
    bias = jnp.stack([toep[:, drow[c, 0]:drow[c, 0] + kh] for c in range(kh)], axis=1)
    bias = bias.transpose(0, 1, 3, 2, 4)
    bias = jnp.where(col_ok[:, None, :], bias * LOG2E, NEG)
    return bias.reshape(-1, kh, GRID_W, kh * GRID_W)


def _na_latent(z_na, ck, cv, bias, n_b, n_tok, p_len):
    rows = n_tok // GRID_W
    kh = min(NA_WIN_H, rows)
    rps = NA_ROWS_PER_STEP
    assert rows % rps == 0
    steps = rows // rps
    return pl.pallas_call(
        functools.partial(_na_lat_kernel, rows=rows, kh=kh, rps=rps),
        grid=(n_b, steps),
        in_specs=[pl.BlockSpec((rps * GRID_W, MIX_W), lambda b, r: (b * steps + r, 0)),
                  pl.BlockSpec((n_tok, MIX_W), lambda b, r: (b, 1)),
                  pl.BlockSpec((n_tok, MIX_W), lambda b, r: (b, 2)),
                  pl.BlockSpec((p_len, MIX_W), lambda b, r: (b, 0)),
                  pl.BlockSpec((p_len, MIX_W), lambda b, r: (b, 0)),
                  pl.BlockSpec(bias.shape, lambda b, r: (0, 0, 0, 0))],
        out_specs=pl.BlockSpec((rps * GRID_W, MIX_W), lambda b, r: (b * steps + r, 0)),
        out_shape=jax.ShapeDtypeStruct((n_b * n_tok, MIX_W), F32),
        compiler_params=_params(("parallel", "arbitrary")),
        name="na_latent",
    )(z_na, z_na, z_na, ck, cv, bias)


def _rope_kernel(q_ref, k_ref, cos_ref, sin_ref, p_ref, qo_ref, ko_ref):
    cos = cos_ref[...]
    sin = sin_ref[...]
    p = p_ref[...]
    q = q_ref[...]
    k = k_ref[...]
    qo_ref[...] = q * cos + _dot_rx(q, p) * sin
    ko_ref[...] = (k * cos + _dot_rx(k, p) * sin).T.astype(BF16)


def _rope(z_diff, cos, sin, perm, n_b, n_tok, tm):
    nt = n_tok // tm
    spec_t = pl.BlockSpec((tm, MIX_W), lambda i: (i % nt, 0))
    return pl.pallas_call(
        _rope_kernel,
        grid=(n_b * nt,),
        in_specs=[pl.BlockSpec((tm, MIX_W), lambda i: (i, 0)),
                  pl.BlockSpec((tm, MIX_W), lambda i: (i, 1)),
                  spec_t, spec_t,
                  pl.BlockSpec(perm.shape, lambda i: (0, 0))],
        out_specs=[pl.BlockSpec((tm, MIX_W), lambda i: (i, 0)),
                   pl.BlockSpec((MIX_W, tm), lambda i: (i // nt, i % nt))],
        out_shape=[jax.ShapeDtypeStruct((n_b * n_tok, MIX_W), F32),
                   jax.ShapeDtypeStruct((n_b * MIX_W, n_tok), BF16)],
        compiler_params=_params(("parallel",)),
        name="diff_rope",
    )(z_diff, z_diff, cos, sin, perm)


def _gdn_pre_kernel(x_ref, prev_ref, next_ref, edge_ref, w_ref, bm_ref, o_ref, *, tm):
    x = x_ref[...]
    w = w_ref[...]
    edge = edge_ref[...]
    row = lax.broadcasted_iota(jnp.int32, (tm, 1), 0)
    x_prev = jnp.where(row == 0, prev_ref[7:8, :], pltpu.roll(x, 1, 0)) * edge[:, 0:1]
    x_next = jnp.where(row == tm - 1, next_ref[0:1, :], pltpu.roll(x, tm - 1, 0)) * edge[:, 1:2]
    y = _silu(x_prev * w[0:1] + x * w[1:2] + x_next * w[2:3])
    bm16 = bm_ref[...]
    q = y[:, 0:MIX_W]
    k = y[:, MIX_W:2 * MIX_W]
    o_ref[:, 0:MIX_W] = q * lax.rsqrt(_head_sum(q * q, bm16) + EPS) * (HEAD_DIM ** -0.5)
    o_ref[:, MIX_W:2 * MIX_W] = k * lax.rsqrt(_head_sum(k * k, bm16) + EPS)
    o_ref[:, 2 * MIX_W:3 * MIX_W] = y[:, 2 * MIX_W:3 * MIX_W]


def _seq_edges(lat_tokens, n_tok, ctx_tokens, l_ctx):
    pos = np.concatenate([np.arange(lat_tokens) % n_tok, np.arange(ctx_tokens) % l_ctx])
    seq = np.concatenate([np.full(lat_tokens, n_tok), np.full(ctx_tokens, l_ctx)])
    edge = np.zeros((lat_tokens + ctx_tokens, 8), np.float32)
    edge[:, 0] = pos != 0
    edge[:, 1] = pos != seq - 1
    return jnp.asarray(edge)


def _gdn_pre(z_gdn, edge, conv_w, bm16, tm):
    t = z_gdn.shape[0]
    w3 = 3 * MIX_W
    nb8 = t // 8
    return pl.pallas_call(
        functools.partial(_gdn_pre_kernel, tm=tm),
        grid=(t // tm,),
        in_specs=[pl.BlockSpec((tm, w3), lambda i: (i, 0)),
                  pl.BlockSpec((8, w3), lambda i: (jnp.maximum(i * (tm // 8) - 1, 0), 0)),
                  pl.BlockSpec((8, w3), lambda i: (jnp.minimum((i + 1) * (tm // 8), nb8 - 1), 0)),
                  pl.BlockSpec((tm, 8), lambda i: (i, 0)),
                  pl.BlockSpec(conv_w.shape, lambda i: (0, 0)),
                  pl.BlockSpec(bm16.shape, lambda i: (0, 0))],
        out_specs=pl.BlockSpec((tm, w3), lambda i: (i, 0)),
        out_shape=jax.ShapeDtypeStruct((t, w3), F32),
        compiler_params=_params(("parallel",)),
        name="gdn_pre",
    )(z_gdn, z_gdn, z_gdn, edge, conv_w, bm16)


def _bd_rows16(x, bm16):
    return jnp.concatenate([x.astype(BF16)] * HEADS, axis=0) * bm16


def _gla_chain(q, k, v, zs, gup, gb, lsm, pm_ref, d, bm, bm16, st):
    nblk = CHUNK // LEAF
    q = q * (HEAD_DIM ** -0.5)
    gp = _bdot(zs, gup) + gb
    g = (jnp.minimum(gp, 0.0) - jnp.log(1.0 + jnp.exp(-jnp.abs(gp)))) * (LOG2E / GATE_NORM)
    yield
    e = _dot_lx(lsm, g)
    b = e[0:CHUNK]
    fb = jnp.exp2(b)
    fbl = jnp.exp2(e[CHUNK:2 * CHUNK])
    ftot = jnp.exp2(e[2 * CHUNK:2 * CHUNK + 1])
    yield
    edge = LEAF - 1 if d == 0 else 0
    blocks = range(nblk - 1) if d == 0 else range(1, nblk)
    b3 = b.reshape(nblk, LEAF, MIX_W)
    b_edge = jnp.broadcast_to(b3[:, edge:edge + 1, :], b3.shape).reshape(CHUNK, MIX_W)
    kf = jnp.exp2(jnp.minimum(b_edge - b, 0.0))
    q_far = jnp.concatenate(
        [(q * jnp.exp2(jnp.minimum(b - b[blk * LEAF + edge:blk * LEAF + edge + 1, :], 0.0))).astype(BF16)
         for blk in blocks], axis=0)
    s_far = lax.dot_general(q_far, _bd_rows16(k * kf, bm16), (((1,), (1,)), ((), ())),
                            preferred_element_type=F32)
    yield
    q_near = jnp.concatenate(
        [(q * jnp.exp2(jnp.minimum(
            b - jnp.broadcast_to(b3[:, r:r + 1, :], b3.shape).reshape(CHUNK, MIX_W), 0.0))).astype(BF16)
         for r in range(LEAF)], axis=0)
    s_near = lax.dot_general(q_near, _bd_rows16(k, bm16), (((1,), (1,)), ((), ())),
                             preferred_element_type=F32)
    yield
    att = None
    for n, blk in enumerate(blocks):
        term = s_far[n * CHUNK:(n + 1) * CHUNK] * pm_ref[d, blk]
        att = term if att is None else att + term
    for r in range(LEAF):
        att = att + s_near[r * CHUNK:(r + 1) * CHUNK] * pm_ref[d, nblk + r]
    yield
    o = _bdot_nt(q * fb, st) + _dot(att.astype(BF16), _bd_rows16(v, bm16))
    yield
    return o, st * ftot + _bdot_tn(v, k * fbl) * bm


def _row_pieces(s, d):
    out = []
    for h in range(HEADS):
        for blk in range(CHUNK // s):
            lo = h * CHUNK + blk * s
            out.append((lo, lo + s, blk % 2 == 1 - d))
    return out


def _tri_inverse(mbd, sm_ref, d):
    r = lax.broadcasted_iota(jnp.int32, (MIX_W, MIX_W), 0)
    c = lax.broadcasted_iota(jnp.int32, (MIX_W, MIX_W), 1)
    eye = jnp.where(r == c, 1.0, 0.0)
    md = mbd * sm_ref[0]
    m2 = _bdot(md, md)
    yield
    m4 = _bdot(m2, m2)
    t = eye - md
    t = t + _bdot(t, m2)
    yield
    t = t + _bdot(t, m4)
    s = LEAF
    for lvl in range(1, sm_ref.shape[0]):
        pieces = _row_pieces(s, d)
        take = lambda x: jnp.concatenate([x[lo:hi] for lo, hi, upd in pieces if upd], axis=0)
        yield
        t16 = t.astype(BF16)
        a = _dot(take(mbd * sm_ref[lvl]).astype(BF16), t16).astype(BF16)
        rows, n = [], 0
        for lo, hi, upd in pieces:
            rows.append(a[n:n + s] if upd else jnp.zeros((s, MIX_W), BF16))
            n += s if upd else 0
        yield
        b = _dot(take(t16), jnp.concatenate(rows, axis=0))
        rows, n = [], 0
        for lo, hi, upd in pieces:
            rows.append(t[lo:hi] - b[n:n + s] if upd else t[lo:hi])
            n += s if upd else 0
        t = jnp.concatenate(rows, axis=0)
        s *= 2
    return t


def _gdn_chain(q, k, v, zs, ea, eb, alog, dtb, lsm, lbt, pm_ref, d, bm, bm16, sm_ref, s):
    g = -jnp.exp(alog) * _softplus(_dot_rx(zs, ea) + dtb)
    beta = jax.nn.sigmoid(_dot_rx(zs, eb))
    yield
    e = _dot_lx(lsm, g)
    dcol = e[0:CHUNK]
    drow = _dot_lx(jnp.ones((8, CHUNK), BF16), g * lbt)[0:1]
    kb = k * beta
    s2 = lax.dot_general(jnp.concatenate([kb, q], axis=0).astype(BF16), _bd_rows16(k, bm16),
                         (((1,), (1,)), ((), ())), preferred_element_type=F32)
    yield
    dec = jnp.exp(jnp.where(pm_ref[d, 1] > 0.0, dcol - drow, NEG))
    m_cat = s2[0:CHUNK] * dec * pm_ref[d, 0]
    att = s2[CHUNK:2 * CHUNK] * dec
    t_inv = yield from _tri_inverse(_bd_rows(m_cat, bm), sm_ref, d)
    t16 = (t_inv[0:CHUNK] + t_inv[CHUNK:2 * CHUNK] + t_inv[2 * CHUNK:3 * CHUNK] + t_inv[3 * CHUNK:]).astype(BF16)
    fb = jnp.exp(dcol)
    fbl = jnp.exp(e[CHUNK:2 * CHUNK])
    ftot = jnp.exp(e[2 * CHUNK:2 * CHUNK + 1])
    yield
    u = _dot(t16, _bd_rows16(v * beta, bm16))
    w = _dot(t16, _bd_rows16(kb * fb, bm16))
    yield
    ws = _dot(jnp.concatenate([w.astype(BF16), (q * fb).astype(BF16)], axis=0), s.astype(BF16))
    v_new = u - ws[0:CHUNK]
    yield
    o = ws[CHUNK:2 * CHUNK] + _dot(att.astype(BF16), _bd_rows16(v_new, bm16))
    return o, s * ftot + lax.dot_general((k * fbl).astype(BF16), v_new.astype(BF16), (((0,), (0,)), ((), ())),
                                         preferred_element_type=F32) * bm


def _scan_kernel(*refs, chains, n_consts, nb, n_chunks):
    ns = len(chains)
    zs_refs, pos = refs[:nb * 2], nb * 2
    streams = []
    for i in range(ns):
        tok = refs[pos:pos + nb * 2 * 3]
        consts = refs[pos + nb * 2 * 3:pos + nb * 2 * 3 + n_consts[i]]
        streams.append((tok, consts, refs[pos + nb * 2 * 3 + n_consts[i]]))
        pos += nb * 2 * 3 + n_consts[i] + 1
    outs, scrs = refs[pos:pos + 3 * ns], refs[pos + 3 * ns:]
    n = pl.program_id(1)

    @pl.when(n == 0)
    def _():
        for i in range(ns):
            scrs[i][...] = streams[i][2][...]

    ids = [(i, j, d) for j in range(nb) for d in range(2) for i in range(ns)]
    gens = []
    for i, j, d in ids:
        tok, consts, _ = streams[i]
        q_ref, k_ref, v_ref = tok[(j * 2 + d) * 3:(j * 2 + d) * 3 + 3]
        gens.append(chains[i](q_ref[...], k_ref[...], v_ref[...], zs_refs[j * 2 + d][...], consts, d, scrs[i][j, d]))
    live = list(range(len(gens)))
    while live:
        for c in list(live):
            try:
                next(gens[c])
            except StopIteration as done:
                i, j, d = ids[c]
                o, s_new = done.value
                outs[3 * i + d][j] = o
                scrs[i][j, d] = s_new
                live.remove(c)

    @pl.when(n == n_chunks - 1)
    def _():
        for i in range(ns):
            outs[3 * i + 2][...] = scrs[i][...]


def _gla_step(q, k, v, zs, consts, d, st):
    gup_ref, gb_ref, lsm_ref, pm_ref, bm_ref, bm16_ref = consts
    return _gla_chain(q, k, v, zs, gup_ref[d], gb_ref[d], lsm_ref[d], pm_ref, d, bm_ref[...], bm16_ref[...], st)


def _gdn_step(q, k, v, zs, consts, d, s):
    ea_ref, eb_ref, alog_ref, dtb_ref, lsm_ref, lbt_ref, pm_ref, bm_ref, bm16_ref, sm_ref = consts
    return _gdn_chain(q, k, v, zs, ea_ref[d], eb_ref[d], alog_ref[d], dtb_ref[d], lsm_ref[d], lbt_ref[d],
                      pm_ref, d, bm_ref[...], bm16_ref[...], sm_ref, s)


def _scan(streams, z_small, n_seq, l_seq, base, seqs_per_step, name):
    nc = l_seq // CHUNK
    nb = seqs_per_step if n_seq % seqs_per_step == 0 else 1

    def chunk(n, d):
        return n + d * (nc - 1 - 2 * n)

    def tok_spec(j, d, col, width):
        return pl.BlockSpec((CHUNK, width),
                            lambda g, n: (base // CHUNK + (g * nb + j) * nc + chunk(n, d), col))

    state_spec = pl.BlockSpec((nb, 2, MIX_W, MIX_W), lambda g, n: (g, 0, 0, 0))
    in_specs = [tok_spec(j, d, 0, SMALL_W) for j in range(nb) for d in range(2)]
    args = [z_small] * (nb * 2)
    out_specs, out_shape, scratch = [], [], []
    for chain, qkv, consts, s0 in streams:
        for j in range(nb):
            for d in range(2):
                in_specs += [tok_spec(j, d, col, MIX_W) for col in range(3)]
                args += [qkv] * 3
        for cst in consts:
            in_specs.append(pl.BlockSpec(cst.shape, functools.partial(lambda g, n, nd: (0,) * nd, nd=cst.ndim)))
            args.append(cst)
        in_specs.append(state_spec)
        args.append(s0)
        out_specs += [pl.BlockSpec((nb, CHUNK, MIX_W), lambda g, n: (g, chunk(n, 0), 0)),
                      pl.BlockSpec((nb, CHUNK, MIX_W), lambda g, n: (g, chunk(n, 1), 0)),
                      state_spec]
        out_shape += [jax.ShapeDtypeStruct((n_seq, l_seq, MIX_W), F32),
                      jax.ShapeDtypeStruct((n_seq, l_seq, MIX_W), F32),
                      jax.ShapeDtypeStruct((n_seq, 2, MIX_W, MIX_W), F32)]
        scratch.append(pltpu.VMEM((nb, 2, MIX_W, MIX_W), F32))
    res = pl.pallas_call(
        functools.partial(_scan_kernel, chains=tuple(st[0] for st in streams),
                          n_consts=tuple(len(st[2]) for st in streams), nb=nb, n_chunks=nc),
        grid=(n_seq // nb, nc),
        in_specs=in_specs,
        out_specs=out_specs,
        out_shape=out_shape,
        scratch_shapes=scratch,
        compiler_params=_params(("parallel", "arbitrary")),
        name=name,
    )(*args)
    return [(res[3 * i].reshape(n_seq * l_seq, MIX_W), res[3 * i + 1].reshape(n_seq * l_seq, MIX_W), res[3 * i + 2])
            for i in range(len(streams))]


def _merge_kernel(xl_ref, xc_ref, mod_ref, n1_ref, na_l, na_c, glaf_l, glaf_c, glab_l, glab_c, glag_ref, gdnf_l, gdnf_c,
                  gdnb_l, gdnb_c, gdng_ref, df_l, df_c, hn_ref, bm_ref, wg_ref, wb_ref, wo_ref, o_ref, *,
                  lam_init, n_lat_tiles):
    is_lat = pl.program_id(0) < n_lat_tiles

    def pick(lat_ref, ctx_ref):
        return jnp.where(is_lat, lat_ref[...], ctx_ref[...])

    mod = mod_ref[0]
    x = pick(xl_ref, xc_ref)
    h = _rms_rows(x, n1_ref[...]) * (1.0 + mod[1:2]) + mod[0:1]
    hb = h.astype(BF16)
    bm16 = bm_ref[...]

    def head_norm(o, w):
        return o * lax.rsqrt(_head_sum(o * o, bm16) * (1.0 / HEAD_DIM) + EPS) * w

    branches = (
        pick(na_l, na_c),
        head_norm(pick(glaf_l, glaf_c) + pick(glab_l, glab_c), hn_ref[0]) * _silu(glag_ref[...]),
        head_norm(pick(gdnf_l, gdnf_c) + pick(gdnb_l, gdnb_c), hn_ref[1]) * _silu(gdng_ref[...]),
        head_norm(pick(df_l, df_c), hn_ref[2]) * (1.0 - lam_init),
    )
    merged = None
    for n, br in enumerate(branches):
        gate = jax.nn.sigmoid(_dot(hb, wg_ref[:, n * D_MODEL:(n + 1) * D_MODEL]))
        term = gate * _bdot(br, wb_ref[n])
        merged = term if merged is None else merged + term
    o_ref[...] = x + mod[2:3] * _bdot(merged, wo_ref[...])


def _merge(xs, t, mod_l, n1, na, gla_f, gla_b, z_gla, gdn_f, gdn_b, z_gdn, df, hn, bm16, wg, wb, wo, tm, mod_row,
           lam_init, n_lat_tiles):
    tok = lambda col: pl.BlockSpec((tm, MIX_W), lambda i: (i, col))
    lat = pl.BlockSpec((tm, MIX_W), lambda i: (jnp.minimum(i, n_lat_tiles - 1), 0))
    ctx = pl.BlockSpec((tm, MIX_W), lambda i: (jnp.maximum(i - n_lat_tiles, 0), 0))
    full = lambda a: pl.BlockSpec(a.shape, lambda i: (0,) * a.ndim)
    return pl.pallas_call(
        functools.partial(_merge_kernel, lam_init=lam_init, n_lat_tiles=n_lat_tiles),
        grid=(t // tm,),
        in_specs=_stream_specs(xs, tm) + [
                  pl.BlockSpec((1, 6, D_MODEL), lambda i: (mod_row(i), 0, 0)),
                  full(n1), lat, ctx, lat, ctx, lat, ctx, tok(3), lat, ctx, lat, ctx, tok(3), lat, ctx,
                  full(hn), full(bm16), full(wg), full(wb), full(wo)],
        out_specs=pl.BlockSpec((tm, D_MODEL), lambda i: (i, 0)),
        out_shape=jax.ShapeDtypeStruct((t, D_MODEL), F32),
        compiler_params=_params(("parallel",)),
        name="merge_out",
    )(xs[0], xs[1], mod_l, n1, *na, *gla_f, *gla_b, z_gla, *gdn_f, *gdn_b, z_gdn, *df, hn, bm16, wg, wb, wo)


def _route(h2, wr_ref, br_ref):
    s_t = jax.nn.sigmoid(_dot3(h2, wr_ref[...])).T
    s = [s_t[e:e + 1] for e in range(N_EXPERTS)]
    sel = [s[e] + br_ref[e] for e in range(N_EXPERTS)]
    best = None
    for g in range(N_GROUPS):
        v = sel[g * GROUP_E:(g + 1) * GROUP_E]
        score = None
        for i in range(GROUP_E):
            for j in range(i + 1, GROUP_E):
                score = v[i] + v[j] if score is None else jnp.maximum(score, v[i] + v[j])
        if best is None:
            best, gi = score, jnp.zeros_like(score)
        else:
            better = score > best
            gi = jnp.where(better, float(g), gi)
            best = jnp.where(better, score, best)
    vals = [jnp.where(gi == float(e // GROUP_E), sel[e], NEG) for e in range(N_EXPERTS)]

    def first_max(v):
        mx, idx = v[0], jnp.zeros_like(v[0])
        for e in range(1, N_EXPERTS):
            better = v[e] > mx
            idx = jnp.where(better, float(e), idx)
            mx = jnp.where(better, v[e], mx)
        return idx

    i1 = first_max(vals)
    i2 = first_max([jnp.where(i1 == float(e), NEG, vals[e]) for e in range(N_EXPERTS)])
    s1 = sum(jnp.where(i1 == float(e), s[e], 0.0) for e in range(N_EXPERTS))
    s2 = sum(jnp.where(i2 == float(e), s[e], 0.0) for e in range(N_EXPERTS))
    inv = 1.0 / (s1 + s2)
    rows = [jnp.where(i1 == float(e), s1 * inv, 0.0) + jnp.where(i2 == float(e), s2 * inv, 0.0)
            for e in range(N_EXPERTS)]
    gates_t = jnp.concatenate(rows + [jnp.zeros((SMALL_W - N_EXPERTS, h2.shape[0]), F32)], axis=0)
    return gates_t.T


def _moe_kernel(x_ref, mod_ref, n2_ref, wr_ref, br_ref, w1_ref, w3_ref, w2_ref, *rest, n_lat_tiles, es):
    if n_lat_tiles is None:
        o_ref, h_scr, g_scr, acc_scr = rest
    else:
        fw_ref, ol_ref, oc_ref, h_scr, g_scr, acc_scr = rest
    g = pl.program_id(1)
    n_steps = N_EXPERTS // es
    mod = mod_ref[0]

    @pl.when(g == 0)
    def _():
        h2 = _rms_rows(x_ref[...], n2_ref[...]) * (1.0 + mod[4:5]) + mod[3:4]
        h_scr[...] = h2.astype(BF16)
        g_scr[...] = _route(h2, wr_ref, br_ref)
        acc_scr[...] = jnp.zeros_like(acc_scr)

    hb = h_scr[...]
    gates = g_scr[...]
    lane = lax.broadcasted_iota(jnp.int32, (1, SMALL_W), 1)
    he = jnp.concatenate(
        [_silu(_dot(hb, w1_ref[e])) * _dot(hb, w3_ref[e])
         * jnp.sum(jnp.where(lane == g * es + e, gates, 0.0), axis=-1, keepdims=True)
         for e in range(es)], axis=1)
    acc_scr[...] += _dot(he.astype(BF16), w2_ref[...])

    if n_lat_tiles is None:
        @pl.when(g == n_steps - 1)
        def _():
            o_ref[...] = x_ref[...] + mod[5:6] * acc_scr[...]
    else:
        is_lat = pl.program_id(0) < n_lat_tiles

        @pl.when((g == n_steps - 1) & is_lat)
        def _():
            ol_ref[...] = _rms_rows(x_ref[...] + mod[5:6] * acc_scr[...], fw_ref[...])

        @pl.when((g == n_steps - 1) & jnp.logical_not(is_lat))
        def _():
            oc_ref[...] = _rms_rows(x_ref[...] + mod[5:6] * acc_scr[...], fw_ref[...])


def _moe(x, mod_l, n2, wr, br, w1, w3, w2, tm, mod_row, final_w=None, n_lat_tiles=None):
    t = x.shape[0]
    es = MOE_EXPERTS_PER_STEP if final_w is None else MOE_LAST_EXPERTS_PER_STEP
    gw = es * D_FF
    once = pl.Buffered(1)
    in_specs = [pl.BlockSpec((tm, D_MODEL), lambda i, g: (i, 0), pipeline_mode=once),
                pl.BlockSpec((1, 6, D_MODEL), lambda i, g: (mod_row(i), 0, 0)),
                pl.BlockSpec(n2.shape, lambda i, g: (0, 0)),
                pl.BlockSpec(wr.shape, lambda i, g: (0, 0)),
                pl.BlockSpec(memory_space=pltpu.SMEM),
                pl.BlockSpec((es, D_MODEL, D_FF), lambda i, g: (g, 0, 0)),
                pl.BlockSpec((es, D_MODEL, D_FF), lambda i, g: (g, 0, 0)),
                pl.BlockSpec((gw, D_MODEL), lambda i, g: (g, 0))]
    args = [x, mod_l, n2, wr, br, w1, w3, w2]
    if final_w is None:
        out_specs = pl.BlockSpec((tm, D_MODEL), lambda i, g: (i, 0), pipeline_mode=once)
        out_shape = jax.ShapeDtypeStruct((t, D_MODEL), F32)
    else:
        in_specs.append(pl.BlockSpec(final_w.shape, lambda i, g: (0, 0)))
        args.append(final_w)
        out_specs = [pl.BlockSpec((tm, D_MODEL), lambda i, g: (jnp.minimum(i, n_lat_tiles - 1), 0), pipeline_mode=once),
                     pl.BlockSpec((tm, D_MODEL), lambda i, g: (jnp.maximum(i - n_lat_tiles, 0), 0), pipeline_mode=once)]
        out_shape = [jax.ShapeDtypeStruct((n_lat_tiles * tm, D_MODEL), F32),
                     jax.ShapeDtypeStruct((t - n_lat_tiles * tm, D_MODEL), F32)]
    return pl.pallas_call(
        functools.partial(_moe_kernel, n_lat_tiles=n_lat_tiles, es=es),
        grid=(t // tm, N_EXPERTS // es),
        in_specs=in_specs,
        out_specs=out_specs,
        out_shape=out_shape,
        scratch_shapes=[pltpu.VMEM((tm, D_MODEL), BF16), pltpu.VMEM((tm, SMALL_W), F32),
                        pltpu.VMEM((tm, D_MODEL), F32)],
        compiler_params=_params(("arbitrary", "arbitrary")),
        name="moe",
    )(*args)


def _block_diag_state(s, transpose):
    if transpose:
        s = jnp.swapaxes(s, -1, -2)
    eye = jnp.eye(HEADS, dtype=s.dtype)
    out = jnp.einsum('...hij,hg->...higj', s, eye)
    return out.reshape(s.shape[:-3] + (MIX_W, MIX_W))


def _diag_blocks(s_bd, transpose):
    s = s_bd.reshape(s_bd.shape[:-2] + (HEADS, HEAD_DIM, HEADS, HEAD_DIM))
    s = jnp.stack([s[..., h, :, h, :] for h in range(HEADS)], axis=-3)
    return jnp.swapaxes(s, -1, -2) if transpose else s


def _pick_tile(*lengths):
    for tm in (512, 256, 128, 64):
        if all(n % tm == 0 for n in lengths):
            return tm
    raise ValueError("token counts must be multiples of 64")


def kernel(x_prompt, x_sample, cache_na_k, cache_na_v, cache_diff_k, cache_diff_v, state_gla, state_gdn, c, c_ctx, w_ada, b_ada, norm1_w, norm2_w, w_in, na_rpb, gla_gate_up, gla_gate_bias, gla_norm_w, gdn_conv_w, gdn_A_log, gdn_dt_bias, gdn_norm_w, diff_lambda, diff_subln_w, w_branch, w_out, w_router, b_router, w_e1, w_e3, w_e2, final_norm_w):
    n_ctx, l_ctx, _ = x_prompt.shape
    n_lat, n_tok, _ = x_sample.shape
    depth = w_ada.shape[0]
    p_len = cache_na_k.shape[2]
    lat_tokens = n_lat * n_tok
    ctx_tokens = n_ctx * l_ctx
    assert n_lat + 1 <= 8 and n_tok % l_ctx == 0 and n_tok % GRID_W == 0 and l_ctx % CHUNK == 0
    tm = _pick_tile(n_tok, ctx_tokens)

    def mod_row(i):
        return jnp.where(i * tm < lat_tokens, (i * tm) // n_tok, n_lat)

    def moe_tile(want):
        return want if (n_tok % want == 0 and ctx_tokens % want == 0) else tm

    def mod_row_moe(tile):
        return lambda i: jnp.where(i * tile < lat_tokens, (i * tile) // n_tok, n_lat)

    tm_moe, tm_moe_last = moe_tile(MOE_TILE), moe_tile(MOE_LAST_TILE)

    tokens = lat_tokens + ctx_tokens
    xs = (x_sample.reshape(lat_tokens, D_MODEL), x_prompt.reshape(ctx_tokens, D_MODEL), 0, lat_tokens // tm)
    cond = jnp.zeros((8, D_MODEL), F32).at[:n_lat].set(c).at[n_lat].set(c_ctx)
    mod = _ada(cond, w_ada, b_ada).reshape(depth, 8, 6, D_MODEL)

    gla_pm = _gla_pair_masks()
    lsm, lbt, pm = _gdn_consts()
    bm = jnp.asarray(_block_mask(), F32)
    bm16 = bm.astype(BF16)
    sm = _solve_masks()
    cos, sin, perm = _rope_consts(n_tok)
    edge = _seq_edges(lat_tokens, n_tok, ctx_tokens, l_ctx)
    lane = np.arange(MIX_W)
    small_rows = np.arange(SMALL_W)
    def expand_cols(first_col):
        return jnp.asarray((small_rows[:, None] == first_col + lane[None, :] // HEAD_DIM).astype(np.float32), BF16)
    ea = jnp.stack([expand_cols(32), expand_cols(36)])
    eb = jnp.stack([expand_cols(40), expand_cols(44)])
    wr = jnp.zeros((D_MODEL, SMALL_W), F32).at[:, :N_EXPERTS].set(w_router)
    br = b_router.astype(F32)

    def w_layer(l):
        wl = w_in[l]
        parts = [wl[:, _OFF[n][0]:_OFF[n][1]] for n in (
            'na_q', 'na_k', 'na_v', 'gla_q', 'gla_k', 'gla_v', 'gla_og', 'gdn_qkv', 'gdn_og', 'diff_q', 'diff_k',
            'diff_v', 'gla_gf', 'gla_gb', 'gdn_af', 'gdn_ab', 'gdn_bf', 'gdn_bb')]
        parts.append(jnp.zeros((D_MODEL, SMALL_W - 2 * GATE_RANK - 4 * HEADS), F32))
        return (jnp.concatenate(parts, axis=1).astype(BF16), wl[:, _OFF['branch_gate'][0]:].astype(BF16),
                w_branch[l].astype(BF16), w_out[l].astype(BF16), w_e1[l].astype(BF16), w_e3[l].astype(BF16),
                w_e2[l].reshape(N_EXPERTS * D_FF, D_MODEL).astype(BF16))

    gup = jnp.zeros((depth, 2, SMALL_W, MIX_W), F32)
    gup = gup.at[:, 0, 0:GATE_RANK].set(gla_gate_up[:, 0]).at[:, 1, GATE_RANK:2 * GATE_RANK].set(gla_gate_up[:, 1])
    gbias = gla_gate_bias[:, :, None, :]
    alog = jnp.repeat(gdn_A_log, HEAD_DIM, axis=-1)[:, :, None, :]
    dtb = jnp.repeat(gdn_dt_bias, HEAD_DIM, axis=-1)[:, :, None, :]
    hn = jnp.stack([jnp.tile(gla_norm_w, (1, HEADS)), jnp.tile(gdn_norm_w, (1, HEADS)),
                    jnp.tile(diff_subln_w, (1, HEADS))], axis=1)[:, :, None, :]
    kh = min(NA_WIN_H, n_tok // GRID_W)
    na_bias = _na_bias_table(na_rpb.reshape(depth * HEADS, 2 * NA_WIN_H - 1, 2 * NA_WIN_W - 1), kh)
    na_bias = na_bias.reshape(depth, HEADS, kh, GRID_W, kh * GRID_W)

    def cache_rows(cache):
        return cache.transpose(1, 0, 2, 3, 4).reshape(depth, n_lat * p_len, MIX_W)

    ck_na, cv_na, cv_df = (cache_rows(t) for t in (cache_na_k, cache_na_v, cache_diff_v))
    ck_df_t = cache_diff_k.reshape(n_lat, depth, p_len, MIX_W).transpose(1, 0, 3, 2).astype(BF16)
    ck_df_t = ck_df_t.reshape(depth, n_lat * MIX_W, p_len)
    s0_gla = _block_diag_state(state_gla, True)
    s0_gdn = _block_diag_state(state_gdn, False)
    zeros_state = jnp.zeros((n_ctx, 2, MIX_W, MIX_W), F32)
    tq_diff = DIFF_Q_TILE if n_tok % DIFF_Q_TILE == 0 else CHUNK

    new_kv, new_gla, new_gdn = [], [], []
    for l in range(depth):
        lam_init = 0.8 - 0.6 * math.exp(-0.3 * l)
        wcat, w_gate, w_branch16, w_out16, w1, w3, w2 = w_layer(l)
        z_na, z_gla, z_gdn, z_diff, z_small = _proj(xs, tokens, mod[l], norm1_w[l][None], wcat, tm, mod_row)

        dl = diff_lambda[l]
        na_lat = _na_latent(z_na, ck_na[l], cv_na[l], na_bias[l], n_lat, n_tok, p_len)
        na_ctx = _attention(z_na, 0, z_na, 1, z_na, 2, dl, n_sub=1, lam_init=0.0, n_seq=n_ctx, lq=l_ctx,
                            lk=l_ctx, q_base=lat_tokens, k_base=lat_tokens, tq=l_ctx, name="na_ctx")

        q_r, k_rt = _rope(z_diff, cos, sin, perm, n_lat, n_tok, tm)
        df_lat = _diff_latent(q_r, k_rt, ck_df_t[l], z_diff, cv_df[l], dl, lam_init, n_lat, n_tok, p_len, tq_diff)
        df_ctx = _attention(z_diff, 0, z_diff, 1, z_diff, 2, dl, n_sub=2, lam_init=lam_init, n_seq=n_ctx,
                            lq=l_ctx, lk=l_ctx, q_base=lat_tokens, k_base=lat_tokens, tq=l_ctx, name="diff_ctx")

        qkv_n = _gdn_pre(z_gdn, edge, gdn_conv_w[l], bm16, tm)
        gla_consts = (gup[l], gbias[l], lsm, gla_pm, bm, bm16)
        gdn_consts = (ea, eb, alog[l], dtb[l], lsm, lbt, pm, bm, bm16, sm)
        (glf_lat, glb_lat, _), (gdf_lat, gdb_lat, _) = _scan(
            [(_gla_step, z_gla, gla_consts, s0_gla[:, l]), (_gdn_step, qkv_n, gdn_consts, s0_gdn[:, l])],
            z_small, n_lat, n_tok, 0, SCAN_SEQS_PER_STEP, "gla_gdn_scan")
        (glf_ctx, glb_ctx, st_ctx), (gdf_ctx, gdb_ctx, s_ctx) = _scan(
            [(_gla_step, z_gla, gla_consts, zeros_state), (_gdn_step, qkv_n, gdn_consts, zeros_state)],
            z_small, n_ctx, l_ctx, lat_tokens, SCAN_SEQS_PER_STEP, "gla_gdn_scan")

        x = _merge(xs, tokens, mod[l], norm1_w[l][None], (na_lat, na_ctx), (glf_lat, glf_ctx), (glb_lat, glb_ctx), z_gla,
                   (gdf_lat, gdf_ctx), (gdb_lat, gdb_ctx), z_gdn, (df_lat, df_ctx), hn[l], bm16,
                   w_gate, w_branch16, w_out16, tm, mod_row, lam_init, lat_tokens // tm)
        if l < depth - 1:
            x = _moe(x, mod[l], norm2_w[l][None], wr, br, w1, w3, w2, tm_moe, mod_row_moe(tm_moe))
            xs = (x, x, lat_tokens // tm, lat_tokens // tm)
        else:
            y_lat, y_ctx = _moe(x, mod[l], norm2_w[l][None], wr, br, w1, w3, w2, tm_moe_last,
                                mod_row_moe(tm_moe_last), final_norm_w[None], lat_tokens // tm_moe_last)

        new_kv.append((z_na[lat_tokens:, MIX_W:], z_diff[lat_tokens:, MIX_W:]))
        new_gla.append(st_ctx)
        new_gdn.append(s_ctx)

    y_sample = y_lat.reshape(n_lat, n_tok, D_MODEL)
    y_prompt = y_ctx.reshape(n_ctx, l_ctx, D_MODEL)
    def ctx_kv(j, col):
        kv = jnp.stack([t[j] for t in new_kv])[:, :, col * MIX_W:(col + 1) * MIX_W]
        return kv.reshape(depth, n_ctx, l_ctx, HEADS, HEAD_DIM).transpose(1, 0, 2, 3, 4)
    new_state_gla = _diag_blocks(jnp.stack(new_gla, axis=1), True)
    new_state_gdn = _diag_blocks(jnp.stack(new_gdn, axis=1), False)
    return (y_prompt, y_sample, ctx_kv(0, 0), ctx_kv(0, 1), ctx_kv(1, 0), ctx_kv(1, 1), new_state_gla, new_state_gdn)
```
